```python
import math
import jax
import jax.numpy as jnp
from jax import lax
import numpy as np

D_MODEL = 2048
BATCH = 4
SEQ = 2048
DEPTH = 4
DEC_BATCH = 128
DEC_SEQ = 1
PAST_LEN = 16384
PAGE_SIZE = 128

F32 = jnp.float32
N_AB = (DEPTH + 1) // 2
N_C = DEPTH // 2
A_HEAD_DIM = 64
A_WIDTH = D_MODEL // 2
A_HEADS = A_WIDTH // A_HEAD_DIM
LORA_W = 64
LORA_A = 64
LORA_G = 160
P_A = 3 * A_WIDTH + LORA_W + LORA_A + LORA_G
A_GN_EPS = 64e-5
B_WIDTH = D_MODEL - A_WIDTH
B_BLOCK_DIM = 64
B_BLOCKS = B_WIDTH // B_BLOCK_DIM
LRU_CONV = 4
LRU_C = 8.0
P_B = 2 * B_WIDTH
C_HEAD_K = 128
C_HEAD_V = 128
C_K_HEADS = D_MODEL // C_HEAD_K
C_V_HEADS = 2 * C_K_HEADS
C_KEY_DIM = C_K_HEADS * C_HEAD_K
C_VAL_DIM = C_V_HEADS * C_HEAD_V
C_QKV_DIM = 2 * C_KEY_DIM + C_VAL_DIM
C_CONV = 4
GDN_CHUNK = 64
P_C = C_QKV_DIM + C_VAL_DIM + 2 * C_V_HEADS
D_FF = 256 * (-(-8 * D_MODEL // (3 * 256)))
FFN_CONV = 3
NORM_EPS = 1e-6

kernel_name = 'hybrid_rwkv7_rglru_gdn_convffn_step'


def _rms_norm(x, gain):
    xf = x.astype(F32)
    y = xf * lax.rsqrt(jnp.mean(xf * xf, axis=-1, keepdims=True) + NORM_EPS)
    return (y * gain.astype(F32)).astype(x.dtype)


def _l2norm(t, eps):
    return t * lax.rsqrt(jnp.sum(t * t, axis=-1, keepdims=True) + eps)


def _causal_dwconv(x, buf, w, b=None):
    width = w.shape[0]
    T = x.shape[1]
    xp = jnp.concatenate([buf.astype(x.dtype), x], axis=1)
    y = sum(xp[:, j:j + T] * w[j] for j in range(width))
    if b is not None:
        y = y + b
    return y, xp[:, T:]


def _wkv7_scan(r, w, k, v, kk, a, S0):
    def step(S, xs):
        r_t, w_t, k_t, v_t, kk_t, a_t = xs
        s_kk = jnp.einsum('bhij,bhj->bhi', S, kk_t)
        S = (S * w_t[:, :, None, :] - s_kk[..., None] * (kk_t * a_t)[:, :, None, :]
             + v_t[..., None] * k_t[:, :, None, :])
        return S, jnp.einsum('bhij,bhj->bhi', S, r_t)
    xs = tuple(jnp.moveaxis(t, 1, 0) for t in (r, w, k, v, kk, a))
    S, ys = lax.scan(step, S0, xs)
    return jnp.moveaxis(ys, 0, 1), S


def _rwkv7_mix(p, prev, S0, mu, w0, w2, a0, a2, g2, k_k, k_a, r_k, gn_w, gn_b):
    Bsz, T, _ = p.shape
    p_prev = jnp.concatenate([prev[:, None].astype(p.dtype), p[:, :-1]], axis=1)
    pm = p + (p_prev - p) * mu
    r, k, v, xw, xa, xg = jnp.split(pm, [A_WIDTH, 2 * A_WIDTH, 3 * A_WIDTH, 3 * A_WIDTH + LORA_W,
                                         3 * A_WIDTH + LORA_W + LORA_A], axis=-1)
    w_ll = -jax.nn.softplus(-(w0 + jnp.tanh(xw) @ w2).astype(F32)) - 0.5
    decay = jnp.exp(-jnp.exp(w_ll))
    a = jax.nn.sigmoid((a0 + xa @ a2).astype(F32))
    g = (jax.nn.sigmoid(xg) @ g2).astype(F32)
    heads = lambda t: t.reshape(Bsz, T, A_HEADS, A_HEAD_DIM)
    kf = k.astype(F32)
    kk = _l2norm(heads(kf * k_k), 1e-12)
    kf = kf * (1.0 + (a - 1.0) * k_a)
    r_h, k_h, v_h, a_h, w_h = heads(r.astype(F32)), heads(kf), heads(v.astype(F32)), heads(a), heads(decay)
    y, S = _wkv7_scan(r_h, w_h, k_h, v_h, kk, a_h, S0.astype(F32))
    mean = jnp.mean(y, axis=-1, keepdims=True)
    var = jnp.mean(jnp.square(y - mean), axis=-1, keepdims=True)
    y = ((y - mean) * lax.rsqrt(var + A_GN_EPS)).reshape(Bsz, T, A_WIDTH) * gn_w + gn_b
    bonus = (jnp.sum(r_h * k_h * r_k, axis=-1, keepdims=True) * v_h).reshape(Bsz, T, A_WIDTH)
    y = (y + bonus) * g
    return y.astype(p.dtype), S, p[:, -1]


def _linear_scan(a, b, h0):
    b = b.at[:, 0].add(a[:, 0] * h0)
    def combine(lhs, rhs):
        a_l, b_l = lhs
        a_r, b_r = rhs
        return a_l * a_r, a_r * b_l + b_r
    _, h = lax.associative_scan(combine, (a, b), axis=1)
    return h, h[:, -1]


def _rglru_mix(p, conv_buf, h0, conv_w, conv_b, wa, ba, wx, bx, lam):
    gate, xb = jnp.split(p, 2, axis=-1)
    xc, new_buf = _causal_dwconv(xb, conv_buf, conv_w, conv_b)
    Bsz, T, _ = xc.shape
    xh = xc.reshape(Bsz, T, B_BLOCKS, B_BLOCK_DIM)
    r_gate = jax.nn.sigmoid((jnp.einsum('bthi,hij->bthj', xh, wa).reshape(Bsz, T, B_WIDTH) + ba).astype(F32))
    i_gate = jax.nn.sigmoid((jnp.einsum('bthi,hij->bthj', xh, wx).reshape(Bsz, T, B_WIDTH) + bx).astype(F32))
    log_a = -LRU_C * r_gate * jax.nn.softplus(-lam.astype(F32))
    a = jnp.exp(log_a)
    b = jnp.sqrt(-jnp.expm1(2.0 * log_a)) * (i_gate * xc.astype(F32))
    h, h_last = _linear_scan(a, b, h0.astype(F32))
    y = h * jax.nn.gelu(gate.astype(F32))
    return y.astype(p.dtype), new_buf, h_last


def _gated_delta_chunked(q, k, v, g, beta, S0):
    Bsz, T, H, DK = k.shape
    DV = v.shape[-1]
    C = min(GDN_CHUNK, T)
    n = -(-T // C)
    pad = n * C - T
    if pad:
        pt = lambda t: jnp.pad(t, [(0, 0), (0, pad)] + [(0, 0)] * (t.ndim - 2))
        q, k, v, g, beta = pt(q), pt(k), pt(v), pt(g), pt(beta)
    ch4 = lambda t: t.reshape(Bsz, n, C, H, t.shape[-1]).transpose(1, 0, 3, 2, 4)
    ch3 = lambda t: t.reshape(Bsz, n, C, H).transpose(1, 0, 3, 2)
    q, k, v = ch4(q), ch4(k), ch4(v)
    g, beta = ch3(g), ch3(beta)
    G = jnp.cumsum(g, axis=-1)
    idx = jnp.arange(C)
    incl = idx[:, None] >= idx[None, :]
    strict = idx[:, None] > idx[None, :]
    decay = jnp.exp(jnp.where(incl, G[..., :, None] - G[..., None, :], -jnp.inf))
    kb = k * beta[..., None]
    L = jnp.einsum('nbhid,nbhjd->nbhij', kb, k) * jnp.where(strict, decay, 0.0)
    eye = jnp.eye(C, dtype=L.dtype)
    rhs = jnp.concatenate([v * beta[..., None], kb * jnp.exp(G)[..., None]], axis=-1)
    sol = lax.linalg.triangular_solve(eye + L, rhs, left_side=True, lower=True, unit_diagonal=True)
    u, w = sol[..., :DV], sol[..., DV:]
    qk = jnp.einsum('nbhid,nbhjd->nbhij', q, k) * decay
    qg = q * jnp.exp(G)[..., None]
    k_tail = k * jnp.exp(G[..., -1:] - G)[..., None]
    g_end = jnp.exp(G[..., -1])
    def step(S, xs):
        u_c, w_c, qg_c, qk_c, kt_c, ge_c = xs
        v_new = u_c - jnp.einsum('bhcd,bhde->bhce', w_c, S)
        o = jnp.einsum('bhcd,bhde->bhce', qg_c, S) + jnp.einsum('bhij,bhje->bhie', qk_c, v_new)
        S = S * ge_c[..., None, None] + jnp.einsum('bhcd,bhce->bhde', kt_c, v_new)
        return S, o
    S, o = lax.scan(step, S0, (u, w, qg, qk, k_tail, g_end))
    o = o.transpose(1, 0, 3, 2, 4).reshape(Bsz, n * C, H, DV)[:, :T]
    return o, S


def _gdn_mix(p, conv_buf, S0, conv_w, a_log, dt_bias, norm_w):
    Bsz, T, _ = p.shape
    qkv, z, b, a = jnp.split(p, [C_QKV_DIM, C_QKV_DIM + C_VAL_DIM, C_QKV_DIM + C_VAL_DIM + C_V_HEADS], axis=-1)
    qkv, new_buf = _causal_dwconv(qkv, conv_buf, conv_w)
    qkv = jax.nn.silu(qkv.astype(F32))
    q = qkv[..., :C_KEY_DIM].reshape(Bsz, T, C_K_HEADS, C_HEAD_K)
    k = qkv[..., C_KEY_DIM:2 * C_KEY_DIM].reshape(Bsz, T, C_K_HEADS, C_HEAD_K)
    v = qkv[..., 2 * C_KEY_DIM:].reshape(Bsz, T, C_V_HEADS, C_HEAD_V)
    rep = C_V_HEADS // C_K_HEADS
    q = jnp.repeat(_l2norm(q, 1e-6) * (C_HEAD_K ** -0.5), rep, axis=2)
    k = jnp.repeat(_l2norm(k, 1e-6), rep, axis=2)
    beta = jax.nn.sigmoid(b.astype(F32))
    g = -jnp.exp(a_log.astype(F32)) * jax.nn.softplus(a.astype(F32) + dt_bias)
    o, S = _gated_delta_chunked(q, k, v, g, beta, S0.astype(F32))
    o = o * lax.rsqrt(jnp.mean(o * o, axis=-1, keepdims=True) + NORM_EPS) * norm_w
    o = o * jax.nn.silu(z.astype(F32).reshape(Bsz, T, C_V_HEADS, C_HEAD_V))
    return o.reshape(Bsz, T, C_VAL_DIM).astype(p.dtype), new_buf, S


def _conv_ffn(h, buf, w_gate, w_up, conv_w, conv_b, w_down):
    u, new_buf = _causal_dwconv(h @ w_gate, buf, conv_w, conv_b)
    return (jax.nn.gelu(u) * (h @ w_up)) @ w_down, new_buf


def _trunk(x, c, wkv0, shift0, lru_h0, lru_conv0, gdn0, gdn_conv0, ffn_conv0, prm):
    n_wkv, n_shift, n_h, n_lconv, n_gdn, n_gconv, n_fconv = [], [], [], [], [], [], []
    for layer in range(DEPTH):
        mod = jax.nn.silu(c) @ prm['w_mod'][layer] + prm['b_mod'][layer]
        sh1, sc1, gt1, sh2, sc2, gt2 = jnp.split(mod[:, None, :], 6, axis=-1)
        h = _rms_norm(x, prm['norm_mix'][layer]) * (1.0 + sc1) + sh1
        i = layer // 2
        if layer % 2 == 0:
            proj = h @ prm['w_in_ab'][i]
            ya, s_new, last = _rwkv7_mix(proj[..., :P_A], shift0[i], wkv0[i], prm['rwkv_mu'][i], prm['rwkv_w0'][i],
                                         prm['rwkv_w2'][i], prm['rwkv_a0'][i], prm['rwkv_a2'][i], prm['rwkv_g2'][i],
                                         prm['rwkv_k_k'][i], prm['rwkv_k_a'][i], prm['rwkv_r_k'][i],
                                         prm['rwkv_gn_w'][i], prm['rwkv_gn_b'][i])
            yb, lbuf, h_last = _rglru_mix(proj[..., P_A:], lru_conv0[i], lru_h0[i], prm['lru_conv_w'][i],
                                          prm['lru_conv_b'][i], prm['lru_wa'][i], prm['lru_ba'][i],
                                          prm['lru_wx'][i], prm['lru_bx'][i], prm['lru_lambda'][i])
            mix = jnp.concatenate([ya, yb], axis=-1) @ prm['w_out_ab'][i]
            n_wkv.append(s_new)
            n_shift.append(last)
            n_h.append(h_last)
            n_lconv.append(lbuf)
        else:
            proj = h @ prm['w_in_c'][i]
            yc, gbuf, s_new = _gdn_mix(proj, gdn_conv0[i], gdn0[i], prm['gdn_conv_w'][i], prm['gdn_a_log'][i],
                                       prm['gdn_dt_bias'][i], prm['gdn_norm_w'][i])
            mix = yc @ prm['w_out_c'][i]
            n_gdn.append(s_new)
            n_gconv.append(gbuf)
        x = x + gt1 * mix
        h = _rms_norm(x, prm['norm_ffn'][layer]) * (1.0 + sc2) + sh2
        f, fbuf = _conv_ffn(h, ffn_conv0[layer], prm['ffn_w_gate'][layer], prm['ffn_w_up'][layer],
                            prm['ffn_conv_w'][layer], prm['ffn_conv_b'][layer], prm['ffn_w_down'][layer])
        x = x + gt2 * f
        n_fconv.append(fbuf)
    y = _rms_norm(x, prm['norm_out'])
    stk = lambda lst, like: jnp.stack(lst).astype(like.dtype)
    return (y, stk(n_wkv, wkv0), stk(n_shift, shift0), stk(n_h, lru_h0), stk(n_lconv, lru_conv0),
            stk(n_gdn, gdn0), stk(n_gconv, gdn_conv0), stk(n_fconv, ffn_conv0))


def setup_inputs(seed: int = 0) -> dict:
    key = jax.random.key(seed)
    keys = jax.random.split(key, 64)
    counter = iter(range(64))
    D = D_MODEL

    def nrm(shape, scale=1.0):
        return scale * jax.random.normal(keys[next(counter)], shape, F32)

    def uni(shape, lo, hi):
        return jax.random.uniform(keys[next(counter)], shape, F32, lo, hi)

    inp = {
        'x_prompt': nrm((BATCH, SEQ, D)),
        'x_sample': nrm((DEC_BATCH, DEC_SEQ, D)),
        'c_prompt': nrm((BATCH, D)),
        'c_sample': nrm((DEC_BATCH, D)),
        'state_rwkv_wkv': nrm((N_AB, DEC_BATCH, A_HEADS, A_HEAD_DIM, A_HEAD_DIM), 0.3),
        'state_rwkv_shift': nrm((N_AB, DEC_BATCH, P_A)),
        'state_lru_h': nrm((N_AB, DEC_BATCH, B_WIDTH), 0.5),
        'state_lru_conv': nrm((N_AB, DEC_BATCH, LRU_CONV - 1, B_WIDTH)),
        'state_gdn': nrm((N_C, DEC_BATCH, C_V_HEADS, C_HEAD_K, C_HEAD_V), 0.1),
        'state_gdn_conv': nrm((N_C, DEC_BATCH, C_CONV - 1, C_QKV_DIM)),
        'state_ffn_conv': nrm((DEPTH, DEC_BATCH, FFN_CONV - 1, D_FF)),
        'w_mod': nrm((DEPTH, D, 6 * D), 0.5 * D ** -0.5),
        'b_mod': nrm((DEPTH, 6 * D), 0.05),
        'norm_mix': 1.0 + nrm((DEPTH, D), 0.05),
        'norm_ffn': 1.0 + nrm((DEPTH, D), 0.05),
        'norm_out': 1.0 + nrm((D,), 0.05),
        'w_in_ab': nrm((N_AB, D, P_A + P_B), D ** -0.5),
        'w_out_ab': nrm((N_AB, A_WIDTH + B_WIDTH, D), (A_WIDTH + B_WIDTH) ** -0.5),
        'rwkv_mu': uni((N_AB, P_A), 0.0, 1.0),
        'rwkv_w0': uni((N_AB, A_WIDTH), -4.0, 1.5),
        'rwkv_w2': nrm((N_AB, LORA_W, A_WIDTH), 0.1 * LORA_W ** -0.5),
        'rwkv_a0': nrm((N_AB, A_WIDTH), 0.1),
        'rwkv_a2': nrm((N_AB, LORA_A, A_WIDTH), LORA_A ** -0.5),
        'rwkv_g2': nrm((N_AB, LORA_G, A_WIDTH), LORA_G ** -0.5),
        'rwkv_k_k': 0.85 + nrm((N_AB, A_WIDTH), 0.05),
        'rwkv_k_a': 1.0 + nrm((N_AB, A_WIDTH), 0.05),
        'rwkv_r_k': nrm((N_AB, A_HEADS, A_HEAD_DIM), 0.1),
        'rwkv_gn_w': 1.0 + nrm((N_AB, A_WIDTH), 0.05),
        'rwkv_gn_b': nrm((N_AB, A_WIDTH), 0.02),
        'lru_conv_w': nrm((N_AB, LRU_CONV, B_WIDTH), LRU_CONV ** -0.5),
        'lru_conv_b': nrm((N_AB, B_WIDTH), 0.02),
        'lru_wa': nrm((N_AB, B_BLOCKS, B_BLOCK_DIM, B_BLOCK_DIM), B_BLOCK_DIM ** -0.5),
        'lru_ba': nrm((N_AB, B_WIDTH), 0.02),
        'lru_wx': nrm((N_AB, B_BLOCKS, B_BLOCK_DIM, B_BLOCK_DIM), B_BLOCK_DIM ** -0.5),
        'lru_bx': nrm((N_AB, B_WIDTH), 0.02),
    }
    a_base = uni((N_AB, B_WIDTH), 0.9, 0.999) ** (1.0 / LRU_C)
    inp['lru_lambda'] = jnp.log(a_base) - jnp.log1p(-a_base)
    inp['w_in_c'] = nrm((N_C, D, P_C), D ** -0.5)
    inp['w_out_c'] = nrm((N_C, C_VAL_DIM, D), C_VAL_DIM ** -0.5)
    inp['gdn_conv_w'] = nrm((N_C, C_CONV, C_QKV_DIM), C_CONV ** -0.5)
    inp['gdn_a_log'] = jnp.log(uni((N_C, C_V_HEADS), 1.0, 16.0))
    dt = jnp.exp(uni((N_C, C_V_HEADS), math.log(1e-3), math.log(1e-1)))
    inp['gdn_dt_bias'] = dt + jnp.log(-jnp.expm1(-dt))
    inp['gdn_norm_w'] = 1.0 + nrm((N_C, C_HEAD_V), 0.05)
    inp['ffn_w_gate'] = nrm((DEPTH, D, D_FF), D ** -0.5)
    inp['ffn_w_up'] = nrm((DEPTH, D, D_FF), D ** -0.5)
    inp['ffn_conv_w'] = nrm((DEPTH, FFN_CONV, D_FF), FFN_CONV ** -0.5)
    inp['ffn_conv_b'] = nrm((DEPTH, D_FF), 0.02)
    inp['ffn_w_down'] = nrm((DEPTH, D_FF, D), D_FF ** -0.5)
    return inp


def reference(x_prompt, x_sample, c_prompt, c_sample, state_rwkv_wkv, state_rwkv_shift, state_lru_h,
              state_lru_conv, state_gdn, state_gdn_conv, state_ffn_conv, w_mod, b_mod, norm_mix, norm_ffn,
              norm_out, w_in_ab, w_out_ab, rwkv_mu, rwkv_w0, rwkv_w2, rwkv_a0, rwkv_a2, rwkv_g2, rwkv_k_k,
              rwkv_k_a, rwkv_r_k, rwkv_gn_w, rwkv_gn_b, lru_conv_w, lru_conv_b, lru_wa, lru_ba, lru_wx, lru_bx,
              lru_lambda, w_in_c, w_out_c, gdn_conv_w, gdn_a_log, gdn_dt_bias, gdn_norm_w, ffn_w_gate,
              ffn_w_up, ffn_conv_w, ffn_conv_b, ffn_w_down):
    prm = dict(w_mod=w_mod, b_mod=b_mod, norm_mix=norm_mix, norm_ffn=norm_ffn, norm_out=norm_out,
               w_in_ab=w_in_ab, w_out_ab=w_out_ab, rwkv_mu=rwkv_mu, rwkv_w0=rwkv_w0, rwkv_w2=rwkv_w2,
               rwkv_a0=rwkv_a0, rwkv_a2=rwkv_a2, rwkv_g2=rwkv_g2, rwkv_k_k=rwkv_k_k, rwkv_k_a=rwkv_k_a,
               rwkv_r_k=rwkv_r_k, rwkv_gn_w=rwkv_gn_w, rwkv_gn_b=rwkv_gn_b, lru_conv_w=lru_conv_w,
               lru_conv_b=lru_conv_b, lru_wa=lru_wa, lru_ba=lru_ba, lru_wx=lru_wx, lru_bx=lru_bx,
               lru_lambda=lru_lambda, w_in_c=w_in_c, w_out_c=w_out_c, gdn_conv_w=gdn_conv_w,
               gdn_a_log=gdn_a_log, gdn_dt_bias=gdn_dt_bias, gdn_norm_w=gdn_norm_w, ffn_w_gate=ffn_w_gate,
               ffn_w_up=ffn_w_up, ffn_conv_w=ffn_conv_w, ffn_conv_b=ffn_conv_b, ffn_w_down=ffn_w_down)
    bp = x_prompt.shape[0]
    zero = lambda st: jnp.zeros((st.shape[0], bp) + st.shape[2:], st.dtype)
    (y_prompt, p_wkv, p_shift, p_lru_h, p_lru_conv, p_gdn, p_gdn_conv, p_ffn_conv) = _trunk(
        x_prompt, c_prompt, zero(state_rwkv_wkv), zero(state_rwkv_shift), zero(state_lru_h),
        zero(state_lru_conv), zero(state_gdn), zero(state_gdn_conv), zero(state_ffn_conv), prm)
    (y_sample, s_wkv, s_shift, s_lru_h, s_lru_conv, s_gdn, s_gdn_conv, s_ffn_conv) = _trunk(
        x_sample, c_sample, state_rwkv_wkv, state_rwkv_shift, state_lru_h, state_lru_conv, state_gdn,
        state_gdn_conv, state_ffn_conv, prm)
    return (y_prompt, y_sample, p_wkv, p_shift, p_lru_h, p_lru_conv, p_gdn, p_gdn_conv, p_ffn_conv,
            s_wkv, s_shift, s_lru_h, s_lru_conv, s_gdn, s_gdn_conv, s_ffn_conv)
```

```python
import functools

import jax
import jax.numpy as jnp
from jax import lax
from jax.experimental import pallas as pl
from jax.experimental.pallas import tpu as pltpu

F32 = jnp.float32
BF16 = jnp.bfloat16

D_MODEL = 2048
DEPTH = 4
A_HEAD_DIM = 64
A_WIDTH = D_MODEL // 2
A_HEADS = A_WIDTH // A_HEAD_DIM
LORA_W = 64
LORA_A = 64
LORA_G = 160
LORA_ALL = LORA_W + LORA_A + LORA_G
P_A_MAIN = 3 * A_WIDTH
P_A = P_A_MAIN + LORA_ALL
A_GN_EPS = 64e-5
B_WIDTH = D_MODEL - A_WIDTH
B_BLOCK_DIM = 64
LRU_C = 8.0
C_HEAD = 128
C_K_HEADS = D_MODEL // C_HEAD
C_V_HEADS = 2 * C_K_HEADS
C_KEY_DIM = C_K_HEADS * C_HEAD
C_VAL_DIM = C_V_HEADS * C_HEAD
C_QKV_DIM = 2 * C_KEY_DIM + C_VAL_DIM
D_FF = 5632
NORM_EPS = 1e-6

LANES = 128
SUBLANES = 8
HIST = SUBLANES
CHUNK = 64
VMEM_LIMIT = 56 * 1024 * 1024
MM_VMEM_BUDGET = 40 * 1024 * 1024

_NN = (((1,), (0,)), ((), ()))
_NT = (((1,), (1,)), ((), ()))
_TN = (((0,), (0,)), ((), ()))


def _cparams(n_axes):
    return pltpu.CompilerParams(dimension_semantics=("arbitrary",) * n_axes,
                                vmem_limit_bytes=VMEM_LIMIT)


def _bdot(a, b, dims=_NN):
    return lax.dot_general(a.astype(BF16), b.astype(BF16), dims, preferred_element_type=F32)


def _split2(a):
    hi = a.astype(BF16)
    lo = (a - hi.astype(F32)).astype(BF16)
    return hi, lo


def _dot3(a, b, dims=_NN):
    ah, al = _split2(a)
    bh, bl = _split2(b)
    d = lambda x, y: lax.dot_general(x, y, dims, preferred_element_type=F32)
    return d(ah, bh) + (d(ah, bl) + d(al, bh))


def _split3(a):
    h0 = a.astype(BF16)
    r1 = a - h0.astype(F32)
    h1 = r1.astype(BF16)
    h2 = (r1 - h1.astype(F32)).astype(BF16)
    return h0, h1, h2


def _sel_l(mask01, x, dims=_NN):
    m = mask01.astype(BF16)
    d = lambda y: lax.dot_general(m, y, dims, preferred_element_type=F32)
    h0, h1, h2 = _split3(x)
    return d(h0) + (d(h1) + d(h2))


def _sel_r(x, mask01):
    m = mask01.astype(BF16)
    d = lambda y: lax.dot_general(y, m, _NN, preferred_element_type=F32)
    h0, h1, h2 = _split3(x)
    return d(h0) + (d(h1) + d(h2))


def _iota2(shape, axis):
    return lax.broadcasted_iota(jnp.int32, shape, axis)


def _tri(n, strict):
    i = _iota2((n, n), 0)
    j = _iota2((n, n), 1)
    return (i > j) if strict else (i >= j)


def _unit_lower_inverse(low):
    n = low.shape[0]
    eye = (_iota2((n, n), 0) == _iota2((n, n), 1)).astype(F32)
    m = -low
    inv = eye + m
    p = m
    span = 2
    while span < n:
        p = _dot3(p, p)
        inv = inv + _dot3(inv, p)
        span *= 2
    return inv


def _neg_expm1(x):
    return -jnp.tanh(0.5 * x) * (jnp.exp(x) + 1.0)


def _segment_sum(x, seg):
    i = _iota2((LANES, LANES), 0) // seg
    j = _iota2((LANES, LANES), 1) // seg
    same = (i == j)
    parts = [_sel_r(x[:, g * LANES:(g + 1) * LANES], same) for g in range(x.shape[1] // LANES)]
    return parts[0] if len(parts) == 1 else jnp.concatenate(parts, axis=1)


def _rwkv_rows(p_main, q_main, p_lora, q_lora, mu_main, mu_lora, w0, w2, a0, a2, g2, k_k, k_a):
    pm = p_main + (q_main - p_main) * mu_main
    pl_ = p_lora + (q_lora - p_lora) * mu_lora
    r = pm[:, :A_WIDTH]
    k = pm[:, A_WIDTH:2 * A_WIDTH]
    v = pm[:, 2 * A_WIDTH:]
    xw = pl_[:, :LORA_W]
    xa = pl_[:, LORA_W:LORA_W + LORA_A]
    xg = pl_[:, LORA_W + LORA_A:LORA_ALL]
    w_ll = -jax.nn.softplus(-(w0 + _bdot(jnp.tanh(xw), w2))) - 0.5
    lw = -jnp.exp(w_ll)
    a = jax.nn.sigmoid(a0 + _bdot(xa, a2))
    g = _bdot(jax.nn.sigmoid(xg), g2)
    kkn = k * k_k
    kf = k * (1.0 + (a - 1.0) * k_a)
    return r, kf, v, kkn, a, lw, g


def _rwkv_head_out(y, r, kf, v, g, r_k, gn_w, gn_b):
    mean = jnp.mean(y, axis=-1, keepdims=True)
    var = jnp.mean(jnp.square(y - mean), axis=-1, keepdims=True)
    yn = (y - mean) * lax.rsqrt(var + A_GN_EPS) * gn_w + gn_b
    bonus = jnp.sum(r * kf * r_k, axis=-1, keepdims=True) * v
    return (yn + bonus) * g


def _lru_rows(gate, xc, wa_bd, wx_bd, ba, bx, lam):
    ng = xc.shape[1] // LANES
    ra = jnp.concatenate([_bdot(xc[:, g * LANES:(g + 1) * LANES], wa_bd[g]) for g in range(ng)], axis=1)
    rx = jnp.concatenate([_bdot(xc[:, g * LANES:(g + 1) * LANES], wx_bd[g]) for g in range(ng)], axis=1)
    r_gate = jax.nn.sigmoid(ra + ba)
    i_gate = jax.nn.sigmoid(rx + bx)
    log_a = -LRU_C * r_gate * jax.nn.softplus(-lam)
    a = jnp.exp(log_a)
    b = jnp.sqrt(_neg_expm1(2.0 * log_a)) * (i_gate * xc)
    return a, b, jax.nn.gelu(gate)


def _gdn_act(y, is_qk, q_scale):
    y = jax.nn.silu(y)
    outs = []
    for h in range(y.shape[1] // C_HEAD):
        yh = y[:, h * C_HEAD:(h + 1) * C_HEAD]
        nrm = yh * lax.rsqrt(jnp.sum(yh * yh, axis=-1, keepdims=True) + 1e-6) * q_scale
        outs.append(jnp.where(is_qk, nrm, yh))
    return jnp.concatenate(outs, axis=1)


def _gdn_head_out(o, z, norm_w):
    o = o * lax.rsqrt(jnp.mean(o * o, axis=-1, keepdims=True) + NORM_EPS) * norm_w
    return o * jax.nn.silu(z)


def _mm_body(*refs, n_x, has_res):
    x_refs = refs[:n_x]
    w_refs = refs[n_x:2 * n_x]
    k = 2 * n_x
    if has_res:
        res_ref, gate_ref = refs[k], refs[k + 1]
        k += 2
    o_ref = refs[k]
    wbf_refs = refs[k + 1:k + 1 + n_x]

    @pl.when((pl.program_id(1) == 0) & (pl.program_id(2) == 0))
    def _():
        for w_ref, wbf_ref in zip(w_refs, wbf_refs):
            wbf_ref[...] = w_ref[...].astype(BF16)

    acc = None
    for x_ref, wbf_ref in zip(x_refs, wbf_refs):
        d = jnp.dot(x_ref[0], wbf_ref[...], preferred_element_type=F32)
        acc = d if acc is None else acc + d
    if has_res:
        acc = res_ref[0] + gate_ref[0] * acc
    o_ref[0] = acc.astype(o_ref.dtype)


def _pick_tn(n_out, k_sum, tm, n_out_bufs):
    for tn in (512, 256, 128):
        if n_out % tn:
            continue
        need = 2 * tm * k_sum * 2 + 2 * k_sum * tn * 4 + k_sum * tn * 2 + 2 * n_out_bufs * tm * tn * 4
        if need <= MM_VMEM_BUDGET:
            return tn
    raise ValueError("no projection tile fits VMEM")


def _mm(xs, ws, *, n_out, col0=0, tm, tn=None, out_dtype=F32, res=None, gate=None, gate_blk=0):
    B, T, _ = xs[0].shape
    ks = [x.shape[-1] for x in xs]
    has_res = res is not None
    if tn is None:
        tn = _pick_tn(n_out, sum(ks), tm, 2 if has_res else 1)
    assert n_out % tn == 0 and col0 % tn == 0 and T % tm == 0
    cb0 = col0 // tn
    in_specs = [pl.BlockSpec((1, tm, k), lambda j, b, t: (b, t, 0)) for k in ks]
    in_specs += [pl.BlockSpec((k, tn), functools.partial(lambda j, b, t, rb: (rb, cb0 + j), rb=rb))
                 for k, (_, rb) in zip(ks, ws)]
    args = list(xs) + [w for w, _ in ws]
    if has_res:
        in_specs.append(pl.BlockSpec((1, tm, tn), lambda j, b, t: (b, t, j)))
        gb0 = gate_blk * (n_out // tn)
        if gate.shape[1] == 1:
            in_specs.append(pl.BlockSpec((1, 1, tn), lambda j, b, t: (b, 0, gb0 + j)))
        else:
            in_specs.append(pl.BlockSpec((1, tm, tn), lambda j, b, t: (b, t, gb0 + j)))
        args += [res, gate]
    return pl.pallas_call(
        functools.partial(_mm_body, n_x=len(xs), has_res=has_res),
        grid=(n_out // tn, B, T // tm),
        in_specs=in_specs,
        out_specs=pl.BlockSpec((1, tm, tn), lambda j, b, t: (b, t, j)),
        out_shape=jax.ShapeDtypeStruct((B, T, n_out), out_dtype),
        scratch_shapes=[pltpu.VMEM((k, tn), BF16) for k in ks],
        compiler_params=_cparams(3),
    )(*args)


def _mod_body(cp_ref, cs_ref, w_ref, b_ref, op_ref, os_ref):
    w = w_ref[0].astype(BF16)
    b = b_ref[0]
    op_ref[0] = _bdot(jax.nn.silu(cp_ref[...]), w) + b
    os_ref[0] = _bdot(jax.nn.silu(cs_ref[...]), w) + b


def _modulation(c_p, c_s, w_mod, b_mod):
    depth, d, n = w_mod.shape
    tn = 1024
    bp, bs = c_p.shape[0], c_s.shape[0]
    return pl.pallas_call(
        _mod_body,
        grid=(depth, n // tn),
        in_specs=[pl.BlockSpec((bp, d), lambda l, j: (0, 0)),
                  pl.BlockSpec((bs, d), lambda l, j: (0, 0)),
                  pl.BlockSpec((1, d, tn), lambda l, j: (l, 0, j)),
                  pl.BlockSpec((1, 1, tn), lambda l, j: (l, 0, j))],
        out_specs=[pl.BlockSpec((1, bp, tn), lambda l, j: (l, 0, j)),
                   pl.BlockSpec((1, bs, tn), lambda l, j: (l, 0, j))],
        out_shape=[jax.ShapeDtypeStruct((depth, bp, n), F32),
                   jax.ShapeDtypeStruct((depth, bs, n), F32)],
        compiler_params=_cparams(2),
    )(c_p, c_s, w_mod, b_mod.reshape(depth, 1, n))


def _norm_body(*refs, has_mod):
    if has_mod:
        x_ref, g_ref, sc_ref, sh_ref, o_ref = refs
    else:
        x_ref, g_ref, o_ref = refs
    x = x_ref[0]
    y = x * lax.rsqrt(jnp.mean(x * x, axis=-1, keepdims=True) + NORM_EPS) * g_ref[...]
    if has_mod:
        y = y * (1.0 + sc_ref[0]) + sh_ref[0]
    o_ref[0] = y.astype(o_ref.dtype)


def _norm(x, gain, mod=None, sc_blk=0, sh_blk=0, *, tt, out_dtype=BF16):
    B, T, d = x.shape
    in_specs = [pl.BlockSpec((1, tt, d), lambda b, t: (b, t, 0)),
                pl.BlockSpec((1, d), lambda b, t: (0, 0))]
    args = [x, gain.reshape(1, d)]
    if mod is not None:
        for blk in (sc_blk, sh_blk):
            if mod.shape[1] == 1:
                in_specs.append(pl.BlockSpec((1, 1, d), functools.partial(lambda b, t, blk: (b, 0, blk), blk=blk)))
            else:
                in_specs.append(pl.BlockSpec((1, tt, d), functools.partial(lambda b, t, blk: (b, t, blk), blk=blk)))
        args += [mod, mod]
    return pl.pallas_call(
        functools.partial(_norm_body, has_mod=mod is not None),
        grid=(B, T // tt),
        in_specs=in_specs,
        out_specs=pl.BlockSpec((1, tt, d), lambda b, t: (b, t, 0)),
        out_shape=jax.ShapeDtypeStruct((B, T, d), out_dtype),
        compiler_params=_cparams(2),
    )(*args)


def _ffn_up_prompt_body(h_ref, wg_ref, wu_ref, cw_ref, cb_ref, act_ref, st_ref, wg_bf, wu_bf, buf):
    t = pl.program_id(2)
    tm = h_ref.shape[1]

    @pl.when((pl.program_id(1) == 0) & (t == 0))
    def _():
        wg_bf[...] = wg_ref[0].astype(BF16)
        wu_bf[...] = wu_ref[0].astype(BF16)

    @pl.when(t == 0)
    def _():
        buf[0:HIST, :] = jnp.zeros((HIST, buf.shape[1]), F32)

    h = h_ref[0]
    buf[HIST:, :] = jnp.dot(h, wg_bf[...], preferred_element_type=F32)
    cw = cw_ref[0]
    u = (buf[HIST - 2:HIST - 2 + tm, :] * cw[0:1] + buf[HIST - 1:HIST - 1 + tm, :] * cw[1:2]
         + buf[HIST:, :] * cw[2:3] + cb_ref[0])
    up = jnp.dot(h, wu_bf[...], preferred_element_type=F32)
    act_ref[0] = (jax.nn.gelu(u) * up).astype(act_ref.dtype)
    st_ref[0] = buf[HIST + tm - 2:HIST + tm, :]
    buf[0:HIST, :] = buf[tm:tm + HIST, :]


def _ffn_up_prompt(h, w_gate, w_up, conv_w, conv_b, layer, *, tm, tn):
    B, T, d = h.shape
    n = w_gate.shape[-1]
    return pl.pallas_call(
        _ffn_up_prompt_body,
        grid=(n // tn, B, T // tm),
        in_specs=[pl.BlockSpec((1, tm, d), lambda j, b, t: (b, t, 0)),
                  pl.BlockSpec((1, d, tn), lambda j, b, t: (layer, 0, j)),
                  pl.BlockSpec((1, d, tn), lambda j, b, t: (layer, 0, j)),
                  pl.BlockSpec((1, 3, tn), lambda j, b, t: (layer, 0, j)),
                  pl.BlockSpec((1, 1, tn), lambda j, b, t: (layer, 0, j))],
        out_specs=[pl.BlockSpec((1, tm, tn), lambda j, b, t: (b, t, j)),
                   pl.BlockSpec((1, 2, tn), lambda j, b, t: (b, 0, j))],
        out_shape=[jax.ShapeDtypeStruct((B, T, n), BF16),
                   jax.ShapeDtypeStruct((B, 2, n), F32)],
        scratch_shapes=[pltpu.VMEM((d, tn), BF16), pltpu.VMEM((d, tn), BF16),
                        pltpu.VMEM((HIST + tm, tn), F32)],
        compiler_params=_cparams(3),
    )(h, w_gate, w_up, conv_w, conv_b.reshape(conv_b.shape[0], 1, n))


def _ffn_up_sample_body(h_ref, wg_ref, wu_ref, cw_ref, cb_ref, st_ref, act_ref, nst_ref):
    h = h_ref[...]
    pre = _bdot(h, wg_ref[0])
    cw = cw_ref[0]
    u = st_ref[:, 0, :] * cw[0:1] + st_ref[:, 1, :] * cw[1:2] + pre * cw[2:3] + cb_ref[0]
    up = _bdot(h, wu_ref[0])
    act_ref[...] = (jax.nn.gelu(u) * up).astype(act_ref.dtype)
    nst_ref[:, 0, :] = st_ref[:, 1, :]
    nst_ref[:, 1, :] = pre


def _ffn_up_sample(h, w_gate, w_up, conv_w, conv_b, state, layer, *, tn):
    bs, d = h.shape
    n = w_gate.shape[-1]
    return pl.pallas_call(
        _ffn_up_sample_body,
        grid=(n // tn,),
        in_specs=[pl.BlockSpec((bs, d), lambda j: (0, 0)),
                  pl.BlockSpec((1, d, tn), lambda j: (layer, 0, j)),
                  pl.BlockSpec((1, d, tn), lambda j: (layer, 0, j)),
                  pl.BlockSpec((1, 3, tn), lambda j: (layer, 0, j)),
                  pl.BlockSpec((1, 1, tn), lambda j: (layer, 0, j)),
                  pl.BlockSpec((bs, 2, tn), lambda j: (0, 0, j))],
        out_specs=[pl.BlockSpec((bs, tn), lambda j: (0, j)),
                   pl.BlockSpec((bs, 2, tn), lambda j: (0, 0, j))],
        out_shape=[jax.ShapeDtypeStruct((bs, n), BF16),
                   jax.ShapeDtypeStruct((bs, 2, n), F32)],
        compiler_params=_cparams(1),
    )(h, w_gate, w_up, conv_w, conv_b.reshape(conv_b.shape[0], 1, n), state)


def _rwkv_prep_prompt_body(pm_ref, pl_ref, mum_ref, mul_ref, w0_ref, w2_ref, a0_ref, a2_ref, g2_ref,
                           kk_ref, ka_ref, r_o, kf_o, v_o, kkn_o, a_o, lw_o, g_o, bufm, bufl):
    t = pl.program_id(1)
    tt = pm_ref.shape[1]

    @pl.when(t == 0)
    def _():
        bufm[0:HIST, :] = jnp.zeros((HIST, bufm.shape[1]), F32)
        bufl[0:HIST, :] = jnp.zeros((HIST, bufl.shape[1]), F32)

    p_main = pm_ref[0]
    p_lora = pl_ref[0][:, :LORA_ALL]
    bufm[HIST:, :] = p_main
    bufl[HIST:, :] = p_lora
    q_main = bufm[HIST - 1:HIST - 1 + tt, :]
    q_lora = bufl[HIST - 1:HIST - 1 + tt, :]
    outs = _rwkv_rows(p_main, q_main, p_lora, q_lora, mum_ref[...], mul_ref[...], w0_ref[...], w2_ref[0],
                      a0_ref[...], a2_ref[0], g2_ref[0], kk_ref[...], ka_ref[...])
    for o_ref, val in zip((r_o, kf_o, v_o, kkn_o, a_o, lw_o, g_o), outs):
        o_ref[0] = val
    bufm[0:HIST, :] = bufm[tt:tt + HIST, :]
    bufl[0:HIST, :] = bufl[tt:tt + HIST, :]


def _row(v):
    return v.reshape(1, -1)


def _rwkv_prep_prompt(p_main, p_lora, prm, i, *, tt):
    B, T, _ = p_main.shape
    lw_pad = p_lora.shape[-1]
    mu = prm['rwkv_mu'][i]
    full = lambda shape: pl.BlockSpec(shape, lambda b, t: (0,) * len(shape))
    lay3 = lambda shape: pl.BlockSpec((1,) + shape, lambda b, t: (i, 0, 0))
    out_spec = pl.BlockSpec((1, tt, A_WIDTH), lambda b, t: (b, t, 0))
    return pl.pallas_call(
        _rwkv_prep_prompt_body,
        grid=(B, T // tt),
        in_specs=[pl.BlockSpec((1, tt, P_A_MAIN), lambda b, t: (b, t, 0)),
                  pl.BlockSpec((1, tt, lw_pad), lambda b, t: (b, t, 0)),
                  full((1, P_A_MAIN)), full((1, LORA_ALL)), full((1, A_WIDTH)),
                  lay3((LORA_W, A_WIDTH)), full((1, A_WIDTH)), lay3((LORA_A, A_WIDTH)),
                  lay3((LORA_G, A_WIDTH)), full((1, A_WIDTH)), full((1, A_WIDTH))],
        out_specs=[out_spec] * 7,
        out_shape=[jax.ShapeDtypeStruct((B, T, A_WIDTH), F32)] * 7,
        scratch_shapes=[pltpu.VMEM((HIST + tt, P_A_MAIN), F32), pltpu.VMEM((HIST + tt, LORA_ALL), F32)],
        compiler_params=_cparams(2),
    )(p_main, p_lora, _row(mu[:P_A_MAIN]), _row(mu[P_A_MAIN:]), _row(prm['rwkv_w0'][i]), prm['rwkv_w2'],
      _row(prm['rwkv_a0'][i]), prm['rwkv_a2'], prm['rwkv_g2'], _row(prm['rwkv_k_k'][i]), _row(prm['rwkv_k_a'][i]))


def _wkv_chunk(S, r, kf, v, kkn, a, lw):
    C = r.shape[0]
    kk = kkn * lax.rsqrt(jnp.sum(kkn * kkn, axis=-1, keepdims=True) + 1e-12)
    bb = kk * a
    incl = _tri(C, False)
    strict = _tri(C, True)
    lc = _sel_l(incl, lw)
    p = jnp.exp(lc)
    pinv = jnp.exp(-lc)
    kt = kf * pinv
    bt = bb * pinv
    at = kk * jnp.exp(lc - lw)
    rt = r * p
    lhs = jnp.concatenate([at, rt], axis=0)
    rhs = jnp.concatenate([bt, kt], axis=0)
    gram = _dot3(lhs, rhs, _NT)
    a_ab = jnp.where(strict, gram[:C, :C], 0.0)
    a_ak = jnp.where(strict, gram[:C, C:], 0.0)
    a_rb = jnp.where(incl, gram[C:, :C], 0.0)
    a_rk = jnp.where(incl, gram[C:, C:], 0.0)
    tinv = _unit_lower_inverse(a_ab)
    xs = _dot3(lhs, S, _NT)
    z = _dot3(tinv, xs[:C] + _dot3(a_ak, v))
    y = xs[C:] + _dot3(a_rk, v) - _dot3(a_rb, z)
    plast = p[C - 1:C, :]
    s_new = S * plast + _dot3(v, kt * plast, _TN) - _dot3(z, bt * plast, _TN)
    return y, s_new


def _wkv_prompt_body(r_ref, kf_ref, v_ref, kkn_ref, a_ref, lw_ref, g_ref, rk_ref, gw_ref, gb_ref,
                     ya_ref, st_ref, s_scr):
    t = pl.program_id(2)
    tt = r_ref.shape[1]
    n = A_HEAD_DIM

    @pl.when(t == 0)
    def _():
        s_scr[...] = jnp.zeros(s_scr.shape, F32)

    for c in range(tt // CHUNK):
        rows = slice(c * CHUNK, (c + 1) * CHUNK)
        ys = []
        for hh in range(LANES // n):
            cols = slice(hh * n, (hh + 1) * n)
            ld = lambda ref: ref[0, rows, cols]
            r, kf, v = ld(r_ref), ld(kf_ref), ld(v_ref)
            y, s_new = _wkv_chunk(s_scr[hh], r, kf, v, ld(kkn_ref), ld(a_ref), ld(lw_ref))
            s_scr[hh] = s_new
            ys.append(_rwkv_head_out(y, r, kf, v, ld(g_ref), rk_ref[:, cols], gw_ref[:, cols], gb_ref[:, cols]))
        ya_ref[0, rows, :] = jnp.concatenate(ys, axis=1).astype(ya_ref.dtype)

    @pl.when(t == pl.num_programs(2) - 1)
    def _():
        st_ref[0] = s_scr[...]


def _wkv_prompt(vals, prm, i, *, tt):
    B, T, _ = vals[0].shape
    hp = A_WIDTH // LANES
    per_pair = LANES // A_HEAD_DIM
    seq_spec = pl.BlockSpec((1, tt, LANES), lambda b, h, t: (b, t, h))
    par_spec = pl.BlockSpec((1, LANES), lambda b, h, t: (0, h))
    return pl.pallas_call(
        _wkv_prompt_body,
        grid=(B, hp, T // tt),
        in_specs=[seq_spec] * 7 + [par_spec] * 3,
        out_specs=[seq_spec,
                   pl.BlockSpec((1, per_pair, A_HEAD_DIM, A_HEAD_DIM), lambda b, h, t: (b, h, 0, 0))],
        out_shape=[jax.ShapeDtypeStruct((B, T, A_WIDTH), BF16),
                   jax.ShapeDtypeStruct((B, A_HEADS, A_HEAD_DIM, A_HEAD_DIM), F32)],
        scratch_shapes=[pltpu.VMEM((per_pair, A_HEAD_DIM, A_HEAD_DIM), F32)],
        compiler_params=_cparams(3),
    )(*vals, _row(prm['rwkv_r_k'][i]), _row(prm['rwkv_gn_w'][i]), _row(prm['rwkv_gn_b'][i]))


def _lru_prompt_body(gate_ref, xb_ref, cw_ref, cb_ref, wa_ref, wx_ref, ba_ref, bx_ref, lam_ref,
                     y_ref, h_ref, cst_ref, buf, h_scr):
    t = pl.program_id(1)
    tt = xb_ref.shape[1]
    w = xb_ref.shape[2]

    @pl.when(t == 0)
    def _():
        buf[0:HIST, :] = jnp.zeros((HIST, w), F32)
        h_scr[...] = jnp.zeros(h_scr.shape, F32)

    buf[HIST:, :] = xb_ref[0]
    cw = cw_ref[0]
    xc = cb_ref[...] + buf[HIST:, :] * cw[3:4]
    for j in range(3):
        xc = xc + buf[HIST - 3 + j:HIST - 3 + j + tt, :] * cw[j:j + 1]
    a, b, gact = _lru_rows(gate_ref[0], xc, wa_ref, wx_ref, ba_ref[...], bx_ref[...], lam_ref[...])
    row = _iota2((tt, w), 0)
    d = 1
    while d < tt:
        a_sh = jnp.where(row >= d, pltpu.roll(a, d, axis=0), 1.0)
        b_sh = jnp.where(row >= d, pltpu.roll(b, d, axis=0), 0.0)
        b = a * b_sh + b
        a = a * a_sh
        d *= 2
    h = a * h_scr[0:1, :] + b
    y_ref[0] = (h * gact).astype(y_ref.dtype)
    h_last = h[tt - 1:tt, :]
    h_scr[0:1, :] = h_last
    h_ref[0] = h_last
    cst_ref[0] = buf[HIST + tt - 3:HIST + tt, :]
    buf[0:HIST, :] = buf[tt:tt + HIST, :]


def _block_diag_pairs(w):
    nb, n, _ = w.shape
    w = w.reshape(nb // 2, 2, n, n)
    z = jnp.zeros((nb // 2, n, n), w.dtype)
    top = jnp.concatenate([w[:, 0], z], axis=2)
    bot = jnp.concatenate([z, w[:, 1]], axis=2)
    return jnp.concatenate([top, bot], axis=1)


def _lru_prompt(pb, prm, i, *, tt):
    B, T, _ = pb.shape
    w = B_WIDTH
    ng = w // LANES
    full = lambda shape: pl.BlockSpec(shape, lambda b, t: (0,) * len(shape))
    return pl.pallas_call(
        _lru_prompt_body,
        grid=(B, T // tt),
        in_specs=[pl.BlockSpec((1, tt, w), lambda b, t: (b, t, 0)),
                  pl.BlockSpec((1, tt, w), lambda b, t: (b, t, 1)),
                  pl.BlockSpec((1, 4, w), lambda b, t: (i, 0, 0)),
                  full((1, w)), full((ng, LANES, LANES)), full((ng, LANES, LANES)),
                  full((1, w)), full((1, w)), full((1, w))],
        out_specs=[pl.BlockSpec((1, tt, w), lambda b, t: (b, t, 0)),
                   pl.BlockSpec((1, 1, w), lambda b, t: (b, 0, 0)),
                   pl.BlockSpec((1, 3, w), lambda b, t: (b, 0, 0))],
        out_shape=[jax.ShapeDtypeStruct((B, T, w), BF16),
                   jax.ShapeDtypeStruct((B, 1, w), F32),
                   jax.ShapeDtypeStruct((B, 3, w), F32)],
        scratch_shapes=[pltpu.VMEM((HIST + tt, w), F32), pltpu.VMEM((SUBLANES, w), F32)],
        compiler_params=_cparams(2),
    )(pb, pb, prm['lru_conv_w'], _row(prm['lru_conv_b'][i]), _block_diag_pairs(prm['lru_wa'][i]),
      _block_diag_pairs(prm['lru_wx'][i]), _row(prm['lru_ba'][i]), _row(prm['lru_bx'][i]),
      _row(prm['lru_lambda'][i]))


def _gdn_prep_prompt_body(x_ref, cw_ref, o_ref, st_ref, buf):
    j = pl.program_id(1)
    t = pl.program_id(2)
    tt = x_ref.shape[1]

    @pl.when(t == 0)
    def _():
        buf[0:HIST, :] = jnp.zeros((HIST, buf.shape[1]), F32)

    buf[HIST:, :] = x_ref[0]
    cw = cw_ref[0]
    y = buf[HIST:, :] * cw[3:4]
    for jj in range(3):
        y = y + buf[HIST - 3 + jj:HIST - 3 + jj + tt, :] * cw[jj:jj + 1]
    n_qk_blocks = 2 * C_KEY_DIM // x_ref.shape[2]
    q_scale = jnp.where(j < n_qk_blocks // 2, C_HEAD ** -0.5, 1.0).astype(F32)
    o_ref[0] = _gdn_act(y, j < n_qk_blocks, q_scale)
    st_ref[0] = buf[HIST + tt - 3:HIST + tt, :]
    buf[0:HIST, :] = buf[tt:tt + HIST, :]


def _gdn_prep_prompt(qkvz, conv_w, i, *, tt, tc):
    B, T, _ = qkvz.shape
    return pl.pallas_call(
        _gdn_prep_prompt_body,
        grid=(B, C_QKV_DIM // tc, T // tt),
        in_specs=[pl.BlockSpec((1, tt, tc), lambda b, j, t: (b, t, j)),
                  pl.BlockSpec((1, 4, tc), lambda b, j, t: (i, 0, j))],
        out_specs=[pl.BlockSpec((1, tt, tc), lambda b, j, t: (b, t, j)),
                   pl.BlockSpec((1, 3, tc), lambda b, j, t: (b, 0, j))],
        out_shape=[jax.ShapeDtypeStruct((B, T, C_QKV_DIM), F32),
                   jax.ShapeDtypeStruct((B, 3, C_QKV_DIM), F32)],
        scratch_shapes=[pltpu.VMEM((HIST + tt, tc), F32)],
        compiler_params=_cparams(3),
    )(qkvz, conv_w)


def _gdn_gates(ba, a_log, dt_bias):
    hv = C_V_HEADS
    beta = jax.nn.sigmoid(ba[:, :hv])
    g = -jnp.exp(a_log) * jax.nn.softplus(ba[:, hv:2 * hv] + dt_bias)
    return beta, g


def _lane_bcast_col(x, col):
    n = x.shape[1]
    onehot = _iota2((n, LANES), 0) == col
    return _sel_r(x, onehot)


def _gdn_chunk(S, q, k, v, beta, g):
    C = q.shape[0]
    incl = _tri(C, False)
    strict = _tri(C, True)
    gc = _sel_l(incl, g)
    gcc = gc[:, :C]
    diff = gcc - gcc.T
    dec_incl = jnp.exp(jnp.where(incl, diff, -jnp.inf))
    dec_strict = jnp.where(strict, dec_incl, 0.0)
    kb = k * beta
    lhs = jnp.concatenate([kb, q], axis=0)
    gram = _dot3(lhs, k, _NT)
    low = gram[:C] * dec_strict
    qk = gram[C:] * dec_incl
    tinv = _unit_lower_inverse(low)
    eg = jnp.exp(gc)
    sol = _dot3(tinv, jnp.concatenate([v * beta, kb * eg], axis=1))
    u, w = sol[:, :C_HEAD], sol[:, C_HEAD:]
    ws = _dot3(jnp.concatenate([w, q * eg], axis=0), S)
    v_new = u - ws[:C]
    o = ws[C:] + _dot3(qk, v_new)
    g_last = gc[C - 1:C, :]
    k_tail = k * jnp.exp(g_last - gc)
    s_new = S * jnp.exp(g_last) + _dot3(k_tail, v_new, _TN)
    return o, s_new


def _gdn_prompt_body(q_ref, k_ref, v_ref, z_ref, ba_ref, alog_ref, dtb_ref, nw_ref, o_ref, st_ref, s_scr):
    hk = pl.program_id(1)
    t = pl.program_id(2)
    tt = q_ref.shape[1]
    rep = C_V_HEADS // C_K_HEADS

    @pl.when(t == 0)
    def _():
        s_scr[...] = jnp.zeros(s_scr.shape, F32)

    for c in range(tt // CHUNK):
        rows = slice(c * CHUNK, (c + 1) * CHUNK)
        q = q_ref[0, rows, :]
        k = k_ref[0, rows, :]
        beta_all, g_all = _gdn_gates(ba_ref[0, rows, :], alog_ref[...], dtb_ref[...])
        outs = []
        for jv in range(rep):
            cols = slice(jv * C_HEAD, (jv + 1) * C_HEAD)
            hv = hk * rep + jv
            beta = _lane_bcast_col(beta_all, hv)
            g = _lane_bcast_col(g_all, hv)
            o, s_new = _gdn_chunk(s_scr[jv], q, k, v_ref[0, rows, cols], beta, g)
            s_scr[jv] = s_new
            outs.append(_gdn_head_out(o, z_ref[0, rows, cols], nw_ref[...]))
        o_ref[0, rows, :] = jnp.concatenate(outs, axis=1).astype(o_ref.dtype)

    @pl.when(t == pl.num_programs(2) - 1)
    def _():
        st_ref[0] = s_scr[...]


def _gdn_prompt(qkv_act, qkvz, ba, prm, i, *, tt):
    B, T, _ = qkv_act.shape
    rep = C_V_HEADS // C_K_HEADS
    vw = rep * C_HEAD
    v_blk0 = 2 * C_KEY_DIM // vw
    z_blk0 = C_QKV_DIM // vw
    full = lambda shape: pl.BlockSpec(shape, lambda b, h, t: (0,) * len(shape))
    return pl.pallas_call(
        _gdn_prompt_body,
        grid=(B, C_K_HEADS, T // tt),
        in_specs=[pl.BlockSpec((1, tt, C_HEAD), lambda b, h, t: (b, t, h)),
                  pl.BlockSpec((1, tt, C_HEAD), lambda b, h, t: (b, t, C_K_HEADS + h)),
                  pl.BlockSpec((1, tt, vw), lambda b, h, t: (b, t, v_blk0 + h)),
                  pl.BlockSpec((1, tt, vw), lambda b, h, t: (b, t, z_blk0 + h)),
                  pl.BlockSpec((1, tt, 2 * C_V_HEADS), lambda b, h, t: (b, t, 0)),
                  full((1, C_V_HEADS)), full((1, C_V_HEADS)), full((1, C_HEAD))],
        out_specs=[pl.BlockSpec((1, tt, vw), lambda b, h, t: (b, t, h)),
                   pl.BlockSpec((1, rep, C_HEAD, C_HEAD), lambda b, h, t: (b, h, 0, 0))],
        out_shape=[jax.ShapeDtypeStruct((B, T, C_VAL_DIM), BF16),
                   jax.ShapeDtypeStruct((B, C_V_HEADS, C_HEAD, C_HEAD), F32)],
        scratch_shapes=[pltpu.VMEM((rep, C_HEAD, C_HEAD), F32)],
        compiler_params=_cparams(3),
    )(qkv_act, qkv_act, qkv_act, qkvz, ba, _row(prm['gdn_a_log'][i]), _row(prm['gdn_dt_bias'][i]),
      _row(prm['gdn_norm_w'][i]))


def _rwkv_prep_sample_body(pm_ref, pl_ref, qm_ref, ql_ref, mum_ref, mul_ref, w0_ref, w2_ref, a0_ref, a2_ref,
                           g2_ref, kk_ref, ka_ref, r_o, kf_o, v_o, kkn_o, a_o, lw_o, g_o):
    outs = _rwkv_rows(pm_ref[...], qm_ref[...], pl_ref[...][:, :LORA_ALL], ql_ref[...], mum_ref[...],
                      mul_ref[...], w0_ref[...], w2_ref[0], a0_ref[...], a2_ref[0], g2_ref[0],
                      kk_ref[...], ka_ref[...])
    for o_ref, val in zip((r_o, kf_o, v_o, kkn_o, a_o, lw_o, g_o), outs):
        o_ref[...] = val


def _rwkv_prep_sample(p_main, p_lora, shift, prm, i):
    bs = p_main.shape[0]
    mu = prm['rwkv_mu'][i]
    full = lambda shape: pl.BlockSpec(shape, lambda s: (0,) * len(shape))
    lay3 = lambda shape: pl.BlockSpec((1,) + shape, lambda s: (i, 0, 0))
    return pl.pallas_call(
        _rwkv_prep_sample_body,
        grid=(1,),
        in_specs=[full(p_main.shape), full(p_lora.shape), full((bs, P_A_MAIN)), full((bs, LORA_ALL)),
                  full((1, P_A_MAIN)), full((1, LORA_ALL)), full((1, A_WIDTH)),
                  lay3((LORA_W, A_WIDTH)), full((1, A_WIDTH)), lay3((LORA_A, A_WIDTH)),
                  lay3((LORA_G, A_WIDTH)), full((1, A_WIDTH)), full((1, A_WIDTH))],
        out_specs=[full((bs, A_WIDTH))] * 7,
        out_shape=[jax.ShapeDtypeStruct((bs, A_WIDTH), F32)] * 7,
        compiler_params=_cparams(1),
    )(p_main, p_lora, shift[:, :P_A_MAIN], shift[:, P_A_MAIN:], _row(mu[:P_A_MAIN]), _row(mu[P_A_MAIN:]),
      _row(prm['rwkv_w0'][i]), prm['rwkv_w2'], _row(prm['rwkv_a0'][i]), prm['rwkv_a2'], prm['rwkv_g2'],
      _row(prm['rwkv_k_k'][i]), _row(prm['rwkv_k_a'][i]))


def _rows8(x):
    return jnp.broadcast_to(x, (SUBLANES, x.shape[1]))


def _wkv_step_body(s_ref, r_ref, kf_ref, v_ref, kkn_ref, a_ref, lw_ref, g_ref, rk_ref, gw_ref, gb_ref,
                   ns_ref, y_ref):
    nh = s_ref.shape[0]
    rid = _iota2((SUBLANES, A_HEAD_DIM), 0)

    def one(n, carry):
        row = lambda ref: ref[pl.ds(n, 1), :]
        S = s_ref[n]
        r, kf, v, kkn, a = row(r_ref), row(kf_ref), row(v_ref), row(kkn_ref), row(a_ref)
        kk = kkn * lax.rsqrt(jnp.sum(kkn * kkn, axis=-1, keepdims=True) + 1e-12)
        u = _dot3(_rows8(kk), S, _NT)
        left = jnp.where(rid == 0, -u, jnp.where(rid == 1, _rows8(v), 0.0))
        right = jnp.where(rid == 0, _rows8(kk * a), jnp.where(rid == 1, _rows8(kf), 0.0))
        s_new = S * jnp.exp(lw_ref[pl.ds(n, 1), :]) + _dot3(left, right, _TN)
        ns_ref[n] = s_new
        y = _dot3(_rows8(r), s_new, _NT)[0:1]
        y_ref[pl.ds(n, 1), :] = _rwkv_head_out(y, r, kf, v, row(g_ref), row(rk_ref), row(gw_ref), row(gb_ref))
        return carry

    lax.fori_loop(0, nh, one, 0)


def _wkv_step(S, vals, prm, i, *, nh):
    n_tot = S.shape[0]
    n = A_HEAD_DIM
    tile = lambda p: jnp.tile(p.reshape(A_HEADS, n), (nh // A_HEADS, 1))
    row_spec = pl.BlockSpec((nh, n), lambda b: (b, 0))
    par_spec = pl.BlockSpec((nh, n), lambda b: (0, 0))
    s_spec = pl.BlockSpec((nh, n, n), lambda b: (b, 0, 0))
    return pl.pallas_call(
        _wkv_step_body,
        grid=(n_tot // nh,),
        in_specs=[s_spec] + [row_spec] * 7 + [par_spec] * 3,
        out_specs=[s_spec, row_spec],
        out_shape=[jax.ShapeDtypeStruct(S.shape, F32), jax.ShapeDtypeStruct((n_tot, n), F32)],
        compiler_params=_cparams(1),
    )(S, *vals, tile(prm['rwkv_r_k'][i]), tile(prm['rwkv_gn_w'][i]), tile(prm['rwkv_gn_b'][i]))


def _lru_sample_body(gate_ref, xb_ref, cst_ref, h0_ref, cw_ref, cb_ref, wa_ref, wx_ref, ba_ref, bx_ref, lam_ref,
                     y_ref, h_ref, ncst_ref):
    cw = cw_ref[0]
    xb = xb_ref[...]
    xc = cb_ref[...] + xb * cw[3:4]
    for j in range(3):
        xc = xc + cst_ref[:, j, :] * cw[j:j + 1]
    a, b, gact = _lru_rows(gate_ref[...], xc, wa_ref, wx_ref, ba_ref[...], bx_ref[...], lam_ref[...])
    h = a * h0_ref[...] + b
    y_ref[...] = (h * gact).astype(y_ref.dtype)
    h_ref[...] = h
    ncst_ref[:, 0, :] = cst_ref[:, 1, :]
    ncst_ref[:, 1, :] = cst_ref[:, 2, :]
    ncst_ref[:, 2, :] = xb


def _lru_sample(pb, conv_state, h0, prm, i):
    bs = pb.shape[0]
    w = B_WIDTH
    ng = w // LANES
    full = lambda shape: pl.BlockSpec(shape, lambda s: (0,) * len(shape))
    return pl.pallas_call(
        _lru_sample_body,
        grid=(1,),
        in_specs=[pl.BlockSpec((bs, w), lambda s: (0, 0)), pl.BlockSpec((bs, w), lambda s: (0, 1)),
                  full((bs, 3, w)), full((bs, w)),
                  pl.BlockSpec((1, 4, w), lambda s: (i, 0, 0)),
                  full((1, w)), full((ng, LANES, LANES)), full((ng, LANES, LANES)),
                  full((1, w)), full((1, w)), full((1, w))],
        out_specs=[full((bs, w)), full((bs, w)), full((bs, 3, w))],
        out_shape=[jax.ShapeDtypeStruct((bs, w), BF16), jax.ShapeDtypeStruct((bs, w), F32),
                   jax.ShapeDtypeStruct((bs, 3, w), F32)],
        compiler_params=_cparams(1),
    )(pb, pb, conv_state, h0, prm['lru_conv_w'], _row(prm['lru_conv_b'][i]),
      _block_diag_pairs(prm['lru_wa'][i]), _block_diag_pairs(prm['lru_wx'][i]), _row(prm['lru_ba'][i]),
      _row(prm['lru_bx'][i]), _row(prm['lru_lambda'][i]))


def _gdn_prep_sample_body(x_ref, cst_ref, cw_ref, o_ref, ncst_ref):
    j = pl.program_id(0)
    cw = cw_ref[0]
    x = x_ref[...]
    y = x * cw[3:4]
    for jj in range(3):
        y = y + cst_ref[:, jj, :] * cw[jj:jj + 1]
    n_qk_blocks = 2 * C_KEY_DIM // x.shape[1]
    q_scale = jnp.where(j < n_qk_blocks // 2, C_HEAD ** -0.5, 1.0).astype(F32)
    o_ref[...] = _gdn_act(y, j < n_qk_blocks, q_scale)
    ncst_ref[:, 0, :] = cst_ref[:, 1, :]
    ncst_ref[:, 1, :] = cst_ref[:, 2, :]
    ncst_ref[:, 2, :] = x


def _gdn_prep_sample(qkvz, conv_state, conv_w, i, *, tc):
    bs = qkvz.shape[0]
    return pl.pallas_call(
        _gdn_prep_sample_body,
        grid=(C_QKV_DIM // tc,),
        in_specs=[pl.BlockSpec((bs, tc), lambda j: (0, j)),
                  pl.BlockSpec((bs, 3, tc), lambda j: (0, 0, j)),
                  pl.BlockSpec((1, 4, tc), lambda j: (i, 0, j))],
        out_specs=[pl.BlockSpec((bs, tc), lambda j: (0, j)),
                   pl.BlockSpec((bs, 3, tc), lambda j: (0, 0, j))],
        out_shape=[jax.ShapeDtypeStruct((bs, C_QKV_DIM), F32),
                   jax.ShapeDtypeStruct((bs, 3, C_QKV_DIM), F32)],
        compiler_params=_cparams(1),
    )(qkvz, conv_state, conv_w)


def _gdn_step_body(s_ref, q_ref, k_ref, v_ref, z_ref, ba_ref, alog_ref, dtb_ref, nw_ref, ns_ref, o_ref):
    bb = s_ref.shape[0]
    rep = C_V_HEADS // C_K_HEADS
    rid = _iota2((SUBLANES, C_HEAD), 0)
    beta_all, g_all = _gdn_gates(ba_ref[:, 0, :], alog_ref[...], dtb_ref[...])
    hk = pl.program_id(1)
    outs = []
    for jv in range(rep):
        hv = hk * rep + jv
        outs.append((_lane_bcast_col(beta_all, hv), _lane_bcast_col(g_all, hv)))
    for b in range(bb):
        q = q_ref[b]
        k = k_ref[b]
        for jv in range(rep):
            cols = slice(jv * C_HEAD, (jv + 1) * C_HEAD)
            beta = outs[jv][0][b:b + 1]
            eg = jnp.exp(outs[jv][1][b:b + 1])
            v = v_ref[b][:, cols]
            S = s_ref[b, jv]
            kb = k * beta
            lhs = jnp.where(rid == 0, _rows8(kb * eg), jnp.where(rid == 1, _rows8(q * eg), 0.0))
            ws = _dot3(lhs, S)
            v_new = v * beta - ws[0:1]
            o = ws[1:2] + jnp.sum(q * k, axis=-1, keepdims=True) * v_new
            left = jnp.where(rid == 0, _rows8(k), 0.0)
            right = jnp.where(rid == 0, _rows8(v_new), 0.0)
            ns_ref[b, jv] = S * eg + _dot3(left, right, _TN)
            o_ref[b, :, cols] = _gdn_head_out(o, z_ref[b][:, cols], nw_ref[...]).astype(o_ref.dtype)


def _gdn_step(S, qkv_act, qkvz, ba, prm, i, *, bb):
    bs = S.shape[0]
    rep = C_V_HEADS // C_K_HEADS
    vw = rep * C_HEAD
    v_blk0 = 2 * C_KEY_DIM // vw
    z_blk0 = C_QKV_DIM // vw
    full = lambda shape: pl.BlockSpec(shape, lambda b, h: (0,) * len(shape))
    s_spec = pl.BlockSpec((bb, rep, C_HEAD, C_HEAD), lambda b, h: (b, h, 0, 0))
    return pl.pallas_call(
        _gdn_step_body,
        grid=(bs // bb, C_K_HEADS),
        in_specs=[s_spec,
                  pl.BlockSpec((bb, 1, C_HEAD), lambda b, h: (b, 0, h)),
                  pl.BlockSpec((bb, 1, C_HEAD), lambda b, h: (b, 0, C_K_HEADS + h)),
                  pl.BlockSpec((bb, 1, vw), lambda b, h: (b, 0, v_blk0 + h)),
                  pl.BlockSpec((bb, 1, vw), lambda b, h: (b, 0, z_blk0 + h)),
                  pl.BlockSpec((bb, 1, 2 * C_V_HEADS), lambda b, h: (b, 0, 0)),
                  full((1, C_V_HEADS)), full((1, C_V_HEADS)), full((1, C_HEAD))],
        out_specs=[s_spec, pl.BlockSpec((bb, 1, vw), lambda b, h: (b, 0, h))],
        out_shape=[jax.ShapeDtypeStruct(S.shape, F32), jax.ShapeDtypeStruct((bs, 1, C_VAL_DIM), BF16)],
        compiler_params=_cparams(2),
    )(S, qkv_act, qkv_act, qkv_act, qkvz, ba, _row(prm['gdn_a_log'][i]), _row(prm['gdn_dt_bias'][i]),
      _row(prm['gdn_norm_w'][i]))


MOD_SH1, MOD_SC1, MOD_GT1, MOD_SH2, MOD_SC2, MOD_GT2 = range(6)


def _in_proj_ab(h, prm, i, tm):
    w = prm['w_in_ab'][i]
    p_main = _mm([h], [(w, 0)], n_out=P_A_MAIN, tm=tm)
    p_lora = _mm([h], [(w, 0)], n_out=3 * LANES, col0=P_A_MAIN, tm=tm, tn=3 * LANES)
    pb = _mm([h], [(w[:, P_A:], 0)], n_out=2 * B_WIDTH, tm=tm)
    return p_main, p_lora, pb


def _in_proj_c(h, prm, i, tm):
    w = prm['w_in_c'][i]
    qkvz = _mm([h], [(w, 0)], n_out=C_QKV_DIM + C_VAL_DIM, tm=tm)
    ba = _mm([h], [(w[:, C_QKV_DIM + C_VAL_DIM:], 0)], n_out=2 * C_V_HEADS, tm=tm, tn=2 * C_V_HEADS)
    return qkvz, ba


def _ffn_down(act, prm, layer, x, mod, tm):
    return _mm([act], [(prm['ffn_w_down'][layer], 0)], n_out=D_MODEL, tm=tm, res=x, gate=mod, gate_blk=MOD_GT2)


def _trunk_prompt(x, mods, prm):
    B, T, _ = x.shape
    tm, tt = min(1024, T), min(256, T)
    wkv, shift, lru_h, lru_conv, gdn, gdn_conv, ffn_conv = [], [], [], [], [], [], []
    for layer in range(DEPTH):
        mod = mods[layer].reshape(B, 1, 6 * D_MODEL)
        i = layer // 2
        h = _norm(x, prm['norm_mix'][layer], mod, MOD_SC1, MOD_SH1, tt=tt)
        if layer % 2 == 0:
            p_main, p_lora, pb = _in_proj_ab(h, prm, i, tm)
            vals = _rwkv_prep_prompt(p_main, p_lora, prm, i, tt=tt)
            ya, s_new = _wkv_prompt(vals, prm, i, tt=tt)
            yb, h_last, cst = _lru_prompt(pb, prm, i, tt=tt)
            w_out = prm['w_out_ab'][i]
            x = _mm([ya, yb], [(w_out, 0), (w_out, 1)], n_out=D_MODEL, tm=tm, res=x, gate=mod, gate_blk=MOD_GT1)
            wkv.append(s_new)
            shift.append(jnp.concatenate([p_main[:, -1], p_lora[:, -1, :LORA_ALL]], axis=-1))
            lru_h.append(h_last[:, 0])
            lru_conv.append(cst)
        else:
            qkvz, ba = _in_proj_c(h, prm, i, tm)
            qkv_act, cst = _gdn_prep_prompt(qkvz, prm['gdn_conv_w'], i, tt=tt, tc=2048)
            yc, s_new = _gdn_prompt(qkv_act, qkvz, ba, prm, i, tt=tt)
            x = _mm([yc], [(prm['w_out_c'][i], 0)], n_out=D_MODEL, tm=tm, res=x, gate=mod, gate_blk=MOD_GT1)
            gdn.append(s_new)
            gdn_conv.append(cst)
        h = _norm(x, prm['norm_ffn'][layer], mod, MOD_SC2, MOD_SH2, tt=tt)
        act, fst = _ffn_up_prompt(h, prm['ffn_w_gate'], prm['ffn_w_up'], prm['ffn_conv_w'], prm['ffn_conv_b'],
                                  layer, tm=tm, tn=512)
        x = _ffn_down(act, prm, layer, x, mod, tm)
        ffn_conv.append(fst)
    y = _norm(x, prm['norm_out'], tt=tt, out_dtype=F32)
    stk = jnp.stack
    return y, stk(wkv), stk(shift), stk(lru_h), stk(lru_conv), stk(gdn), stk(gdn_conv), stk(ffn_conv)


def _trunk_sample(x, mods, st, prm):
    bs = x.shape[0]
    x = x.reshape(1, bs, D_MODEL)
    wkv0, shift0, lru_h0, lru_conv0, gdn0, gdn_conv0, ffn_conv0 = st
    wkv, shift, lru_h, lru_conv, gdn, gdn_conv, ffn_conv = [], [], [], [], [], [], []
    for layer in range(DEPTH):
        mod = mods[layer].reshape(1, bs, 6 * D_MODEL)
        i = layer // 2
        h = _norm(x, prm['norm_mix'][layer], mod, MOD_SC1, MOD_SH1, tt=bs)
        if layer % 2 == 0:
            p_main, p_lora, pb = _in_proj_ab(h, prm, i, bs)
            p_main, p_lora, pb = p_main[0], p_lora[0], pb[0]
            vals = _rwkv_prep_sample(p_main, p_lora, shift0[i], prm, i)
            heads = lambda a: a.reshape(bs * A_HEADS, A_HEAD_DIM)
            s_new, ya = _wkv_step(wkv0[i].reshape(bs * A_HEADS, A_HEAD_DIM, A_HEAD_DIM),
                                  [heads(a) for a in vals], prm, i, nh=8 * A_HEADS)
            ya = ya.reshape(1, bs, A_WIDTH).astype(BF16)
            yb, h_last, cst = _lru_sample(pb, lru_conv0[i], lru_h0[i], prm, i)
            w_out = prm['w_out_ab'][i]
            x = _mm([ya, yb[None]], [(w_out, 0), (w_out, 1)], n_out=D_MODEL, tm=bs, res=x, gate=mod,
                    gate_blk=MOD_GT1)
            wkv.append(s_new.reshape(wkv0[i].shape))
            shift.append(jnp.concatenate([p_main, p_lora[:, :LORA_ALL]], axis=-1))
            lru_h.append(h_last)
            lru_conv.append(cst)
        else:
            qkvz, ba = _in_proj_c(h, prm, i, bs)
            qkv_act, cst = _gdn_prep_sample(qkvz[0], gdn_conv0[i], prm['gdn_conv_w'], i, tc=2048)
            s_new, yc = _gdn_step(gdn0[i], qkv_act[:, None], qkvz[0][:, None], ba[0][:, None], prm, i, bb=8)
            x = _mm([yc.reshape(1, bs, C_VAL_DIM)], [(prm['w_out_c'][i], 0)], n_out=D_MODEL, tm=bs, res=x,
                    gate=mod, gate_blk=MOD_GT1)
            gdn.append(s_new)
            gdn_conv.append(cst)
        h = _norm(x, prm['norm_ffn'][layer], mod, MOD_SC2, MOD_SH2, tt=bs)
        act, fst = _ffn_up_sample(h[0], prm['ffn_w_gate'], prm['ffn_w_up'], prm['ffn_conv_w'],
                                  prm['ffn_conv_b'], ffn_conv0[layer], layer, tn=512)
        x = _ffn_down(act[None], prm, layer, x, mod, bs)
        ffn_conv.append(fst)
    y = _norm(x, prm['norm_out'], tt=bs, out_dtype=F32).reshape(bs, 1, D_MODEL)
    stk = jnp.stack
    return y, stk(wkv), stk(shift), stk(lru_h), stk(lru_conv), stk(gdn), stk(gdn_conv), stk(ffn_conv)


def kernel(x_prompt, x_sample, c_prompt, c_sample, state_rwkv_wkv, state_rwkv_shift, state_lru_h, state_lru_conv, state_gdn, state_gdn_conv, state_ffn_conv, w_mod, b_mod, norm_mix, norm_ffn, norm_out, w_in_ab, w_out_ab, rwkv_mu, rwkv_w0, rwkv_w2, rwkv_a0, rwkv_a2, rwkv_g2, rwkv_k_k, rwkv_k_a, rwkv_r_k, rwkv_gn_w, rwkv_gn_b, lru_conv_w, lru_conv_b, lru_wa, lru_ba, lru_wx, lru_bx, lru_lambda, w_in_c, w_out_c, gdn_conv_w, gdn_a_log, gdn_dt_bias, gdn_norm_w, ffn_w_gate, ffn_w_up, ffn_conv_w, ffn_conv_b, ffn_w_down):
    prm = dict(norm_mix=norm_mix, norm_ffn=norm_ffn, norm_out=norm_out, w_in_ab=w_in_ab, w_out_ab=w_out_ab,
               rwkv_mu=rwkv_mu, rwkv_w0=rwkv_w0, rwkv_w2=rwkv_w2, rwkv_a0=rwkv_a0, rwkv_a2=rwkv_a2,
               rwkv_g2=rwkv_g2, rwkv_k_k=rwkv_k_k, rwkv_k_a=rwkv_k_a,
               rwkv_r_k=rwkv_r_k.reshape(rwkv_r_k.shape[0], A_WIDTH), rwkv_gn_w=rwkv_gn_w,
               rwkv_gn_b=rwkv_gn_b, lru_conv_w=lru_conv_w, lru_conv_b=lru_conv_b, lru_wa=lru_wa, lru_ba=lru_ba,
               lru_wx=lru_wx, lru_bx=lru_bx, lru_lambda=lru_lambda, w_in_c=w_in_c, w_out_c=w_out_c,
               gdn_conv_w=gdn_conv_w, gdn_a_log=gdn_a_log, gdn_dt_bias=gdn_dt_bias, gdn_norm_w=gdn_norm_w,
               ffn_w_gate=ffn_w_gate, ffn_w_up=ffn_w_up, ffn_conv_w=ffn_conv_w, ffn_conv_b=ffn_conv_b,
               ffn_w_down=ffn_w_down)
    mods_p, mods_s = _modulation(c_prompt, c_sample, w_mod, b_mod)
    outs_p = _trunk_prompt(x_prompt, mods_p, prm)
    outs_s = _trunk_sample(x_sample, mods_s,
                           (state_rwkv_wkv, state_rwkv_shift, state_lru_h, state_lru_conv, state_gdn,
                            state_gdn_conv, state_ffn_conv), prm)
    return (outs_p[0], outs_s[0]) + tuple(outs_p[1:]) + tuple(outs_s[1:])
```

```python
import functools

import jax
import jax.numpy as jnp
from jax import lax
from jax.experimental import pallas as pl
from jax.experimental.pallas import tpu as pltpu

F32 = jnp.float32
BF16 = jnp.bfloat16

D_MODEL = 2048
DEPTH = 4
A_HEAD_DIM = 64
A_WIDTH = D_MODEL // 2
A_HEADS = A_WIDTH // A_HEAD_DIM
LORA_W = 64
LORA_A = 64
LORA_G = 160
LORA_ALL = LORA_W + LORA_A + LORA_G
P_A_MAIN = 3 * A_WIDTH
P_A = P_A_MAIN + LORA_ALL
A_GN_EPS = 64e-5
B_WIDTH = D_MODEL - A_WIDTH
B_BLOCK_DIM = 64
LRU_C = 8.0
C_HEAD = 128
C_K_HEADS = D_MODEL // C_HEAD
C_V_HEADS = 2 * C_K_HEADS
C_KEY_DIM = C_K_HEADS * C_HEAD
C_VAL_DIM = C_V_HEADS * C_HEAD
C_QKV_DIM = 2 * C_KEY_DIM + C_VAL_DIM
D_FF = 5632
NORM_EPS = 1e-6

LANES = 128
SUBLANES = 8
HIST = SUBLANES
CHUNK = 64
VMEM_LIMIT = 56 * 1024 * 1024
MM_VMEM_BUDGET = 40 * 1024 * 1024

_NN = (((1,), (0,)), ((), ()))
_NT = (((1,), (1,)), ((), ()))
_TN = (((0,), (0,)), ((), ()))
_BNN = (((2,), (1,)), ((0,), (0,)))
_BNT = (((2,), (2,)), ((0,), (0,)))
_BTN = (((1,), (1,)), ((0,), (0,)))


def _cparams(n_axes):
    return pltpu.CompilerParams(dimension_semantics=("arbitrary",) * n_axes,
                                vmem_limit_bytes=VMEM_LIMIT)


def _bdot(a, b, dims=_NN):
    return lax.dot_general(a.astype(BF16), b.astype(BF16), dims, preferred_element_type=F32)


def _split2(a):
    hi = a.astype(BF16)
    lo = (a - hi.astype(F32)).astype(BF16)
    return hi, lo


def _dot3(a, b, dims=_NN):
    ah, al = _split2(a)
    bh, bl = _split2(b)
    d = lambda x, y: lax.dot_general(x, y, dims, preferred_element_type=F32)
    return d(ah, bh) + (d(ah, bl) + d(al, bh))


def _split3(a):
    h0 = a.astype(BF16)
    r1 = a - h0.astype(F32)
    h1 = r1.astype(BF16)
    h2 = (r1 - h1.astype(F32)).astype(BF16)
    return h0, h1, h2


def _sel_l(mask01, x, dims=_NN):
    m = mask01.astype(BF16)
    d = lambda y: lax.dot_general(m, y, dims, preferred_element_type=F32)
    h0, h1, h2 = _split3(x)
    return d(h0) + (d(h1) + d(h2))


def _sel_r(x, mask01):
    m = mask01.astype(BF16)
    d = lambda y: lax.dot_general(y, m, _NN, preferred_element_type=F32)
    h0, h1, h2 = _split3(x)
    return d(h0) + (d(h1) + d(h2))


def _iota2(shape, axis):
    return lax.broadcasted_iota(jnp.int32, shape, axis)


def _tri(n, strict):
    i = _iota2((n, n), 0)
    j = _iota2((n, n), 1)
    return (i > j) if strict else (i >= j)


def _unit_lower_inverse(low, mm):
    n = low.shape[-1]
    dims = _BNN if low.ndim == 3 else _NN
    eye = (_iota2((n, n), 0) == _iota2((n, n), 1)).astype(F32)
    m = -low
    inv = eye + m
    p = m
    span = 2
    while span < n:
        p = mm(p, p, dims)
        inv = inv + mm(inv, p, dims)
        span *= 2
    return inv


def _block_tri(tt, block):
    i = _iota2((tt, tt), 0)
    j = _iota2((tt, tt), 1)
    return (i // block == j // block) & (i >= j)


def _neg_expm1(x):
    return -jnp.tanh(0.5 * x) * (jnp.exp(x) + 1.0)


def _segment_sum(x, seg):
    i = _iota2((LANES, LANES), 0) // seg
    j = _iota2((LANES, LANES), 1) // seg
    same = (i == j)
    parts = [_sel_r(x[:, g * LANES:(g + 1) * LANES], same) for g in range(x.shape[1] // LANES)]
    return parts[0] if len(parts) == 1 else jnp.concatenate(parts, axis=1)


def _rwkv_rows(p_main, q_main, p_lora, q_lora, mu_main, mu_lora, w0, w2, a0, a2, g2, k_k, k_a):
    pm = p_main + (q_main - p_main) * mu_main
    pl_ = p_lora + (q_lora - p_lora) * mu_lora
    r = pm[:, :A_WIDTH]
    k = pm[:, A_WIDTH:2 * A_WIDTH]
    v = pm[:, 2 * A_WIDTH:]
    xw = pl_[:, :LORA_W]
    xa = pl_[:, LORA_W:LORA_W + LORA_A]
    xg = pl_[:, LORA_W + LORA_A:LORA_ALL]
    w_ll = -jax.nn.softplus(-(w0 + _bdot(jnp.tanh(xw), w2))) - 0.5
    lw = -jnp.exp(w_ll)
    a = jax.nn.sigmoid(a0 + _bdot(xa, a2))
    g = _bdot(jax.nn.sigmoid(xg), g2)
    kkn = k * k_k
    kf = k * (1.0 + (a - 1.0) * k_a)
    return r, kf, v, kkn, a, lw, g


def _rwkv_head_out(y, r, kf, v, g, r_k, gn_w, gn_b):
    mean = jnp.mean(y, axis=-1, keepdims=True)
    var = jnp.mean(jnp.square(y - mean), axis=-1, keepdims=True)
    yn = (y - mean) * lax.rsqrt(var + A_GN_EPS) * gn_w + gn_b
    bonus = jnp.sum(r * kf * r_k, axis=-1, keepdims=True) * v
    return (yn + bonus) * g


def _lru_rows(gate, xc, wa_bd, wx_bd, ba, bx, lam):
    ng = xc.shape[1] // LANES
    ra = jnp.concatenate([_bdot(xc[:, g * LANES:(g + 1) * LANES], wa_bd[g]) for g in range(ng)], axis=1)
    rx = jnp.concatenate([_bdot(xc[:, g * LANES:(g + 1) * LANES], wx_bd[g]) for g in range(ng)], axis=1)
    r_gate = jax.nn.sigmoid(ra + ba)
    i_gate = jax.nn.sigmoid(rx + bx)
    log_a = -LRU_C * r_gate * jax.nn.softplus(-lam)
    a = jnp.exp(log_a)
    b = jnp.sqrt(_neg_expm1(2.0 * log_a)) * (i_gate * xc)
    return a, b, jax.nn.gelu(gate)


def _gdn_act(y, is_qk, q_scale):
    y = jax.nn.silu(y)
    outs = []
    for h in range(y.shape[1] // C_HEAD):
        yh = y[:, h * C_HEAD:(h + 1) * C_HEAD]
        nrm = yh * lax.rsqrt(jnp.sum(yh * yh, axis=-1, keepdims=True) + 1e-6) * q_scale
        outs.append(jnp.where(is_qk, nrm, yh))
    return jnp.concatenate(outs, axis=1)


def _gdn_head_out(o, z, norm_w):
    o = o * lax.rsqrt(jnp.mean(o * o, axis=-1, keepdims=True) + NORM_EPS) * norm_w
    return o * jax.nn.silu(z)


def _mm_body(*refs, n_x, has_res):
    x_refs = refs[:n_x]
    w_refs = refs[n_x:2 * n_x]
    k = 2 * n_x
    if has_res:
        res_ref, gate_ref = refs[k], refs[k + 1]
        k += 2
    o_ref = refs[k]
    wbf_refs = refs[k + 1:k + 1 + n_x]

    @pl.when((pl.program_id(1) == 0) & (pl.program_id(2) == 0))
    def _():
        for w_ref, wbf_ref in zip(w_refs, wbf_refs):
            wbf_ref[...] = w_ref[...].astype(BF16)

    acc = None
    for x_ref, wbf_ref in zip(x_refs, wbf_refs):
        d = jnp.dot(x_ref[0], wbf_ref[...], preferred_element_type=F32)
        acc = d if acc is None else acc + d
    if has_res:
        acc = res_ref[0] + gate_ref[0] * acc
    o_ref[0] = acc.astype(o_ref.dtype)


def _pick_tn(n_out, k_sum, tm, n_out_bufs):
    for tn in (512, 256, 128):
        if n_out % tn:
            continue
        need = 2 * tm * k_sum * 2 + 2 * k_sum * tn * 4 + k_sum * tn * 2 + 2 * n_out_bufs * tm * tn * 4
        if need <= MM_VMEM_BUDGET:
            return tn
    raise ValueError("no projection tile fits VMEM")


def _mm(xs, ws, *, n_out, col0=0, tm, tn=None, out_dtype=F32, res=None, gate=None, gate_blk=0, name="proj"):
    B, T, _ = xs[0].shape
    ks = [x.shape[-1] for x in xs]
    has_res = res is not None
    if tn is None:
        tn = _pick_tn(n_out, sum(ks), tm, 2 if has_res else 1)
    assert n_out % tn == 0 and col0 % tn == 0 and T % tm == 0
    cb0 = col0 // tn
    in_specs = [pl.BlockSpec((1, tm, k), lambda j, b, t: (b, t, 0)) for k in ks]
    in_specs += [pl.BlockSpec((k, tn), functools.partial(lambda j, b, t, rb: (rb, cb0 + j), rb=rb))
                 for k, (_, rb) in zip(ks, ws)]
    args = list(xs) + [w for w, _ in ws]
    if has_res:
        in_specs.append(pl.BlockSpec((1, tm, tn), lambda j, b, t: (b, t, j)))
        gb0 = gate_blk * (n_out // tn)
        if gate.shape[1] == 1:
            in_specs.append(pl.BlockSpec((1, 1, tn), lambda j, b, t: (b, 0, gb0 + j)))
        else:
            in_specs.append(pl.BlockSpec((1, tm, tn), lambda j, b, t: (b, t, gb0 + j)))
        args += [res, gate]
    return pl.pallas_call(
        functools.partial(_mm_body, n_x=len(xs), has_res=has_res),
        grid=(n_out // tn, B, T // tm),
        in_specs=in_specs,
        out_specs=pl.BlockSpec((1, tm, tn), lambda j, b, t: (b, t, j)),
        out_shape=jax.ShapeDtypeStruct((B, T, n_out), out_dtype),
        scratch_shapes=[pltpu.VMEM((k, tn), BF16) for k in ks],
        compiler_params=_cparams(3),
        name=name,
    )(*args)


def _mod_body(cp_ref, cs_ref, w_ref, b_ref, op_ref, os_ref):
    w = w_ref[0].astype(BF16)
    b = b_ref[0]
    op_ref[0] = _bdot(jax.nn.silu(cp_ref[...]), w) + b
    os_ref[0] = _bdot(jax.nn.silu(cs_ref[...]), w) + b


def _modulation(c_p, c_s, w_mod, b_mod):
    depth, d, n = w_mod.shape
    tn = 1024
    bp, bs = c_p.shape[0], c_s.shape[0]
    return pl.pallas_call(
        _mod_body,
        grid=(depth, n // tn),
        in_specs=[pl.BlockSpec((bp, d), lambda l, j: (0, 0)),
                  pl.BlockSpec((bs, d), lambda l, j: (0, 0)),
                  pl.BlockSpec((1, d, tn), lambda l, j: (l, 0, j)),
                  pl.BlockSpec((1, 1, tn), lambda l, j: (l, 0, j))],
        out_specs=[pl.BlockSpec((1, bp, tn), lambda l, j: (l, 0, j)),
                   pl.BlockSpec((1, bs, tn), lambda l, j: (l, 0, j))],
        out_shape=[jax.ShapeDtypeStruct((depth, bp, n), F32),
                   jax.ShapeDtypeStruct((depth, bs, n), F32)],
        compiler_params=_cparams(2),
        name="modulation",
    )(c_p, c_s, w_mod, b_mod.reshape(depth, 1, n))


def _norm_body(*refs, has_mod):
    if has_mod:
        x_ref, g_ref, sc_ref, sh_ref, o_ref = refs
    else:
        x_ref, g_ref, o_ref = refs
    x = x_ref[0]
    y = x * lax.rsqrt(jnp.mean(x * x, axis=-1, keepdims=True) + NORM_EPS) * g_ref[...]
    if has_mod:
        y = y * (1.0 + sc_ref[0]) + sh_ref[0]
    o_ref[0] = y.astype(o_ref.dtype)


def _norm(x, gain, mod=None, sc_blk=0, sh_blk=0, *, tt, out_dtype=BF16):
    B, T, d = x.shape
    in_specs = [pl.BlockSpec((1, tt, d), lambda b, t: (b, t, 0)),
                pl.BlockSpec((1, d), lambda b, t: (0, 0))]
    args = [x, gain.reshape(1, d)]
    if mod is not None:
        for blk in (sc_blk, sh_blk):
            if mod.shape[1] == 1:
                in_specs.append(pl.BlockSpec((1, 1, d), functools.partial(lambda b, t, blk: (b, 0, blk), blk=blk)))
            else:
                in_specs.append(pl.BlockSpec((1, tt, d), functools.partial(lambda b, t, blk: (b, t, blk), blk=blk)))
        args += [mod, mod]
    return pl.pallas_call(
        functools.partial(_norm_body, has_mod=mod is not None),
        grid=(B, T // tt),
        in_specs=in_specs,
        out_specs=pl.BlockSpec((1, tt, d), lambda b, t: (b, t, 0)),
        out_shape=jax.ShapeDtypeStruct((B, T, d), out_dtype),
        compiler_params=_cparams(2),
        name="norm",
    )(*args)


def _ffn_up_prompt_body(h_ref, wg_ref, wu_ref, cw_ref, cb_ref, act_ref, st_ref, wg_bf, wu_bf, buf):
    t = pl.program_id(2)
    tm = h_ref.shape[1]

    @pl.when((pl.program_id(1) == 0) & (t == 0))
    def _():
        wg_bf[...] = wg_ref[0].astype(BF16)
        wu_bf[...] = wu_ref[0].astype(BF16)

    @pl.when(t == 0)
    def _():
        buf[0:HIST, :] = jnp.zeros((HIST, buf.shape[1]), F32)

    h = h_ref[0]
    buf[HIST:, :] = jnp.dot(h, wg_bf[...], preferred_element_type=F32)
    cw = cw_ref[0]
    u = (buf[HIST - 2:HIST - 2 + tm, :] * cw[0:1] + buf[HIST - 1:HIST - 1 + tm, :] * cw[1:2]
         + buf[HIST:, :] * cw[2:3] + cb_ref[0])
    up = jnp.dot(h, wu_bf[...], preferred_element_type=F32)
    act_ref[0] = (jax.nn.gelu(u) * up).astype(act_ref.dtype)
    st_ref[0] = buf[HIST + tm - 2:HIST + tm, :]
    buf[0:HIST, :] = buf[tm:tm + HIST, :]


def _ffn_up_prompt(h, w_gate, w_up, conv_w, conv_b, layer, *, tm, tn):
    B, T, d = h.shape
    n = w_gate.shape[-1]
    return pl.pallas_call(
        _ffn_up_prompt_body,
        grid=(n // tn, B, T // tm),
        in_specs=[pl.BlockSpec((1, tm, d), lambda j, b, t: (b, t, 0)),
                  pl.BlockSpec((1, d, tn), lambda j, b, t: (layer, 0, j)),
                  pl.BlockSpec((1, d, tn), lambda j, b, t: (layer, 0, j)),
                  pl.BlockSpec((1, 3, tn), lambda j, b, t: (layer, 0, j)),
                  pl.BlockSpec((1, 1, tn), lambda j, b, t: (layer, 0, j))],
        out_specs=[pl.BlockSpec((1, tm, tn), lambda j, b, t: (b, t, j)),
                   pl.BlockSpec((1, 2, tn), lambda j, b, t: (b, 0, j))],
        out_shape=[jax.ShapeDtypeStruct((B, T, n), BF16),
                   jax.ShapeDtypeStruct((B, 2, n), F32)],
        scratch_shapes=[pltpu.VMEM((d, tn), BF16), pltpu.VMEM((d, tn), BF16),
                        pltpu.VMEM((HIST + tm, tn), F32)],
        compiler_params=_cparams(3),
        name="ffn_up_prompt",
    )(h, w_gate, w_up, conv_w, conv_b.reshape(conv_b.shape[0], 1, n))


def _ffn_up_sample_body(h_ref, wg_ref, wu_ref, cw_ref, cb_ref, st_ref, act_ref, nst_ref):
    h = h_ref[...]
    pre = _bdot(h, wg_ref[0])
    cw = cw_ref[0]
    u = st_ref[:, 0, :] * cw[0:1] + st_ref[:, 1, :] * cw[1:2] + pre * cw[2:3] + cb_ref[0]
    up = _bdot(h, wu_ref[0])
    act_ref[...] = (jax.nn.gelu(u) * up).astype(act_ref.dtype)
    nst_ref[:, 0, :] = st_ref[:, 1, :]
    nst_ref[:, 1, :] = pre


def _ffn_up_sample(h, w_gate, w_up, conv_w, conv_b, state, layer, *, tn):
    bs, d = h.shape
    n = w_gate.shape[-1]
    return pl.pallas_call(
        _ffn_up_sample_body,
        grid=(n // tn,),
        in_specs=[pl.BlockSpec((bs, d), lambda j: (0, 0)),
                  pl.BlockSpec((1, d, tn), lambda j: (layer, 0, j)),
                  pl.BlockSpec((1, d, tn), lambda j: (layer, 0, j)),
                  pl.BlockSpec((1, 3, tn), lambda j: (layer, 0, j)),
                  pl.BlockSpec((1, 1, tn), lambda j: (layer, 0, j)),
                  pl.BlockSpec((bs, 2, tn), lambda j: (0, 0, j))],
        out_specs=[pl.BlockSpec((bs, tn), lambda j: (0, j)),
                   pl.BlockSpec((bs, 2, tn), lambda j: (0, 0, j))],
        out_shape=[jax.ShapeDtypeStruct((bs, n), BF16),
                   jax.ShapeDtypeStruct((bs, 2, n), F32)],
        compiler_params=_cparams(1),
        name="ffn_up_sample",
    )(h, w_gate, w_up, conv_w, conv_b.reshape(conv_b.shape[0], 1, n), state)


def _rwkv_prep_prompt_body(pm_ref, pl_ref, mum_ref, mul_ref, w0_ref, w2_ref, a0_ref, a2_ref, g2_ref,
                           kk_ref, ka_ref, r_o, kf_o, v_o, kkn_o, a_o, lw_o, g_o, bufm, bufl):
    t = pl.program_id(1)
    tt = pm_ref.shape[1]

    @pl.when(t == 0)
    def _():
        bufm[0:HIST, :] = jnp.zeros((HIST, bufm.shape[1]), F32)
        bufl[0:HIST, :] = jnp.zeros((HIST, bufl.shape[1]), F32)

    p_main = pm_ref[0]
    p_lora = pl_ref[0][:, :LORA_ALL]
    bufm[HIST:, :] = p_main
    bufl[HIST:, :] = p_lora
    q_main = bufm[HIST - 1:HIST - 1 + tt, :]
    q_lora = bufl[HIST - 1:HIST - 1 + tt, :]
    outs = _rwkv_rows(p_main, q_main, p_lora, q_lora, mum_ref[...], mul_ref[...], w0_ref[...], w2_ref[0],
                      a0_ref[...], a2_ref[0], g2_ref[0], kk_ref[...], ka_ref[...])
    for o_ref, val in zip((r_o, kf_o, v_o, kkn_o, a_o, lw_o, g_o), outs):
        o_ref[0] = val
    bufm[0:HIST, :] = bufm[tt:tt + HIST, :]
    bufl[0:HIST, :] = bufl[tt:tt + HIST, :]


def _row(v):
    return v.reshape(1, -1)


def _rwkv_prep_prompt(p_main, p_lora, prm, i, *, tt):
    B, T, _ = p_main.shape
    lw_pad = p_lora.shape[-1]
    mu = prm['rwkv_mu'][i]
    full = lambda shape: pl.BlockSpec(shape, lambda b, t: (0,) * len(shape))
    lay3 = lambda shape: pl.BlockSpec((1,) + shape, lambda b, t: (i, 0, 0))
    out_spec = pl.BlockSpec((1, tt, A_WIDTH), lambda b, t: (b, t, 0))
    return pl.pallas_call(
        _rwkv_prep_prompt_body,
        grid=(B, T // tt),
        in_specs=[pl.BlockSpec((1, tt, P_A_MAIN), lambda b, t: (b, t, 0)),
                  pl.BlockSpec((1, tt, lw_pad), lambda b, t: (b, t, 0)),
                  full((1, P_A_MAIN)), full((1, LORA_ALL)), full((1, A_WIDTH)),
                  lay3((LORA_W, A_WIDTH)), full((1, A_WIDTH)), lay3((LORA_A, A_WIDTH)),
                  lay3((LORA_G, A_WIDTH)), full((1, A_WIDTH)), full((1, A_WIDTH))],
        out_specs=[out_spec] * 7,
        out_shape=[jax.ShapeDtypeStruct((B, T, A_WIDTH), F32)] * 7,
        scratch_shapes=[pltpu.VMEM((HIST + tt, P_A_MAIN), F32), pltpu.VMEM((HIST + tt, LORA_ALL), F32)],
        compiler_params=_cparams(2),
        name="rwkv_prep_prompt",
    )(p_main, p_lora, _row(mu[:P_A_MAIN]), _row(mu[P_A_MAIN:]), _row(prm['rwkv_w0'][i]), prm['rwkv_w2'],
      _row(prm['rwkv_a0'][i]), prm['rwkv_a2'], prm['rwkv_g2'], _row(prm['rwkv_k_k'][i]), _row(prm['rwkv_k_a'][i]))


def _wkv_prompt_body(r_ref, kf_ref, v_ref, kkn_ref, a_ref, lw_ref, g_ref, rk_ref, gw_ref, gb_ref,
                     ya_ref, st_ref, s_scr):
    t = pl.program_id(2)
    tt = r_ref.shape[1]
    n = A_HEAD_DIM
    C = CHUNK
    nc = tt // C
    nh = r_ref.shape[2] // n

    @pl.when(t == 0)
    def _():
        s_scr[...] = jnp.zeros(s_scr.shape, F32)

    def units(x):
        parts = [x[:, h * n:(h + 1) * n].reshape(nc, C, n) for h in range(nh)]
        return jnp.stack(parts, axis=1).reshape(nc * nh, C, n)

    lw_all = lw_ref[0]
    lc = units(_sel_l(_block_tri(tt, C), lw_all))
    r, kf, v, kkn, a, lw = map(units, (r_ref[0], kf_ref[0], v_ref[0], kkn_ref[0], a_ref[0], lw_all))
    incl = _tri(C, False)
    strict = _tri(C, True)

    kk = kkn * lax.rsqrt(jnp.sum(kkn * kkn, axis=-1, keepdims=True) + 1e-12)
    bb = kk * a
    p = jnp.exp(lc)
    pinv = jnp.exp(-lc)
    kt = kf * pinv
    bt = bb * pinv
    at = kk * jnp.exp(lc - lw)
    rt = r * p
    gram = _bdot(jnp.concatenate([at, rt], axis=1), jnp.concatenate([bt, kt], axis=1), _BNT)
    a_ab = jnp.where(strict, gram[:, :C, :C], 0.0)
    a_ak = jnp.where(strict, gram[:, :C, C:], 0.0)
    a_rb = jnp.where(incl, gram[:, C:, :C], 0.0)
    a_rk = jnp.where(incl, gram[:, C:, C:], 0.0)
    tinv = _unit_lower_inverse(a_ab, _bdot)
    av = _bdot(jnp.concatenate([a_ak, a_rk], axis=1), v, _BNN)
    tz = _bdot(tinv, jnp.concatenate([at, av[:, :C]], axis=2), _BNN)
    rz = _bdot(a_rb, tz, _BNN)
    lhs_s = jnp.concatenate([tz[:, :, :n], rt - rz[:, :, :n]], axis=1)
    z0 = tz[:, :, n:]
    y0 = av[:, C:] - rz[:, :, n:]
    plast = p[:, C - 1:C, :]
    upd = jnp.concatenate([kt * plast, -(bt * plast)], axis=1)

    S = s_scr[...]
    ys = []
    for c in range(nc):
        sl = slice(c * nh, (c + 1) * nh)
        xs = _bdot(lhs_s[sl], S, _BNT)
        z = xs[:, :C] + z0[sl]
        ys.append(xs[:, C:] + y0[sl])
        S = S * plast[sl] + _bdot(jnp.concatenate([v[sl], z], axis=1), upd[sl], _BTN)
    s_scr[...] = S

    outs = []
    for h in range(nh):
        cols = slice(h * n, (h + 1) * n)
        y = jnp.concatenate([ys[c][h] for c in range(nc)], axis=0)
        outs.append(_rwkv_head_out(y, r_ref[0, :, cols], kf_ref[0, :, cols], v_ref[0, :, cols],
                                   g_ref[0, :, cols], rk_ref[:, cols], gw_ref[:, cols], gb_ref[:, cols]))
    ya_ref[0] = jnp.concatenate(outs, axis=1).astype(ya_ref.dtype)

    @pl.when(t == pl.num_programs(2) - 1)
    def _():
        st_ref[0] = S


def _wkv_prompt(vals, prm, i, *, tt, heads_per_step):
    B, T, _ = vals[0].shape
    wblk = heads_per_step * A_HEAD_DIM
    seq_spec = pl.BlockSpec((1, tt, wblk), lambda b, h, t: (b, t, h))
    par_spec = pl.BlockSpec((1, wblk), lambda b, h, t: (0, h))
    return pl.pallas_call(
        _wkv_prompt_body,
        grid=(B, A_HEADS // heads_per_step, T // tt),
        in_specs=[seq_spec] * 7 + [par_spec] * 3,
        out_specs=[seq_spec,
                   pl.BlockSpec((1, heads_per_step, A_HEAD_DIM, A_HEAD_DIM), lambda b, h, t: (b, h, 0, 0))],
        out_shape=[jax.ShapeDtypeStruct((B, T, A_WIDTH), BF16),
                   jax.ShapeDtypeStruct((B, A_HEADS, A_HEAD_DIM, A_HEAD_DIM), F32)],
        scratch_shapes=[pltpu.VMEM((heads_per_step, A_HEAD_DIM, A_HEAD_DIM), F32)],
        compiler_params=_cparams(3),
        name="wkv_prompt",
    )(*vals, _row(prm['rwkv_r_k'][i]), _row(prm['rwkv_gn_w'][i]), _row(prm['rwkv_gn_b'][i]))


def _lru_prompt_body(gate_ref, xb_ref, cw_ref, cb_ref, wa_ref, wx_ref, ba_ref, bx_ref, lam_ref,
                     y_ref, h_ref, cst_ref, buf, h_scr):
    t = pl.program_id(1)
    tt = xb_ref.shape[1]
    w = xb_ref.shape[2]

    @pl.when(t == 0)
    def _():
        buf[0:HIST, :] = jnp.zeros((HIST, w), F32)
        h_scr[...] = jnp.zeros(h_scr.shape, F32)

    buf[HIST:, :] = xb_ref[0]
    cw = cw_ref[0]
    xc = cb_ref[...] + buf[HIST:, :] * cw[3:4]
    for j in range(3):
        xc = xc + buf[HIST - 3 + j:HIST - 3 + j + tt, :] * cw[j:j + 1]
    a, b, gact = _lru_rows(gate_ref[0], xc, wa_ref, wx_ref, ba_ref[...], bx_ref[...], lam_ref[...])
    row = _iota2((tt, w), 0)
    d = 1
    while d < tt:
        a_sh = jnp.where(row >= d, pltpu.roll(a, d, axis=0), 1.0)
        b_sh = jnp.where(row >= d, pltpu.roll(b, d, axis=0), 0.0)
        b = a * b_sh + b
        a = a * a_sh
        d *= 2
    h = a * h_scr[0:1, :] + b
    y_ref[0] = (h * gact).astype(y_ref.dtype)
    h_last = h[tt - 1:tt, :]
    h_scr[0:1, :] = h_last
    h_ref[0] = h_last
    cst_ref[0] = buf[HIST + tt - 3:HIST + tt, :]
    buf[0:HIST, :] = buf[tt:tt + HIST, :]


def _block_diag_pairs(w):
    nb, n, _ = w.shape
    w = w.reshape(nb // 2, 2, n, n)
    z = jnp.zeros((nb // 2, n, n), w.dtype)
    top = jnp.concatenate([w[:, 0], z], axis=2)
    bot = jnp.concatenate([z, w[:, 1]], axis=2)
    return jnp.concatenate([top, bot], axis=1)


def _lru_prompt(pb, prm, i, *, tt):
    B, T, _ = pb.shape
    w = B_WIDTH
    ng = w // LANES
    full = lambda shape: pl.BlockSpec(shape, lambda b, t: (0,) * len(shape))
    return pl.pallas_call(
        _lru_prompt_body,
        grid=(B, T // tt),
        in_specs=[pl.BlockSpec((1, tt, w), lambda b, t: (b, t, 0)),
                  pl.BlockSpec((1, tt, w), lambda b, t: (b, t, 1)),
                  pl.BlockSpec((1, 4, w), lambda b, t: (i, 0, 0)),
                  full((1, w)), full((ng, LANES, LANES)), full((ng, LANES, LANES)),
                  full((1, w)), full((1, w)), full((1, w))],
        out_specs=[pl.BlockSpec((1, tt, w), lambda b, t: (b, t, 0)),
                   pl.BlockSpec((1, 1, w), lambda b, t: (b, 0, 0)),
                   pl.BlockSpec((1, 3, w), lambda b, t: (b, 0, 0))],
        out_shape=[jax.ShapeDtypeStruct((B, T, w), BF16),
                   jax.ShapeDtypeStruct((B, 1, w), F32),
                   jax.ShapeDtypeStruct((B, 3, w), F32)],
        scratch_shapes=[pltpu.VMEM((HIST + tt, w), F32), pltpu.VMEM((SUBLANES, w), F32)],
        compiler_params=_cparams(2),
        name="lru_prompt",
    )(pb, pb, prm['lru_conv_w'], _row(prm['lru_conv_b'][i]), _block_diag_pairs(prm['lru_wa'][i]),
      _block_diag_pairs(prm['lru_wx'][i]), _row(prm['lru_ba'][i]), _row(prm['lru_bx'][i]),
      _row(prm['lru_lambda'][i]))


def _gdn_prep_prompt_body(x_ref, cw_ref, o_ref, st_ref, buf):
    j = pl.program_id(1)
    t = pl.program_id(2)
    tt = x_ref.shape[1]

    @pl.when(t == 0)
    def _():
        buf[0:HIST, :] = jnp.zeros((HIST, buf.shape[1]), F32)

    buf[HIST:, :] = x_ref[0]
    cw = cw_ref[0]
    y = buf[HIST:, :] * cw[3:4]
    for jj in range(3):
        y = y + buf[HIST - 3 + jj:HIST - 3 + jj + tt, :] * cw[jj:jj + 1]
    n_qk_blocks = 2 * C_KEY_DIM // x_ref.shape[2]
    q_scale = jnp.where(j < n_qk_blocks // 2, C_HEAD ** -0.5, 1.0).astype(F32)
    o_ref[0] = _gdn_act(y, j < n_qk_blocks, q_scale)
    st_ref[0] = buf[HIST + tt - 3:HIST + tt, :]
    buf[0:HIST, :] = buf[tt:tt + HIST, :]


def _gdn_prep_prompt(qkvz, conv_w, i, *, tt, tc):
    B, T, _ = qkvz.shape
    return pl.pallas_call(
        _gdn_prep_prompt_body,
        grid=(B, C_QKV_DIM // tc, T // tt),
        in_specs=[pl.BlockSpec((1, tt, tc), lambda b, j, t: (b, t, j)),
                  pl.BlockSpec((1, 4, tc), lambda b, j, t: (i, 0, j))],
        out_specs=[pl.BlockSpec((1, tt, tc), lambda b, j, t: (b, t, j)),
                   pl.BlockSpec((1, 3, tc), lambda b, j, t: (b, 0, j))],
        out_shape=[jax.ShapeDtypeStruct((B, T, C_QKV_DIM), F32),
                   jax.ShapeDtypeStruct((B, 3, C_QKV_DIM), F32)],
        scratch_shapes=[pltpu.VMEM((HIST + tt, tc), F32)],
        compiler_params=_cparams(3),
        name="gdn_prep_prompt",
    )(qkvz, conv_w)


def _gdn_gates(ba, a_log, dt_bias):
    hv = C_V_HEADS
    beta = jax.nn.sigmoid(ba[..., :hv])
    g = -jnp.exp(a_log) * jax.nn.softplus(ba[..., hv:2 * hv] + dt_bias)
    return beta, g


def _lane_bcast_col(x, col):
    n = x.shape[1]
    onehot = _iota2((n, LANES), 0) == col
    return _sel_r(x, onehot)


def _gdn_prompt_body(q_ref, k_ref, v_ref, z_ref, ba_ref, alog_ref, dtb_ref, nw_ref, o_ref, st_ref, s_scr):
    hblk = pl.program_id(1)
    t = pl.program_id(2)
    tt = q_ref.shape[1]
    C = CHUNK
    nc = tt // C
    rep = C_V_HEADS // C_K_HEADS
    n_kh = q_ref.shape[2] // C_HEAD
    nh = n_kh * rep

    @pl.when(t == 0)
    def _():
        s_scr[...] = jnp.zeros(s_scr.shape, F32)

    beta_all, g_all = _gdn_gates(ba_ref[0], alog_ref[...], dtb_ref[...])
    gc_all = _sel_l(_block_tri(tt, C), g_all)
    blocks = lambda x: x.reshape(nc, C, x.shape[-1])
    qs, ks, vs, betas, gcs = [], [], [], [], []
    for kh in range(n_kh):
        q_h = blocks(q_ref[0, :, kh * C_HEAD:(kh + 1) * C_HEAD])
        k_h = blocks(k_ref[0, :, kh * C_HEAD:(kh + 1) * C_HEAD])
        for jv in range(rep):
            hu = kh * rep + jv
            hv = hblk * nh + hu
            qs.append(q_h)
            ks.append(k_h)
            vs.append(blocks(v_ref[0, :, hu * C_HEAD:(hu + 1) * C_HEAD]))
            betas.append(blocks(_lane_bcast_col(beta_all, hv)))
            gcs.append(blocks(_lane_bcast_col(gc_all, hv)))
    stack = lambda xs: jnp.stack(xs, axis=1).reshape(nc * nh, C, xs[0].shape[-1])
    q, k, v, beta, gc = map(stack, (qs, ks, vs, betas, gcs))
    incl = _tri(C, False)
    strict = _tri(C, True)

    gcc = gc[:, :, :C]
    diff = gcc - jnp.swapaxes(gcc, 1, 2)
    dec_incl = jnp.exp(jnp.where(incl, diff, -jnp.inf))
    dec_strict = jnp.where(strict, dec_incl, 0.0)
    kb = k * beta
    gram = _bdot(jnp.concatenate([kb, q], axis=1), k, _BNT)
    low = gram[:, :C] * dec_strict
    qk = gram[:, C:] * dec_incl
    tinv = _unit_lower_inverse(low, _dot3)
    eg = jnp.exp(gc)
    sol = _dot3(tinv, jnp.concatenate([v * beta, kb * eg], axis=2), _BNN)
    u = sol[:, :, :C_HEAD]
    lhs_s = jnp.concatenate([sol[:, :, C_HEAD:], q * eg], axis=1)
    g_last = gc[:, C - 1:C, :]
    k_tail = k * jnp.exp(g_last - gc)
    e_last = jnp.exp(g_last)

    S = s_scr[...]
    os_ = []
    for c in range(nc):
        sl = slice(c * nh, (c + 1) * nh)
        ws = _bdot(lhs_s[sl], S, _BNN)
        v_new = u[sl] - ws[:, :C]
        os_.append(ws[:, C:] + _bdot(qk[sl], v_new, _BNN))
        S = S * e_last[sl] + _bdot(k_tail[sl], v_new, _BTN)
    s_scr[...] = S

    outs = []
    for hu in range(nh):
        cols = slice(hu * C_HEAD, (hu + 1) * C_HEAD)
        o = jnp.concatenate([os_[c][hu] for c in range(nc)], axis=0)
        outs.append(_gdn_head_out(o, z_ref[0, :, cols], nw_ref[...]))
    o_ref[0] = jnp.concatenate(outs, axis=1).astype(o_ref.dtype)

    @pl.when(t == pl.num_programs(2) - 1)
    def _():
        st_ref[0] = S


def _gdn_prompt(qkv_act, qkvz, ba, prm, i, *, tt, kheads_per_step):
    B, T, _ = qkv_act.shape
    rep = C_V_HEADS // C_K_HEADS
    kw = kheads_per_step * C_HEAD
    vw = rep * kw
    k_blk0 = C_KEY_DIM // kw
    v_blk0 = 2 * C_KEY_DIM // vw
    z_blk0 = C_QKV_DIM // vw
    nh = kheads_per_step * rep
    full = lambda shape: pl.BlockSpec(shape, lambda b, h, t: (0,) * len(shape))
    return pl.pallas_call(
        _gdn_prompt_body,
        grid=(B, C_K_HEADS // kheads_per_step, T // tt),
        in_specs=[pl.BlockSpec((1, tt, kw), lambda b, h, t: (b, t, h)),
                  pl.BlockSpec((1, tt, kw), lambda b, h, t: (b, t, k_blk0 + h)),
                  pl.BlockSpec((1, tt, vw), lambda b, h, t: (b, t, v_blk0 + h)),
                  pl.BlockSpec((1, tt, vw), lambda b, h, t: (b, t, z_blk0 + h)),
                  pl.BlockSpec((1, tt, 2 * C_V_HEADS), lambda b, h, t: (b, t, 0)),
                  full((1, C_V_HEADS)), full((1, C_V_HEADS)), full((1, C_HEAD))],
        out_specs=[pl.BlockSpec((1, tt, vw), lambda b, h, t: (b, t, h)),
                   pl.BlockSpec((1, nh, C_HEAD, C_HEAD), lambda b, h, t: (b, h, 0, 0))],
        out_shape=[jax.ShapeDtypeStruct((B, T, C_VAL_DIM), BF16),
                   jax.ShapeDtypeStruct((B, C_V_HEADS, C_HEAD, C_HEAD), F32)],
        scratch_shapes=[pltpu.VMEM((nh, C_HEAD, C_HEAD), F32)],
        compiler_params=_cparams(3),
        name="gdn_prompt",
    )(qkv_act, qkv_act, qkv_act, qkvz, ba, _row(prm['gdn_a_log'][i]), _row(prm['gdn_dt_bias'][i]),
      _row(prm['gdn_norm_w'][i]))


def _rwkv_prep_sample_body(pm_ref, pl_ref, qm_ref, ql_ref, mum_ref, mul_ref, w0_ref, w2_ref, a0_ref, a2_ref,
                           g2_ref, kk_ref, ka_ref, r_o, kf_o, v_o, kkn_o, a_o, lw_o, g_o):
    outs = _rwkv_rows(pm_ref[...], qm_ref[...], pl_ref[...][:, :LORA_ALL], ql_ref[...], mum_ref[...],
                      mul_ref[...], w0_ref[...], w2_ref[0], a0_ref[...], a2_ref[0], g2_ref[0],
                      kk_ref[...], ka_ref[...])
    for o_ref, val in zip((r_o, kf_o, v_o, kkn_o, a_o, lw_o, g_o), outs):
        o_ref[...] = val


def _rwkv_prep_sample(p_main, p_lora, shift, prm, i):
    bs = p_main.shape[0]
    mu = prm['rwkv_mu'][i]
    full = lambda shape: pl.BlockSpec(shape, lambda s: (0,) * len(shape))
    lay3 = lambda shape: pl.BlockSpec((1,) + shape, lambda s: (i, 0, 0))
    return pl.pallas_call(
        _rwkv_prep_sample_body,
        grid=(1,),
        in_specs=[full(p_main.shape), full(p_lora.shape), full((bs, P_A_MAIN)), full((bs, LORA_ALL)),
                  full((1, P_A_MAIN)), full((1, LORA_ALL)), full((1, A_WIDTH)),
                  lay3((LORA_W, A_WIDTH)), full((1, A_WIDTH)), lay3((LORA_A, A_WIDTH)),
                  lay3((LORA_G, A_WIDTH)), full((1, A_WIDTH)), full((1, A_WIDTH))],
        out_specs=[full((bs, A_WIDTH))] * 7,
        out_shape=[jax.ShapeDtypeStruct((bs, A_WIDTH), F32)] * 7,
        compiler_params=_cparams(1),
        name="rwkv_prep_sample",
    )(p_main, p_lora, shift[:, :P_A_MAIN], shift[:, P_A_MAIN:], _row(mu[:P_A_MAIN]), _row(mu[P_A_MAIN:]),
      _row(prm['rwkv_w0'][i]), prm['rwkv_w2'], _row(prm['rwkv_a0'][i]), prm['rwkv_a2'], prm['rwkv_g2'],
      _row(prm['rwkv_k_k'][i]), _row(prm['rwkv_k_a'][i]))


STEP_GROUP = 16


def _wkv_step_body(s_ref, r_ref, kf_ref, v_ref, kkn_ref, a_ref, lw_ref, g_ref, rk_ref, gw_ref, gb_ref,
                   ns_ref, y_ref):
    nh = s_ref.shape[0]
    n = A_HEAD_DIM
    G = STEP_GROUP
    rid = lax.broadcasted_iota(jnp.int32, (G, SUBLANES, n), 1)
    up = lambda x: jnp.broadcast_to(x.reshape(G, 1, n), (G, SUBLANES, n))

    def group(i, carry):
        g0 = pl.multiple_of(i * G, G)
        rows = lambda ref: ref[pl.ds(g0, G), :]
        S = s_ref[pl.ds(g0, G)]
        r, kf, v, kkn, a = rows(r_ref), rows(kf_ref), rows(v_ref), rows(kkn_ref), rows(a_ref)
        kk = kkn * lax.rsqrt(jnp.sum(kkn * kkn, axis=-1, keepdims=True) + 1e-12)
        u = _dot3(up(kk), S, _BNT)
        left = jnp.where(rid == 0, -u, jnp.where(rid == 1, up(v), 0.0))
        right = jnp.where(rid == 0, up(kk * a), jnp.where(rid == 1, up(kf), 0.0))
        s_new = S * jnp.exp(rows(lw_ref)).reshape(G, 1, n) + _dot3(left, right, _BTN)
        ns_ref[pl.ds(g0, G)] = s_new
        y = _dot3(up(r), s_new, _BNT)[:, 0, :]
        y_ref[pl.ds(g0, G), :] = _rwkv_head_out(y, r, kf, v, rows(g_ref), rows(rk_ref), rows(gw_ref),
                                                rows(gb_ref))
        return carry

    lax.fori_loop(0, nh // G, group, 0)


def _wkv_step(S, vals, prm, i, *, nh):
    n_tot = S.shape[0]
    n = A_HEAD_DIM
    tile = lambda p: jnp.tile(p.reshape(A_HEADS, n), (nh // A_HEADS, 1))
    row_spec = pl.BlockSpec((nh, n), lambda b: (b, 0))
    par_spec = pl.BlockSpec((nh, n), lambda b: (0, 0))
    s_spec = pl.BlockSpec((nh, n, n), lambda b: (b, 0, 0))
    return pl.pallas_call(
        _wkv_step_body,
        grid=(n_tot // nh,),
        in_specs=[s_spec] + [row_spec] * 7 + [par_spec] * 3,
        out_specs=[s_spec, row_spec],
        out_shape=[jax.ShapeDtypeStruct(S.shape, F32), jax.ShapeDtypeStruct((n_tot, n), F32)],
        compiler_params=_cparams(1),
        name="wkv_step",
    )(S, *vals, tile(prm['rwkv_r_k'][i]), tile(prm['rwkv_gn_w'][i]), tile(prm['rwkv_gn_b'][i]))


def _lru_sample_body(gate_ref, xb_ref, cst_ref, h0_ref, cw_ref, cb_ref, wa_ref, wx_ref, ba_ref, bx_ref, lam_ref,
                     y_ref, h_ref, ncst_ref):
    cw = cw_ref[0]
    xb = xb_ref[...]
    xc = cb_ref[...] + xb * cw[3:4]
    for j in range(3):
        xc = xc + cst_ref[:, j, :] * cw[j:j + 1]
    a, b, gact = _lru_rows(gate_ref[...], xc, wa_ref, wx_ref, ba_ref[...], bx_ref[...], lam_ref[...])
    h = a * h0_ref[...] + b
    y_ref[...] = (h * gact).astype(y_ref.dtype)
    h_ref[...] = h
    ncst_ref[:, 0, :] = cst_ref[:, 1, :]
    ncst_ref[:, 1, :] = cst_ref[:, 2, :]
    ncst_ref[:, 2, :] = xb


def _lru_sample(pb, conv_state, h0, prm, i):
    bs = pb.shape[0]
    w = B_WIDTH
    ng = w // LANES
    full = lambda shape: pl.BlockSpec(shape, lambda s: (0,) * len(shape))
    return pl.pallas_call(
        _lru_sample_body,
        grid=(1,),
        in_specs=[pl.BlockSpec((bs, w), lambda s: (0, 0)), pl.BlockSpec((bs, w), lambda s: (0, 1)),
                  full((bs, 3, w)), full((bs, w)),
                  pl.BlockSpec((1, 4, w), lambda s: (i, 0, 0)),
                  full((1, w)), full((ng, LANES, LANES)), full((ng, LANES, LANES)),
                  full((1, w)), full((1, w)), full((1, w))],
        out_specs=[full((bs, w)), full((bs, w)), full((bs, 3, w))],
        out_shape=[jax.ShapeDtypeStruct((bs, w), BF16), jax.ShapeDtypeStruct((bs, w), F32),
                   jax.ShapeDtypeStruct((bs, 3, w), F32)],
        compiler_params=_cparams(1),
        name="lru_sample",
    )(pb, pb, conv_state, h0, prm['lru_conv_w'], _row(prm['lru_conv_b'][i]),
      _block_diag_pairs(prm['lru_wa'][i]), _block_diag_pairs(prm['lru_wx'][i]), _row(prm['lru_ba'][i]),
      _row(prm['lru_bx'][i]), _row(prm['lru_lambda'][i]))


def _gdn_prep_sample_body(x_ref, cst_ref, cw_ref, o_ref, ncst_ref):
    j = pl.program_id(0)
    cw = cw_ref[0]
    x = x_ref[...]
    y = x * cw[3:4]
    for jj in range(3):
        y = y + cst_ref[:, jj, :] * cw[jj:jj + 1]
    n_qk_blocks = 2 * C_KEY_DIM // x.shape[1]
    q_scale = jnp.where(j < n_qk_blocks // 2, C_HEAD ** -0.5, 1.0).astype(F32)
    o_ref[...] = _gdn_act(y, j < n_qk_blocks, q_scale)
    ncst_ref[:, 0, :] = cst_ref[:, 1, :]
    ncst_ref[:, 1, :] = cst_ref[:, 2, :]
    ncst_ref[:, 2, :] = x


def _gdn_prep_sample(qkvz, conv_state, conv_w, i, *, tc):
    bs = qkvz.shape[0]
    return pl.pallas_call(
        _gdn_prep_sample_body,
        grid=(C_QKV_DIM // tc,),
        in_specs=[pl.BlockSpec((bs, tc), lambda j: (0, j)),
                  pl.BlockSpec((bs, 3, tc), lambda j: (0, 0, j)),
                  pl.BlockSpec((1, 4, tc), lambda j: (i, 0, j))],
        out_specs=[pl.BlockSpec((bs, tc), lambda j: (0, j)),
                   pl.BlockSpec((bs, 3, tc), lambda j: (0, 0, j))],
        out_shape=[jax.ShapeDtypeStruct((bs, C_QKV_DIM), F32),
                   jax.ShapeDtypeStruct((bs, 3, C_QKV_DIM), F32)],
        compiler_params=_cparams(1),
        name="gdn_prep_sample",
    )(qkvz, conv_state, conv_w)


def _gdn_step_body(s_ref, q_ref, k_ref, v_ref, z_ref, ba_ref, alog_ref, dtb_ref, nw_ref, ns_ref, o_ref):
    bb = s_ref.shape[0]
    rep = C_V_HEADS // C_K_HEADS
    hk = pl.program_id(1)
    beta_all, g_all = _gdn_gates(ba_ref[...], alog_ref[...], dtb_ref[...])
    lane = lax.broadcasted_iota(jnp.int32, beta_all.shape, 2)
    pick = lambda x, col: jnp.sum(jnp.where(lane == col, x, 0.0), axis=-1, keepdims=True)
    rid = lax.broadcasted_iota(jnp.int32, (bb, SUBLANES, C_HEAD), 1)
    up = lambda x: jnp.broadcast_to(x, (bb, SUBLANES, C_HEAD))
    q = q_ref[...]
    k = k_ref[...]
    qk = jnp.sum(q * k, axis=-1, keepdims=True)
    for jv in range(rep):
        cols = slice(jv * C_HEAD, (jv + 1) * C_HEAD)
        hv = hk * rep + jv
        beta = pick(beta_all, hv)
        eg = jnp.exp(pick(g_all, hv))
        v = v_ref[:, :, cols]
        S = s_ref[:, jv]
        kb = k * beta
        lhs = jnp.where(rid == 0, up(kb * eg), jnp.where(rid == 1, up(q * eg), 0.0))
        ws = _dot3(lhs, S, _BNN)
        v_new = v * beta - ws[:, 0:1]
        o = ws[:, 1:2] + qk * v_new
        left = jnp.where(rid == 0, up(k), 0.0)
        right = jnp.where(rid == 0, up(v_new), 0.0)
        ns_ref[:, jv] = S * eg + _dot3(left, right, _BTN)
        o_ref[:, :, cols] = _gdn_head_out(o, z_ref[:, :, cols], nw_ref[...]).astype(o_ref.dtype)


def _gdn_step(S, qkv_act, qkvz, ba, prm, i, *, bb):
    bs = S.shape[0]
    rep = C_V_HEADS // C_K_HEADS
    vw = rep * C_HEAD
    v_blk0 = 2 * C_KEY_DIM // vw
    z_blk0 = C_QKV_DIM // vw
    full = lambda shape: pl.BlockSpec(shape, lambda b, h: (0,) * len(shape))
    s_spec = pl.BlockSpec((bb, rep, C_HEAD, C_HEAD), lambda b, h: (b, h, 0, 0))
    return pl.pallas_call(
        _gdn_step_body,
        grid=(bs // bb, C_K_HEADS),
        in_specs=[s_spec,
                  pl.BlockSpec((bb, 1, C_HEAD), lambda b, h: (b, 0, h)),
                  pl.BlockSpec((bb, 1, C_HEAD), lambda b, h: (b, 0, C_K_HEADS + h)),
                  pl.BlockSpec((bb, 1, vw), lambda b, h: (b, 0, v_blk0 + h)),
                  pl.BlockSpec((bb, 1, vw), lambda b, h: (b, 0, z_blk0 + h)),
                  pl.BlockSpec((bb, 1, 2 * C_V_HEADS), lambda b, h: (b, 0, 0)),
                  full((1, C_V_HEADS)), full((1, C_V_HEADS)), full((1, C_HEAD))],
        out_specs=[s_spec, pl.BlockSpec((bb, 1, vw), lambda b, h: (b, 0, h))],
        out_shape=[jax.ShapeDtypeStruct(S.shape, F32), jax.ShapeDtypeStruct((bs, 1, C_VAL_DIM), BF16)],
        compiler_params=_cparams(2),
        name="gdn_step",
    )(S, qkv_act, qkv_act, qkv_act, qkvz, ba, _row(prm['gdn_a_log'][i]), _row(prm['gdn_dt_bias'][i]),
      _row(prm['gdn_norm_w'][i]))


MOD_SH1, MOD_SC1, MOD_GT1, MOD_SH2, MOD_SC2, MOD_GT2 = range(6)


def _in_proj_ab(h, prm, i, tm):
    w = prm['w_in_ab'][i]
    p_main = _mm([h], [(w, 0)], n_out=P_A_MAIN, tm=tm, name="in_ab_main")
    p_lora = _mm([h], [(w, 0)], n_out=3 * LANES, col0=P_A_MAIN, tm=tm, tn=3 * LANES, name="in_ab_lora")
    pb = _mm([h], [(w[:, P_A:], 0)], n_out=2 * B_WIDTH, tm=tm, name="in_ab_lru")
    return p_main, p_lora, pb


def _in_proj_c(h, prm, i, tm):
    w = prm['w_in_c'][i]
    qkvz = _mm([h], [(w, 0)], n_out=C_QKV_DIM + C_VAL_DIM, tm=tm, name="in_c_main")
    ba = _mm([h], [(w[:, C_QKV_DIM + C_VAL_DIM:], 0)], n_out=2 * C_V_HEADS, tm=tm, tn=2 * C_V_HEADS,
             name="in_c_gates")
    return qkvz, ba


def _ffn_down(act, prm, layer, x, mod, tm):
    return _mm([act], [(prm['ffn_w_down'][layer], 0)], n_out=D_MODEL, tm=tm, res=x, gate=mod, gate_blk=MOD_GT2,
               name="ffn_down")


def _trunk_prompt(x, mods, prm):
    B, T, _ = x.shape
    tm, tt = min(1024, T), min(256, T)
    wkv, shift, lru_h, lru_conv, gdn, gdn_conv, ffn_conv = [], [], [], [], [], [], []
    for layer in range(DEPTH):
        mod = mods[layer].reshape(B, 1, 6 * D_MODEL)
        i = layer // 2
        h = _norm(x, prm['norm_mix'][layer], mod, MOD_SC1, MOD_SH1, tt=tt)
        if layer % 2 == 0:
            p_main, p_lora, pb = _in_proj_ab(h, prm, i, tm)
            vals = _rwkv_prep_prompt(p_main, p_lora, prm, i, tt=tt)
            ya, s_new = _wkv_prompt(vals, prm, i, tt=tt, heads_per_step=4)
            yb, h_last, cst = _lru_prompt(pb, prm, i, tt=tt)
            w_out = prm['w_out_ab'][i]
            x = _mm([ya, yb], [(w_out, 0), (w_out, 1)], n_out=D_MODEL, tm=tm, res=x, gate=mod, gate_blk=MOD_GT1)
            wkv.append(s_new)
            shift.append(jnp.concatenate([p_main[:, -1], p_lora[:, -1, :LORA_ALL]], axis=-1))
            lru_h.append(h_last[:, 0])
            lru_conv.append(cst)
        else:
            qkvz, ba = _in_proj_c(h, prm, i, tm)
            qkv_act, cst = _gdn_prep_prompt(qkvz, prm['gdn_conv_w'], i, tt=tt, tc=2048)
            yc, s_new = _gdn_prompt(qkv_act, qkvz, ba, prm, i, tt=tt, kheads_per_step=2)
            x = _mm([yc], [(prm['w_out_c'][i], 0)], n_out=D_MODEL, tm=tm, res=x, gate=mod, gate_blk=MOD_GT1)
            gdn.append(s_new)
            gdn_conv.append(cst)
        h = _norm(x, prm['norm_ffn'][layer], mod, MOD_SC2, MOD_SH2, tt=tt)
        act, fst = _ffn_up_prompt(h, prm['ffn_w_gate'], prm['ffn_w_up'], prm['ffn_conv_w'], prm['ffn_conv_b'],
                                  layer, tm=tm, tn=512)
        x = _ffn_down(act, prm, layer, x, mod, tm)
        ffn_conv.append(fst)
    y = _norm(x, prm['norm_out'], tt=tt, out_dtype=F32)
    stk = jnp.stack
    return y, stk(wkv), stk(shift), stk(lru_h), stk(lru_conv), stk(gdn), stk(gdn_conv), stk(ffn_conv)


def _trunk_sample(x, mods, st, prm):
    bs = x.shape[0]
    x = x.reshape(1, bs, D_MODEL)
    wkv0, shift0, lru_h0, lru_conv0, gdn0, gdn_conv0, ffn_conv0 = st
    wkv, shift, lru_h, lru_conv, gdn, gdn_conv, ffn_conv = [], [], [], [], [], [], []
    for layer in range(DEPTH):
        mod = mods[layer].reshape(1, bs, 6 * D_MODEL)
        i = layer // 2
        h = _norm(x, prm['norm_mix'][layer], mod, MOD_SC1, MOD_SH1, tt=bs)
        if layer % 2 == 0:
            p_main, p_lora, pb = _in_proj_ab(h, prm, i, bs)
            p_main, p_lora, pb = p_main[0], p_lora[0], pb[0]
            vals = _rwkv_prep_sample(p_main, p_lora, shift0[i], prm, i)
            heads = lambda a: a.reshape(bs * A_HEADS, A_HEAD_DIM)
            s_new, ya = _wkv_step(wkv0[i].reshape(bs * A_HEADS, A_HEAD_DIM, A_HEAD_DIM),
                                  [heads(a) for a in vals], prm, i, nh=8 * A_HEADS)
            ya = ya.reshape(1, bs, A_WIDTH).astype(BF16)
            yb, h_last, cst = _lru_sample(pb, lru_conv0[i], lru_h0[i], prm, i)
            w_out = prm['w_out_ab'][i]
            x = _mm([ya, yb[None]], [(w_out, 0), (w_out, 1)], n_out=D_MODEL, tm=bs, res=x, gate=mod,
                    gate_blk=MOD_GT1)
            wkv.append(s_new.reshape(wkv0[i].shape))
            shift.append(jnp.concatenate([p_main, p_lora[:, :LORA_ALL]], axis=-1))
            lru_h.append(h_last)
            lru_conv.append(cst)
        else:
            qkvz, ba = _in_proj_c(h, prm, i, bs)
            qkv_act, cst = _gdn_prep_sample(qkvz[0], gdn_conv0[i], prm['gdn_conv_w'], i, tc=2048)
            s_new, yc = _gdn_step(gdn0[i], qkv_act[:, None], qkvz[0][:, None], ba[0][:, None], prm, i,
                                  bb=min(STEP_GROUP, bs))
            x = _mm([yc.reshape(1, bs, C_VAL_DIM)], [(prm['w_out_c'][i], 0)], n_out=D_MODEL, tm=bs, res=x,
                    gate=mod, gate_blk=MOD_GT1)
            gdn.append(s_new)
            gdn_conv.append(cst)
        h = _norm(x, prm['norm_ffn'][layer], mod, MOD_SC2, MOD_SH2, tt=bs)
        act, fst = _ffn_up_sample(h[0], prm['ffn_w_gate'], prm['ffn_w_up'], prm['ffn_conv_w'],
                                  prm['ffn_conv_b'], ffn_conv0[layer], layer, tn=512)
        x = _ffn_down(act[None], prm, layer, x, mod, bs)
        ffn_conv.append(fst)
    y = _norm(x, prm['norm_out'], tt=bs, out_dtype=F32).reshape(bs, 1, D_MODEL)
    stk = jnp.stack
    return y, stk(wkv), stk(shift), stk(lru_h), stk(lru_conv), stk(gdn), stk(gdn_conv), stk(ffn_conv)


def kernel(x_prompt, x_sample, c_prompt, c_sample, state_rwkv_wkv, state_rwkv_shift, state_lru_h, state_lru_conv, state_gdn, state_gdn_conv, state_ffn_conv, w_mod, b_mod, norm_mix, norm_ffn, norm_out, w_in_ab, w_out_ab, rwkv_mu, rwkv_w0, rwkv_w2, rwkv_a0, rwkv_a2, rwkv_g2, rwkv_k_k, rwkv_k_a, rwkv_r_k, rwkv_gn_w, rwkv_gn_b, lru_conv_w, lru_conv_b, lru_wa, lru_ba, lru_wx, lru_bx, lru_lambda, w_in_c, w_out_c, gdn_conv_w, gdn_a_log, gdn_dt_bias, gdn_norm_w, ffn_w_gate, ffn_w_up, ffn_conv_w, ffn_conv_b, ffn_w_down):
    prm = dict(norm_mix=norm_mix, norm_ffn=norm_ffn, norm_out=norm_out, w_in_ab=w_in_ab, w_out_ab=w_out_ab,
               rwkv_mu=rwkv_mu, rwkv_w0=rwkv_w0, rwkv_w2=rwkv_w2, rwkv_a0=rwkv_a0, rwkv_a2=rwkv_a2,
               rwkv_g2=rwkv_g2, rwkv_k_k=rwkv_k_k, rwkv_k_a=rwkv_k_a,
               rwkv_r_k=rwkv_r_k.reshape(rwkv_r_k.shape[0], A_WIDTH), rwkv_gn_w=rwkv_gn_w,
               rwkv_gn_b=rwkv_gn_b, lru_conv_w=lru_conv_w, lru_conv_b=lru_conv_b, lru_wa=lru_wa, lru_ba=lru_ba,
               lru_wx=lru_wx, lru_bx=lru_bx, lru_lambda=lru_lambda, w_in_c=w_in_c, w_out_c=w_out_c,
               gdn_conv_w=gdn_conv_w, gdn_a_log=gdn_a_log, gdn_dt_bias=gdn_dt_bias, gdn_norm_w=gdn_norm_w,
               ffn_w_gate=ffn_w_gate, ffn_w_up=ffn_w_up, ffn_conv_w=ffn_conv_w, ffn_conv_b=ffn_conv_b,
               ffn_w_down=ffn_w_down)
    mods_p, mods_s = _modulation(c_prompt, c_sample, w_mod, b_mod)
    outs_p = _trunk_prompt(x_prompt, mods_p, prm)
    outs_s = _trunk_sample(x_sample, mods_s,
                           (state_rwkv_wkv, state_rwkv_shift, state_lru_h, state_lru_conv, state_gdn,
                            state_gdn_conv, state_ffn_conv), prm)
    return (outs_p[0], outs_s[0]) + tuple(outs_p[1:]) + tuple(outs_s[1:])
```

```python
import functools

import jax
import jax.numpy as jnp
from jax import lax
from jax.experimental import pallas as pl
from jax.experimental.pallas import tpu as pltpu

F32 = jnp.float32
BF16 = jnp.bfloat16

D_MODEL = 2048
DEPTH = 4
A_HEAD_DIM = 64
A_WIDTH = D_MODEL // 2
A_HEADS = A_WIDTH // A_HEAD_DIM
LORA_W = 64
LORA_A = 64
LORA_G = 160
LORA_ALL = LORA_W + LORA_A + LORA_G
P_A_MAIN = 3 * A_WIDTH
P_A = P_A_MAIN + LORA_ALL
A_GN_EPS = 64e-5
B_WIDTH = D_MODEL - A_WIDTH
B_BLOCK_DIM = 64
LRU_C = 8.0
C_HEAD = 128
C_K_HEADS = D_MODEL // C_HEAD
C_V_HEADS = 2 * C_K_HEADS
C_KEY_DIM = C_K_HEADS * C_HEAD
C_VAL_DIM = C_V_HEADS * C_HEAD
C_QKV_DIM = 2 * C_KEY_DIM + C_VAL_DIM
D_FF = 5632
NORM_EPS = 1e-6

LANES = 128
SUBLANES = 8
HIST = SUBLANES
CHUNK = 64
VMEM_LIMIT = 56 * 1024 * 1024
MM_VMEM_BUDGET = 46 * 1024 * 1024

_NN = (((1,), (0,)), ((), ()))
_NT = (((1,), (1,)), ((), ()))
_TN = (((0,), (0,)), ((), ()))
_BNN = (((2,), (1,)), ((0,), (0,)))
_BNT = (((2,), (2,)), ((0,), (0,)))
_BTN = (((1,), (1,)), ((0,), (0,)))


def _cparams(n_axes):
    return pltpu.CompilerParams(dimension_semantics=("arbitrary",) * n_axes,
                                vmem_limit_bytes=VMEM_LIMIT)


def _bdot(a, b, dims=_NN):
    return lax.dot_general(a.astype(BF16), b.astype(BF16), dims, preferred_element_type=F32)


def _split2(a):
    hi = a.astype(BF16)
    lo = (a - hi.astype(F32)).astype(BF16)
    return hi, lo


def _dot3(a, b, dims=_NN):
    ah, al = _split2(a)
    bh, bl = _split2(b)
    d = lambda x, y: lax.dot_general(x, y, dims, preferred_element_type=F32)
    return d(ah, bh) + (d(ah, bl) + d(al, bh))


def _split3(a):
    h0 = a.astype(BF16)
    r1 = a - h0.astype(F32)
    h1 = r1.astype(BF16)
    h2 = (r1 - h1.astype(F32)).astype(BF16)
    return h0, h1, h2


def _sel_l(mask01, x, dims=_NN):
    m = mask01.astype(BF16)
    d = lambda y: lax.dot_general(m, y, dims, preferred_element_type=F32)
    h0, h1, h2 = _split3(x)
    return d(h0) + (d(h1) + d(h2))


def _sel_r(x, mask01):
    m = mask01.astype(BF16)
    d = lambda y: lax.dot_general(y, m, _NN, preferred_element_type=F32)
    h0, h1, h2 = _split3(x)
    return d(h0) + (d(h1) + d(h2))


def _iota2(shape, axis):
    return lax.broadcasted_iota(jnp.int32, shape, axis)


def _tri(n, strict):
    i = _iota2((n, n), 0)
    j = _iota2((n, n), 1)
    return (i > j) if strict else (i >= j)


def _unit_lower_inverse(low, mm):
    n = low.shape[-1]
    dims = _BNN if low.ndim == 3 else _NN
    eye = (_iota2((n, n), 0) == _iota2((n, n), 1)).astype(F32)
    m = -low
    inv = eye + m
    p = m
    span = 2
    while span < n:
        p = mm(p, p, dims)
        inv = inv + mm(inv, p, dims)
        span *= 2
    return inv


def _unit_lower_inverse_refined(low):
    n = low.shape[-1]
    dims = _BNN if low.ndim == 3 else _NN
    eye = (_iota2((n, n), 0) == _iota2((n, n), 1)).astype(F32)
    x0 = _unit_lower_inverse(low, _bdot)
    resid = eye - _dot3(eye + low, x0, dims)
    return x0 + _dot3(x0, resid, dims)


def _block_tri(tt, block):
    i = _iota2((tt, tt), 0)
    j = _iota2((tt, tt), 1)
    return (i // block == j // block) & (i >= j)


def _neg_expm1(x):
    return -jnp.tanh(0.5 * x) * (jnp.exp(x) + 1.0)


def _segment_sum(x, seg):
    i = _iota2((LANES, LANES), 0) // seg
    j = _iota2((LANES, LANES), 1) // seg
    same = (i == j)
    parts = [_sel_r(x[:, g * LANES:(g + 1) * LANES], same) for g in range(x.shape[1] // LANES)]
    return parts[0] if len(parts) == 1 else jnp.concatenate(parts, axis=1)


def _rwkv_rows(p_main, q_main, p_lora, q_lora, mu_main, mu_lora, w0, w2, a0, a2, g2, k_k, k_a):
    pm = p_main + (q_main - p_main) * mu_main
    pl_ = p_lora + (q_lora - p_lora) * mu_lora
    r = pm[:, :A_WIDTH]
    k = pm[:, A_WIDTH:2 * A_WIDTH]
    v = pm[:, 2 * A_WIDTH:]
    xw = pl_[:, :LORA_W]
    xa = pl_[:, LORA_W:LORA_W + LORA_A]
    xg = pl_[:, LORA_W + LORA_A:LORA_ALL]
    w_ll = -jax.nn.softplus(-(w0 + _bdot(jnp.tanh(xw), w2))) - 0.5
    lw = -jnp.exp(w_ll)
    a = jax.nn.sigmoid(a0 + _bdot(xa, a2))
    g = _bdot(jax.nn.sigmoid(xg), g2)
    kkn = k * k_k
    kf = k * (1.0 + (a - 1.0) * k_a)
    return r, kf, v, kkn, a, lw, g


def _rwkv_head_out(y, r, kf, v, g, r_k, gn_w, gn_b):
    mean = jnp.mean(y, axis=-1, keepdims=True)
    var = jnp.mean(jnp.square(y - mean), axis=-1, keepdims=True)
    yn = (y - mean) * lax.rsqrt(var + A_GN_EPS) * gn_w + gn_b
    bonus = jnp.sum(r * kf * r_k, axis=-1, keepdims=True) * v
    return (yn + bonus) * g


def _lru_rows(gate, xc, wa_bd, wx_bd, ba, bx, lam):
    ng = xc.shape[1] // LANES
    ra = jnp.concatenate([_bdot(xc[:, g * LANES:(g + 1) * LANES], wa_bd[g]) for g in range(ng)], axis=1)
    rx = jnp.concatenate([_bdot(xc[:, g * LANES:(g + 1) * LANES], wx_bd[g]) for g in range(ng)], axis=1)
    r_gate = jax.nn.sigmoid(ra + ba)
    i_gate = jax.nn.sigmoid(rx + bx)
    log_a = -LRU_C * r_gate * jax.nn.softplus(-lam)
    a = jnp.exp(log_a)
    b = jnp.sqrt(_neg_expm1(2.0 * log_a)) * (i_gate * xc)
    return a, b, jax.nn.gelu(gate)


def _gdn_act(y, is_qk, q_scale):
    y = jax.nn.silu(y)
    outs = []
    for h in range(y.shape[1] // C_HEAD):
        yh = y[:, h * C_HEAD:(h + 1) * C_HEAD]
        nrm = yh * lax.rsqrt(jnp.sum(yh * yh, axis=-1, keepdims=True) + 1e-6) * q_scale
        outs.append(jnp.where(is_qk, nrm, yh))
    return jnp.concatenate(outs, axis=1)


def _gdn_head_out(o, z, norm_w):
    o = o * lax.rsqrt(jnp.mean(o * o, axis=-1, keepdims=True) + NORM_EPS) * norm_w
    return o * jax.nn.silu(z)


def _mm_body(*refs, n_x, has_res):
    x_refs = refs[:n_x]
    w_refs = refs[n_x:2 * n_x]
    k = 2 * n_x
    if has_res:
        res_ref, gate_ref = refs[k], refs[k + 1]
        k += 2
    o_ref = refs[k]
    wbf_refs = refs[k + 1:k + 1 + n_x]

    @pl.when((pl.program_id(1) == 0) & (pl.program_id(2) == 0))
    def _():
        for w_ref, wbf_ref in zip(w_refs, wbf_refs):
            wbf_ref[...] = w_ref[...].astype(BF16)

    acc = None
    for x_ref, wbf_ref in zip(x_refs, wbf_refs):
        d = jnp.dot(x_ref[0], wbf_ref[...], preferred_element_type=F32)
        acc = d if acc is None else acc + d
    if has_res:
        acc = res_ref[0] + gate_ref[0] * acc
    o_ref[0] = acc.astype(o_ref.dtype)


def _pick_tiles(n_out, k_sum, rows, seq, n_out_bufs):
    best = None
    for tm in [t for t in (1024, 512, 256, 128) if seq % t == 0] or [seq]:
        for tn in (1024, 512, 256, 128):
            if n_out % tn:
                continue
            need = (2 * tm * k_sum * 2 + 2 * k_sum * tn * 4 + k_sum * tn * 2
                    + (2 * n_out_bufs + 1) * tm * tn * 4)
            if need > MM_VMEM_BUDGET:
                continue
            traffic = (n_out // tn) * rows * k_sum * 2
            key = (traffic, -tn, -tm)
            if best is None or key < best[0]:
                best = (key, tm, tn)
    if best is None:
        raise ValueError("no projection tile fits VMEM")
    return best[1], best[2]


def _mm(xs, ws, *, n_out, col0=0, tm=None, tn=None, out_dtype=F32, res=None, gate=None, gate_blk=0, name="proj"):
    B, T, _ = xs[0].shape
    ks = [x.shape[-1] for x in xs]
    has_res = res is not None
    if tn is None:
        tm, tn = _pick_tiles(n_out, sum(ks), B * T, T, 2 if has_res else 1)
    elif tm is None:
        tm = min(1024, T)
    assert n_out % tn == 0 and col0 % tn == 0 and T % tm == 0
    cb0 = col0 // tn
    in_specs = [pl.BlockSpec((1, tm, k), lambda j, b, t: (b, t, 0)) for k in ks]
    in_specs += [pl.BlockSpec((k, tn), functools.partial(lambda j, b, t, rb: (rb, cb0 + j), rb=rb))
                 for k, (_, rb) in zip(ks, ws)]
    args = list(xs) + [w for w, _ in ws]
    if has_res:
        in_specs.append(pl.BlockSpec((1, tm, tn), lambda j, b, t: (b, t, j)))
        gb0 = gate_blk * (n_out // tn)
        if gate.shape[1] == 1:
            in_specs.append(pl.BlockSpec((1, 1, tn), lambda j, b, t: (b, 0, gb0 + j)))
        else:
            in_specs.append(pl.BlockSpec((1, tm, tn), lambda j, b, t: (b, t, gb0 + j)))
        args += [res, gate]
    return pl.pallas_call(
        functools.partial(_mm_body, n_x=len(xs), has_res=has_res),
        grid=(n_out // tn, B, T // tm),
        in_specs=in_specs,
        out_specs=pl.BlockSpec((1, tm, tn), lambda j, b, t: (b, t, j)),
        out_shape=jax.ShapeDtypeStruct((B, T, n_out), out_dtype),
        scratch_shapes=[pltpu.VMEM((k, tn), BF16) for k in ks],
        compiler_params=_cparams(3),
        name=name,
    )(*args)


def _mod_body(c_ref, w_ref, b_ref, op_ref, os_ref):
    bs = os_ref.shape[1]
    y = _bdot(jax.nn.silu(c_ref[...]), w_ref[0]) + b_ref[0]
    os_ref[0] = y[:bs]
    op_ref[0] = y[bs:]


def _modulation(c_p, c_s, w_mod, b_mod):
    depth, d, n = w_mod.shape
    tn = 1024
    bp, bs = c_p.shape[0], c_s.shape[0]
    return pl.pallas_call(
        _mod_body,
        grid=(depth, n // tn),
        in_specs=[pl.BlockSpec((bs + bp, d), lambda l, j: (0, 0)),
                  pl.BlockSpec((1, d, tn), lambda l, j: (l, 0, j)),
                  pl.BlockSpec((1, 1, tn), lambda l, j: (l, 0, j))],
        out_specs=[pl.BlockSpec((1, bp, tn), lambda l, j: (l, 0, j)),
                   pl.BlockSpec((1, bs, tn), lambda l, j: (l, 0, j))],
        out_shape=[jax.ShapeDtypeStruct((depth, bp, n), F32),
                   jax.ShapeDtypeStruct((depth, bs, n), F32)],
        compiler_params=_cparams(2),
        name="modulation",
    )(jnp.concatenate([c_s, c_p], axis=0), w_mod, b_mod.reshape(depth, 1, n))


def _norm_body(*refs, has_mod):
    if has_mod:
        x_ref, g_ref, sc_ref, sh_ref, o_ref = refs
    else:
        x_ref, g_ref, o_ref = refs
    x = x_ref[0]
    y = x * lax.rsqrt(jnp.mean(x * x, axis=-1, keepdims=True) + NORM_EPS) * g_ref[...]
    if has_mod:
        y = y * (1.0 + sc_ref[0]) + sh_ref[0]
    o_ref[0] = y.astype(o_ref.dtype)


def _norm(x, gain, mod=None, sc_blk=0, sh_blk=0, *, tt, out_dtype=BF16):
    B, T, d = x.shape
    in_specs = [pl.BlockSpec((1, tt, d), lambda b, t: (b, t, 0)),
                pl.BlockSpec((1, d), lambda b, t: (0, 0))]
    args = [x, gain.reshape(1, d)]
    if mod is not None:
        for blk in (sc_blk, sh_blk):
            if mod.shape[1] == 1:
                in_specs.append(pl.BlockSpec((1, 1, d), functools.partial(lambda b, t, blk: (b, 0, blk), blk=blk)))
            else:
                in_specs.append(pl.BlockSpec((1, tt, d), functools.partial(lambda b, t, blk: (b, t, blk), blk=blk)))
        args += [mod, mod]
    return pl.pallas_call(
        functools.partial(_norm_body, has_mod=mod is not None),
        grid=(B, T // tt),
        in_specs=in_specs,
        out_specs=pl.BlockSpec((1, tt, d), lambda b, t: (b, t, 0)),
        out_shape=jax.ShapeDtypeStruct((B, T, d), out_dtype),
        compiler_params=_cparams(2),
        name="norm",
    )(*args)


def _ffn_up_prompt_body(h_ref, wg_ref, wu_ref, cw_ref, cb_ref, act_ref, st_ref, wg_bf, wu_bf, buf):
    t = pl.program_id(2)
    tm = h_ref.shape[1]

    @pl.when((pl.program_id(1) == 0) & (t == 0))
    def _():
        wg_bf[...] = wg_ref[0].astype(BF16)
        wu_bf[...] = wu_ref[0].astype(BF16)

    @pl.when(t == 0)
    def _():
        buf[0:HIST, :] = jnp.zeros((HIST, buf.shape[1]), F32)

    h = h_ref[0]
    buf[HIST:, :] = jnp.dot(h, wg_bf[...], preferred_element_type=F32)
    cw = cw_ref[0]
    u = (buf[HIST - 2:HIST - 2 + tm, :] * cw[0:1] + buf[HIST - 1:HIST - 1 + tm, :] * cw[1:2]
         + buf[HIST:, :] * cw[2:3] + cb_ref[0])
    up = jnp.dot(h, wu_bf[...], preferred_element_type=F32)
    act_ref[0] = (jax.nn.gelu(u) * up).astype(act_ref.dtype)
    st_ref[0] = buf[HIST + tm - 2:HIST + tm, :]
    buf[0:HIST, :] = buf[tm:tm + HIST, :]


def _ffn_up_prompt(h, w_gate, w_up, conv_w, conv_b, layer, *, tm, tn):
    B, T, d = h.shape
    n = w_gate.shape[-1]
    return pl.pallas_call(
        _ffn_up_prompt_body,
        grid=(n // tn, B, T // tm),
        in_specs=[pl.BlockSpec((1, tm, d), lambda j, b, t: (b, t, 0)),
                  pl.BlockSpec((1, d, tn), lambda j, b, t: (layer, 0, j)),
                  pl.BlockSpec((1, d, tn), lambda j, b, t: (layer, 0, j)),
                  pl.BlockSpec((1, 3, tn), lambda j, b, t: (layer, 0, j)),
                  pl.BlockSpec((1, 1, tn), lambda j, b, t: (layer, 0, j))],
        out_specs=[pl.BlockSpec((1, tm, tn), lambda j, b, t: (b, t, j)),
                   pl.BlockSpec((1, 2, tn), lambda j, b, t: (b, 0, j))],
        out_shape=[jax.ShapeDtypeStruct((B, T, n), BF16),
                   jax.ShapeDtypeStruct((B, 2, n), F32)],
        scratch_shapes=[pltpu.VMEM((d, tn), BF16), pltpu.VMEM((d, tn), BF16),
                        pltpu.VMEM((HIST + tm, tn), F32)],
        compiler_params=_cparams(3),
        name="ffn_up_prompt",
    )(h, w_gate, w_up, conv_w, conv_b.reshape(conv_b.shape[0], 1, n))


def _ffn_up_sample_body(h_ref, wg_ref, wu_ref, cw_ref, cb_ref, st_ref, act_ref, nst_ref):
    h = h_ref[...]
    pre = _bdot(h, wg_ref[0])
    cw = cw_ref[0]
    u = st_ref[:, 0, :] * cw[0:1] + st_ref[:, 1, :] * cw[1:2] + pre * cw[2:3] + cb_ref[0]
    up = _bdot(h, wu_ref[0])
    act_ref[...] = (jax.nn.gelu(u) * up).astype(act_ref.dtype)
    nst_ref[:, 0, :] = st_ref[:, 1, :]
    nst_ref[:, 1, :] = pre


def _ffn_up_sample(h, w_gate, w_up, conv_w, conv_b, state, layer, *, tn):
    bs, d = h.shape
    n = w_gate.shape[-1]
    return pl.pallas_call(
        _ffn_up_sample_body,
        grid=(n // tn,),
        in_specs=[pl.BlockSpec((bs, d), lambda j: (0, 0)),
                  pl.BlockSpec((1, d, tn), lambda j: (layer, 0, j)),
                  pl.BlockSpec((1, d, tn), lambda j: (layer, 0, j)),
                  pl.BlockSpec((1, 3, tn), lambda j: (layer, 0, j)),
                  pl.BlockSpec((1, 1, tn), lambda j: (layer, 0, j)),
                  pl.BlockSpec((bs, 2, tn), lambda j: (0, 0, j))],
        out_specs=[pl.BlockSpec((bs, tn), lambda j: (0, j)),
                   pl.BlockSpec((bs, 2, tn), lambda j: (0, 0, j))],
        out_shape=[jax.ShapeDtypeStruct((bs, n), BF16),
                   jax.ShapeDtypeStruct((bs, 2, n), F32)],
        compiler_params=_cparams(1),
        name="ffn_up_sample",
    )(h, w_gate, w_up, conv_w, conv_b.reshape(conv_b.shape[0], 1, n), state)


def _rwkv_prep_prompt_body(pm_ref, pl_ref, mum_ref, mul_ref, w0_ref, w2_ref, a0_ref, a2_ref, g2_ref,
                           kk_ref, ka_ref, r_o, kf_o, v_o, kkn_o, a_o, lw_o, g_o, bufm, bufl):
    t = pl.program_id(1)
    tt = pm_ref.shape[1]

    @pl.when(t == 0)
    def _():
        bufm[0:HIST, :] = jnp.zeros((HIST, bufm.shape[1]), F32)
        bufl[0:HIST, :] = jnp.zeros((HIST, bufl.shape[1]), F32)

    p_main = pm_ref[0]
    p_lora = pl_ref[0][:, :LORA_ALL]
    bufm[HIST:, :] = p_main
    bufl[HIST:, :] = p_lora
    q_main = bufm[HIST - 1:HIST - 1 + tt, :]
    q_lora = bufl[HIST - 1:HIST - 1 + tt, :]
    outs = _rwkv_rows(p_main, q_main, p_lora, q_lora, mum_ref[...], mul_ref[...], w0_ref[...], w2_ref[0],
                      a0_ref[...], a2_ref[0], g2_ref[0], kk_ref[...], ka_ref[...])
    for o_ref, val in zip((r_o, kf_o, v_o, kkn_o, a_o, lw_o, g_o), outs):
        o_ref[0] = val
    bufm[0:HIST, :] = bufm[tt:tt + HIST, :]
    bufl[0:HIST, :] = bufl[tt:tt + HIST, :]


def _row(v):
    return v.reshape(1, -1)


def _rwkv_prep_prompt(p_main, p_lora, prm, i, *, tt):
    B, T, _ = p_main.shape
    lw_pad = p_lora.shape[-1]
    mu = prm['rwkv_mu'][i]
    full = lambda shape: pl.BlockSpec(shape, lambda b, t: (0,) * len(shape))
    lay3 = lambda shape: pl.BlockSpec((1,) + shape, lambda b, t: (i, 0, 0))
    out_spec = pl.BlockSpec((1, tt, A_WIDTH), lambda b, t: (b, t, 0))
    return pl.pallas_call(
        _rwkv_prep_prompt_body,
        grid=(B, T // tt),
        in_specs=[pl.BlockSpec((1, tt, P_A_MAIN), lambda b, t: (b, t, 0)),
                  pl.BlockSpec((1, tt, lw_pad), lambda b, t: (b, t, 0)),
                  full((1, P_A_MAIN)), full((1, LORA_ALL)), full((1, A_WIDTH)),
                  lay3((LORA_W, A_WIDTH)), full((1, A_WIDTH)), lay3((LORA_A, A_WIDTH)),
                  lay3((LORA_G, A_WIDTH)), full((1, A_WIDTH)), full((1, A_WIDTH))],
        out_specs=[out_spec] * 7,
        out_shape=[jax.ShapeDtypeStruct((B, T, A_WIDTH), F32)] * 7,
        scratch_shapes=[pltpu.VMEM((HIST + tt, P_A_MAIN), F32), pltpu.VMEM((HIST + tt, LORA_ALL), F32)],
        compiler_params=_cparams(2),
        name="rwkv_prep_prompt",
    )(p_main, p_lora, _row(mu[:P_A_MAIN]), _row(mu[P_A_MAIN:]), _row(prm['rwkv_w0'][i]), prm['rwkv_w2'],
      _row(prm['rwkv_a0'][i]), prm['rwkv_a2'], prm['rwkv_g2'], _row(prm['rwkv_k_k'][i]), _row(prm['rwkv_k_a'][i]))


def _wkv_prompt_body(r_ref, kf_ref, v_ref, kkn_ref, a_ref, lw_ref, g_ref, rk_ref, gw_ref, gb_ref,
                     ya_ref, st_ref, s_scr):
    t = pl.program_id(2)
    tt = r_ref.shape[1]
    n = A_HEAD_DIM
    C = CHUNK
    nc = tt // C
    nh = r_ref.shape[2] // n

    @pl.when(t == 0)
    def _():
        s_scr[...] = jnp.zeros(s_scr.shape, F32)

    def units(x):
        parts = [x[:, h * n:(h + 1) * n].reshape(nc, C, n) for h in range(nh)]
        return jnp.stack(parts, axis=1).reshape(nc * nh, C, n)

    lw_all = lw_ref[0]
    lc = units(_sel_l(_block_tri(tt, C), lw_all))
    r, kf, v, kkn, a, lw = map(units, (r_ref[0], kf_ref[0], v_ref[0], kkn_ref[0], a_ref[0], lw_all))
    incl = _tri(C, False)
    strict = _tri(C, True)

    kk = kkn * lax.rsqrt(jnp.sum(kkn * kkn, axis=-1, keepdims=True) + 1e-12)
    bb = kk * a
    p = jnp.exp(lc)
    pinv = jnp.exp(-lc)
    kt = kf * pinv
    bt = bb * pinv
    at = kk * jnp.exp(lc - lw)
    rt = r * p
    gram = _bdot(jnp.concatenate([at, rt], axis=1), jnp.concatenate([bt, kt], axis=1), _BNT)
    a_ab = jnp.where(strict, gram[:, :C, :C], 0.0)
    a_ak = jnp.where(strict, gram[:, :C, C:], 0.0)
    a_rb = jnp.where(incl, gram[:, C:, :C], 0.0)
    a_rk = jnp.where(incl, gram[:, C:, C:], 0.0)
    tinv = _unit_lower_inverse(a_ab, _bdot)
    av = _bdot(jnp.concatenate([a_ak, a_rk], axis=1), v, _BNN)
    tz = _bdot(tinv, jnp.concatenate([at, av[:, :C]], axis=2), _BNN)
    rz = _bdot(a_rb, tz, _BNN)
    lhs_s = jnp.concatenate([tz[:, :, :n], rt - rz[:, :, :n]], axis=1)
    z0 = tz[:, :, n:]
    y0 = av[:, C:] - rz[:, :, n:]
    plast = p[:, C - 1:C, :]
    upd = jnp.concatenate([kt * plast, -(bt * plast)], axis=1)

    S = s_scr[...]
    ys = []
    for c in range(nc):
        sl = slice(c * nh, (c + 1) * nh)
        xs = _bdot(lhs_s[sl], S, _BNT)
        z = xs[:, :C] + z0[sl]
        ys.append(xs[:, C:] + y0[sl])
        S = S * plast[sl] + _bdot(jnp.concatenate([v[sl], z], axis=1), upd[sl], _BTN)
    s_scr[...] = S

    outs = []
    for h in range(nh):
        cols = slice(h * n, (h + 1) * n)
        y = jnp.concatenate([ys[c][h] for c in range(nc)], axis=0)
        outs.append(_rwkv_head_out(y, r_ref[0, :, cols], kf_ref[0, :, cols], v_ref[0, :, cols],
                                   g_ref[0, :, cols], rk_ref[:, cols], gw_ref[:, cols], gb_ref[:, cols]))
    ya_ref[0] = jnp.concatenate(outs, axis=1).astype(ya_ref.dtype)

    @pl.when(t == pl.num_programs(2) - 1)
    def _():
        st_ref[0] = S


def _wkv_prompt(vals, prm, i, *, tt, heads_per_step):
    B, T, _ = vals[0].shape
    wblk = heads_per_step * A_HEAD_DIM
    seq_spec = pl.BlockSpec((1, tt, wblk), lambda b, h, t: (b, t, h))
    par_spec = pl.BlockSpec((1, wblk), lambda b, h, t: (0, h))
    return pl.pallas_call(
        _wkv_prompt_body,
        grid=(B, A_HEADS // heads_per_step, T // tt),
        in_specs=[seq_spec] * 7 + [par_spec] * 3,
        out_specs=[seq_spec,
                   pl.BlockSpec((1, heads_per_step, A_HEAD_DIM, A_HEAD_DIM), lambda b, h, t: (b, h, 0, 0))],
        out_shape=[jax.ShapeDtypeStruct((B, T, A_WIDTH), BF16),
                   jax.ShapeDtypeStruct((B, A_HEADS, A_HEAD_DIM, A_HEAD_DIM), F32)],
        scratch_shapes=[pltpu.VMEM((heads_per_step, A_HEAD_DIM, A_HEAD_DIM), F32)],
        compiler_params=_cparams(3),
        name="wkv_prompt",
    )(*vals, _row(prm['rwkv_r_k'][i]), _row(prm['rwkv_gn_w'][i]), _row(prm['rwkv_gn_b'][i]))


def _lru_prompt_body(gate_ref, xb_ref, cw_ref, cb_ref, wa_ref, wx_ref, ba_ref, bx_ref, lam_ref,
                     y_ref, h_ref, cst_ref, buf, h_scr):
    t = pl.program_id(1)
    tt = xb_ref.shape[1]
    w = xb_ref.shape[2]

    @pl.when(t == 0)
    def _():
        buf[0:HIST, :] = jnp.zeros((HIST, w), F32)
        h_scr[...] = jnp.zeros(h_scr.shape, F32)

    buf[HIST:, :] = xb_ref[0]
    cw = cw_ref[0]
    xc = cb_ref[...] + buf[HIST:, :] * cw[3:4]
    for j in range(3):
        xc = xc + buf[HIST - 3 + j:HIST - 3 + j + tt, :] * cw[j:j + 1]
    a, b, gact = _lru_rows(gate_ref[0], xc, wa_ref, wx_ref, ba_ref[...], bx_ref[...], lam_ref[...])
    row = _iota2((tt, w), 0)
    d = 1
    while d < tt:
        a_sh = jnp.where(row >= d, pltpu.roll(a, d, axis=0), 1.0)
        b_sh = jnp.where(row >= d, pltpu.roll(b, d, axis=0), 0.0)
        b = a * b_sh + b
        a = a * a_sh
        d *= 2
    h = a * h_scr[0:1, :] + b
    y_ref[0] = (h * gact).astype(y_ref.dtype)
    h_last = h[tt - 1:tt, :]
    h_scr[0:1, :] = h_last
    h_ref[0] = h_last
    cst_ref[0] = buf[HIST + tt - 3:HIST + tt, :]
    buf[0:HIST, :] = buf[tt:tt + HIST, :]


def _block_diag_pairs(w):
    nb, n, _ = w.shape
    w = w.reshape(nb // 2, 2, n, n)
    z = jnp.zeros((nb // 2, n, n), w.dtype)
    top = jnp.concatenate([w[:, 0], z], axis=2)
    bot = jnp.concatenate([z, w[:, 1]], axis=2)
    return jnp.concatenate([top, bot], axis=1)


def _lru_prompt(pb, prm, i, *, tt):
    B, T, _ = pb.shape
    w = B_WIDTH
    ng = w // LANES
    full = lambda shape: pl.BlockSpec(shape, lambda b, t: (0,) * len(shape))
    return pl.pallas_call(
        _lru_prompt_body,
        grid=(B, T // tt),
        in_specs=[pl.BlockSpec((1, tt, w), lambda b, t: (b, t, 0)),
                  pl.BlockSpec((1, tt, w), lambda b, t: (b, t, 1)),
                  pl.BlockSpec((1, 4, w), lambda b, t: (i, 0, 0)),
                  full((1, w)), full((ng, LANES, LANES)), full((ng, LANES, LANES)),
                  full((1, w)), full((1, w)), full((1, w))],
        out_specs=[pl.BlockSpec((1, tt, w), lambda b, t: (b, t, 0)),
                   pl.BlockSpec((1, 1, w), lambda b, t: (b, 0, 0)),
                   pl.BlockSpec((1, 3, w), lambda b, t: (b, 0, 0))],
        out_shape=[jax.ShapeDtypeStruct((B, T, w), BF16),
                   jax.ShapeDtypeStruct((B, 1, w), F32),
                   jax.ShapeDtypeStruct((B, 3, w), F32)],
        scratch_shapes=[pltpu.VMEM((HIST + tt, w), F32), pltpu.VMEM((SUBLANES, w), F32)],
        compiler_params=_cparams(2),
        name="lru_prompt",
    )(pb, pb, prm['lru_conv_w'], _row(prm['lru_conv_b'][i]), _block_diag_pairs(prm['lru_wa'][i]),
      _block_diag_pairs(prm['lru_wx'][i]), _row(prm['lru_ba'][i]), _row(prm['lru_bx'][i]),
      _row(prm['lru_lambda'][i]))


def _in_c_conv_prompt_body(h_ref, w_ref, cw_ref, o_ref, st_ref, w_bf, buf):
    j = pl.program_id(0)
    t = pl.program_id(2)
    tm = h_ref.shape[1]
    tn = w_ref.shape[1]

    @pl.when((pl.program_id(1) == 0) & (t == 0))
    def _():
        w_bf[...] = w_ref[...].astype(BF16)

    @pl.when(t == 0)
    def _():
        buf[0:HIST, :] = jnp.zeros((HIST, tn), F32)

    buf[HIST:, :] = jnp.dot(h_ref[0], w_bf[...], preferred_element_type=F32)
    cw = cw_ref[0]
    y = buf[HIST:, :] * cw[3:4]
    for jj in range(3):
        y = y + buf[HIST - 3 + jj:HIST - 3 + jj + tm, :] * cw[jj:jj + 1]
    n_qk_blocks = 2 * C_KEY_DIM // tn
    q_scale = jnp.where(j < n_qk_blocks // 2, C_HEAD ** -0.5, 1.0).astype(F32)
    o_ref[0] = _gdn_act(y, j < n_qk_blocks, q_scale)
    st_ref[0] = buf[HIST + tm - 3:HIST + tm, :]
    buf[0:HIST, :] = buf[tm:tm + HIST, :]


def _in_c_conv_prompt(h, w, conv_w, i, *, tm, tn):
    B, T, d = h.shape
    return pl.pallas_call(
        _in_c_conv_prompt_body,
        grid=(C_QKV_DIM // tn, B, T // tm),
        in_specs=[pl.BlockSpec((1, tm, d), lambda j, b, t: (b, t, 0)),
                  pl.BlockSpec((d, tn), lambda j, b, t: (0, j)),
                  pl.BlockSpec((1, 4, tn), lambda j, b, t: (i, 0, j))],
        out_specs=[pl.BlockSpec((1, tm, tn), lambda j, b, t: (b, t, j)),
                   pl.BlockSpec((1, 3, tn), lambda j, b, t: (b, 0, j))],
        out_shape=[jax.ShapeDtypeStruct((B, T, C_QKV_DIM), F32),
                   jax.ShapeDtypeStruct((B, 3, C_QKV_DIM), F32)],
        scratch_shapes=[pltpu.VMEM((d, tn), BF16), pltpu.VMEM((HIST + tm, tn), F32)],
        compiler_params=_cparams(3),
        name="in_c_conv_prompt",
    )(h, w, conv_w)


def _gdn_gates(ba, a_log, dt_bias):
    hv = C_V_HEADS
    beta = jax.nn.sigmoid(ba[..., :hv])
    g = -jnp.exp(a_log) * jax.nn.softplus(ba[..., hv:2 * hv] + dt_bias)
    return beta, g


def _lane_bcast_col(x, col):
    n = x.shape[1]
    onehot = _iota2((n, LANES), 0) == col
    return _sel_r(x, onehot)


def _gdn_prompt_body(q_ref, k_ref, v_ref, z_ref, ba_ref, alog_ref, dtb_ref, nw_ref, o_ref, st_ref, s_scr):
    hblk = pl.program_id(1)
    t = pl.program_id(2)
    tt = q_ref.shape[1]
    C = CHUNK
    nc = tt // C
    rep = C_V_HEADS // C_K_HEADS
    n_kh = q_ref.shape[2] // C_HEAD
    nh = n_kh * rep

    @pl.when(t == 0)
    def _():
        s_scr[...] = jnp.zeros(s_scr.shape, F32)

    beta_all, g_all = _gdn_gates(ba_ref[0], alog_ref[...], dtb_ref[...])
    gc_all = _sel_l(_block_tri(tt, C), g_all)
    blocks = lambda x: x.reshape(nc, C, x.shape[-1])
    qs, ks, vs, betas, gcs = [], [], [], [], []
    for kh in range(n_kh):
        q_h = blocks(q_ref[0, :, kh * C_HEAD:(kh + 1) * C_HEAD])
        k_h = blocks(k_ref[0, :, kh * C_HEAD:(kh + 1) * C_HEAD])
        for jv in range(rep):
            hu = kh * rep + jv
            hv = hblk * nh + hu
            qs.append(q_h)
            ks.append(k_h)
            vs.append(blocks(v_ref[0, :, hu * C_HEAD:(hu + 1) * C_HEAD]))
            betas.append(blocks(_lane_bcast_col(beta_all, hv)))
            gcs.append(blocks(_lane_bcast_col(gc_all, hv)))
    stack = lambda xs: jnp.stack(xs, axis=1).reshape(nc * nh, C, xs[0].shape[-1])
    q, k, v, beta, gc = map(stack, (qs, ks, vs, betas, gcs))
    incl = _tri(C, False)
    strict = _tri(C, True)

    gcc = gc[:, :, :C]
    diff = gcc - jnp.swapaxes(gcc, 1, 2)
    dec_incl = jnp.exp(jnp.where(incl, diff, -jnp.inf))
    dec_strict = jnp.where(strict, dec_incl, 0.0)
    kb = k * beta
    gram = _bdot(jnp.concatenate([kb, q], axis=1), k, _BNT)
    low = gram[:, :C] * dec_strict
    qk = gram[:, C:] * dec_incl
    tinv = _unit_lower_inverse_refined(low)
    eg = jnp.exp(gc)
    sol = _dot3(tinv, jnp.concatenate([v * beta, kb * eg], axis=2), _BNN)
    u = sol[:, :, :C_HEAD]
    lhs_s = jnp.concatenate([sol[:, :, C_HEAD:], q * eg], axis=1)
    g_last = gc[:, C - 1:C, :]
    k_tail = k * jnp.exp(g_last - gc)
    e_last = jnp.exp(g_last)

    S = s_scr[...]
    os_ = []
    for c in range(nc):
        sl = slice(c * nh, (c + 1) * nh)
        ws = _bdot(lhs_s[sl], S, _BNN)
        v_new = u[sl] - ws[:, :C]
        os_.append(ws[:, C:] + _bdot(qk[sl], v_new, _BNN))
        S = S * e_last[sl] + _bdot(k_tail[sl], v_new, _BTN)
    s_scr[...] = S

    outs = []
    for hu in range(nh):
        cols = slice(hu * C_HEAD, (hu + 1) * C_HEAD)
        o = jnp.concatenate([os_[c][hu] for c in range(nc)], axis=0)
        outs.append(_gdn_head_out(o, z_ref[0, :, cols], nw_ref[...]))
    o_ref[0] = jnp.concatenate(outs, axis=1).astype(o_ref.dtype)

    @pl.when(t == pl.num_programs(2) - 1)
    def _():
        st_ref[0] = S


def _gdn_prompt(qkv_act, z, ba, prm, i, *, tt, kheads_per_step):
    B, T, _ = qkv_act.shape
    rep = C_V_HEADS // C_K_HEADS
    kw = kheads_per_step * C_HEAD
    vw = rep * kw
    k_blk0 = C_KEY_DIM // kw
    v_blk0 = 2 * C_KEY_DIM // vw
    z_blk0 = 0
    nh = kheads_per_step * rep
    full = lambda shape: pl.BlockSpec(shape, lambda b, h, t: (0,) * len(shape))
    return pl.pallas_call(
        _gdn_prompt_body,
        grid=(B, C_K_HEADS // kheads_per_step, T // tt),
        in_specs=[pl.BlockSpec((1, tt, kw), lambda b, h, t: (b, t, h)),
                  pl.BlockSpec((1, tt, kw), lambda b, h, t: (b, t, k_blk0 + h)),
                  pl.BlockSpec((1, tt, vw), lambda b, h, t: (b, t, v_blk0 + h)),
                  pl.BlockSpec((1, tt, vw), lambda b, h, t: (b, t, z_blk0 + h)),
                  pl.BlockSpec((1, tt, 2 * C_V_HEADS), lambda b, h, t: (b, t, 0)),
                  full((1, C_V_HEADS)), full((1, C_V_HEADS)), full((1, C_HEAD))],
        out_specs=[pl.BlockSpec((1, tt, vw), lambda b, h, t: (b, t, h)),
                   pl.BlockSpec((1, nh, C_HEAD, C_HEAD), lambda b, h, t: (b, h, 0, 0))],
        out_shape=[jax.ShapeDtypeStruct((B, T, C_VAL_DIM), BF16),
                   jax.ShapeDtypeStruct((B, C_V_HEADS, C_HEAD, C_HEAD), F32)],
        scratch_shapes=[pltpu.VMEM((nh, C_HEAD, C_HEAD), F32)],
        compiler_params=_cparams(3),
        name="gdn_prompt",
    )(qkv_act, qkv_act, qkv_act, z, ba, _row(prm['gdn_a_log'][i]), _row(prm['gdn_dt_bias'][i]),
      _row(prm['gdn_norm_w'][i]))


def _rwkv_prep_sample_body(pm_ref, pl_ref, qm_ref, ql_ref, mum_ref, mul_ref, w0_ref, w2_ref, a0_ref, a2_ref,
                           g2_ref, kk_ref, ka_ref, r_o, kf_o, v_o, kkn_o, a_o, lw_o, g_o):
    outs = _rwkv_rows(pm_ref[...], qm_ref[...], pl_ref[...][:, :LORA_ALL], ql_ref[...], mum_ref[...],
                      mul_ref[...], w0_ref[...], w2_ref[0], a0_ref[...], a2_ref[0], g2_ref[0],
                      kk_ref[...], ka_ref[...])
    for o_ref, val in zip((r_o, kf_o, v_o, kkn_o, a_o, lw_o, g_o), outs):
        o_ref[...] = val


def _rwkv_prep_sample(p_main, p_lora, shift, prm, i):
    bs = p_main.shape[0]
    mu = prm['rwkv_mu'][i]
    full = lambda shape: pl.BlockSpec(shape, lambda s: (0,) * len(shape))
    lay3 = lambda shape: pl.BlockSpec((1,) + shape, lambda s: (i, 0, 0))
    return pl.pallas_call(
        _rwkv_prep_sample_body,
        grid=(1,),
        in_specs=[full(p_main.shape), full(p_lora.shape), full((bs, P_A_MAIN)), full((bs, LORA_ALL)),
                  full((1, P_A_MAIN)), full((1, LORA_ALL)), full((1, A_WIDTH)),
                  lay3((LORA_W, A_WIDTH)), full((1, A_WIDTH)), lay3((LORA_A, A_WIDTH)),
                  lay3((LORA_G, A_WIDTH)), full((1, A_WIDTH)), full((1, A_WIDTH))],
        out_specs=[full((bs, A_WIDTH))] * 7,
        out_shape=[jax.ShapeDtypeStruct((bs, A_WIDTH), F32)] * 7,
        compiler_params=_cparams(1),
        name="rwkv_prep_sample",
    )(p_main, p_lora, shift[:, :P_A_MAIN], shift[:, P_A_MAIN:], _row(mu[:P_A_MAIN]), _row(mu[P_A_MAIN:]),
      _row(prm['rwkv_w0'][i]), prm['rwkv_w2'], _row(prm['rwkv_a0'][i]), prm['rwkv_a2'], prm['rwkv_g2'],
      _row(prm['rwkv_k_k'][i]), _row(prm['rwkv_k_a'][i]))


STEP_GROUP = 16


def _wkv_step_body(*refs, n_state):
    s_ref = refs[0]
    r_ref, kf_ref, v_ref, kkn_ref, a_ref, lw_ref, g_ref, rk_ref, gw_ref, gb_ref, ns_ref, y_ref = refs[n_state:]
    nh = s_ref.shape[0]
    n = A_HEAD_DIM
    G = STEP_GROUP
    rid = lax.broadcasted_iota(jnp.int32, (G, SUBLANES, n), 1)
    up = lambda x: jnp.broadcast_to(x.reshape(G, 1, n), (G, SUBLANES, n))

    def group(i, carry):
        g0 = pl.multiple_of(i * G, G)
        rows = lambda ref: ref[pl.ds(g0, G), :]
        S = s_ref[pl.ds(g0, G)]
        r, kf, v, kkn, a = rows(r_ref), rows(kf_ref), rows(v_ref), rows(kkn_ref), rows(a_ref)
        kk = kkn * lax.rsqrt(jnp.sum(kkn * kkn, axis=-1, keepdims=True) + 1e-12)
        u = _dot3(up(kk), S, _BNT)
        left = jnp.where(rid == 0, -u, jnp.where(rid == 1, up(v), 0.0))
        right = jnp.where(rid == 0, up(kk * a), jnp.where(rid == 1, up(kf), 0.0))
        s_new = S * jnp.exp(rows(lw_ref)).reshape(G, 1, n) + _dot3(left, right, _BTN)
        ns_ref[pl.ds(g0, G)] = s_new
        y = _dot3(up(r), s_new, _BNT)[:, 0, :]
        y_ref[pl.ds(g0, G), :] = _rwkv_head_out(y, r, kf, v, rows(g_ref), rows(rk_ref), rows(gw_ref),
                                                rows(gb_ref))
        return carry

    lax.fori_loop(0, nh // G, group, 0)


def _layer_state_io(state_all, new_all, spec):
    if new_all is None:
        return [spec], [state_all], {}
    return [spec, pl.BlockSpec(memory_space=pl.ANY)], [state_all, new_all], {1: 0}


def _wkv_step(S_all, new_all, vals, prm, i, *, nh):
    n_tot = S_all.shape[1]
    n = A_HEAD_DIM
    tile = lambda p: jnp.tile(p.reshape(A_HEADS, n), (nh // A_HEADS, 1))
    row_spec = pl.BlockSpec((nh, n), lambda b: (b, 0))
    par_spec = pl.BlockSpec((nh, n), lambda b: (0, 0))
    s_spec = pl.BlockSpec((None, nh, n, n), lambda b: (i, b, 0, 0))
    st_specs, st_args, aliases = _layer_state_io(S_all, new_all, s_spec)
    return pl.pallas_call(
        functools.partial(_wkv_step_body, n_state=len(st_args)),
        grid=(n_tot // nh,),
        in_specs=st_specs + [row_spec] * 7 + [par_spec] * 3,
        out_specs=[s_spec, row_spec],
        out_shape=[jax.ShapeDtypeStruct(S_all.shape, F32), jax.ShapeDtypeStruct((n_tot, n), F32)],
        input_output_aliases=aliases,
        compiler_params=_cparams(1),
        name="wkv_step",
    )(*st_args, *vals, tile(prm['rwkv_r_k'][i]), tile(prm['rwkv_gn_w'][i]), tile(prm['rwkv_gn_b'][i]))


def _lru_sample_body(gate_ref, xb_ref, cst_ref, h0_ref, cw_ref, cb_ref, wa_ref, wx_ref, ba_ref, bx_ref, lam_ref,
                     y_ref, h_ref, ncst_ref):
    cw = cw_ref[0]
    xb = xb_ref[...]
    xc = cb_ref[...] + xb * cw[3:4]
    for j in range(3):
        xc = xc + cst_ref[:, j, :] * cw[j:j + 1]
    a, b, gact = _lru_rows(gate_ref[...], xc, wa_ref, wx_ref, ba_ref[...], bx_ref[...], lam_ref[...])
    h = a * h0_ref[...] + b
    y_ref[...] = (h * gact).astype(y_ref.dtype)
    h_ref[...] = h
    ncst_ref[:, 0, :] = cst_ref[:, 1, :]
    ncst_ref[:, 1, :] = cst_ref[:, 2, :]
    ncst_ref[:, 2, :] = xb


def _lru_sample(pb, conv_state, h0, prm, i):
    bs = pb.shape[0]
    w = B_WIDTH
    ng = w // LANES
    full = lambda shape: pl.BlockSpec(shape, lambda s: (0,) * len(shape))
    return pl.pallas_call(
        _lru_sample_body,
        grid=(1,),
        in_specs=[pl.BlockSpec((bs, w), lambda s: (0, 0)), pl.BlockSpec((bs, w), lambda s: (0, 1)),
                  full((bs, 3, w)), full((bs, w)),
                  pl.BlockSpec((1, 4, w), lambda s: (i, 0, 0)),
                  full((1, w)), full((ng, LANES, LANES)), full((ng, LANES, LANES)),
                  full((1, w)), full((1, w)), full((1, w))],
        out_specs=[full((bs, w)), full((bs, w)), full((bs, 3, w))],
        out_shape=[jax.ShapeDtypeStruct((bs, w), BF16), jax.ShapeDtypeStruct((bs, w), F32),
                   jax.ShapeDtypeStruct((bs, 3, w), F32)],
        compiler_params=_cparams(1),
        name="lru_sample",
    )(pb, pb, conv_state, h0, prm['lru_conv_w'], _row(prm['lru_conv_b'][i]),
      _block_diag_pairs(prm['lru_wa'][i]), _block_diag_pairs(prm['lru_wx'][i]), _row(prm['lru_ba'][i]),
      _row(prm['lru_bx'][i]), _row(prm['lru_lambda'][i]))


def _gdn_prep_sample_body(x_ref, cst_ref, cw_ref, o_ref, ncst_ref):
    j = pl.program_id(0)
    cw = cw_ref[0]
    x = x_ref[...]
    y = x * cw[3:4]
    for jj in range(3):
        y = y + cst_ref[:, jj, :] * cw[jj:jj + 1]
    n_qk_blocks = 2 * C_KEY_DIM // x.shape[1]
    q_scale = jnp.where(j < n_qk_blocks // 2, C_HEAD ** -0.5, 1.0).astype(F32)
    o_ref[...] = _gdn_act(y, j < n_qk_blocks, q_scale)
    ncst_ref[:, 0, :] = cst_ref[:, 1, :]
    ncst_ref[:, 1, :] = cst_ref[:, 2, :]
    ncst_ref[:, 2, :] = x


def _gdn_prep_sample(qkvz, conv_state, conv_w, i, *, tc):
    bs = qkvz.shape[0]
    return pl.pallas_call(
        _gdn_prep_sample_body,
        grid=(C_QKV_DIM // tc,),
        in_specs=[pl.BlockSpec((bs, tc), lambda j: (0, j)),
                  pl.BlockSpec((bs, 3, tc), lambda j: (0, 0, j)),
                  pl.BlockSpec((1, 4, tc), lambda j: (i, 0, j))],
        out_specs=[pl.BlockSpec((bs, tc), lambda j: (0, j)),
                   pl.BlockSpec((bs, 3, tc), lambda j: (0, 0, j))],
        out_shape=[jax.ShapeDtypeStruct((bs, C_QKV_DIM), F32),
                   jax.ShapeDtypeStruct((bs, 3, C_QKV_DIM), F32)],
        compiler_params=_cparams(1),
        name="gdn_prep_sample",
    )(qkvz, conv_state, conv_w)


def _gdn_step_body(*refs, n_state):
    s_ref = refs[0]
    q_ref, k_ref, v_ref, z_ref, ba_ref, alog_ref, dtb_ref, nw_ref, ns_ref, o_ref = refs[n_state:]
    bb = s_ref.shape[0]
    rep = C_V_HEADS // C_K_HEADS
    hk = pl.program_id(1)
    beta_all, g_all = _gdn_gates(ba_ref[...], alog_ref[...], dtb_ref[...])
    lane = lax.broadcasted_iota(jnp.int32, beta_all.shape, 2)
    pick = lambda x, col: jnp.sum(jnp.where(lane == col, x, 0.0), axis=-1, keepdims=True)
    rid = lax.broadcasted_iota(jnp.int32, (bb, SUBLANES, C_HEAD), 1)
    up = lambda x: jnp.broadcast_to(x, (bb, SUBLANES, C_HEAD))
    q = q_ref[...]
    k = k_ref[...]
    qk = jnp.sum(q * k, axis=-1, keepdims=True)
    for jv in range(rep):
        cols = slice(jv * C_HEAD, (jv + 1) * C_HEAD)
        hv = hk * rep + jv
        beta = pick(beta_all, hv)
        eg = jnp.exp(pick(g_all, hv))
        v = v_ref[:, :, cols]
        S = s_ref[:, jv]
        kb = k * beta
        lhs = jnp.where(rid == 0, up(kb * eg), jnp.where(rid == 1, up(q * eg), 0.0))
        ws = _dot3(lhs, S, _BNN)
        v_new = v * beta - ws[:, 0:1]
        o = ws[:, 1:2] + qk * v_new
        left = jnp.where(rid == 0, up(k), 0.0)
        right = jnp.where(rid == 0, up(v_new), 0.0)
        ns_ref[:, jv] = S * eg + _dot3(left, right, _BTN)
        o_ref[:, :, cols] = _gdn_head_out(o, z_ref[:, :, cols], nw_ref[...]).astype(o_ref.dtype)


def _gdn_step(S_all, new_all, qkv_act, qkvz, ba, prm, i, *, bb):
    bs = S_all.shape[1]
    rep = C_V_HEADS // C_K_HEADS
    vw = rep * C_HEAD
    v_blk0 = 2 * C_KEY_DIM // vw
    z_blk0 = C_QKV_DIM // vw
    full = lambda shape: pl.BlockSpec(shape, lambda b, h: (0,) * len(shape))
    s_spec = pl.BlockSpec((None, bb, rep, C_HEAD, C_HEAD), lambda b, h: (i, b, h, 0, 0))
    st_specs, st_args, aliases = _layer_state_io(S_all, new_all, s_spec)
    return pl.pallas_call(
        functools.partial(_gdn_step_body, n_state=len(st_args)),
        grid=(bs // bb, C_K_HEADS),
        in_specs=st_specs + [
                  pl.BlockSpec((bb, 1, C_HEAD), lambda b, h: (b, 0, h)),
                  pl.BlockSpec((bb, 1, C_HEAD), lambda b, h: (b, 0, C_K_HEADS + h)),
                  pl.BlockSpec((bb, 1, vw), lambda b, h: (b, 0, v_blk0 + h)),
                  pl.BlockSpec((bb, 1, vw), lambda b, h: (b, 0, z_blk0 + h)),
                  pl.BlockSpec((bb, 1, 2 * C_V_HEADS), lambda b, h: (b, 0, 0)),
                  full((1, C_V_HEADS)), full((1, C_V_HEADS)), full((1, C_HEAD))],
        out_specs=[s_spec, pl.BlockSpec((bb, 1, vw), lambda b, h: (b, 0, h))],
        out_shape=[jax.ShapeDtypeStruct(S_all.shape, F32), jax.ShapeDtypeStruct((bs, 1, C_VAL_DIM), BF16)],
        input_output_aliases=aliases,
        compiler_params=_cparams(2),
        name="gdn_step",
    )(*st_args, qkv_act, qkv_act, qkv_act, qkvz, ba, _row(prm['gdn_a_log'][i]), _row(prm['gdn_dt_bias'][i]),
      _row(prm['gdn_norm_w'][i]))


MOD_SH1, MOD_SC1, MOD_GT1, MOD_SH2, MOD_SC2, MOD_GT2 = range(6)


def _in_proj_ab(h, prm, i, tm):
    w = prm['w_in_ab'][i]
    p_main = _mm([h], [(w, 0)], n_out=P_A_MAIN, tm=tm, name="in_ab_main")
    p_lora = _mm([h], [(w, 0)], n_out=3 * LANES, col0=P_A_MAIN, tm=tm, tn=3 * LANES, name="in_ab_lora")
    pb = _mm([h], [(w[:, P_A:], 0)], n_out=2 * B_WIDTH, tm=tm, name="in_ab_lru")
    return p_main, p_lora, pb


def _in_proj_c(h, prm, i, tm):
    w = prm['w_in_c'][i]
    qkvz = _mm([h], [(w, 0)], n_out=C_QKV_DIM + C_VAL_DIM, tm=tm, name="in_c_main")
    ba = _mm([h], [(w[:, C_QKV_DIM + C_VAL_DIM:], 0)], n_out=2 * C_V_HEADS, tm=tm, tn=2 * C_V_HEADS,
             name="in_c_gates")
    return qkvz, ba


def _ffn_down(act, prm, layer, x, mod, tm):
    return _mm([act], [(prm['ffn_w_down'][layer], 0)], n_out=D_MODEL, tm=tm, res=x, gate=mod, gate_blk=MOD_GT2,
               name="ffn_down")


def _trunk_prompt(x, mods, prm):
    B, T, _ = x.shape
    tm, tt = min(1024, T), min(256, T)
    wkv, shift, lru_h, lru_conv, gdn, gdn_conv, ffn_conv = [], [], [], [], [], [], []
    for layer in range(DEPTH):
        mod = mods[layer].reshape(B, 1, 6 * D_MODEL)
        i = layer // 2
        h = _norm(x, prm['norm_mix'][layer], mod, MOD_SC1, MOD_SH1, tt=tt)
        if layer % 2 == 0:
            p_main, p_lora, pb = _in_proj_ab(h, prm, i, tm)
            vals = _rwkv_prep_prompt(p_main, p_lora, prm, i, tt=tt)
            ya, s_new = _wkv_prompt(vals, prm, i, tt=tt, heads_per_step=4)
            yb, h_last, cst = _lru_prompt(pb, prm, i, tt=tt)
            w_out = prm['w_out_ab'][i]
            x = _mm([ya, yb], [(w_out, 0), (w_out, 1)], n_out=D_MODEL, tm=tm, res=x, gate=mod, gate_blk=MOD_GT1)
            wkv.append(s_new)
            shift.append(jnp.concatenate([p_main[:, -1], p_lora[:, -1, :LORA_ALL]], axis=-1))
            lru_h.append(h_last[:, 0])
            lru_conv.append(cst)
        else:
            w_c = prm['w_in_c'][i]
            qkv_act, cst = _in_c_conv_prompt(h, w_c, prm['gdn_conv_w'], i, tm=tm, tn=512)
            z = _mm([h], [(w_c, 0)], n_out=C_VAL_DIM, col0=C_QKV_DIM, name="in_c_z")
            ba = _mm([h], [(w_c[:, C_QKV_DIM + C_VAL_DIM:], 0)], n_out=2 * C_V_HEADS, tm=tm, tn=2 * C_V_HEADS,
                     name="in_c_gates")
            yc, s_new = _gdn_prompt(qkv_act, z, ba, prm, i, tt=tt, kheads_per_step=2)
            x = _mm([yc], [(prm['w_out_c'][i], 0)], n_out=D_MODEL, tm=tm, res=x, gate=mod, gate_blk=MOD_GT1)
            gdn.append(s_new)
            gdn_conv.append(cst)
        h = _norm(x, prm['norm_ffn'][layer], mod, MOD_SC2, MOD_SH2, tt=tt)
        act, fst = _ffn_up_prompt(h, prm['ffn_w_gate'], prm['ffn_w_up'], prm['ffn_conv_w'], prm['ffn_conv_b'],
                                  layer, tm=tm, tn=512)
        x = _ffn_down(act, prm, layer, x, mod, tm)
        ffn_conv.append(fst)
    y = _norm(x, prm['norm_out'], tt=tt, out_dtype=F32)
    stk = jnp.stack
    return y, stk(wkv), stk(shift), stk(lru_h), stk(lru_conv), stk(gdn), stk(gdn_conv), stk(ffn_conv)


def _trunk_sample(x, mods, st, prm):
    bs = x.shape[0]
    x = x.reshape(1, bs, D_MODEL)
    wkv0, shift0, lru_h0, lru_conv0, gdn0, gdn_conv0, ffn_conv0 = st
    wkv_new = gdn_new = None
    shift, lru_h, lru_conv, gdn_conv, ffn_conv = [], [], [], [], []
    for layer in range(DEPTH):
        mod = mods[layer].reshape(1, bs, 6 * D_MODEL)
        i = layer // 2
        h = _norm(x, prm['norm_mix'][layer], mod, MOD_SC1, MOD_SH1, tt=bs)
        if layer % 2 == 0:
            p_main, p_lora, pb = _in_proj_ab(h, prm, i, bs)
            p_main, p_lora, pb = p_main[0], p_lora[0], pb[0]
            vals = _rwkv_prep_sample(p_main, p_lora, shift0[i], prm, i)
            heads = lambda a: a.reshape(bs * A_HEADS, A_HEAD_DIM)
            wkv_new, ya = _wkv_step(wkv0.reshape(-1, bs * A_HEADS, A_HEAD_DIM, A_HEAD_DIM), wkv_new,
                                    [heads(a) for a in vals], prm, i, nh=8 * A_HEADS)
            ya = ya.reshape(1, bs, A_WIDTH).astype(BF16)
            yb, h_last, cst = _lru_sample(pb, lru_conv0[i], lru_h0[i], prm, i)
            w_out = prm['w_out_ab'][i]
            x = _mm([ya, yb[None]], [(w_out, 0), (w_out, 1)], n_out=D_MODEL, tm=bs, res=x, gate=mod,
                    gate_blk=MOD_GT1)
            shift.append(jnp.concatenate([p_main, p_lora[:, :LORA_ALL]], axis=-1))
            lru_h.append(h_last)
            lru_conv.append(cst)
        else:
            qkvz, ba = _in_proj_c(h, prm, i, bs)
            qkv_act, cst = _gdn_prep_sample(qkvz[0], gdn_conv0[i], prm['gdn_conv_w'], i, tc=2048)
            gdn_new, yc = _gdn_step(gdn0, gdn_new, qkv_act[:, None], qkvz[0][:, None], ba[0][:, None], prm, i,
                                    bb=min(STEP_GROUP, bs))
            x = _mm([yc.reshape(1, bs, C_VAL_DIM)], [(prm['w_out_c'][i], 0)], n_out=D_MODEL, tm=bs, res=x,
                    gate=mod, gate_blk=MOD_GT1)
            gdn_conv.append(cst)
        h = _norm(x, prm['norm_ffn'][layer], mod, MOD_SC2, MOD_SH2, tt=bs)
        act, fst = _ffn_up_sample(h[0], prm['ffn_w_gate'], prm['ffn_w_up'], prm['ffn_conv_w'],
                                  prm['ffn_conv_b'], ffn_conv0[layer], layer, tn=512)
        x = _ffn_down(act[None], prm, layer, x, mod, bs)
        ffn_conv.append(fst)
    y = _norm(x, prm['norm_out'], tt=bs, out_dtype=F32).reshape(bs, 1, D_MODEL)
    stk = jnp.stack
    return (y, wkv_new.reshape(wkv0.shape), stk(shift), stk(lru_h), stk(lru_conv), gdn_new, stk(gdn_conv),
            stk(ffn_conv))


def kernel(x_prompt, x_sample, c_prompt, c_sample, state_rwkv_wkv, state_rwkv_shift, state_lru_h, state_lru_conv, state_gdn, state_gdn_conv, state_ffn_conv, w_mod, b_mod, norm_mix, norm_ffn, norm_out, w_in_ab, w_out_ab, rwkv_mu, rwkv_w0, rwkv_w2, rwkv_a0, rwkv_a2, rwkv_g2, rwkv_k_k, rwkv_k_a, rwkv_r_k, rwkv_gn_w, rwkv_gn_b, lru_conv_w, lru_conv_b, lru_wa, lru_ba, lru_wx, lru_bx, lru_lambda, w_in_c, w_out_c, gdn_conv_w, gdn_a_log, gdn_dt_bias, gdn_norm_w, ffn_w_gate, ffn_w_up, ffn_conv_w, ffn_conv_b, ffn_w_down):
    prm = dict(norm_mix=norm_mix, norm_ffn=norm_ffn, norm_out=norm_out, w_in_ab=w_in_ab, w_out_ab=w_out_ab,
               rwkv_mu=rwkv_mu, rwkv_w0=rwkv_w0, rwkv_w2=rwkv_w2, rwkv_a0=rwkv_a0, rwkv_a2=rwkv_a2,
               rwkv_g2=rwkv_g2, rwkv_k_k=rwkv_k_k, rwkv_k_a=rwkv_k_a,
               rwkv_r_k=rwkv_r_k.reshape(rwkv_r_k.shape[0], A_WIDTH), rwkv_gn_w=rwkv_gn_w,
               rwkv_gn_b=rwkv_gn_b, lru_conv_w=lru_conv_w, lru_conv_b=lru_conv_b, lru_wa=lru_wa, lru_ba=lru_ba,
               lru_wx=lru_wx, lru_bx=lru_bx, lru_lambda=lru_lambda, w_in_c=w_in_c, w_out_c=w_out_c,
               gdn_conv_w=gdn_conv_w, gdn_a_log=gdn_a_log, gdn_dt_bias=gdn_dt_bias, gdn_norm_w=gdn_norm_w,
               ffn_w_gate=ffn_w_gate, ffn_w_up=ffn_w_up, ffn_conv_w=ffn_conv_w, ffn_conv_b=ffn_conv_b,
               ffn_w_down=ffn_w_down)
    mods_p, mods_s = _modulation(c_prompt, c_sample, w_mod, b_mod)
    outs_p = _trunk_prompt(x_prompt, mods_p, prm)
    outs_s = _trunk_sample(x_sample, mods_s,
                           (state_rwkv_wkv, state_rwkv_shift, state_lru_h, state_lru_conv, state_gdn,
                            state_gdn_conv, state_ffn_conv), prm)
    return (outs_p[0], outs_s[0]) + tuple(outs_p[1:]) + tuple(outs_s[1:])
```

```python
import functools

import jax
import jax.numpy as jnp
from jax import lax
from jax.experimental import pallas as pl
from jax.experimental.pallas import tpu as pltpu

F32 = jnp.float32
BF16 = jnp.bfloat16

D_MODEL = 2048
DEPTH = 4
A_HEAD_DIM = 64
A_WIDTH = D_MODEL // 2
A_HEADS = A_WIDTH // A_HEAD_DIM
LORA_W = 64
LORA_A = 64
LORA_G = 160
LORA_ALL = LORA_W + LORA_A + LORA_G
P_A_MAIN = 3 * A_WIDTH
P_A = P_A_MAIN + LORA_ALL
A_GN_EPS = 64e-5
B_WIDTH = D_MODEL - A_WIDTH
B_BLOCK_DIM = 64
LRU_C = 8.0
C_HEAD = 128
C_K_HEADS = D_MODEL // C_HEAD
C_V_HEADS = 2 * C_K_HEADS
C_KEY_DIM = C_K_HEADS * C_HEAD
C_VAL_DIM = C_V_HEADS * C_HEAD
C_QKV_DIM = 2 * C_KEY_DIM + C_VAL_DIM
D_FF = 5632
NORM_EPS = 1e-6

LANES = 128
SUBLANES = 8
HIST = SUBLANES
CHUNK = 64
VMEM_LIMIT = 56 * 1024 * 1024
MM_VMEM_BUDGET = 46 * 1024 * 1024

_NN = (((1,), (0,)), ((), ()))
_NT = (((1,), (1,)), ((), ()))
_TN = (((0,), (0,)), ((), ()))
_BNN = (((2,), (1,)), ((0,), (0,)))
_BNT = (((2,), (2,)), ((0,), (0,)))
_BTN = (((1,), (1,)), ((0,), (0,)))


def _cparams(n_axes):
    return pltpu.CompilerParams(dimension_semantics=("arbitrary",) * n_axes,
                                vmem_limit_bytes=VMEM_LIMIT)


def _bdot(a, b, dims=_NN):
    return lax.dot_general(a.astype(BF16), b.astype(BF16), dims, preferred_element_type=F32)


def _split2(a):
    hi = a.astype(BF16)
    lo = (a - hi.astype(F32)).astype(BF16)
    return hi, lo


def _dot3(a, b, dims=_NN):
    ah, al = _split2(a)
    bh, bl = _split2(b)
    d = lambda x, y: lax.dot_general(x, y, dims, preferred_element_type=F32)
    return d(ah, bh) + (d(ah, bl) + d(al, bh))


def _split3(a):
    h0 = a.astype(BF16)
    r1 = a - h0.astype(F32)
    h1 = r1.astype(BF16)
    h2 = (r1 - h1.astype(F32)).astype(BF16)
    return h0, h1, h2


def _sel_l(mask01, x, dims=_NN):
    m = mask01.astype(BF16)
    d = lambda y: lax.dot_general(m, y, dims, preferred_element_type=F32)
    h0, h1, h2 = _split3(x)
    return d(h0) + (d(h1) + d(h2))


def _sel_r(x, mask01):
    m = mask01.astype(BF16)
    d = lambda y: lax.dot_general(y, m, _NN, preferred_element_type=F32)
    h0, h1, h2 = _split3(x)
    return d(h0) + (d(h1) + d(h2))


def _iota2(shape, axis):
    return lax.broadcasted_iota(jnp.int32, shape, axis)


def _tri(n, strict):
    i = _iota2((n, n), 0)
    j = _iota2((n, n), 1)
    return (i > j) if strict else (i >= j)


def _unit_lower_inverse(low, mm):
    n = low.shape[-1]
    dims = _BNN if low.ndim == 3 else _NN
    eye = (_iota2((n, n), 0) == _iota2((n, n), 1)).astype(F32)
    m = -low
    inv = eye + m
    p = m
    span = 2
    while span < n:
        p = mm(p, p, dims)
        inv = inv + mm(inv, p, dims)
        span *= 2
    return inv


def _unit_lower_inverse_refined(low):
    n = low.shape[-1]
    dims = _BNN if low.ndim == 3 else _NN
    eye = (_iota2((n, n), 0) == _iota2((n, n), 1)).astype(F32)
    x0 = _unit_lower_inverse(low, _bdot)
    resid = eye - _dot3(eye + low, x0, dims)
    return x0 + _bdot(x0, resid, dims)


def _block_tri(tt, block):
    i = _iota2((tt, tt), 0)
    j = _iota2((tt, tt), 1)
    return (i // block == j // block) & (i >= j)


def _neg_expm1(x):
    return -jnp.tanh(0.5 * x) * (jnp.exp(x) + 1.0)


def _segment_sum(x, seg):
    i = _iota2((LANES, LANES), 0) // seg
    j = _iota2((LANES, LANES), 1) // seg
    same = (i == j)
    parts = [_sel_r(x[:, g * LANES:(g + 1) * LANES], same) for g in range(x.shape[1] // LANES)]
    return parts[0] if len(parts) == 1 else jnp.concatenate(parts, axis=1)


def _rwkv_rows(p_main, q_main, p_lora, q_lora, mu_main, mu_lora, w0, w2, a0, a2, g2, k_k, k_a):
    pm = p_main + (q_main - p_main) * mu_main
    pl_ = p_lora + (q_lora - p_lora) * mu_lora
    r = pm[:, :A_WIDTH]
    k = pm[:, A_WIDTH:2 * A_WIDTH]
    v = pm[:, 2 * A_WIDTH:]
    xw = pl_[:, :LORA_W]
    xa = pl_[:, LORA_W:LORA_W + LORA_A]
    xg = pl_[:, LORA_W + LORA_A:LORA_ALL]
    w_ll = -jax.nn.softplus(-(w0 + _bdot(jnp.tanh(xw), w2))) - 0.5
    lw = -jnp.exp(w_ll)
    a = jax.nn.sigmoid(a0 + _bdot(xa, a2))
    g = _bdot(jax.nn.sigmoid(xg), g2)
    kkn = k * k_k
    kf = k * (1.0 + (a - 1.0) * k_a)
    return r, kf, v, kkn, a, lw, g


def _rwkv_head_out(y, r, kf, v, g, r_k, gn_w, gn_b):
    mean = jnp.mean(y, axis=-1, keepdims=True)
    var = jnp.mean(jnp.square(y - mean), axis=-1, keepdims=True)
    yn = (y - mean) * lax.rsqrt(var + A_GN_EPS) * gn_w + gn_b
    bonus = jnp.sum(r * kf * r_k, axis=-1, keepdims=True) * v
    return (yn + bonus) * g


def _lru_rows(gate, xc, wa_bd, wx_bd, ba, bx, lam):
    ng = xc.shape[1] // LANES
    ra = jnp.concatenate([_bdot(xc[:, g * LANES:(g + 1) * LANES], wa_bd[g]) for g in range(ng)], axis=1)
    rx = jnp.concatenate([_bdot(xc[:, g * LANES:(g + 1) * LANES], wx_bd[g]) for g in range(ng)], axis=1)
    r_gate = jax.nn.sigmoid(ra + ba)
    i_gate = jax.nn.sigmoid(rx + bx)
    log_a = -LRU_C * r_gate * jax.nn.softplus(-lam)
    a = jnp.exp(log_a)
    b = jnp.sqrt(_neg_expm1(2.0 * log_a)) * (i_gate * xc)
    return a, b, jax.nn.gelu(gate)


def _gdn_act(y, is_qk, q_scale):
    y = jax.nn.silu(y)
    outs = []
    for h in range(y.shape[1] // C_HEAD):
        yh = y[:, h * C_HEAD:(h + 1) * C_HEAD]
        nrm = yh * lax.rsqrt(jnp.sum(yh * yh, axis=-1, keepdims=True) + 1e-6) * q_scale
        outs.append(jnp.where(is_qk, nrm, yh))
    return jnp.concatenate(outs, axis=1)


def _gdn_head_out(o, z, norm_w):
    o = o * lax.rsqrt(jnp.mean(o * o, axis=-1, keepdims=True) + NORM_EPS) * norm_w
    return o * jax.nn.silu(z)


def _w_block(w_ref):
    return w_ref[0] if len(w_ref.shape) == 3 else w_ref[...]


def _mm_body(*refs, n_x, has_res, transposed):
    x_refs = refs[:n_x]
    w_refs = refs[n_x:2 * n_x]
    k = 2 * n_x
    if has_res:
        res_ref, gate_ref = refs[k], refs[k + 1]
        k += 2
    o_ref = refs[k]
    wbf_refs = refs[k + 1:k + 1 + n_x]

    @pl.when((pl.program_id(1) == 0) & (pl.program_id(2) == 0))
    def _():
        for w_ref, wbf_ref in zip(w_refs, wbf_refs):
            wbf_ref[...] = _w_block(w_ref).astype(BF16)

    acc = None
    for x_ref, wbf_ref, tr in zip(x_refs, wbf_refs, transposed):
        d = lax.dot_general(x_ref[0], wbf_ref[...], _NT if tr else _NN, preferred_element_type=F32)
        acc = d if acc is None else acc + d
    if has_res:
        acc = res_ref[0] + gate_ref[0] * acc
    o_ref[0] = acc.astype(o_ref.dtype)


def _pick_tiles(n_out, k_sum, rows, seq, n_out_bufs):
    best = None
    for tm in [t for t in (1024, 512, 256, 128) if seq % t == 0] or [seq]:
        for tn in (1024, 512, 256, 128):
            if n_out % tn:
                continue
            need = (2 * tm * k_sum * 2 + 2 * k_sum * tn * 4 + k_sum * tn * 2
                    + (2 * n_out_bufs + 1) * tm * tn * 4)
            if need > MM_VMEM_BUDGET:
                continue
            traffic = (n_out // tn) * rows * k_sum * 2
            key = (traffic, -tn, -tm)
            if best is None or key < best[0]:
                best = (key, tm, tn)
    if best is None:
        raise ValueError("no projection tile fits VMEM")
    return best[1], best[2]


def _mm(xs, ws, *, n_out, col0=0, tm=None, tn=None, out_dtype=F32, res=None, gate=None, gate_blk=0, name="proj"):
    B, T, _ = xs[0].shape
    ks = [x.shape[-1] for x in xs]
    has_res = res is not None
    if tn is None:
        tm, tn = _pick_tiles(n_out, sum(ks), B * T, T, 2 if has_res else 1)
    elif tm is None:
        tm = min(1024, T)
    assert n_out % tn == 0 and T % tm == 0
    in_specs = [pl.BlockSpec((1, tm, k), lambda j, b, t: (b, t, 0)) for k in ks]
    scratch = []
    for k, (w, layer, rb, tr) in zip(ks, ws):
        if tr:
            assert col0 % SUBLANES == 0 and tn % SUBLANES == 0 and w.shape[2] == k
            in_specs.append(pl.BlockSpec(
                (pl.Element(1), pl.Element(tn), pl.Element(k)),
                functools.partial(lambda j, b, t, layer: (layer, pl.multiple_of(col0 + j * tn, SUBLANES), 0),
                                  layer=layer)))
            scratch.append(pltpu.VMEM((tn, k), BF16))
        else:
            assert col0 % tn == 0
            in_specs.append(pl.BlockSpec(
                (None, k, tn),
                functools.partial(lambda j, b, t, layer, rb: (layer, rb, col0 // tn + j), layer=layer, rb=rb)))
            scratch.append(pltpu.VMEM((k, tn), BF16))
    args = list(xs) + [w for w, _, _, _ in ws]
    if has_res:
        in_specs.append(pl.BlockSpec((1, tm, tn), lambda j, b, t: (b, t, j)))
        gb0 = gate_blk * (n_out // tn)
        if gate.shape[1] == 1:
            in_specs.append(pl.BlockSpec((1, 1, tn), lambda j, b, t: (b, 0, gb0 + j)))
        else:
            in_specs.append(pl.BlockSpec((1, tm, tn), lambda j, b, t: (b, t, gb0 + j)))
        args += [res, gate]
    return pl.pallas_call(
        functools.partial(_mm_body, n_x=len(xs), has_res=has_res, transposed=tuple(w[3] for w in ws)),
        grid=(n_out // tn, B, T // tm),
        in_specs=in_specs,
        out_specs=pl.BlockSpec((1, tm, tn), lambda j, b, t: (b, t, j)),
        out_shape=jax.ShapeDtypeStruct((B, T, n_out), out_dtype),
        scratch_shapes=scratch,
        compiler_params=_cparams(3),
        name=name,
    )(*args)


def _mod_body(c_ref, w_ref, b_ref, op_ref, os_ref):
    bs = os_ref.shape[1]
    y = _bdot(jax.nn.silu(c_ref[...]), w_ref[0]) + b_ref[0]
    os_ref[0] = y[:bs]
    op_ref[0] = y[bs:]


def _modulation(c_p, c_s, w_mod, b_mod):
    depth, d, n = w_mod.shape
    tn = 1024
    bp, bs = c_p.shape[0], c_s.shape[0]
    return pl.pallas_call(
        _mod_body,
        grid=(depth, n // tn),
        in_specs=[pl.BlockSpec((bs + bp, d), lambda l, j: (0, 0)),
                  pl.BlockSpec((1, d, tn), lambda l, j: (l, 0, j)),
                  pl.BlockSpec((1, 1, tn), lambda l, j: (l, 0, j))],
        out_specs=[pl.BlockSpec((1, bp, tn), lambda l, j: (l, 0, j)),
                   pl.BlockSpec((1, bs, tn), lambda l, j: (l, 0, j))],
        out_shape=[jax.ShapeDtypeStruct((depth, bp, n), F32),
                   jax.ShapeDtypeStruct((depth, bs, n), F32)],
        compiler_params=_cparams(2),
        name="modulation",
    )(jnp.concatenate([c_s, c_p], axis=0), w_mod, b_mod.reshape(depth, 1, n))


def _norm_body(*refs, has_mod):
    if has_mod:
        x_ref, g_ref, sc_ref, sh_ref, o_ref = refs
    else:
        x_ref, g_ref, o_ref = refs
    x = x_ref[0]
    y = x * lax.rsqrt(jnp.mean(x * x, axis=-1, keepdims=True) + NORM_EPS) * g_ref[...]
    if has_mod:
        y = y * (1.0 + sc_ref[0]) + sh_ref[0]
    o_ref[0] = y.astype(o_ref.dtype)


def _norm(x, gain, mod=None, sc_blk=0, sh_blk=0, *, tt, out_dtype=BF16):
    B, T, d = x.shape
    in_specs = [pl.BlockSpec((1, tt, d), lambda b, t: (b, t, 0)),
                pl.BlockSpec((1, d), lambda b, t: (0, 0))]
    args = [x, gain.reshape(1, d)]
    if mod is not None:
        for blk in (sc_blk, sh_blk):
            if mod.shape[1] == 1:
                in_specs.append(pl.BlockSpec((1, 1, d), functools.partial(lambda b, t, blk: (b, 0, blk), blk=blk)))
            else:
                in_specs.append(pl.BlockSpec((1, tt, d), functools.partial(lambda b, t, blk: (b, t, blk), blk=blk)))
        args += [mod, mod]
    return pl.pallas_call(
        functools.partial(_norm_body, has_mod=mod is not None),
        grid=(B, T // tt),
        in_specs=in_specs,
        out_specs=pl.BlockSpec((1, tt, d), lambda b, t: (b, t, 0)),
        out_shape=jax.ShapeDtypeStruct((B, T, d), out_dtype),
        compiler_params=_cparams(2),
        name="norm",
    )(*args)


def _ffn_up_prompt_body(h_ref, wg_ref, wu_ref, cw_ref, cb_ref, act_ref, st_ref, wg_bf, wu_bf, buf):
    t = pl.program_id(2)
    tm = h_ref.shape[1]

    @pl.when((pl.program_id(1) == 0) & (t == 0))
    def _():
        wg_bf[...] = wg_ref[0].astype(BF16)
        wu_bf[...] = wu_ref[0].astype(BF16)

    @pl.when(t == 0)
    def _():
        buf[0:HIST, :] = jnp.zeros((HIST, buf.shape[1]), F32)

    h = h_ref[0]
    buf[HIST:, :] = jnp.dot(h, wg_bf[...], preferred_element_type=F32)
    cw = cw_ref[0]
    u = (buf[HIST - 2:HIST - 2 + tm, :] * cw[0:1] + buf[HIST - 1:HIST - 1 + tm, :] * cw[1:2]
         + buf[HIST:, :] * cw[2:3] + cb_ref[0])
    up = jnp.dot(h, wu_bf[...], preferred_element_type=F32)
    act_ref[0] = (jax.nn.gelu(u) * up).astype(act_ref.dtype)
    st_ref[0] = buf[HIST + tm - 2:HIST + tm, :]
    buf[0:HIST, :] = buf[tm:tm + HIST, :]


def _ffn_up_prompt(h, w_gate, w_up, conv_w, conv_b, layer, *, tm, tn):
    B, T, d = h.shape
    n = w_gate.shape[-1]
    return pl.pallas_call(
        _ffn_up_prompt_body,
        grid=(n // tn, B, T // tm),
        in_specs=[pl.BlockSpec((1, tm, d), lambda j, b, t: (b, t, 0)),
                  pl.BlockSpec((1, d, tn), lambda j, b, t: (layer, 0, j)),
                  pl.BlockSpec((1, d, tn), lambda j, b, t: (layer, 0, j)),
                  pl.BlockSpec((1, 3, tn), lambda j, b, t: (layer, 0, j)),
                  pl.BlockSpec((1, 1, tn), lambda j, b, t: (layer, 0, j))],
        out_specs=[pl.BlockSpec((1, tm, tn), lambda j, b, t: (b, t, j)),
                   pl.BlockSpec((1, 2, tn), lambda j, b, t: (b, 0, j))],
        out_shape=[jax.ShapeDtypeStruct((B, T, n), BF16),
                   jax.ShapeDtypeStruct((B, 2, n), F32)],
        scratch_shapes=[pltpu.VMEM((d, tn), BF16), pltpu.VMEM((d, tn), BF16),
                        pltpu.VMEM((HIST + tm, tn), F32)],
        compiler_params=_cparams(3),
        name="ffn_up_prompt",
    )(h, w_gate, w_up, conv_w, conv_b.reshape(conv_b.shape[0], 1, n))


def _ffn_up_sample_body(h_ref, wg_ref, wu_ref, cw_ref, cb_ref, st_ref, act_ref, nst_ref):
    h = h_ref[...]
    pre = _bdot(h, wg_ref[0])
    cw = cw_ref[0]
    u = st_ref[:, 0, :] * cw[0:1] + st_ref[:, 1, :] * cw[1:2] + pre * cw[2:3] + cb_ref[0]
    up = _bdot(h, wu_ref[0])
    act_ref[...] = (jax.nn.gelu(u) * up).astype(act_ref.dtype)
    nst_ref[:, 0, :] = st_ref[:, 1, :]
    nst_ref[:, 1, :] = pre


def _ffn_up_sample(h, w_gate, w_up, conv_w, conv_b, state, layer, *, tn):
    bs, d = h.shape
    n = w_gate.shape[-1]
    return pl.pallas_call(
        _ffn_up_sample_body,
        grid=(n // tn,),
        in_specs=[pl.BlockSpec((bs, d), lambda j: (0, 0)),
                  pl.BlockSpec((1, d, tn), lambda j: (layer, 0, j)),
                  pl.BlockSpec((1, d, tn), lambda j: (layer, 0, j)),
                  pl.BlockSpec((1, 3, tn), lambda j: (layer, 0, j)),
                  pl.BlockSpec((1, 1, tn), lambda j: (layer, 0, j)),
                  pl.BlockSpec((bs, 2, tn), lambda j: (0, 0, j))],
        out_specs=[pl.BlockSpec((bs, tn), lambda j: (0, j)),
                   pl.BlockSpec((bs, 2, tn), lambda j: (0, 0, j))],
        out_shape=[jax.ShapeDtypeStruct((bs, n), BF16),
                   jax.ShapeDtypeStruct((bs, 2, n), F32)],
        compiler_params=_cparams(1),
        name="ffn_up_sample",
    )(h, w_gate, w_up, conv_w, conv_b.reshape(conv_b.shape[0], 1, n), state)


def _rwkv_prep_prompt_body(pm_ref, pl_ref, mum_ref, mul_ref, w0_ref, w2_ref, a0_ref, a2_ref, g2_ref,
                           kk_ref, ka_ref, r_o, kf_o, v_o, kkn_o, a_o, lw_o, g_o, bufm, bufl):
    t = pl.program_id(1)
    tt = pm_ref.shape[1]

    @pl.when(t == 0)
    def _():
        bufm[0:HIST, :] = jnp.zeros((HIST, bufm.shape[1]), F32)
        bufl[0:HIST, :] = jnp.zeros((HIST, bufl.shape[1]), F32)

    p_main = pm_ref[0]
    p_lora = pl_ref[0][:, :LORA_ALL]
    bufm[HIST:, :] = p_main
    bufl[HIST:, :] = p_lora
    q_main = bufm[HIST - 1:HIST - 1 + tt, :]
    q_lora = bufl[HIST - 1:HIST - 1 + tt, :]
    outs = _rwkv_rows(p_main, q_main, p_lora, q_lora, mum_ref[...], mul_ref[...], w0_ref[...], w2_ref[0],
                      a0_ref[...], a2_ref[0], g2_ref[0], kk_ref[...], ka_ref[...])
    for o_ref, val in zip((r_o, kf_o, v_o, kkn_o, a_o, lw_o, g_o), outs):
        o_ref[0] = val
    bufm[0:HIST, :] = bufm[tt:tt + HIST, :]
    bufl[0:HIST, :] = bufl[tt:tt + HIST, :]


def _row(v):
    return v.reshape(1, -1)


def _rwkv_prep_prompt(p_main, p_lora, prm, i, *, tt):
    B, T, _ = p_main.shape
    lw_pad = p_lora.shape[-1]
    mu = prm['rwkv_mu'][i]
    full = lambda shape: pl.BlockSpec(shape, lambda b, t: (0,) * len(shape))
    lay3 = lambda shape: pl.BlockSpec((1,) + shape, lambda b, t: (i, 0, 0))
    out_spec = pl.BlockSpec((1, tt, A_WIDTH), lambda b, t: (b, t, 0))
    return pl.pallas_call(
        _rwkv_prep_prompt_body,
        grid=(B, T // tt),
        in_specs=[pl.BlockSpec((1, tt, P_A_MAIN), lambda b, t: (b, t, 0)),
                  pl.BlockSpec((1, tt, lw_pad), lambda b, t: (b, t, 0)),
                  full((1, P_A_MAIN)), full((1, LORA_ALL)), full((1, A_WIDTH)),
                  lay3((LORA_W, A_WIDTH)), full((1, A_WIDTH)), lay3((LORA_A, A_WIDTH)),
                  lay3((LORA_G, A_WIDTH)), full((1, A_WIDTH)), full((1, A_WIDTH))],
        out_specs=[out_spec] * 7,
        out_shape=[jax.ShapeDtypeStruct((B, T, A_WIDTH), F32)] * 7,
        scratch_shapes=[pltpu.VMEM((HIST + tt, P_A_MAIN), F32), pltpu.VMEM((HIST + tt, LORA_ALL), F32)],
        compiler_params=_cparams(2),
        name="rwkv_prep_prompt",
    )(p_main, p_lora, _row(mu[:P_A_MAIN]), _row(mu[P_A_MAIN:]), _row(prm['rwkv_w0'][i]), prm['rwkv_w2'],
      _row(prm['rwkv_a0'][i]), prm['rwkv_a2'], prm['rwkv_g2'], _row(prm['rwkv_k_k'][i]), _row(prm['rwkv_k_a'][i]))


def _wkv_prompt_body(r_ref, kf_ref, v_ref, kkn_ref, a_ref, lw_ref, g_ref, rk_ref, gw_ref, gb_ref,
                     ya_ref, st_ref, s_scr):
    t = pl.program_id(2)
    tt = r_ref.shape[1]
    n = A_HEAD_DIM
    C = CHUNK
    nc = tt // C
    nh = r_ref.shape[2] // n

    @pl.when(t == 0)
    def _():
        s_scr[...] = jnp.zeros(s_scr.shape, F32)

    def units(x):
        parts = [x[:, h * n:(h + 1) * n].reshape(nc, C, n) for h in range(nh)]
        return jnp.stack(parts, axis=1).reshape(nc * nh, C, n)

    lw_all = lw_ref[0]
    lc = units(_sel_l(_block_tri(tt, C), lw_all))
    r, kf, v, kkn, a, lw = map(units, (r_ref[0], kf_ref[0], v_ref[0], kkn_ref[0], a_ref[0], lw_all))
    incl = _tri(C, False)
    strict = _tri(C, True)

    kk = kkn * lax.rsqrt(jnp.sum(kkn * kkn, axis=-1, keepdims=True) + 1e-12)
    bb = kk * a
    p = jnp.exp(lc)
    pinv = jnp.exp(-lc)
    kt = kf * pinv
    bt = bb * pinv
    at = kk * jnp.exp(lc - lw)
    rt = r * p
    gram = _bdot(jnp.concatenate([at, rt], axis=1), jnp.concatenate([bt, kt], axis=1), _BNT)
    a_ab = jnp.where(strict, gram[:, :C, :C], 0.0)
    a_ak = jnp.where(strict, gram[:, :C, C:], 0.0)
    a_rb = jnp.where(incl, gram[:, C:, :C], 0.0)
    a_rk = jnp.where(incl, gram[:, C:, C:], 0.0)
    tinv = _unit_lower_inverse(a_ab, _bdot)
    av = _bdot(jnp.concatenate([a_ak, a_rk], axis=1), v, _BNN)
    tz = _bdot(tinv, jnp.concatenate([at, av[:, :C]], axis=2), _BNN)
    rz = _bdot(a_rb, tz, _BNN)
    lhs_s = jnp.concatenate([tz[:, :, :n], rt - rz[:, :, :n]], axis=1)
    z0 = tz[:, :, n:]
    y0 = av[:, C:] - rz[:, :, n:]
    plast = p[:, C - 1:C, :]
    upd = jnp.concatenate([kt * plast, -(bt * plast)], axis=1)

    S = s_scr[...]
    ys = []
    for c in range(nc):
        sl = slice(c * nh, (c + 1) * nh)
        xs = _bdot(lhs_s[sl], S, _BNT)
        z = xs[:, :C] + z0[sl]
        ys.append(xs[:, C:] + y0[sl])
        S = S * plast[sl] + _bdot(jnp.concatenate([v[sl], z], axis=1), upd[sl], _BTN)
    s_scr[...] = S

    outs = []
    for h in range(nh):
        cols = slice(h * n, (h + 1) * n)
        y = jnp.concatenate([ys[c][h] for c in range(nc)], axis=0)
        outs.append(_rwkv_head_out(y, r_ref[0, :, cols], kf_ref[0, :, cols], v_ref[0, :, cols],
                                   g_ref[0, :, cols], rk_ref[:, cols], gw_ref[:, cols], gb_ref[:, cols]))
    ya_ref[0] = jnp.concatenate(outs, axis=1).astype(ya_ref.dtype)

    @pl.when(t == pl.num_programs(2) - 1)
    def _():
        st_ref[0] = S


def _wkv_prompt(vals, prm, i, *, tt, heads_per_step):
    B, T, _ = vals[0].shape
    wblk = heads_per_step * A_HEAD_DIM
    seq_spec = pl.BlockSpec((1, tt, wblk), lambda b, h, t: (b, t, h))
    par_spec = pl.BlockSpec((1, wblk), lambda b, h, t: (0, h))
    return pl.pallas_call(
        _wkv_prompt_body,
        grid=(B, A_HEADS // heads_per_step, T // tt),
        in_specs=[seq_spec] * 7 + [par_spec] * 3,
        out_specs=[seq_spec,
                   pl.BlockSpec((1, heads_per_step, A_HEAD_DIM, A_HEAD_DIM), lambda b, h, t: (b, h, 0, 0))],
        out_shape=[jax.ShapeDtypeStruct((B, T, A_WIDTH), BF16),
                   jax.ShapeDtypeStruct((B, A_HEADS, A_HEAD_DIM, A_HEAD_DIM), F32)],
        scratch_shapes=[pltpu.VMEM((heads_per_step, A_HEAD_DIM, A_HEAD_DIM), F32)],
        compiler_params=_cparams(3),
        name="wkv_prompt",
    )(*vals, _row(prm['rwkv_r_k'][i]), _row(prm['rwkv_gn_w'][i]), _row(prm['rwkv_gn_b'][i]))


def _lru_prompt_body(gate_ref, xb_ref, cw_ref, cb_ref, wa_ref, wx_ref, ba_ref, bx_ref, lam_ref,
                     y_ref, h_ref, cst_ref, buf, h_scr):
    t = pl.program_id(1)
    tt = xb_ref.shape[1]
    w = xb_ref.shape[2]

    @pl.when(t == 0)
    def _():
        buf[0:HIST, :] = jnp.zeros((HIST, w), F32)
        h_scr[...] = jnp.zeros(h_scr.shape, F32)

    buf[HIST:, :] = xb_ref[0]
    cw = cw_ref[0]
    xc = cb_ref[...] + buf[HIST:, :] * cw[3:4]
    for j in range(3):
        xc = xc + buf[HIST - 3 + j:HIST - 3 + j + tt, :] * cw[j:j + 1]
    a, b, gact = _lru_rows(gate_ref[0], xc, wa_ref, wx_ref, ba_ref[...], bx_ref[...], lam_ref[...])
    row = _iota2((tt, w), 0)
    d = 1
    while d < tt:
        a_sh = jnp.where(row >= d, pltpu.roll(a, d, axis=0), 1.0)
        b_sh = jnp.where(row >= d, pltpu.roll(b, d, axis=0), 0.0)
        b = a * b_sh + b
        a = a * a_sh
        d *= 2
    h = a * h_scr[0:1, :] + b
    y_ref[0] = (h * gact).astype(y_ref.dtype)
    h_last = h[tt - 1:tt, :]
    h_scr[0:1, :] = h_last
    h_ref[0] = h_last
    cst_ref[0] = buf[HIST + tt - 3:HIST + tt, :]
    buf[0:HIST, :] = buf[tt:tt + HIST, :]


def _block_diag_pairs(w):
    nb, n, _ = w.shape
    w = w.reshape(nb // 2, 2, n, n)
    z = jnp.zeros((nb // 2, n, n), w.dtype)
    top = jnp.concatenate([w[:, 0], z], axis=2)
    bot = jnp.concatenate([z, w[:, 1]], axis=2)
    return jnp.concatenate([top, bot], axis=1)


def _lru_prompt(pb, prm, i, *, tt):
    B, T, _ = pb.shape
    w = B_WIDTH
    ng = w // LANES
    full = lambda shape: pl.BlockSpec(shape, lambda b, t: (0,) * len(shape))
    return pl.pallas_call(
        _lru_prompt_body,
        grid=(B, T // tt),
        in_specs=[pl.BlockSpec((1, tt, w), lambda b, t: (b, t, 0)),
                  pl.BlockSpec((1, tt, w), lambda b, t: (b, t, 1)),
                  pl.BlockSpec((1, 4, w), lambda b, t: (i, 0, 0)),
                  full((1, w)), full((ng, LANES, LANES)), full((ng, LANES, LANES)),
                  full((1, w)), full((1, w)), full((1, w))],
        out_specs=[pl.BlockSpec((1, tt, w), lambda b, t: (b, t, 0)),
                   pl.BlockSpec((1, 1, w), lambda b, t: (b, 0, 0)),
                   pl.BlockSpec((1, 3, w), lambda b, t: (b, 0, 0))],
        out_shape=[jax.ShapeDtypeStruct((B, T, w), BF16),
                   jax.ShapeDtypeStruct((B, 1, w), F32),
                   jax.ShapeDtypeStruct((B, 3, w), F32)],
        scratch_shapes=[pltpu.VMEM((HIST + tt, w), F32), pltpu.VMEM((SUBLANES, w), F32)],
        compiler_params=_cparams(2),
        name="lru_prompt",
    )(pb, pb, prm['lru_conv_w'], _row(prm['lru_conv_b'][i]), _block_diag_pairs(prm['lru_wa'][i]),
      _block_diag_pairs(prm['lru_wx'][i]), _row(prm['lru_ba'][i]), _row(prm['lru_bx'][i]),
      _row(prm['lru_lambda'][i]))


def _in_c_conv_prompt_body(h_ref, w_ref, cw_ref, o_ref, st_ref, w_bf, buf):
    j = pl.program_id(0)
    t = pl.program_id(2)
    tm = h_ref.shape[1]
    tn = w_ref.shape[1]

    @pl.when((pl.program_id(1) == 0) & (t == 0))
    def _():
        w_bf[...] = w_ref[0].astype(BF16)

    @pl.when(t == 0)
    def _():
        buf[0:HIST, :] = jnp.zeros((HIST, tn), F32)

    buf[HIST:, :] = lax.dot_general(h_ref[0], w_bf[...], _NT, preferred_element_type=F32)
    cw = cw_ref[0]
    y = buf[HIST:, :] * cw[3:4]
    for jj in range(3):
        y = y + buf[HIST - 3 + jj:HIST - 3 + jj + tm, :] * cw[jj:jj + 1]
    n_qk_blocks = 2 * C_KEY_DIM // tn
    q_scale = jnp.where(j < n_qk_blocks // 2, C_HEAD ** -0.5, 1.0).astype(F32)
    o_ref[0] = _gdn_act(y, j < n_qk_blocks, q_scale)
    st_ref[0] = buf[HIST + tm - 3:HIST + tm, :]
    buf[0:HIST, :] = buf[tm:tm + HIST, :]


def _in_c_conv_prompt(h, w_t, conv_w, i, *, tm, tn):
    B, T, d = h.shape
    return pl.pallas_call(
        _in_c_conv_prompt_body,
        grid=(C_QKV_DIM // tn, B, T // tm),
        in_specs=[pl.BlockSpec((1, tm, d), lambda j, b, t: (b, t, 0)),
                  pl.BlockSpec((pl.Element(1), pl.Element(tn), pl.Element(d)),
                               lambda j, b, t: (i, pl.multiple_of(j * tn, SUBLANES), 0)),
                  pl.BlockSpec((1, 4, tn), lambda j, b, t: (i, 0, j))],
        out_specs=[pl.BlockSpec((1, tm, tn), lambda j, b, t: (b, t, j)),
                   pl.BlockSpec((1, 3, tn), lambda j, b, t: (b, 0, j))],
        out_shape=[jax.ShapeDtypeStruct((B, T, C_QKV_DIM), F32),
                   jax.ShapeDtypeStruct((B, 3, C_QKV_DIM), F32)],
        scratch_shapes=[pltpu.VMEM((tn, d), BF16), pltpu.VMEM((HIST + tm, tn), F32)],
        compiler_params=_cparams(3),
        name="in_c_conv_prompt",
    )(h, w_t, conv_w)


def _gdn_gates(ba, a_log, dt_bias):
    hv = C_V_HEADS
    beta = jax.nn.sigmoid(ba[..., :hv])
    g = -jnp.exp(a_log) * jax.nn.softplus(ba[..., hv:2 * hv] + dt_bias)
    return beta, g


def _lane_bcast_col(x, col):
    n = x.shape[1]
    onehot = _iota2((n, LANES), 0) == col
    return _sel_r(x, onehot)


def _gdn_prompt_body(q_ref, k_ref, v_ref, z_ref, ba_ref, alog_ref, dtb_ref, nw_ref, o_ref, st_ref, s_scr):
    hblk = pl.program_id(1)
    t = pl.program_id(2)
    tt = q_ref.shape[1]
    C = CHUNK
    nc = tt // C
    rep = C_V_HEADS // C_K_HEADS
    n_kh = q_ref.shape[2] // C_HEAD
    nh = n_kh * rep

    @pl.when(t == 0)
    def _():
        s_scr[...] = jnp.zeros(s_scr.shape, F32)

    beta_all, g_all = _gdn_gates(ba_ref[0], alog_ref[...], dtb_ref[...])
    gc_all = _sel_l(_block_tri(tt, C), g_all)
    blocks = lambda x: x.reshape(nc, C, x.shape[-1])
    qs, ks, vs, betas, gcs = [], [], [], [], []
    for kh in range(n_kh):
        q_h = blocks(q_ref[0, :, kh * C_HEAD:(kh + 1) * C_HEAD])
        k_h = blocks(k_ref[0, :, kh * C_HEAD:(kh + 1) * C_HEAD])
        for jv in range(rep):
            hu = kh * rep + jv
            hv = hblk * nh + hu
            qs.append(q_h)
            ks.append(k_h)
            vs.append(blocks(v_ref[0, :, hu * C_HEAD:(hu + 1) * C_HEAD]))
            betas.append(blocks(_lane_bcast_col(beta_all, hv)))
            gcs.append(blocks(_lane_bcast_col(gc_all, hv)))
    stack = lambda xs: jnp.stack(xs, axis=1).reshape(nc * nh, C, xs[0].shape[-1])
    q, k, v, beta, gc = map(stack, (qs, ks, vs, betas, gcs))
    incl = _tri(C, False)
    strict = _tri(C, True)

    gcc = gc[:, :, :C]
    diff = gcc - jnp.swapaxes(gcc, 1, 2)
    dec_incl = jnp.exp(jnp.where(incl, diff, -jnp.inf))
    dec_strict = jnp.where(strict, dec_incl, 0.0)
    kb = k * beta
    gram = _bdot(jnp.concatenate([kb, q], axis=1), k, _BNT)
    low = gram[:, :C] * dec_strict
    qk = gram[:, C:] * dec_incl
    tinv = _unit_lower_inverse_refined(low)
    eg = jnp.exp(gc)
    sol = _bdot(tinv, jnp.concatenate([v * beta, kb * eg], axis=2), _BNN)
    u = sol[:, :, :C_HEAD]
    lhs_s = jnp.concatenate([sol[:, :, C_HEAD:], q * eg], axis=1)
    g_last = gc[:, C - 1:C, :]
    k_tail = k * jnp.exp(g_last - gc)
    e_last = jnp.exp(g_last)

    S = s_scr[...]
    os_ = []
    for c in range(nc):
        sl = slice(c * nh, (c + 1) * nh)
        ws = _bdot(lhs_s[sl], S, _BNN)
        v_new = u[sl] - ws[:, :C]
        os_.append(ws[:, C:] + _bdot(qk[sl], v_new, _BNN))
        S = S * e_last[sl] + _bdot(k_tail[sl], v_new, _BTN)
    s_scr[...] = S

    outs = []
    for hu in range(nh):
        cols = slice(hu * C_HEAD, (hu + 1) * C_HEAD)
        o = jnp.concatenate([os_[c][hu] for c in range(nc)], axis=0)
        outs.append(_gdn_head_out(o, z_ref[0, :, cols], nw_ref[...]))
    o_ref[0] = jnp.concatenate(outs, axis=1).astype(o_ref.dtype)

    @pl.when(t == pl.num_programs(2) - 1)
    def _():
        st_ref[0] = S


def _gdn_prompt(qkv_act, z, ba, prm, i, *, tt, kheads_per_step):
    B, T, _ = qkv_act.shape
    rep = C_V_HEADS // C_K_HEADS
    kw = kheads_per_step * C_HEAD
    vw = rep * kw
    k_blk0 = C_KEY_DIM // kw
    v_blk0 = 2 * C_KEY_DIM // vw
    z_blk0 = 0
    nh = kheads_per_step * rep
    full = lambda shape: pl.BlockSpec(shape, lambda b, h, t: (0,) * len(shape))
    return pl.pallas_call(
        _gdn_prompt_body,
        grid=(B, C_K_HEADS // kheads_per_step, T // tt),
        in_specs=[pl.BlockSpec((1, tt, kw), lambda b, h, t: (b, t, h)),
                  pl.BlockSpec((1, tt, kw), lambda b, h, t: (b, t, k_blk0 + h)),
                  pl.BlockSpec((1, tt, vw), lambda b, h, t: (b, t, v_blk0 + h)),
                  pl.BlockSpec((1, tt, vw), lambda b, h, t: (b, t, z_blk0 + h)),
                  pl.BlockSpec((1, tt, 2 * C_V_HEADS), lambda b, h, t: (b, t, 0)),
                  full((1, C_V_HEADS)), full((1, C_V_HEADS)), full((1, C_HEAD))],
        out_specs=[pl.BlockSpec((1, tt, vw), lambda b, h, t: (b, t, h)),
                   pl.BlockSpec((1, nh, C_HEAD, C_HEAD), lambda b, h, t: (b, h, 0, 0))],
        out_shape=[jax.ShapeDtypeStruct((B, T, C_VAL_DIM), BF16),
                   jax.ShapeDtypeStruct((B, C_V_HEADS, C_HEAD, C_HEAD), F32)],
        scratch_shapes=[pltpu.VMEM((nh, C_HEAD, C_HEAD), F32)],
        compiler_params=_cparams(3),
        name="gdn_prompt",
    )(qkv_act, qkv_act, qkv_act, z, ba, _row(prm['gdn_a_log'][i]), _row(prm['gdn_dt_bias'][i]),
      _row(prm['gdn_norm_w'][i]))


def _rwkv_prep_sample_body(pm_ref, pl_ref, qm_ref, ql_ref, mum_ref, mul_ref, w0_ref, w2_ref, a0_ref, a2_ref,
                           g2_ref, kk_ref, ka_ref, r_o, kf_o, v_o, kkn_o, a_o, lw_o, g_o):
    outs = _rwkv_rows(pm_ref[...], qm_ref[...], pl_ref[...][:, :LORA_ALL], ql_ref[...], mum_ref[...],
                      mul_ref[...], w0_ref[...], w2_ref[0], a0_ref[...], a2_ref[0], g2_ref[0],
                      kk_ref[...], ka_ref[...])
    for o_ref, val in zip((r_o, kf_o, v_o, kkn_o, a_o, lw_o, g_o), outs):
        o_ref[...] = val


def _rwkv_prep_sample(p_main, p_lora, shift, prm, i):
    bs = p_main.shape[0]
    mu = prm['rwkv_mu'][i]
    full = lambda shape: pl.BlockSpec(shape, lambda s: (0,) * len(shape))
    lay3 = lambda shape: pl.BlockSpec((1,) + shape, lambda s: (i, 0, 0))
    return pl.pallas_call(
        _rwkv_prep_sample_body,
        grid=(1,),
        in_specs=[full(p_main.shape), full(p_lora.shape), full((bs, P_A_MAIN)), full((bs, LORA_ALL)),
                  full((1, P_A_MAIN)), full((1, LORA_ALL)), full((1, A_WIDTH)),
                  lay3((LORA_W, A_WIDTH)), full((1, A_WIDTH)), lay3((LORA_A, A_WIDTH)),
                  lay3((LORA_G, A_WIDTH)), full((1, A_WIDTH)), full((1, A_WIDTH))],
        out_specs=[full((bs, A_WIDTH))] * 7,
        out_shape=[jax.ShapeDtypeStruct((bs, A_WIDTH), F32)] * 7,
        compiler_params=_cparams(1),
        name="rwkv_prep_sample",
    )(p_main, p_lora, shift[:, :P_A_MAIN], shift[:, P_A_MAIN:], _row(mu[:P_A_MAIN]), _row(mu[P_A_MAIN:]),
      _row(prm['rwkv_w0'][i]), prm['rwkv_w2'], _row(prm['rwkv_a0'][i]), prm['rwkv_a2'], prm['rwkv_g2'],
      _row(prm['rwkv_k_k'][i]), _row(prm['rwkv_k_a'][i]))


STEP_GROUP = 16


def _wkv_step_body(*refs, n_state):
    s_ref = refs[0]
    r_ref, kf_ref, v_ref, kkn_ref, a_ref, lw_ref, g_ref, rk_ref, gw_ref, gb_ref, ns_ref, y_ref = refs[n_state:]
    nh = s_ref.shape[0]
    n = A_HEAD_DIM
    G = STEP_GROUP
    rid = lax.broadcasted_iota(jnp.int32, (G, SUBLANES, n), 1)
    up = lambda x: jnp.broadcast_to(x.reshape(G, 1, n), (G, SUBLANES, n))

    def group(i, carry):
        g0 = pl.multiple_of(i * G, G)
        rows = lambda ref: ref[pl.ds(g0, G), :]
        S = s_ref[pl.ds(g0, G)]
        r, kf, v, kkn, a = rows(r_ref), rows(kf_ref), rows(v_ref), rows(kkn_ref), rows(a_ref)
        kk = kkn * lax.rsqrt(jnp.sum(kkn * kkn, axis=-1, keepdims=True) + 1e-12)
        u = _dot3(up(kk), S, _BNT)
        left = jnp.where(rid == 0, -u, jnp.where(rid == 1, up(v), 0.0))
        right = jnp.where(rid == 0, up(kk * a), jnp.where(rid == 1, up(kf), 0.0))
        s_new = S * jnp.exp(rows(lw_ref)).reshape(G, 1, n) + _dot3(left, right, _BTN)
        ns_ref[pl.ds(g0, G)] = s_new
        y = _dot3(up(r), s_new, _BNT)[:, 0, :]
        y_ref[pl.ds(g0, G), :] = _rwkv_head_out(y, r, kf, v, rows(g_ref), rows(rk_ref), rows(gw_ref),
                                                rows(gb_ref))
        return carry

    lax.fori_loop(0, nh // G, group, 0)


def _layer_state_io(state_all, new_all, spec):
    if new_all is None:
        return [spec], [state_all], {}
    return [spec, pl.BlockSpec(memory_space=pl.ANY)], [state_all, new_all], {1: 0}


def _wkv_step(S_all, new_all, vals, prm, i, *, nh):
    n_tot = S_all.shape[1]
    n = A_HEAD_DIM
    tile = lambda p: jnp.tile(p.reshape(A_HEADS, n), (nh // A_HEADS, 1))
    row_spec = pl.BlockSpec((nh, n), lambda b: (b, 0))
    par_spec = pl.BlockSpec((nh, n), lambda b: (0, 0))
    s_spec = pl.BlockSpec((None, nh, n, n), lambda b: (i, b, 0, 0))
    st_specs, st_args, aliases = _layer_state_io(S_all, new_all, s_spec)
    return pl.pallas_call(
        functools.partial(_wkv_step_body, n_state=len(st_args)),
        grid=(n_tot // nh,),
        in_specs=st_specs + [row_spec] * 7 + [par_spec] * 3,
        out_specs=[s_spec, row_spec],
        out_shape=[jax.ShapeDtypeStruct(S_all.shape, F32), jax.ShapeDtypeStruct((n_tot, n), F32)],
        input_output_aliases=aliases,
        compiler_params=_cparams(1),
        name="wkv_step",
    )(*st_args, *vals, tile(prm['rwkv_r_k'][i]), tile(prm['rwkv_gn_w'][i]), tile(prm['rwkv_gn_b'][i]))


def _lru_sample_body(gate_ref, xb_ref, cst_ref, h0_ref, cw_ref, cb_ref, wa_ref, wx_ref, ba_ref, bx_ref, lam_ref,
                     y_ref, h_ref, ncst_ref):
    cw = cw_ref[0]
    xb = xb_ref[...]
    xc = cb_ref[...] + xb * cw[3:4]
    for j in range(3):
        xc = xc + cst_ref[:, j, :] * cw[j:j + 1]
    a, b, gact = _lru_rows(gate_ref[...], xc, wa_ref, wx_ref, ba_ref[...], bx_ref[...], lam_ref[...])
    h = a * h0_ref[...] + b
    y_ref[...] = (h * gact).astype(y_ref.dtype)
    h_ref[...] = h
    ncst_ref[:, 0, :] = cst_ref[:, 1, :]
    ncst_ref[:, 1, :] = cst_ref[:, 2, :]
    ncst_ref[:, 2, :] = xb


def _lru_sample(pb, conv_state, h0, prm, i):
    bs = pb.shape[0]
    w = B_WIDTH
    ng = w // LANES
    full = lambda shape: pl.BlockSpec(shape, lambda s: (0,) * len(shape))
    return pl.pallas_call(
        _lru_sample_body,
        grid=(1,),
        in_specs=[pl.BlockSpec((bs, w), lambda s: (0, 0)), pl.BlockSpec((bs, w), lambda s: (0, 1)),
                  full((bs, 3, w)), full((bs, w)),
                  pl.BlockSpec((1, 4, w), lambda s: (i, 0, 0)),
                  full((1, w)), full((ng, LANES, LANES)), full((ng, LANES, LANES)),
                  full((1, w)), full((1, w)), full((1, w))],
        out_specs=[full((bs, w)), full((bs, w)), full((bs, 3, w))],
        out_shape=[jax.ShapeDtypeStruct((bs, w), BF16), jax.ShapeDtypeStruct((bs, w), F32),
                   jax.ShapeDtypeStruct((bs, 3, w), F32)],
        compiler_params=_cparams(1),
        name="lru_sample",
    )(pb, pb, conv_state, h0, prm['lru_conv_w'], _row(prm['lru_conv_b'][i]),
      _block_diag_pairs(prm['lru_wa'][i]), _block_diag_pairs(prm['lru_wx'][i]), _row(prm['lru_ba'][i]),
      _row(prm['lru_bx'][i]), _row(prm['lru_lambda'][i]))


def _gdn_prep_sample_body(x_ref, cst_ref, cw_ref, o_ref, ncst_ref):
    j = pl.program_id(0)
    cw = cw_ref[0]
    x = x_ref[...]
    y = x * cw[3:4]
    for jj in range(3):
        y = y + cst_ref[:, jj, :] * cw[jj:jj + 1]
    n_qk_blocks = 2 * C_KEY_DIM // x.shape[1]
    q_scale = jnp.where(j < n_qk_blocks // 2, C_HEAD ** -0.5, 1.0).astype(F32)
    o_ref[...] = _gdn_act(y, j < n_qk_blocks, q_scale)
    ncst_ref[:, 0, :] = cst_ref[:, 1, :]
    ncst_ref[:, 1, :] = cst_ref[:, 2, :]
    ncst_ref[:, 2, :] = x


def _gdn_prep_sample(qkvz, conv_state, conv_w, i, *, tc):
    bs = qkvz.shape[0]
    return pl.pallas_call(
        _gdn_prep_sample_body,
        grid=(C_QKV_DIM // tc,),
        in_specs=[pl.BlockSpec((bs, tc), lambda j: (0, j)),
                  pl.BlockSpec((bs, 3, tc), lambda j: (0, 0, j)),
                  pl.BlockSpec((1, 4, tc), lambda j: (i, 0, j))],
        out_specs=[pl.BlockSpec((bs, tc), lambda j: (0, j)),
                   pl.BlockSpec((bs, 3, tc), lambda j: (0, 0, j))],
        out_shape=[jax.ShapeDtypeStruct((bs, C_QKV_DIM), F32),
                   jax.ShapeDtypeStruct((bs, 3, C_QKV_DIM), F32)],
        compiler_params=_cparams(1),
        name="gdn_prep_sample",
    )(qkvz, conv_state, conv_w)


def _gdn_step_body(*refs, n_state):
    s_ref = refs[0]
    q_ref, k_ref, v_ref, z_ref, ba_ref, alog_ref, dtb_ref, nw_ref, ns_ref, o_ref = refs[n_state:]
    bb = s_ref.shape[0]
    rep = C_V_HEADS // C_K_HEADS
    hk = pl.program_id(1)
    beta_all, g_all = _gdn_gates(ba_ref[...], alog_ref[...], dtb_ref[...])
    lane = lax.broadcasted_iota(jnp.int32, beta_all.shape, 2)
    pick = lambda x, col: jnp.sum(jnp.where(lane == col, x, 0.0), axis=-1, keepdims=True)
    rid = lax.broadcasted_iota(jnp.int32, (bb, SUBLANES, C_HEAD), 1)
    up = lambda x: jnp.broadcast_to(x, (bb, SUBLANES, C_HEAD))
    q = q_ref[...]
    k = k_ref[...]
    qk = jnp.sum(q * k, axis=-1, keepdims=True)
    for jv in range(rep):
        cols = slice(jv * C_HEAD, (jv + 1) * C_HEAD)
        hv = hk * rep + jv
        beta = pick(beta_all, hv)
        eg = jnp.exp(pick(g_all, hv))
        v = v_ref[:, :, cols]
        S = s_ref[:, jv]
        kb = k * beta
        lhs = jnp.where(rid == 0, up(kb * eg), jnp.where(rid == 1, up(q * eg), 0.0))
        ws = _dot3(lhs, S, _BNN)
        v_new = v * beta - ws[:, 0:1]
        o = ws[:, 1:2] + qk * v_new
        left = jnp.where(rid == 0, up(k), 0.0)
        right = jnp.where(rid == 0, up(v_new), 0.0)
        ns_ref[:, jv] = S * eg + _dot3(left, right, _BTN)
        o_ref[:, :, cols] = _gdn_head_out(o, z_ref[:, :, cols], nw_ref[...]).astype(o_ref.dtype)


def _gdn_step(S_all, new_all, qkv_act, qkvz, ba, prm, i, *, bb):
    bs = S_all.shape[1]
    rep = C_V_HEADS // C_K_HEADS
    vw = rep * C_HEAD
    v_blk0 = 2 * C_KEY_DIM // vw
    z_blk0 = C_QKV_DIM // vw
    full = lambda shape: pl.BlockSpec(shape, lambda b, h: (0,) * len(shape))
    s_spec = pl.BlockSpec((None, bb, rep, C_HEAD, C_HEAD), lambda b, h: (i, b, h, 0, 0))
    st_specs, st_args, aliases = _layer_state_io(S_all, new_all, s_spec)
    return pl.pallas_call(
        functools.partial(_gdn_step_body, n_state=len(st_args)),
        grid=(bs // bb, C_K_HEADS),
        in_specs=st_specs + [
                  pl.BlockSpec((bb, 1, C_HEAD), lambda b, h: (b, 0, h)),
                  pl.BlockSpec((bb, 1, C_HEAD), lambda b, h: (b, 0, C_K_HEADS + h)),
                  pl.BlockSpec((bb, 1, vw), lambda b, h: (b, 0, v_blk0 + h)),
                  pl.BlockSpec((bb, 1, vw), lambda b, h: (b, 0, z_blk0 + h)),
                  pl.BlockSpec((bb, 1, 2 * C_V_HEADS), lambda b, h: (b, 0, 0)),
                  full((1, C_V_HEADS)), full((1, C_V_HEADS)), full((1, C_HEAD))],
        out_specs=[s_spec, pl.BlockSpec((bb, 1, vw), lambda b, h: (b, 0, h))],
        out_shape=[jax.ShapeDtypeStruct(S_all.shape, F32), jax.ShapeDtypeStruct((bs, 1, C_VAL_DIM), BF16)],
        input_output_aliases=aliases,
        compiler_params=_cparams(2),
        name="gdn_step",
    )(*st_args, qkv_act, qkv_act, qkv_act, qkvz, ba, _row(prm['gdn_a_log'][i]), _row(prm['gdn_dt_bias'][i]),
      _row(prm['gdn_norm_w'][i]))


MOD_SH1, MOD_SC1, MOD_GT1, MOD_SH2, MOD_SC2, MOD_GT2 = range(6)


def _in_proj_ab(h, prm, i):
    w = [(prm['w_in_ab_t'], i, 0, True)]
    p_main = _mm([h], w, n_out=P_A_MAIN, name="in_ab_main")
    p_lora = _mm([h], w, n_out=LORA_ALL, col0=P_A_MAIN, tn=LORA_ALL, name="in_ab_lora")
    pb = _mm([h], w, n_out=2 * B_WIDTH, col0=P_A, name="in_ab_lru")
    return p_main, p_lora, pb


def _in_proj_c_gates(h, prm, i):
    return _mm([h], [(prm['w_in_c_t'], i, 0, True)], n_out=2 * C_V_HEADS, col0=C_QKV_DIM + C_VAL_DIM,
               tn=2 * C_V_HEADS, name="in_c_gates")


def _ffn_down(act, prm, layer, x, mod):
    return _mm([act], [(prm['ffn_w_down'], layer, 0, False)], n_out=D_MODEL, res=x, gate=mod, gate_blk=MOD_GT2,
               name="ffn_down")


def _out_proj_ab(ya, yb, prm, i, x, mod):
    w = prm['w_out_ab']
    return _mm([ya, yb], [(w, i, 0, False), (w, i, 1, False)], n_out=D_MODEL, res=x, gate=mod, gate_blk=MOD_GT1,
               name="out_ab")


def _out_proj_c(yc, prm, i, x, mod):
    return _mm([yc], [(prm['w_out_c'], i, 0, False)], n_out=D_MODEL, res=x, gate=mod, gate_blk=MOD_GT1,
               name="out_c")


def _trunk_prompt(x, mods, prm):
    B, T, _ = x.shape
    tm, tt = min(1024, T), min(256, T)
    wkv, shift, lru_h, lru_conv, gdn, gdn_conv, ffn_conv = [], [], [], [], [], [], []
    for layer in range(DEPTH):
        mod = mods[layer].reshape(B, 1, 6 * D_MODEL)
        i = layer // 2
        h = _norm(x, prm['norm_mix'][layer], mod, MOD_SC1, MOD_SH1, tt=tt)
        if layer % 2 == 0:
            p_main, p_lora, pb = _in_proj_ab(h, prm, i)
            vals = _rwkv_prep_prompt(p_main, p_lora, prm, i, tt=tt)
            ya, s_new = _wkv_prompt(vals, prm, i, tt=tt, heads_per_step=4)
            yb, h_last, cst = _lru_prompt(pb, prm, i, tt=tt)
            x = _out_proj_ab(ya, yb, prm, i, x, mod)
            wkv.append(s_new)
            shift.append(jnp.concatenate([p_main[:, -1], p_lora[:, -1, :LORA_ALL]], axis=-1))
            lru_h.append(h_last[:, 0])
            lru_conv.append(cst)
        else:
            qkv_act, cst = _in_c_conv_prompt(h, prm['w_in_c_t'], prm['gdn_conv_w'], i, tm=tm, tn=512)
            z = _mm([h], [(prm['w_in_c_t'], i, 0, True)], n_out=C_VAL_DIM, col0=C_QKV_DIM, name="in_c_z")
            ba = _in_proj_c_gates(h, prm, i)
            yc, s_new = _gdn_prompt(qkv_act, z, ba, prm, i, tt=tt, kheads_per_step=2)
            x = _out_proj_c(yc, prm, i, x, mod)
            gdn.append(s_new)
            gdn_conv.append(cst)
        h = _norm(x, prm['norm_ffn'][layer], mod, MOD_SC2, MOD_SH2, tt=tt)
        act, fst = _ffn_up_prompt(h, prm['ffn_w_gate'], prm['ffn_w_up'], prm['ffn_conv_w'], prm['ffn_conv_b'],
                                  layer, tm=tm, tn=512)
        x = _ffn_down(act, prm, layer, x, mod)
        ffn_conv.append(fst)
    y = _norm(x, prm['norm_out'], tt=tt, out_dtype=F32)
    stk = jnp.stack
    return y, stk(wkv), stk(shift), stk(lru_h), stk(lru_conv), stk(gdn), stk(gdn_conv), stk(ffn_conv)


def _trunk_sample(x, mods, st, prm):
    bs = x.shape[0]
    x = x.reshape(1, bs, D_MODEL)
    wkv0, shift0, lru_h0, lru_conv0, gdn0, gdn_conv0, ffn_conv0 = st
    wkv_new = gdn_new = None
    shift, lru_h, lru_conv, gdn_conv, ffn_conv = [], [], [], [], []
    for layer in range(DEPTH):
        mod = mods[layer].reshape(1, bs, 6 * D_MODEL)
        i = layer // 2
        h = _norm(x, prm['norm_mix'][layer], mod, MOD_SC1, MOD_SH1, tt=bs)
        if layer % 2 == 0:
            p_main, p_lora, pb = _in_proj_ab(h, prm, i)
            p_main, p_lora, pb = p_main[0], p_lora[0], pb[0]
            vals = _rwkv_prep_sample(p_main, p_lora, shift0[i], prm, i)
            heads = lambda a: a.reshape(bs * A_HEADS, A_HEAD_DIM)
            wkv_new, ya = _wkv_step(wkv0.reshape(-1, bs * A_HEADS, A_HEAD_DIM, A_HEAD_DIM), wkv_new,
                                    [heads(a) for a in vals], prm, i, nh=8 * A_HEADS)
            ya = ya.reshape(1, bs, A_WIDTH).astype(BF16)
            yb, h_last, cst = _lru_sample(pb, lru_conv0[i], lru_h0[i], prm, i)
            x = _out_proj_ab(ya, yb[None], prm, i, x, mod)
            shift.append(jnp.concatenate([p_main, p_lora[:, :LORA_ALL]], axis=-1))
            lru_h.append(h_last)
            lru_conv.append(cst)
        else:
            qkvz = _mm([h], [(prm['w_in_c_t'], i, 0, True)], n_out=C_QKV_DIM + C_VAL_DIM, name="in_c_main")
            ba = _in_proj_c_gates(h, prm, i)
            qkv_act, cst = _gdn_prep_sample(qkvz[0], gdn_conv0[i], prm['gdn_conv_w'], i, tc=2048)
            gdn_new, yc = _gdn_step(gdn0, gdn_new, qkv_act[:, None], qkvz[0][:, None], ba[0][:, None], prm, i,
                                    bb=min(STEP_GROUP, bs))
            x = _out_proj_c(yc.reshape(1, bs, C_VAL_DIM), prm, i, x, mod)
            gdn_conv.append(cst)
        h = _norm(x, prm['norm_ffn'][layer], mod, MOD_SC2, MOD_SH2, tt=bs)
        act, fst = _ffn_up_sample(h[0], prm['ffn_w_gate'], prm['ffn_w_up'], prm['ffn_conv_w'],
                                  prm['ffn_conv_b'], ffn_conv0[layer], layer, tn=512)
        x = _ffn_down(act[None], prm, layer, x, mod)
        ffn_conv.append(fst)
    y = _norm(x, prm['norm_out'], tt=bs, out_dtype=F32).reshape(bs, 1, D_MODEL)
    stk = jnp.stack
    return (y, wkv_new.reshape(wkv0.shape), stk(shift), stk(lru_h), stk(lru_conv), gdn_new, stk(gdn_conv),
            stk(ffn_conv))


def kernel(x_prompt, x_sample, c_prompt, c_sample, state_rwkv_wkv, state_rwkv_shift, state_lru_h, state_lru_conv, state_gdn, state_gdn_conv, state_ffn_conv, w_mod, b_mod, norm_mix, norm_ffn, norm_out, w_in_ab, w_out_ab, rwkv_mu, rwkv_w0, rwkv_w2, rwkv_a0, rwkv_a2, rwkv_g2, rwkv_k_k, rwkv_k_a, rwkv_r_k, rwkv_gn_w, rwkv_gn_b, lru_conv_w, lru_conv_b, lru_wa, lru_ba, lru_wx, lru_bx, lru_lambda, w_in_c, w_out_c, gdn_conv_w, gdn_a_log, gdn_dt_bias, gdn_norm_w, ffn_w_gate, ffn_w_up, ffn_conv_w, ffn_conv_b, ffn_w_down):
    prm = dict(norm_mix=norm_mix, norm_ffn=norm_ffn, norm_out=norm_out, w_out_ab=w_out_ab,
               w_in_ab_t=jnp.swapaxes(w_in_ab, 1, 2), w_in_c_t=jnp.swapaxes(w_in_c, 1, 2),
               rwkv_mu=rwkv_mu, rwkv_w0=rwkv_w0, rwkv_w2=rwkv_w2, rwkv_a0=rwkv_a0, rwkv_a2=rwkv_a2,
               rwkv_g2=rwkv_g2, rwkv_k_k=rwkv_k_k, rwkv_k_a=rwkv_k_a,
               rwkv_r_k=rwkv_r_k.reshape(rwkv_r_k.shape[0], A_WIDTH), rwkv_gn_w=rwkv_gn_w,
               rwkv_gn_b=rwkv_gn_b, lru_conv_w=lru_conv_w, lru_conv_b=lru_conv_b, lru_wa=lru_wa, lru_ba=lru_ba,
               lru_wx=lru_wx, lru_bx=lru_bx, lru_lambda=lru_lambda, w_in_c=w_in_c, w_out_c=w_out_c,
               gdn_conv_w=gdn_conv_w, gdn_a_log=gdn_a_log, gdn_dt_bias=gdn_dt_bias, gdn_norm_w=gdn_norm_w,
               ffn_w_gate=ffn_w_gate, ffn_w_up=ffn_w_up, ffn_conv_w=ffn_conv_w, ffn_conv_b=ffn_conv_b,
               ffn_w_down=ffn_w_down)
    mods_p, mods_s = _modulation(c_prompt, c_sample, w_mod, b_mod)
    outs_p = _trunk_prompt(x_prompt, mods_p, prm)
    outs_s = _trunk_sample(x_sample, mods_s,
                           (state_rwkv_wkv, state_rwkv_shift, state_lru_h, state_lru_conv, state_gdn,
                            state_gdn_conv, state_ffn_conv), prm)
    return (outs_p[0], outs_s[0]) + tuple(outs_p[1:]) + tuple(outs_s[1:])
```

```python
import functools

import jax
import jax.numpy as jnp
from jax import lax
from jax.experimental import pallas as pl
from jax.experimental.pallas import tpu as pltpu

F32 = jnp.float32
BF16 = jnp.bfloat16

D_MODEL = 2048
DEPTH = 4
A_HEAD_DIM = 64
A_WIDTH = D_MODEL // 2
A_HEADS = A_WIDTH // A_HEAD_DIM
LORA_W = 64
LORA_A = 64
LORA_G = 160
LORA_ALL = LORA_W + LORA_A + LORA_G
P_A_MAIN = 3 * A_WIDTH
P_A = P_A_MAIN + LORA_ALL
A_GN_EPS = 64e-5
B_WIDTH = D_MODEL - A_WIDTH
B_BLOCK_DIM = 64
LRU_C = 8.0
C_HEAD = 128
C_K_HEADS = D_MODEL // C_HEAD
C_V_HEADS = 2 * C_K_HEADS
C_KEY_DIM = C_K_HEADS * C_HEAD
C_VAL_DIM = C_V_HEADS * C_HEAD
C_QKV_DIM = 2 * C_KEY_DIM + C_VAL_DIM
D_FF = 5632
NORM_EPS = 1e-6

LANES = 128
SUBLANES = 8
HIST = SUBLANES
CHUNK = 64
VMEM_LIMIT = 56 * 1024 * 1024
MM_VMEM_BUDGET = 46 * 1024 * 1024

_NN = (((1,), (0,)), ((), ()))
_NT = (((1,), (1,)), ((), ()))
_TN = (((0,), (0,)), ((), ()))
_BNN = (((2,), (1,)), ((0,), (0,)))
_BNT = (((2,), (2,)), ((0,), (0,)))
_BTN = (((1,), (1,)), ((0,), (0,)))


def _cparams(n_axes):
    return pltpu.CompilerParams(dimension_semantics=("arbitrary",) * n_axes,
                                vmem_limit_bytes=VMEM_LIMIT)


def _bdot(a, b, dims=_NN):
    return lax.dot_general(a.astype(BF16), b.astype(BF16), dims, preferred_element_type=F32)


def _split2(a):
    hi = a.astype(BF16)
    lo = (a - hi.astype(F32)).astype(BF16)
    return hi, lo


def _dot3(a, b, dims=_NN):
    ah, al = _split2(a)
    bh, bl = _split2(b)
    d = lambda x, y: lax.dot_general(x, y, dims, preferred_element_type=F32)
    return d(ah, bh) + (d(ah, bl) + d(al, bh))


def _split3(a):
    h0 = a.astype(BF16)
    r1 = a - h0.astype(F32)
    h1 = r1.astype(BF16)
    h2 = (r1 - h1.astype(F32)).astype(BF16)
    return h0, h1, h2


def _sel_l(mask01, x, dims=_NN):
    m = mask01.astype(BF16)
    d = lambda y: lax.dot_general(m, y, dims, preferred_element_type=F32)
    h0, h1, h2 = _split3(x)
    return d(h0) + (d(h1) + d(h2))


def _sel_r(x, mask01):
    m = mask01.astype(BF16)
    d = lambda y: lax.dot_general(y, m, _NN, preferred_element_type=F32)
    h0, h1, h2 = _split3(x)
    return d(h0) + (d(h1) + d(h2))


def _iota2(shape, axis):
    return lax.broadcasted_iota(jnp.int32, shape, axis)


def _tri(n, strict):
    i = _iota2((n, n), 0)
    j = _iota2((n, n), 1)
    return (i > j) if strict else (i >= j)


def _unit_lower_inverse(low, mm):
    n = low.shape[-1]
    dims = _BNN if low.ndim == 3 else _NN
    eye = (_iota2((n, n), 0) == _iota2((n, n), 1)).astype(F32)
    m = -low
    inv = eye + m
    p = m
    span = 2
    while span < n:
        p = mm(p, p, dims)
        inv = inv + mm(inv, p, dims)
        span *= 2
    return inv


def _unit_lower_inverse_refined(low):
    n = low.shape[-1]
    dims = _BNN if low.ndim == 3 else _NN
    eye = (_iota2((n, n), 0) == _iota2((n, n), 1)).astype(F32)
    x0 = _unit_lower_inverse(low, _bdot)
    resid = eye - _dot3(eye + low, x0, dims)
    return x0 + _bdot(x0, resid, dims)


def _block_tri(tt, block):
    i = _iota2((tt, tt), 0)
    j = _iota2((tt, tt), 1)
    return (i // block == j // block) & (i >= j)


def _neg_expm1(x):
    return -jnp.tanh(0.5 * x) * (jnp.exp(x) + 1.0)


def _segment_sum(x, seg):
    i = _iota2((LANES, LANES), 0) // seg
    j = _iota2((LANES, LANES), 1) // seg
    same = (i == j)
    parts = [_sel_r(x[:, g * LANES:(g + 1) * LANES], same) for g in range(x.shape[1] // LANES)]
    return parts[0] if len(parts) == 1 else jnp.concatenate(parts, axis=1)


def _rwkv_rows(p_main, q_main, p_lora, q_lora, mu_main, mu_lora, w0, w2, a0, a2, g2, k_k, k_a):
    pm = p_main + (q_main - p_main) * mu_main
    pl_ = p_lora + (q_lora - p_lora) * mu_lora
    r = pm[:, :A_WIDTH]
    k = pm[:, A_WIDTH:2 * A_WIDTH]
    v = pm[:, 2 * A_WIDTH:]
    xw = pl_[:, :LORA_W]
    xa = pl_[:, LORA_W:LORA_W + LORA_A]
    xg = pl_[:, LORA_W + LORA_A:LORA_ALL]
    w_ll = -jax.nn.softplus(-(w0 + _bdot(jnp.tanh(xw), w2))) - 0.5
    lw = -jnp.exp(w_ll)
    a = jax.nn.sigmoid(a0 + _bdot(xa, a2))
    g = _bdot(jax.nn.sigmoid(xg), g2)
    kkn = k * k_k
    kf = k * (1.0 + (a - 1.0) * k_a)
    return r, kf, v, kkn, a, lw, g


def _rwkv_head_out(y, r, kf, v, g, r_k, gn_w, gn_b):
    mean = jnp.mean(y, axis=-1, keepdims=True)
    var = jnp.mean(jnp.square(y - mean), axis=-1, keepdims=True)
    yn = (y - mean) * lax.rsqrt(var + A_GN_EPS) * gn_w + gn_b
    bonus = jnp.sum(r * kf * r_k, axis=-1, keepdims=True) * v
    return (yn + bonus) * g


def _lru_rows(gate, xc, wa_bd, wx_bd, ba, bx, lam):
    ng = xc.shape[1] // LANES
    ra = jnp.concatenate([_bdot(xc[:, g * LANES:(g + 1) * LANES], wa_bd[g]) for g in range(ng)], axis=1)
    rx = jnp.concatenate([_bdot(xc[:, g * LANES:(g + 1) * LANES], wx_bd[g]) for g in range(ng)], axis=1)
    r_gate = jax.nn.sigmoid(ra + ba)
    i_gate = jax.nn.sigmoid(rx + bx)
    log_a = -LRU_C * r_gate * jax.nn.softplus(-lam)
    a = jnp.exp(log_a)
    b = jnp.sqrt(_neg_expm1(2.0 * log_a)) * (i_gate * xc)
    return a, b, jax.nn.gelu(gate)


def _gdn_act(y, is_qk, q_scale):
    y = jax.nn.silu(y)
    outs = []
    for h in range(y.shape[1] // C_HEAD):
        yh = y[:, h * C_HEAD:(h + 1) * C_HEAD]
        nrm = yh * lax.rsqrt(jnp.sum(yh * yh, axis=-1, keepdims=True) + 1e-6) * q_scale
        outs.append(jnp.where(is_qk, nrm, yh))
    return jnp.concatenate(outs, axis=1)


def _gdn_head_out(o, z, norm_w):
    o = o * lax.rsqrt(jnp.mean(o * o, axis=-1, keepdims=True) + NORM_EPS) * norm_w
    return o * jax.nn.silu(z)


def _w_block(w_ref):
    return w_ref[0] if len(w_ref.shape) == 3 else w_ref[...]


def _mm_body(*refs, n_x, has_res, transposed):
    x_refs = refs[:n_x]
    w_refs = refs[n_x:2 * n_x]
    k = 2 * n_x
    if has_res:
        res_ref, gate_ref = refs[k], refs[k + 1]
        k += 2
    o_ref = refs[k]
    wbf_refs = refs[k + 1:k + 1 + n_x]

    @pl.when((pl.program_id(1) == 0) & (pl.program_id(2) == 0))
    def _():
        for w_ref, wbf_ref in zip(w_refs, wbf_refs):
            wbf_ref[...] = _w_block(w_ref).astype(BF16)

    acc = None
    for x_ref, wbf_ref, tr in zip(x_refs, wbf_refs, transposed):
        d = lax.dot_general(x_ref[0], wbf_ref[...], _NT if tr else _NN, preferred_element_type=F32)
        acc = d if acc is None else acc + d
    if has_res:
        acc = res_ref[0] + gate_ref[0] * acc
    o_ref[0] = acc.astype(o_ref.dtype)


def _pick_tiles(n_out, k_sum, rows, seq, n_out_bufs):
    best = None
    for tm in [t for t in (1024, 512, 256, 128) if seq % t == 0] or [seq]:
        for tn in (1024, 512, 256, 128):
            if n_out % tn:
                continue
            need = (2 * tm * k_sum * 2 + 2 * k_sum * tn * 4 + k_sum * tn * 2
                    + (2 * n_out_bufs + 1) * tm * tn * 4)
            if need > MM_VMEM_BUDGET:
                continue
            traffic = (n_out // tn) * rows * k_sum * 2
            key = (traffic, -tn, -tm)
            if best is None or key < best[0]:
                best = (key, tm, tn)
    if best is None:
        raise ValueError("no projection tile fits VMEM")
    return best[1], best[2]


def _mm(xs, ws, *, n_out, col0=0, tm=None, tn=None, out_dtype=F32, res=None, gate=None, gate_blk=0, name="proj"):
    B, T, _ = xs[0].shape
    ks = [x.shape[-1] for x in xs]
    has_res = res is not None
    if tn is None:
        tm, tn = _pick_tiles(n_out, sum(ks), B * T, T, 2 if has_res else 1)
    elif tm is None:
        tm = min(1024, T)
    assert n_out % tn == 0 and T % tm == 0
    in_specs = [pl.BlockSpec((1, tm, k), lambda j, b, t: (b, t, 0)) for k in ks]
    scratch = []
    for k, (w, layer, rb, tr) in zip(ks, ws):
        if tr:
            assert col0 % SUBLANES == 0 and tn % SUBLANES == 0 and w.shape[2] == k
            in_specs.append(pl.BlockSpec(
                (pl.Element(1), pl.Element(tn), pl.Element(k)),
                functools.partial(lambda j, b, t, layer: (layer, pl.multiple_of(col0 + j * tn, SUBLANES), 0),
                                  layer=layer)))
            scratch.append(pltpu.VMEM((tn, k), BF16))
        else:
            assert col0 % tn == 0
            in_specs.append(pl.BlockSpec(
                (None, k, tn),
                functools.partial(lambda j, b, t, layer, rb: (layer, rb, col0 // tn + j), layer=layer, rb=rb)))
            scratch.append(pltpu.VMEM((k, tn), BF16))
    args = list(xs) + [w for w, _, _, _ in ws]
    if has_res:
        in_specs.append(pl.BlockSpec((1, tm, tn), lambda j, b, t: (b, t, j)))
        gb0 = gate_blk * (n_out // tn)
        if gate.shape[1] == 1:
            in_specs.append(pl.BlockSpec((1, 1, tn), lambda j, b, t: (b, 0, gb0 + j)))
        else:
            in_specs.append(pl.BlockSpec((1, tm, tn), lambda j, b, t: (b, t, gb0 + j)))
        args += [res, gate]
    return pl.pallas_call(
        functools.partial(_mm_body, n_x=len(xs), has_res=has_res, transposed=tuple(w[3] for w in ws)),
        grid=(n_out // tn, B, T // tm),
        in_specs=in_specs,
        out_specs=pl.BlockSpec((1, tm, tn), lambda j, b, t: (b, t, j)),
        out_shape=jax.ShapeDtypeStruct((B, T, n_out), out_dtype),
        scratch_shapes=scratch,
        compiler_params=_cparams(3),
        name=name,
    )(*args)


def _mod_body(c_ref, w_ref, b_ref, op_ref, os_ref):
    bs = os_ref.shape[1]
    y = _bdot(jax.nn.silu(c_ref[...]), w_ref[0]) + b_ref[0]
    os_ref[0] = y[:bs]
    op_ref[0] = y[bs:]


def _modulation(c_p, c_s, w_mod, b_mod):
    depth, d, n = w_mod.shape
    tn = 1024
    bp, bs = c_p.shape[0], c_s.shape[0]
    return pl.pallas_call(
        _mod_body,
        grid=(depth, n // tn),
        in_specs=[pl.BlockSpec((bs + bp, d), lambda l, j: (0, 0)),
                  pl.BlockSpec((1, d, tn), lambda l, j: (l, 0, j)),
                  pl.BlockSpec((1, 1, tn), lambda l, j: (l, 0, j))],
        out_specs=[pl.BlockSpec((1, bp, tn), lambda l, j: (l, 0, j)),
                   pl.BlockSpec((1, bs, tn), lambda l, j: (l, 0, j))],
        out_shape=[jax.ShapeDtypeStruct((depth, bp, n), F32),
                   jax.ShapeDtypeStruct((depth, bs, n), F32)],
        compiler_params=_cparams(2),
        name="modulation",
    )(jnp.concatenate([c_s, c_p], axis=0), w_mod, b_mod.reshape(depth, 1, n))


def _norm_body(*refs, has_mod):
    if has_mod:
        x_ref, g_ref, sc_ref, sh_ref, o_ref = refs
    else:
        x_ref, g_ref, o_ref = refs
    x = x_ref[0]
    y = x * lax.rsqrt(jnp.mean(x * x, axis=-1, keepdims=True) + NORM_EPS) * g_ref[...]
    if has_mod:
        y = y * (1.0 + sc_ref[0]) + sh_ref[0]
    o_ref[0] = y.astype(o_ref.dtype)


def _norm(x, gain, mod=None, sc_blk=0, sh_blk=0, *, tt, out_dtype=BF16):
    B, T, d = x.shape
    in_specs = [pl.BlockSpec((1, tt, d), lambda b, t: (b, t, 0)),
                pl.BlockSpec((1, d), lambda b, t: (0, 0))]
    args = [x, gain.reshape(1, d)]
    if mod is not None:
        for blk in (sc_blk, sh_blk):
            if mod.shape[1] == 1:
                in_specs.append(pl.BlockSpec((1, 1, d), functools.partial(lambda b, t, blk: (b, 0, blk), blk=blk)))
            else:
                in_specs.append(pl.BlockSpec((1, tt, d), functools.partial(lambda b, t, blk: (b, t, blk), blk=blk)))
        args += [mod, mod]
    return pl.pallas_call(
        functools.partial(_norm_body, has_mod=mod is not None),
        grid=(B, T // tt),
        in_specs=in_specs,
        out_specs=pl.BlockSpec((1, tt, d), lambda b, t: (b, t, 0)),
        out_shape=jax.ShapeDtypeStruct((B, T, d), out_dtype),
        compiler_params=_cparams(2),
        name="norm",
    )(*args)


def _ffn_up_prompt_body(h_ref, wg_ref, wu_ref, cw_ref, cb_ref, act_ref, st_ref, wg_bf, wu_bf, buf):
    t = pl.program_id(2)
    tm = h_ref.shape[1]

    @pl.when((pl.program_id(1) == 0) & (t == 0))
    def _():
        wg_bf[...] = wg_ref[0].astype(BF16)
        wu_bf[...] = wu_ref[0].astype(BF16)

    @pl.when(t == 0)
    def _():
        buf[0:HIST, :] = jnp.zeros((HIST, buf.shape[1]), F32)

    h = h_ref[0]
    buf[HIST:, :] = jnp.dot(h, wg_bf[...], preferred_element_type=F32)
    cw = cw_ref[0]
    u = (buf[HIST - 2:HIST - 2 + tm, :] * cw[0:1] + buf[HIST - 1:HIST - 1 + tm, :] * cw[1:2]
         + buf[HIST:, :] * cw[2:3] + cb_ref[0])
    up = jnp.dot(h, wu_bf[...], preferred_element_type=F32)
    act_ref[0] = (jax.nn.gelu(u) * up).astype(act_ref.dtype)
    st_ref[0] = buf[HIST + tm - 2:HIST + tm, :]
    buf[0:HIST, :] = buf[tm:tm + HIST, :]


def _ffn_up_prompt(h, w_gate, w_up, conv_w, conv_b, layer, *, tm, tn):
    B, T, d = h.shape
    n = w_gate.shape[-1]
    return pl.pallas_call(
        _ffn_up_prompt_body,
        grid=(n // tn, B, T // tm),
        in_specs=[pl.BlockSpec((1, tm, d), lambda j, b, t: (b, t, 0)),
                  pl.BlockSpec((1, d, tn), lambda j, b, t: (layer, 0, j)),
                  pl.BlockSpec((1, d, tn), lambda j, b, t: (layer, 0, j)),
                  pl.BlockSpec((1, 3, tn), lambda j, b, t: (layer, 0, j)),
                  pl.BlockSpec((1, 1, tn), lambda j, b, t: (layer, 0, j))],
        out_specs=[pl.BlockSpec((1, tm, tn), lambda j, b, t: (b, t, j)),
                   pl.BlockSpec((1, 2, tn), lambda j, b, t: (b, 0, j))],
        out_shape=[jax.ShapeDtypeStruct((B, T, n), BF16),
                   jax.ShapeDtypeStruct((B, 2, n), F32)],
        scratch_shapes=[pltpu.VMEM((d, tn), BF16), pltpu.VMEM((d, tn), BF16),
                        pltpu.VMEM((HIST + tm, tn), F32)],
        compiler_params=_cparams(3),
        name="ffn_up_prompt",
    )(h, w_gate, w_up, conv_w, conv_b.reshape(conv_b.shape[0], 1, n))


def _ffn_up_sample_body(h_ref, wg_ref, wu_ref, cw_ref, cb_ref, st_ref, act_ref, nst_ref):
    h = h_ref[...]
    pre = _bdot(h, wg_ref[0])
    cw = cw_ref[0]
    u = st_ref[:, 0, :] * cw[0:1] + st_ref[:, 1, :] * cw[1:2] + pre * cw[2:3] + cb_ref[0]
    up = _bdot(h, wu_ref[0])
    act_ref[...] = (jax.nn.gelu(u) * up).astype(act_ref.dtype)
    nst_ref[:, 0, :] = st_ref[:, 1, :]
    nst_ref[:, 1, :] = pre


def _ffn_up_sample(h, w_gate, w_up, conv_w, conv_b, state, layer, *, tn):
    bs, d = h.shape
    n = w_gate.shape[-1]
    return pl.pallas_call(
        _ffn_up_sample_body,
        grid=(n // tn,),
        in_specs=[pl.BlockSpec((bs, d), lambda j: (0, 0)),
                  pl.BlockSpec((1, d, tn), lambda j: (layer, 0, j)),
                  pl.BlockSpec((1, d, tn), lambda j: (layer, 0, j)),
                  pl.BlockSpec((1, 3, tn), lambda j: (layer, 0, j)),
                  pl.BlockSpec((1, 1, tn), lambda j: (layer, 0, j)),
                  pl.BlockSpec((bs, 2, tn), lambda j: (0, 0, j))],
        out_specs=[pl.BlockSpec((bs, tn), lambda j: (0, j)),
                   pl.BlockSpec((bs, 2, tn), lambda j: (0, 0, j))],
        out_shape=[jax.ShapeDtypeStruct((bs, n), BF16),
                   jax.ShapeDtypeStruct((bs, 2, n), F32)],
        compiler_params=_cparams(1),
        name="ffn_up_sample",
    )(h, w_gate, w_up, conv_w, conv_b.reshape(conv_b.shape[0], 1, n), state)


def _rwkv_prep_prompt_body(pm_ref, pl_ref, mum_ref, mul_ref, w0_ref, w2_ref, a0_ref, a2_ref, g2_ref,
                           kk_ref, ka_ref, r_o, kf_o, v_o, kkn_o, a_o, lw_o, g_o, bufm, bufl):
    t = pl.program_id(1)
    tt = pm_ref.shape[1]

    @pl.when(t == 0)
    def _():
        bufm[0:HIST, :] = jnp.zeros((HIST, bufm.shape[1]), F32)
        bufl[0:HIST, :] = jnp.zeros((HIST, bufl.shape[1]), F32)

    p_main = pm_ref[0]
    p_lora = pl_ref[0][:, :LORA_ALL]
    bufm[HIST:, :] = p_main
    bufl[HIST:, :] = p_lora
    q_main = bufm[HIST - 1:HIST - 1 + tt, :]
    q_lora = bufl[HIST - 1:HIST - 1 + tt, :]
    outs = _rwkv_rows(p_main, q_main, p_lora, q_lora, mum_ref[...], mul_ref[...], w0_ref[...], w2_ref[0],
                      a0_ref[...], a2_ref[0], g2_ref[0], kk_ref[...], ka_ref[...])
    for o_ref, val in zip((r_o, kf_o, v_o, kkn_o, a_o, lw_o, g_o), outs):
        o_ref[0] = val
    bufm[0:HIST, :] = bufm[tt:tt + HIST, :]
    bufl[0:HIST, :] = bufl[tt:tt + HIST, :]


def _row(v):
    return v.reshape(1, -1)


def _rwkv_prep_prompt(p_main, p_lora, prm, i, *, tt):
    B, T, _ = p_main.shape
    lw_pad = p_lora.shape[-1]
    mu = prm['rwkv_mu'][i]
    full = lambda shape: pl.BlockSpec(shape, lambda b, t: (0,) * len(shape))
    lay3 = lambda shape: pl.BlockSpec((1,) + shape, lambda b, t: (i, 0, 0))
    out_spec = pl.BlockSpec((1, tt, A_WIDTH), lambda b, t: (b, t, 0))
    return pl.pallas_call(
        _rwkv_prep_prompt_body,
        grid=(B, T // tt),
        in_specs=[pl.BlockSpec((1, tt, P_A_MAIN), lambda b, t: (b, t, 0)),
                  pl.BlockSpec((1, tt, lw_pad), lambda b, t: (b, t, 0)),
                  full((1, P_A_MAIN)), full((1, LORA_ALL)), full((1, A_WIDTH)),
                  lay3((LORA_W, A_WIDTH)), full((1, A_WIDTH)), lay3((LORA_A, A_WIDTH)),
                  lay3((LORA_G, A_WIDTH)), full((1, A_WIDTH)), full((1, A_WIDTH))],
        out_specs=[out_spec] * 7,
        out_shape=[jax.ShapeDtypeStruct((B, T, A_WIDTH), F32)] * 7,
        scratch_shapes=[pltpu.VMEM((HIST + tt, P_A_MAIN), F32), pltpu.VMEM((HIST + tt, LORA_ALL), F32)],
        compiler_params=_cparams(2),
        name="rwkv_prep_prompt",
    )(p_main, p_lora, _row(mu[:P_A_MAIN]), _row(mu[P_A_MAIN:]), _row(prm['rwkv_w0'][i]), prm['rwkv_w2'],
      _row(prm['rwkv_a0'][i]), prm['rwkv_a2'], prm['rwkv_g2'], _row(prm['rwkv_k_k'][i]), _row(prm['rwkv_k_a'][i]))


def _wkv_prompt_body(r_ref, kf_ref, v_ref, kkn_ref, a_ref, lw_ref, g_ref, rk_ref, gw_ref, gb_ref,
                     ya_ref, st_ref, s_scr):
    t = pl.program_id(2)
    tt = r_ref.shape[1]
    n = A_HEAD_DIM
    C = CHUNK
    nc = tt // C
    nh = r_ref.shape[2] // n

    @pl.when(t == 0)
    def _():
        s_scr[...] = jnp.zeros(s_scr.shape, F32)

    def units(x):
        parts = [x[:, h * n:(h + 1) * n].reshape(nc, C, n) for h in range(nh)]
        return jnp.stack(parts, axis=1).reshape(nc * nh, C, n)

    lw_all = lw_ref[0]
    lc = units(_sel_l(_block_tri(tt, C), lw_all))
    r, kf, v, kkn, a, lw = map(units, (r_ref[0], kf_ref[0], v_ref[0], kkn_ref[0], a_ref[0], lw_all))
    incl = _tri(C, False)
    strict = _tri(C, True)

    kk = kkn * lax.rsqrt(jnp.sum(kkn * kkn, axis=-1, keepdims=True) + 1e-12)
    bb = kk * a
    p = jnp.exp(lc)
    pinv = jnp.exp(-lc)
    kt = kf * pinv
    bt = bb * pinv
    at = kk * jnp.exp(lc - lw)
    rt = r * p
    gram = _bdot(jnp.concatenate([at, rt], axis=1), jnp.concatenate([bt, kt], axis=1), _BNT)
    a_ab = jnp.where(strict, gram[:, :C, :C], 0.0)
    a_ak = jnp.where(strict, gram[:, :C, C:], 0.0)
    a_rb = jnp.where(incl, gram[:, C:, :C], 0.0)
    a_rk = jnp.where(incl, gram[:, C:, C:], 0.0)
    tinv = _unit_lower_inverse(a_ab, _bdot)
    av = _bdot(jnp.concatenate([a_ak, a_rk], axis=1), v, _BNN)
    tz = _bdot(tinv, jnp.concatenate([at, av[:, :C]], axis=2), _BNN)
    rz = _bdot(a_rb, tz, _BNN)
    lhs_s = jnp.concatenate([tz[:, :, :n], rt - rz[:, :, :n]], axis=1)
    z0 = tz[:, :, n:]
    y0 = av[:, C:] - rz[:, :, n:]
    plast = p[:, C - 1:C, :]
    upd = jnp.concatenate([kt * plast, -(bt * plast)], axis=1)

    S = s_scr[...]
    ys = []
    for c in range(nc):
        sl = slice(c * nh, (c + 1) * nh)
        xs = _bdot(lhs_s[sl], S, _BNT)
        z = xs[:, :C] + z0[sl]
        ys.append(xs[:, C:] + y0[sl])
        S = S * plast[sl] + _bdot(jnp.concatenate([v[sl], z], axis=1), upd[sl], _BTN)
    s_scr[...] = S

    outs = []
    for h in range(nh):
        cols = slice(h * n, (h + 1) * n)
        y = jnp.concatenate([ys[c][h] for c in range(nc)], axis=0)
        outs.append(_rwkv_head_out(y, r_ref[0, :, cols], kf_ref[0, :, cols], v_ref[0, :, cols],
                                   g_ref[0, :, cols], rk_ref[:, cols], gw_ref[:, cols], gb_ref[:, cols]))
    ya_ref[0] = jnp.concatenate(outs, axis=1).astype(ya_ref.dtype)

    @pl.when(t == pl.num_programs(2) - 1)
    def _():
        st_ref[0] = S


def _wkv_prompt(vals, prm, i, *, tt, heads_per_step):
    B, T, _ = vals[0].shape
    wblk = heads_per_step * A_HEAD_DIM
    seq_spec = pl.BlockSpec((1, tt, wblk), lambda b, h, t: (b, t, h))
    par_spec = pl.BlockSpec((1, wblk), lambda b, h, t: (0, h))
    return pl.pallas_call(
        _wkv_prompt_body,
        grid=(B, A_HEADS // heads_per_step, T // tt),
        in_specs=[seq_spec] * 7 + [par_spec] * 3,
        out_specs=[seq_spec,
                   pl.BlockSpec((1, heads_per_step, A_HEAD_DIM, A_HEAD_DIM), lambda b, h, t: (b, h, 0, 0))],
        out_shape=[jax.ShapeDtypeStruct((B, T, A_WIDTH), BF16),
                   jax.ShapeDtypeStruct((B, A_HEADS, A_HEAD_DIM, A_HEAD_DIM), F32)],
        scratch_shapes=[pltpu.VMEM((heads_per_step, A_HEAD_DIM, A_HEAD_DIM), F32)],
        compiler_params=_cparams(3),
        name="wkv_prompt",
    )(*vals, _row(prm['rwkv_r_k'][i]), _row(prm['rwkv_gn_w'][i]), _row(prm['rwkv_gn_b'][i]))


def _lru_prompt_body(gate_ref, xb_ref, cw_ref, cb_ref, wa_ref, wx_ref, ba_ref, bx_ref, lam_ref,
                     y_ref, h_ref, cst_ref, buf, h_scr):
    t = pl.program_id(1)
    tt = xb_ref.shape[1]
    w = xb_ref.shape[2]

    @pl.when(t == 0)
    def _():
        buf[0:HIST, :] = jnp.zeros((HIST, w), F32)
        h_scr[...] = jnp.zeros(h_scr.shape, F32)

    buf[HIST:, :] = xb_ref[0]
    cw = cw_ref[0]
    xc = cb_ref[...] + buf[HIST:, :] * cw[3:4]
    for j in range(3):
        xc = xc + buf[HIST - 3 + j:HIST - 3 + j + tt, :] * cw[j:j + 1]
    a, b, gact = _lru_rows(gate_ref[0], xc, wa_ref, wx_ref, ba_ref[...], bx_ref[...], lam_ref[...])
    row = _iota2((tt, w), 0)
    d = 1
    while d < tt:
        a_sh = jnp.where(row >= d, pltpu.roll(a, d, axis=0), 1.0)
        b_sh = jnp.where(row >= d, pltpu.roll(b, d, axis=0), 0.0)
        b = a * b_sh + b
        a = a * a_sh
        d *= 2
    h = a * h_scr[0:1, :] + b
    y_ref[0] = (h * gact).astype(y_ref.dtype)
    h_last = h[tt - 1:tt, :]
    h_scr[0:1, :] = h_last
    h_ref[0] = h_last
    cst_ref[0] = buf[HIST + tt - 3:HIST + tt, :]
    buf[0:HIST, :] = buf[tt:tt + HIST, :]


def _block_diag_pairs(w):
    nb, n, _ = w.shape
    w = w.reshape(nb // 2, 2, n, n)
    z = jnp.zeros((nb // 2, n, n), w.dtype)
    top = jnp.concatenate([w[:, 0], z], axis=2)
    bot = jnp.concatenate([z, w[:, 1]], axis=2)
    return jnp.concatenate([top, bot], axis=1)


def _lru_prompt(pb, prm, i, *, tt):
    B, T, _ = pb.shape
    w = B_WIDTH
    ng = w // LANES
    full = lambda shape: pl.BlockSpec(shape, lambda b, t: (0,) * len(shape))
    return pl.pallas_call(
        _lru_prompt_body,
        grid=(B, T // tt),
        in_specs=[pl.BlockSpec((1, tt, w), lambda b, t: (b, t, 0)),
                  pl.BlockSpec((1, tt, w), lambda b, t: (b, t, 1)),
                  pl.BlockSpec((1, 4, w), lambda b, t: (i, 0, 0)),
                  full((1, w)), full((ng, LANES, LANES)), full((ng, LANES, LANES)),
                  full((1, w)), full((1, w)), full((1, w))],
        out_specs=[pl.BlockSpec((1, tt, w), lambda b, t: (b, t, 0)),
                   pl.BlockSpec((1, 1, w), lambda b, t: (b, 0, 0)),
                   pl.BlockSpec((1, 3, w), lambda b, t: (b, 0, 0))],
        out_shape=[jax.ShapeDtypeStruct((B, T, w), BF16),
                   jax.ShapeDtypeStruct((B, 1, w), F32),
                   jax.ShapeDtypeStruct((B, 3, w), F32)],
        scratch_shapes=[pltpu.VMEM((HIST + tt, w), F32), pltpu.VMEM((SUBLANES, w), F32)],
        compiler_params=_cparams(2),
        name="lru_prompt",
    )(pb, pb, prm['lru_conv_w'], _row(prm['lru_conv_b'][i]), _block_diag_pairs(prm['lru_wa'][i]),
      _block_diag_pairs(prm['lru_wx'][i]), _row(prm['lru_ba'][i]), _row(prm['lru_bx'][i]),
      _row(prm['lru_lambda'][i]))


def _in_c_conv_prompt_body(h_ref, w_ref, cw_ref, o_ref, st_ref, w_bf, buf):
    j = pl.program_id(0)
    t = pl.program_id(2)
    tm = h_ref.shape[1]
    tn = w_ref.shape[1]

    @pl.when((pl.program_id(1) == 0) & (t == 0))
    def _():
        w_bf[...] = w_ref[0].astype(BF16)

    @pl.when(t == 0)
    def _():
        buf[0:HIST, :] = jnp.zeros((HIST, tn), F32)

    buf[HIST:, :] = lax.dot_general(h_ref[0], w_bf[...], _NT, preferred_element_type=F32)
    cw = cw_ref[0]
    y = buf[HIST:, :] * cw[3:4]
    for jj in range(3):
        y = y + buf[HIST - 3 + jj:HIST - 3 + jj + tm, :] * cw[jj:jj + 1]
    n_qk_blocks = 2 * C_KEY_DIM // tn
    q_scale = jnp.where(j < n_qk_blocks // 2, C_HEAD ** -0.5, 1.0).astype(F32)
    o_ref[0] = _gdn_act(y, j < n_qk_blocks, q_scale)
    st_ref[0] = buf[HIST + tm - 3:HIST + tm, :]
    buf[0:HIST, :] = buf[tm:tm + HIST, :]


def _in_c_conv_prompt(h, w_t, conv_w, i, *, tm, tn):
    B, T, d = h.shape
    return pl.pallas_call(
        _in_c_conv_prompt_body,
        grid=(C_QKV_DIM // tn, B, T // tm),
        in_specs=[pl.BlockSpec((1, tm, d), lambda j, b, t: (b, t, 0)),
                  pl.BlockSpec((pl.Element(1), pl.Element(tn), pl.Element(d)),
                               lambda j, b, t: (i, pl.multiple_of(j * tn, SUBLANES), 0)),
                  pl.BlockSpec((1, 4, tn), lambda j, b, t: (i, 0, j))],
        out_specs=[pl.BlockSpec((1, tm, tn), lambda j, b, t: (b, t, j)),
                   pl.BlockSpec((1, 3, tn), lambda j, b, t: (b, 0, j))],
        out_shape=[jax.ShapeDtypeStruct((B, T, C_QKV_DIM), F32),
                   jax.ShapeDtypeStruct((B, 3, C_QKV_DIM), F32)],
        scratch_shapes=[pltpu.VMEM((tn, d), BF16), pltpu.VMEM((HIST + tm, tn), F32)],
        compiler_params=_cparams(3),
        name="in_c_conv_prompt",
    )(h, w_t, conv_w)


def _gdn_gates(ba, a_log, dt_bias):
    hv = C_V_HEADS
    beta = jax.nn.sigmoid(ba[..., :hv])
    g = -jnp.exp(a_log) * jax.nn.softplus(ba[..., hv:2 * hv] + dt_bias)
    return beta, g


def _lane_bcast_col(x, col):
    n = x.shape[1]
    onehot = _iota2((n, LANES), 0) == col
    return _sel_r(x, onehot)


def _gdn_prompt_body(q_ref, k_ref, v_ref, z_ref, ba_ref, alog_ref, dtb_ref, nw_ref, o_ref, st_ref, s_scr):
    hblk = pl.program_id(1)
    t = pl.program_id(2)
    tt = q_ref.shape[1]
    C = CHUNK
    nc = tt // C
    rep = C_V_HEADS // C_K_HEADS
    n_kh = q_ref.shape[2] // C_HEAD
    nh = n_kh * rep

    @pl.when(t == 0)
    def _():
        s_scr[...] = jnp.zeros(s_scr.shape, F32)

    beta_all, g_all = _gdn_gates(ba_ref[0], alog_ref[...], dtb_ref[...])
    gc_all = _sel_l(_block_tri(tt, C), g_all)
    blocks = lambda x: x.reshape(nc, C, x.shape[-1])
    qs, ks, vs, betas, gcs = [], [], [], [], []
    for kh in range(n_kh):
        q_h = blocks(q_ref[0, :, kh * C_HEAD:(kh + 1) * C_HEAD])
        k_h = blocks(k_ref[0, :, kh * C_HEAD:(kh + 1) * C_HEAD])
        for jv in range(rep):
            hu = kh * rep + jv
            hv = hblk * nh + hu
            qs.append(q_h)
            ks.append(k_h)
            vs.append(blocks(v_ref[0, :, hu * C_HEAD:(hu + 1) * C_HEAD]))
            betas.append(blocks(_lane_bcast_col(beta_all, hv)))
            gcs.append(blocks(_lane_bcast_col(gc_all, hv)))
    stack = lambda xs: jnp.stack(xs, axis=1).reshape(nc * nh, C, xs[0].shape[-1])
    q, k, v, beta, gc = map(stack, (qs, ks, vs, betas, gcs))
    incl = _tri(C, False)
    strict = _tri(C, True)

    gcc = gc[:, :, :C]
    diff = gcc - jnp.swapaxes(gcc, 1, 2)
    dec_incl = jnp.exp(jnp.where(incl, diff, -jnp.inf))
    dec_strict = jnp.where(strict, dec_incl, 0.0)
    kb = k * beta
    gram = _bdot(jnp.concatenate([kb, q], axis=1), k, _BNT)
    low = gram[:, :C] * dec_strict
    qk = gram[:, C:] * dec_incl
    tinv = _unit_lower_inverse_refined(low)
    eg = jnp.exp(gc)
    sol = _bdot(tinv, jnp.concatenate([v * beta, kb * eg], axis=2), _BNN)
    u = sol[:, :, :C_HEAD]
    lhs_s = jnp.concatenate([sol[:, :, C_HEAD:], q * eg], axis=1)
    g_last = gc[:, C - 1:C, :]
    k_tail = k * jnp.exp(g_last - gc)
    e_last = jnp.exp(g_last)

    S = s_scr[...]
    os_ = []
    for c in range(nc):
        sl = slice(c * nh, (c + 1) * nh)
        ws = _bdot(lhs_s[sl], S, _BNN)
        v_new = u[sl] - ws[:, :C]
        os_.append(ws[:, C:] + _bdot(qk[sl], v_new, _BNN))
        S = S * e_last[sl] + _bdot(k_tail[sl], v_new, _BTN)
    s_scr[...] = S

    outs = []
    for hu in range(nh):
        cols = slice(hu * C_HEAD, (hu + 1) * C_HEAD)
        o = jnp.concatenate([os_[c][hu] for c in range(nc)], axis=0)
        outs.append(_gdn_head_out(o, z_ref[0, :, cols], nw_ref[...]))
    o_ref[0] = jnp.concatenate(outs, axis=1).astype(o_ref.dtype)

    @pl.when(t == pl.num_programs(2) - 1)
    def _():
        st_ref[0] = S


def _gdn_prompt(qkv_act, z, ba, prm, i, *, tt, kheads_per_step):
    B, T, _ = qkv_act.shape
    rep = C_V_HEADS // C_K_HEADS
    kw = kheads_per_step * C_HEAD
    vw = rep * kw
    k_blk0 = C_KEY_DIM // kw
    v_blk0 = 2 * C_KEY_DIM // vw
    z_blk0 = 0
    nh = kheads_per_step * rep
    full = lambda shape: pl.BlockSpec(shape, lambda b, h, t: (0,) * len(shape))
    return pl.pallas_call(
        _gdn_prompt_body,
        grid=(B, C_K_HEADS // kheads_per_step, T // tt),
        in_specs=[pl.BlockSpec((1, tt, kw), lambda b, h, t: (b, t, h)),
                  pl.BlockSpec((1, tt, kw), lambda b, h, t: (b, t, k_blk0 + h)),
                  pl.BlockSpec((1, tt, vw), lambda b, h, t: (b, t, v_blk0 + h)),
                  pl.BlockSpec((1, tt, vw), lambda b, h, t: (b, t, z_blk0 + h)),
                  pl.BlockSpec((1, tt, 2 * C_V_HEADS), lambda b, h, t: (b, t, 0)),
                  full((1, C_V_HEADS)), full((1, C_V_HEADS)), full((1, C_HEAD))],
        out_specs=[pl.BlockSpec((1, tt, vw), lambda b, h, t: (b, t, h)),
                   pl.BlockSpec((1, nh, C_HEAD, C_HEAD), lambda b, h, t: (b, h, 0, 0))],
        out_shape=[jax.ShapeDtypeStruct((B, T, C_VAL_DIM), BF16),
                   jax.ShapeDtypeStruct((B, C_V_HEADS, C_HEAD, C_HEAD), F32)],
        scratch_shapes=[pltpu.VMEM((nh, C_HEAD, C_HEAD), F32)],
        compiler_params=_cparams(3),
        name="gdn_prompt",
    )(qkv_act, qkv_act, qkv_act, z, ba, _row(prm['gdn_a_log'][i]), _row(prm['gdn_dt_bias'][i]),
      _row(prm['gdn_norm_w'][i]))


def _rwkv_prep_sample_body(pm_ref, pl_ref, qm_ref, ql_ref, mum_ref, mul_ref, w0_ref, w2_ref, a0_ref, a2_ref,
                           g2_ref, kk_ref, ka_ref, r_o, kf_o, v_o, kkn_o, a_o, lw_o, g_o):
    outs = _rwkv_rows(pm_ref[...], qm_ref[...], pl_ref[...][:, :LORA_ALL], ql_ref[...], mum_ref[...],
                      mul_ref[...], w0_ref[...], w2_ref[0], a0_ref[...], a2_ref[0], g2_ref[0],
                      kk_ref[...], ka_ref[...])
    for o_ref, val in zip((r_o, kf_o, v_o, kkn_o, a_o, lw_o, g_o), outs):
        o_ref[...] = val.T


def _rwkv_prep_sample(p_main, p_lora, shift, prm, i):
    bs = p_main.shape[0]
    mu = prm['rwkv_mu'][i]
    full = lambda shape: pl.BlockSpec(shape, lambda s: (0,) * len(shape))
    lay3 = lambda shape: pl.BlockSpec((1,) + shape, lambda s: (i, 0, 0))
    return pl.pallas_call(
        _rwkv_prep_sample_body,
        grid=(1,),
        in_specs=[full(p_main.shape), full(p_lora.shape), full((bs, P_A_MAIN)), full((bs, LORA_ALL)),
                  full((1, P_A_MAIN)), full((1, LORA_ALL)), full((1, A_WIDTH)),
                  lay3((LORA_W, A_WIDTH)), full((1, A_WIDTH)), lay3((LORA_A, A_WIDTH)),
                  lay3((LORA_G, A_WIDTH)), full((1, A_WIDTH)), full((1, A_WIDTH))],
        out_specs=[full((A_WIDTH, bs))] * 7,
        out_shape=[jax.ShapeDtypeStruct((A_WIDTH, bs), F32)] * 7,
        compiler_params=_cparams(1),
        name="rwkv_prep_sample",
    )(p_main, p_lora, shift[:, :P_A_MAIN], shift[:, P_A_MAIN:], _row(mu[:P_A_MAIN]), _row(mu[P_A_MAIN:]),
      _row(prm['rwkv_w0'][i]), prm['rwkv_w2'], _row(prm['rwkv_a0'][i]), prm['rwkv_a2'], prm['rwkv_g2'],
      _row(prm['rwkv_k_k'][i]), _row(prm['rwkv_k_a'][i]))


STEP_GROUP = 16


def _wkv_step_body(*refs, n_state):
    s_ref = refs[0]
    r_ref, kf_ref, v_ref, kkn_ref, a_ref, lw_ref, g_ref, rk_ref, gw_ref, gb_ref, ns_ref, y_ref = refs[n_state:]
    n = A_HEAD_DIM
    outs = []
    for h in range(s_ref.shape[0]):
        rows = slice(h * n, (h + 1) * n)
        ld = lambda ref: ref[rows, :]
        r, kf, v, kkn, a = ld(r_ref), ld(kf_ref), ld(v_ref), ld(kkn_ref), ld(a_ref)
        kk = kkn * lax.rsqrt(jnp.sum(kkn * kkn, axis=0, keepdims=True) + 1e-12)
        S = s_ref[h]
        u = jnp.sum(S * kk[None], axis=1)
        s_new = S * jnp.exp(ld(lw_ref))[None] - u[:, None, :] * (kk * a)[None] + v[:, None, :] * kf[None]
        ns_ref[h] = s_new
        y = jnp.sum(s_new * r[None], axis=1)
        mean = jnp.mean(y, axis=0, keepdims=True)
        var = jnp.mean(jnp.square(y - mean), axis=0, keepdims=True)
        yn = (y - mean) * lax.rsqrt(var + A_GN_EPS) * ld(gw_ref) + ld(gb_ref)
        bonus = jnp.sum(r * kf * ld(rk_ref), axis=0, keepdims=True) * v
        outs.append(((yn + bonus) * ld(g_ref)).T)
    y_ref[...] = jnp.concatenate(outs, axis=1).astype(y_ref.dtype)


def _layer_state_io(state_all, new_all, spec):
    if new_all is None:
        return [spec], [state_all], {}
    return [spec, pl.BlockSpec(memory_space=pl.ANY)], [state_all, new_all], {1: 0}


def _wkv_step(S_all, new_all, vals, prm, i, *, heads_per_step):
    _, n_heads, n, _, bs = S_all.shape
    wblk = heads_per_step * n
    lanes = lambda p: jnp.broadcast_to(p.reshape(A_WIDTH, 1), (A_WIDTH, bs))
    row_spec = pl.BlockSpec((wblk, bs), lambda h: (h, 0))
    s_spec = pl.BlockSpec((None, heads_per_step, n, n, bs), lambda h: (i, h, 0, 0, 0))
    st_specs, st_args, aliases = _layer_state_io(S_all, new_all, s_spec)
    return pl.pallas_call(
        functools.partial(_wkv_step_body, n_state=len(st_args)),
        grid=(n_heads // heads_per_step,),
        in_specs=st_specs + [row_spec] * 10,
        out_specs=[s_spec, pl.BlockSpec((bs, wblk), lambda h: (0, h))],
        out_shape=[jax.ShapeDtypeStruct(S_all.shape, F32), jax.ShapeDtypeStruct((bs, A_WIDTH), BF16)],
        input_output_aliases=aliases,
        compiler_params=_cparams(1),
        name="wkv_step",
    )(*st_args, *vals, lanes(prm['rwkv_r_k'][i]), lanes(prm['rwkv_gn_w'][i]), lanes(prm['rwkv_gn_b'][i]))


def _lru_sample_body(gate_ref, xb_ref, cst_ref, h0_ref, cw_ref, cb_ref, wa_ref, wx_ref, ba_ref, bx_ref, lam_ref,
                     y_ref, h_ref, ncst_ref):
    cw = cw_ref[0]
    xb = xb_ref[...]
    xc = cb_ref[...] + xb * cw[3:4]
    for j in range(3):
        xc = xc + cst_ref[:, j, :] * cw[j:j + 1]
    a, b, gact = _lru_rows(gate_ref[...], xc, wa_ref, wx_ref, ba_ref[...], bx_ref[...], lam_ref[...])
    h = a * h0_ref[...] + b
    y_ref[...] = (h * gact).astype(y_ref.dtype)
    h_ref[...] = h
    ncst_ref[:, 0, :] = cst_ref[:, 1, :]
    ncst_ref[:, 1, :] = cst_ref[:, 2, :]
    ncst_ref[:, 2, :] = xb


def _lru_sample(pb, conv_state, h0, prm, i):
    bs = pb.shape[0]
    w = B_WIDTH
    ng = w // LANES
    full = lambda shape: pl.BlockSpec(shape, lambda s: (0,) * len(shape))
    return pl.pallas_call(
        _lru_sample_body,
        grid=(1,),
        in_specs=[pl.BlockSpec((bs, w), lambda s: (0, 0)), pl.BlockSpec((bs, w), lambda s: (0, 1)),
                  full((bs, 3, w)), full((bs, w)),
                  pl.BlockSpec((1, 4, w), lambda s: (i, 0, 0)),
                  full((1, w)), full((ng, LANES, LANES)), full((ng, LANES, LANES)),
                  full((1, w)), full((1, w)), full((1, w))],
        out_specs=[full((bs, w)), full((bs, w)), full((bs, 3, w))],
        out_shape=[jax.ShapeDtypeStruct((bs, w), BF16), jax.ShapeDtypeStruct((bs, w), F32),
                   jax.ShapeDtypeStruct((bs, 3, w), F32)],
        compiler_params=_cparams(1),
        name="lru_sample",
    )(pb, pb, conv_state, h0, prm['lru_conv_w'], _row(prm['lru_conv_b'][i]),
      _block_diag_pairs(prm['lru_wa'][i]), _block_diag_pairs(prm['lru_wx'][i]), _row(prm['lru_ba'][i]),
      _row(prm['lru_bx'][i]), _row(prm['lru_lambda'][i]))


def _gdn_prep_sample_body(x_ref, cst_ref, cw_ref, o_ref, ncst_ref):
    j = pl.program_id(0)
    cw = cw_ref[0]
    x = x_ref[...]
    y = x * cw[3:4]
    for jj in range(3):
        y = y + cst_ref[:, jj, :] * cw[jj:jj + 1]
    n_qk_blocks = 2 * C_KEY_DIM // x.shape[1]
    q_scale = jnp.where(j < n_qk_blocks // 2, C_HEAD ** -0.5, 1.0).astype(F32)
    o_ref[...] = _gdn_act(y, j < n_qk_blocks, q_scale)
    ncst_ref[:, 0, :] = cst_ref[:, 1, :]
    ncst_ref[:, 1, :] = cst_ref[:, 2, :]
    ncst_ref[:, 2, :] = x


def _gdn_prep_sample(qkvz, conv_state, conv_w, i, *, tc):
    bs = qkvz.shape[0]
    return pl.pallas_call(
        _gdn_prep_sample_body,
        grid=(C_QKV_DIM // tc,),
        in_specs=[pl.BlockSpec((bs, tc), lambda j: (0, j)),
                  pl.BlockSpec((bs, 3, tc), lambda j: (0, 0, j)),
                  pl.BlockSpec((1, 4, tc), lambda j: (i, 0, j))],
        out_specs=[pl.BlockSpec((bs, tc), lambda j: (0, j)),
                   pl.BlockSpec((bs, 3, tc), lambda j: (0, 0, j))],
        out_shape=[jax.ShapeDtypeStruct((bs, C_QKV_DIM), F32),
                   jax.ShapeDtypeStruct((bs, 3, C_QKV_DIM), F32)],
        compiler_params=_cparams(1),
        name="gdn_prep_sample",
    )(qkvz, conv_state, conv_w)


def _gdn_step_body(*refs, n_state):
    s_ref = refs[0]
    q_ref, k_ref, v_ref, z_ref, ba_ref, alog_ref, dtb_ref, nw_ref, ns_ref, o_ref = refs[n_state:]
    bb = s_ref.shape[0]
    rep = C_V_HEADS // C_K_HEADS
    hk = pl.program_id(1)
    beta_all, g_all = _gdn_gates(ba_ref[...], alog_ref[...], dtb_ref[...])
    lane = lax.broadcasted_iota(jnp.int32, beta_all.shape, 2)
    pick = lambda x, col: jnp.sum(jnp.where(lane == col, x, 0.0), axis=-1, keepdims=True)
    rid = lax.broadcasted_iota(jnp.int32, (bb, SUBLANES, C_HEAD), 1)
    up = lambda x: jnp.broadcast_to(x, (bb, SUBLANES, C_HEAD))
    q = q_ref[...]
    k = k_ref[...]
    qk = jnp.sum(q * k, axis=-1, keepdims=True)
    for jv in range(rep):
        cols = slice(jv * C_HEAD, (jv + 1) * C_HEAD)
        hv = hk * rep + jv
        beta = pick(beta_all, hv)
        eg = jnp.exp(pick(g_all, hv))
        v = v_ref[:, :, cols]
        S = s_ref[:, jv]
        kb = k * beta
        lhs = jnp.where(rid == 0, up(kb * eg), jnp.where(rid == 1, up(q * eg), 0.0))
        ws = _dot3(lhs, S, _BNN)
        v_new = v * beta - ws[:, 0:1]
        o = ws[:, 1:2] + qk * v_new
        left = jnp.where(rid == 0, up(k), 0.0)
        right = jnp.where(rid == 0, up(v_new), 0.0)
        ns_ref[:, jv] = S * eg + _dot3(left, right, _BTN)
        o_ref[:, :, cols] = _gdn_head_out(o, z_ref[:, :, cols], nw_ref[...]).astype(o_ref.dtype)


def _gdn_step(S_all, new_all, qkv_act, qkvz, ba, prm, i, *, bb):
    bs = S_all.shape[1]
    rep = C_V_HEADS // C_K_HEADS
    vw = rep * C_HEAD
    v_blk0 = 2 * C_KEY_DIM // vw
    z_blk0 = C_QKV_DIM // vw
    full = lambda shape: pl.BlockSpec(shape, lambda b, h: (0,) * len(shape))
    s_spec = pl.BlockSpec((None, bb, rep, C_HEAD, C_HEAD), lambda b, h: (i, b, h, 0, 0))
    st_specs, st_args, aliases = _layer_state_io(S_all, new_all, s_spec)
    return pl.pallas_call(
        functools.partial(_gdn_step_body, n_state=len(st_args)),
        grid=(bs // bb, C_K_HEADS),
        in_specs=st_specs + [
                  pl.BlockSpec((bb, 1, C_HEAD), lambda b, h: (b, 0, h)),
                  pl.BlockSpec((bb, 1, C_HEAD), lambda b, h: (b, 0, C_K_HEADS + h)),
                  pl.BlockSpec((bb, 1, vw), lambda b, h: (b, 0, v_blk0 + h)),
                  pl.BlockSpec((bb, 1, vw), lambda b, h: (b, 0, z_blk0 + h)),
                  pl.BlockSpec((bb, 1, 2 * C_V_HEADS), lambda b, h: (b, 0, 0)),
                  full((1, C_V_HEADS)), full((1, C_V_HEADS)), full((1, C_HEAD))],
        out_specs=[s_spec, pl.BlockSpec((bb, 1, vw), lambda b, h: (b, 0, h))],
        out_shape=[jax.ShapeDtypeStruct(S_all.shape, F32), jax.ShapeDtypeStruct((bs, 1, C_VAL_DIM), BF16)],
        input_output_aliases=aliases,
        compiler_params=_cparams(2),
        name="gdn_step",
    )(*st_args, qkv_act, qkv_act, qkv_act, qkvz, ba, _row(prm['gdn_a_log'][i]), _row(prm['gdn_dt_bias'][i]),
      _row(prm['gdn_norm_w'][i]))


MOD_SH1, MOD_SC1, MOD_GT1, MOD_SH2, MOD_SC2, MOD_GT2 = range(6)


def _in_proj_ab(h, prm, i):
    w = [(prm['w_in_ab_t'], i, 0, True)]
    p_main = _mm([h], w, n_out=P_A_MAIN, name="in_ab_main")
    p_lora = _mm([h], w, n_out=LORA_ALL, col0=P_A_MAIN, tn=LORA_ALL, name="in_ab_lora")
    pb = _mm([h], w, n_out=2 * B_WIDTH, col0=P_A, name="in_ab_lru")
    return p_main, p_lora, pb


def _in_proj_c_gates(h, prm, i):
    return _mm([h], [(prm['w_in_c_t'], i, 0, True)], n_out=2 * C_V_HEADS, col0=C_QKV_DIM + C_VAL_DIM,
               tn=2 * C_V_HEADS, name="in_c_gates")


def _ffn_down(act, prm, layer, x, mod):
    return _mm([act], [(prm['ffn_w_down'], layer, 0, False)], n_out=D_MODEL, res=x, gate=mod, gate_blk=MOD_GT2,
               name="ffn_down")


def _out_proj_ab(ya, yb, prm, i, x, mod):
    w = prm['w_out_ab']
    return _mm([ya, yb], [(w, i, 0, False), (w, i, 1, False)], n_out=D_MODEL, res=x, gate=mod, gate_blk=MOD_GT1,
               name="out_ab")


def _out_proj_c(yc, prm, i, x, mod):
    return _mm([yc], [(prm['w_out_c'], i, 0, False)], n_out=D_MODEL, res=x, gate=mod, gate_blk=MOD_GT1,
               name="out_c")


def _trunk_prompt(x, mods, prm):
    B, T, _ = x.shape
    tm, tt = min(1024, T), min(256, T)
    wkv, shift, lru_h, lru_conv, gdn, gdn_conv, ffn_conv = [], [], [], [], [], [], []
    for layer in range(DEPTH):
        mod = mods[layer].reshape(B, 1, 6 * D_MODEL)
        i = layer // 2
        h = _norm(x, prm['norm_mix'][layer], mod, MOD_SC1, MOD_SH1, tt=tt)
        if layer % 2 == 0:
            p_main, p_lora, pb = _in_proj_ab(h, prm, i)
            vals = _rwkv_prep_prompt(p_main, p_lora, prm, i, tt=tt)
            ya, s_new = _wkv_prompt(vals, prm, i, tt=tt, heads_per_step=A_HEADS)
            yb, h_last, cst = _lru_prompt(pb, prm, i, tt=tt)
            x = _out_proj_ab(ya, yb, prm, i, x, mod)
            wkv.append(s_new)
            shift.append(jnp.concatenate([p_main[:, -1], p_lora[:, -1, :LORA_ALL]], axis=-1))
            lru_h.append(h_last[:, 0])
            lru_conv.append(cst)
        else:
            qkv_act, cst = _in_c_conv_prompt(h, prm['w_in_c_t'], prm['gdn_conv_w'], i, tm=tm, tn=512)
            z = _mm([h], [(prm['w_in_c_t'], i, 0, True)], n_out=C_VAL_DIM, col0=C_QKV_DIM, name="in_c_z")
            ba = _in_proj_c_gates(h, prm, i)
            yc, s_new = _gdn_prompt(qkv_act, z, ba, prm, i, tt=min(2 * CHUNK, T), kheads_per_step=C_K_HEADS // 2)
            x = _out_proj_c(yc, prm, i, x, mod)
            gdn.append(s_new)
            gdn_conv.append(cst)
        h = _norm(x, prm['norm_ffn'][layer], mod, MOD_SC2, MOD_SH2, tt=tt)
        act, fst = _ffn_up_prompt(h, prm['ffn_w_gate'], prm['ffn_w_up'], prm['ffn_conv_w'], prm['ffn_conv_b'],
                                  layer, tm=tm, tn=512)
        x = _ffn_down(act, prm, layer, x, mod)
        ffn_conv.append(fst)
    y = _norm(x, prm['norm_out'], tt=tt, out_dtype=F32)
    stk = jnp.stack
    return y, stk(wkv), stk(shift), stk(lru_h), stk(lru_conv), stk(gdn), stk(gdn_conv), stk(ffn_conv)


def _trunk_sample(x, mods, st, prm):
    bs = x.shape[0]
    x = x.reshape(1, bs, D_MODEL)
    wkv0, shift0, lru_h0, lru_conv0, gdn0, gdn_conv0, ffn_conv0 = st
    wkv_new = gdn_new = None
    wkv0_t = jnp.transpose(wkv0, (0, 2, 3, 4, 1))
    shift, lru_h, lru_conv, gdn_conv, ffn_conv = [], [], [], [], []
    for layer in range(DEPTH):
        mod = mods[layer].reshape(1, bs, 6 * D_MODEL)
        i = layer // 2
        h = _norm(x, prm['norm_mix'][layer], mod, MOD_SC1, MOD_SH1, tt=bs)
        if layer % 2 == 0:
            p_main, p_lora, pb = _in_proj_ab(h, prm, i)
            p_main, p_lora, pb = p_main[0], p_lora[0], pb[0]
            vals = _rwkv_prep_sample(p_main, p_lora, shift0[i], prm, i)
            wkv_new, ya = _wkv_step(wkv0_t, wkv_new, vals, prm, i, heads_per_step=2)
            yb, h_last, cst = _lru_sample(pb, lru_conv0[i], lru_h0[i], prm, i)
            x = _out_proj_ab(ya[None], yb[None], prm, i, x, mod)
            shift.append(jnp.concatenate([p_main, p_lora[:, :LORA_ALL]], axis=-1))
            lru_h.append(h_last)
            lru_conv.append(cst)
        else:
            qkvz = _mm([h], [(prm['w_in_c_t'], i, 0, True)], n_out=C_QKV_DIM + C_VAL_DIM, name="in_c_main")
            ba = _in_proj_c_gates(h, prm, i)
            qkv_act, cst = _gdn_prep_sample(qkvz[0], gdn_conv0[i], prm['gdn_conv_w'], i, tc=2048)
            gdn_new, yc = _gdn_step(gdn0, gdn_new, qkv_act[:, None], qkvz[0][:, None], ba[0][:, None], prm, i,
                                    bb=min(STEP_GROUP, bs))
            x = _out_proj_c(yc.reshape(1, bs, C_VAL_DIM), prm, i, x, mod)
            gdn_conv.append(cst)
        h = _norm(x, prm['norm_ffn'][layer], mod, MOD_SC2, MOD_SH2, tt=bs)
        act, fst = _ffn_up_sample(h[0], prm['ffn_w_gate'], prm['ffn_w_up'], prm['ffn_conv_w'],
                                  prm['ffn_conv_b'], ffn_conv0[layer], layer, tn=512)
        x = _ffn_down(act[None], prm, layer, x, mod)
        ffn_conv.append(fst)
    y = _norm(x, prm['norm_out'], tt=bs, out_dtype=F32).reshape(bs, 1, D_MODEL)
    stk = jnp.stack
    return (y, jnp.transpose(wkv_new, (0, 4, 1, 2, 3)), stk(shift), stk(lru_h), stk(lru_conv), gdn_new, stk(gdn_conv),
            stk(ffn_conv))


def kernel(x_prompt, x_sample, c_prompt, c_sample, state_rwkv_wkv, state_rwkv_shift, state_lru_h, state_lru_conv, state_gdn, state_gdn_conv, state_ffn_conv, w_mod, b_mod, norm_mix, norm_ffn, norm_out, w_in_ab, w_out_ab, rwkv_mu, rwkv_w0, rwkv_w2, rwkv_a0, rwkv_a2, rwkv_g2, rwkv_k_k, rwkv_k_a, rwkv_r_k, rwkv_gn_w, rwkv_gn_b, lru_conv_w, lru_conv_b, lru_wa, lru_ba, lru_wx, lru_bx, lru_lambda, w_in_c, w_out_c, gdn_conv_w, gdn_a_log, gdn_dt_bias, gdn_norm_w, ffn_w_gate, ffn_w_up, ffn_conv_w, ffn_conv_b, ffn_w_down):
    prm = dict(norm_mix=norm_mix, norm_ffn=norm_ffn, norm_out=norm_out, w_out_ab=w_out_ab,
               w_in_ab_t=jnp.swapaxes(w_in_ab, 1, 2), w_in_c_t=jnp.swapaxes(w_in_c, 1, 2),
               rwkv_mu=rwkv_mu, rwkv_w0=rwkv_w0, rwkv_w2=rwkv_w2, rwkv_a0=rwkv_a0, rwkv_a2=rwkv_a2,
               rwkv_g2=rwkv_g2, rwkv_k_k=rwkv_k_k, rwkv_k_a=rwkv_k_a,
               rwkv_r_k=rwkv_r_k.reshape(rwkv_r_k.shape[0], A_WIDTH), rwkv_gn_w=rwkv_gn_w,
               rwkv_gn_b=rwkv_gn_b, lru_conv_w=lru_conv_w, lru_conv_b=lru_conv_b, lru_wa=lru_wa, lru_ba=lru_ba,
               lru_wx=lru_wx, lru_bx=lru_bx, lru_lambda=lru_lambda, w_in_c=w_in_c, w_out_c=w_out_c,
               gdn_conv_w=gdn_conv_w, gdn_a_log=gdn_a_log, gdn_dt_bias=gdn_dt_bias, gdn_norm_w=gdn_norm_w,
               ffn_w_gate=ffn_w_gate, ffn_w_up=ffn_w_up, ffn_conv_w=ffn_conv_w, ffn_conv_b=ffn_conv_b,
               ffn_w_down=ffn_w_down)
    mods_p, mods_s = _modulation(c_prompt, c_sample, w_mod, b_mod)
    outs_p = _trunk_prompt(x_prompt, mods_p, prm)
    outs_s = _trunk_sample(x_sample, mods_s,
                           (state_rwkv_wkv, state_rwkv_shift, state_lru_h, state_lru_conv, state_gdn,
                            state_gdn_conv, state_ffn_conv), prm)
    return (outs_p[0], outs_s[0]) + tuple(outs_p[1:]) + tuple(outs_s[1:])
```

```python
import functools

import jax
import jax.numpy as jnp
from jax import lax
from jax.experimental import pallas as pl
from jax.experimental.pallas import tpu as pltpu

F32 = jnp.float32
BF16 = jnp.bfloat16

D_MODEL = 2048
DEPTH = 4
A_HEAD_DIM = 64
A_WIDTH = D_MODEL // 2
A_HEADS = A_WIDTH // A_HEAD_DIM
LORA_W = 64
LORA_A = 64
LORA_G = 160
LORA_ALL = LORA_W + LORA_A + LORA_G
P_A_MAIN = 3 * A_WIDTH
P_A = P_A_MAIN + LORA_ALL
A_GN_EPS = 64e-5
B_WIDTH = D_MODEL - A_WIDTH
B_BLOCK_DIM = 64
LRU_C = 8.0
C_HEAD = 128
C_K_HEADS = D_MODEL // C_HEAD
C_V_HEADS = 2 * C_K_HEADS
C_KEY_DIM = C_K_HEADS * C_HEAD
C_VAL_DIM = C_V_HEADS * C_HEAD
C_QKV_DIM = 2 * C_KEY_DIM + C_VAL_DIM
D_FF = 5632
NORM_EPS = 1e-6

LANES = 128
SUBLANES = 8
HIST = SUBLANES
CHUNK = 64
VMEM_LIMIT = 56 * 1024 * 1024
MM_VMEM_BUDGET = 46 * 1024 * 1024

_NN = (((1,), (0,)), ((), ()))
_NT = (((1,), (1,)), ((), ()))
_TN = (((0,), (0,)), ((), ()))
_BNN = (((2,), (1,)), ((0,), (0,)))
_BNT = (((2,), (2,)), ((0,), (0,)))
_BTN = (((1,), (1,)), ((0,), (0,)))


def _cparams(n_axes):
    return pltpu.CompilerParams(dimension_semantics=("arbitrary",) * n_axes,
                                vmem_limit_bytes=VMEM_LIMIT)


def _bdot(a, b, dims=_NN):
    return lax.dot_general(a.astype(BF16), b.astype(BF16), dims, preferred_element_type=F32)


def _split2(a):
    hi = a.astype(BF16)
    lo = (a - hi.astype(F32)).astype(BF16)
    return hi, lo


def _dot3(a, b, dims=_NN):
    ah, al = _split2(a)
    bh, bl = _split2(b)
    d = lambda x, y: lax.dot_general(x, y, dims, preferred_element_type=F32)
    return d(ah, bh) + (d(ah, bl) + d(al, bh))


def _split3(a):
    h0 = a.astype(BF16)
    r1 = a - h0.astype(F32)
    h1 = r1.astype(BF16)
    h2 = (r1 - h1.astype(F32)).astype(BF16)
    return h0, h1, h2


def _sel_l(mask01, x, dims=_NN):
    m = mask01.astype(BF16)
    d = lambda y: lax.dot_general(m, y, dims, preferred_element_type=F32)
    h0, h1, h2 = _split3(x)
    return d(h0) + (d(h1) + d(h2))


def _sel_r(x, mask01):
    m = mask01.astype(BF16)
    d = lambda y: lax.dot_general(y, m, _NN, preferred_element_type=F32)
    h0, h1, h2 = _split3(x)
    return d(h0) + (d(h1) + d(h2))


def _iota2(shape, axis):
    return lax.broadcasted_iota(jnp.int32, shape, axis)


def _tri(n, strict):
    i = _iota2((n, n), 0)
    j = _iota2((n, n), 1)
    return (i > j) if strict else (i >= j)


def _unit_lower_inverse(low, mm):
    n = low.shape[-1]
    dims = _BNN if low.ndim == 3 else _NN
    eye = (_iota2((n, n), 0) == _iota2((n, n), 1)).astype(F32)
    m = -low
    inv = eye + m
    p = m
    span = 2
    while span < n:
        p = mm(p, p, dims)
        inv = inv + mm(inv, p, dims)
        span *= 2
    return inv


def _unit_lower_inverse_refined(low):
    n = low.shape[-1]
    dims = _BNN if low.ndim == 3 else _NN
    eye = (_iota2((n, n), 0) == _iota2((n, n), 1)).astype(F32)
    x0 = _unit_lower_inverse(low, _bdot)
    resid = eye - _dot3(eye + low, x0, dims)
    return x0 + _bdot(x0, resid, dims)


def _block_tri(tt, block):
    i = _iota2((tt, tt), 0)
    j = _iota2((tt, tt), 1)
    return (i // block == j // block) & (i >= j)


def _neg_expm1(x):
    return -jnp.tanh(0.5 * x) * (jnp.exp(x) + 1.0)


def _segment_sum(x, seg):
    i = _iota2((LANES, LANES), 0) // seg
    j = _iota2((LANES, LANES), 1) // seg
    same = (i == j)
    parts = [_sel_r(x[:, g * LANES:(g + 1) * LANES], same) for g in range(x.shape[1] // LANES)]
    return parts[0] if len(parts) == 1 else jnp.concatenate(parts, axis=1)


def _rwkv_rows(p_main, q_main, p_lora, q_lora, mu_main, mu_lora, w0, w2, a0, a2, g2, k_k, k_a):
    pm = p_main + (q_main - p_main) * mu_main
    pl_ = p_lora + (q_lora - p_lora) * mu_lora
    r = pm[:, :A_WIDTH]
    k = pm[:, A_WIDTH:2 * A_WIDTH]
    v = pm[:, 2 * A_WIDTH:]
    xw = pl_[:, :LORA_W]
    xa = pl_[:, LORA_W:LORA_W + LORA_A]
    xg = pl_[:, LORA_W + LORA_A:LORA_ALL]
    w_ll = -jax.nn.softplus(-(w0 + _bdot(jnp.tanh(xw), w2))) - 0.5
    lw = -jnp.exp(w_ll)
    a = jax.nn.sigmoid(a0 + _bdot(xa, a2))
    g = _bdot(jax.nn.sigmoid(xg), g2)
    kkn = k * k_k
    kf = k * (1.0 + (a - 1.0) * k_a)
    return r, kf, v, kkn, a, lw, g


def _rwkv_head_out(y, r, kf, v, g, r_k, gn_w, gn_b):
    mean = jnp.mean(y, axis=-1, keepdims=True)
    var = jnp.mean(jnp.square(y - mean), axis=-1, keepdims=True)
    yn = (y - mean) * lax.rsqrt(var + A_GN_EPS) * gn_w + gn_b
    bonus = jnp.sum(r * kf * r_k, axis=-1, keepdims=True) * v
    return (yn + bonus) * g


def _lru_rows(gate, xc, wa_bd, wx_bd, ba, bx, lam):
    ng = xc.shape[1] // LANES
    ra = jnp.concatenate([_bdot(xc[:, g * LANES:(g + 1) * LANES], wa_bd[g]) for g in range(ng)], axis=1)
    rx = jnp.concatenate([_bdot(xc[:, g * LANES:(g + 1) * LANES], wx_bd[g]) for g in range(ng)], axis=1)
    r_gate = jax.nn.sigmoid(ra + ba)
    i_gate = jax.nn.sigmoid(rx + bx)
    log_a = -LRU_C * r_gate * jax.nn.softplus(-lam)
    a = jnp.exp(log_a)
    b = jnp.sqrt(_neg_expm1(2.0 * log_a)) * (i_gate * xc)
    return a, b, jax.nn.gelu(gate)


def _gdn_act(y, is_qk, q_scale):
    y = jax.nn.silu(y)
    outs = []
    for h in range(y.shape[1] // C_HEAD):
        yh = y[:, h * C_HEAD:(h + 1) * C_HEAD]
        nrm = yh * lax.rsqrt(jnp.sum(yh * yh, axis=-1, keepdims=True) + 1e-6) * q_scale
        outs.append(jnp.where(is_qk, nrm, yh))
    return jnp.concatenate(outs, axis=1)


def _gdn_head_out(o, z, norm_w):
    o = o * lax.rsqrt(jnp.mean(o * o, axis=-1, keepdims=True) + NORM_EPS) * norm_w
    return o * jax.nn.silu(z)


def _w_block(w_ref):
    return w_ref[0] if len(w_ref.shape) == 3 else w_ref[...]


def _mm_body(*refs, n_x, has_res, transposed):
    x_refs = refs[:n_x]
    w_refs = refs[n_x:2 * n_x]
    k = 2 * n_x
    if has_res:
        res_ref, gate_ref = refs[k], refs[k + 1]
        k += 2
    o_ref = refs[k]
    wbf_refs = refs[k + 1:k + 1 + n_x]

    @pl.when((pl.program_id(1) == 0) & (pl.program_id(2) == 0))
    def _():
        for w_ref, wbf_ref in zip(w_refs, wbf_refs):
            wbf_ref[...] = _w_block(w_ref).astype(BF16)

    acc = None
    for x_ref, wbf_ref, tr in zip(x_refs, wbf_refs, transposed):
        d = lax.dot_general(x_ref[0], wbf_ref[...], _NT if tr else _NN, preferred_element_type=F32)
        acc = d if acc is None else acc + d
    if has_res:
        acc = res_ref[0] + gate_ref[0] * acc
    o_ref[0] = acc.astype(o_ref.dtype)


def _pick_tiles(n_out, k_sum, rows, seq, n_out_bufs):
    best = None
    for tm in [t for t in (1024, 512, 256, 128) if seq % t == 0] or [seq]:
        for tn in (1024, 512, 256, 128):
            if n_out % tn:
                continue
            need = (2 * tm * k_sum * 2 + 2 * k_sum * tn * 4 + k_sum * tn * 2
                    + (2 * n_out_bufs + 1) * tm * tn * 4)
            if need > MM_VMEM_BUDGET:
                continue
            traffic = (n_out // tn) * rows * k_sum * 2
            key = (tm < min(512, seq), traffic, -tn, -tm)
            if best is None or key < best[0]:
                best = (key, tm, tn)
    if best is None:
        raise ValueError("no projection tile fits VMEM")
    return best[1], best[2]


def _mm(xs, ws, *, n_out, col0=0, tm=None, tn=None, out_dtype=F32, res=None, gate=None, gate_blk=0, name="proj"):
    B, T, _ = xs[0].shape
    ks = [x.shape[-1] for x in xs]
    has_res = res is not None
    if tn is None:
        tm, tn = _pick_tiles(n_out, sum(ks), B * T, T, 2 if has_res else 1)
    elif tm is None:
        tm = min(1024, T)
    assert n_out % tn == 0 and T % tm == 0
    in_specs = [pl.BlockSpec((1, tm, k), lambda j, b, t: (b, t, 0)) for k in ks]
    scratch = []
    for k, (w, layer, rb, tr) in zip(ks, ws):
        if tr:
            assert col0 % SUBLANES == 0 and tn % SUBLANES == 0 and w.shape[2] == k
            in_specs.append(pl.BlockSpec(
                (pl.Element(1), pl.Element(tn), pl.Element(k)),
                functools.partial(lambda j, b, t, layer: (layer, pl.multiple_of(col0 + j * tn, SUBLANES), 0),
                                  layer=layer)))
            scratch.append(pltpu.VMEM((tn, k), BF16))
        else:
            assert col0 % tn == 0
            in_specs.append(pl.BlockSpec(
                (None, k, tn),
                functools.partial(lambda j, b, t, layer, rb: (layer, rb, col0 // tn + j), layer=layer, rb=rb)))
            scratch.append(pltpu.VMEM((k, tn), BF16))
    args = list(xs) + [w for w, _, _, _ in ws]
    if has_res:
        in_specs.append(pl.BlockSpec((1, tm, tn), lambda j, b, t: (b, t, j)))
        gb0 = gate_blk * (n_out // tn)
        if gate.shape[1] == 1:
            in_specs.append(pl.BlockSpec((1, 1, tn), lambda j, b, t: (b, 0, gb0 + j)))
        else:
            in_specs.append(pl.BlockSpec((1, tm, tn), lambda j, b, t: (b, t, gb0 + j)))
        args += [res, gate]
    return pl.pallas_call(
        functools.partial(_mm_body, n_x=len(xs), has_res=has_res, transposed=tuple(w[3] for w in ws)),
        grid=(n_out // tn, B, T // tm),
        in_specs=in_specs,
        out_specs=pl.BlockSpec((1, tm, tn), lambda j, b, t: (b, t, j)),
        out_shape=jax.ShapeDtypeStruct((B, T, n_out), out_dtype),
        scratch_shapes=scratch,
        compiler_params=_cparams(3),
        name=name,
    )(*args)


def _mod_body(c_ref, w_ref, b_ref, op_ref, os_ref):
    bs = os_ref.shape[1]
    y = _bdot(jax.nn.silu(c_ref[...]), w_ref[0]) + b_ref[0]
    os_ref[0] = y[:bs]
    op_ref[0] = y[bs:]


def _modulation(c_p, c_s, w_mod, b_mod):
    depth, d, n = w_mod.shape
    tn = 1024
    bp, bs = c_p.shape[0], c_s.shape[0]
    return pl.pallas_call(
        _mod_body,
        grid=(depth, n // tn),
        in_specs=[pl.BlockSpec((bs + bp, d), lambda l, j: (0, 0)),
                  pl.BlockSpec((1, d, tn), lambda l, j: (l, 0, j)),
                  pl.BlockSpec((1, 1, tn), lambda l, j: (l, 0, j))],
        out_specs=[pl.BlockSpec((1, bp, tn), lambda l, j: (l, 0, j)),
                   pl.BlockSpec((1, bs, tn), lambda l, j: (l, 0, j))],
        out_shape=[jax.ShapeDtypeStruct((depth, bp, n), F32),
                   jax.ShapeDtypeStruct((depth, bs, n), F32)],
        compiler_params=_cparams(2),
        name="modulation",
    )(jnp.concatenate([c_s, c_p], axis=0), w_mod, b_mod.reshape(depth, 1, n))


def _norm_body(*refs, has_mod):
    if has_mod:
        x_ref, g_ref, sc_ref, sh_ref, o_ref = refs
    else:
        x_ref, g_ref, o_ref = refs
    x = x_ref[0]
    y = x * lax.rsqrt(jnp.mean(x * x, axis=-1, keepdims=True) + NORM_EPS) * g_ref[...]
    if has_mod:
        y = y * (1.0 + sc_ref[0]) + sh_ref[0]
    o_ref[0] = y.astype(o_ref.dtype)


def _norm(x, gain, mod=None, sc_blk=0, sh_blk=0, *, tt, out_dtype=BF16):
    B, T, d = x.shape
    in_specs = [pl.BlockSpec((1, tt, d), lambda b, t: (b, t, 0)),
                pl.BlockSpec((1, d), lambda b, t: (0, 0))]
    args = [x, gain.reshape(1, d)]
    if mod is not None:
        for blk in (sc_blk, sh_blk):
            if mod.shape[1] == 1:
                in_specs.append(pl.BlockSpec((1, 1, d), functools.partial(lambda b, t, blk: (b, 0, blk), blk=blk)))
            else:
                in_specs.append(pl.BlockSpec((1, tt, d), functools.partial(lambda b, t, blk: (b, t, blk), blk=blk)))
        args += [mod, mod]
    return pl.pallas_call(
        functools.partial(_norm_body, has_mod=mod is not None),
        grid=(B, T // tt),
        in_specs=in_specs,
        out_specs=pl.BlockSpec((1, tt, d), lambda b, t: (b, t, 0)),
        out_shape=jax.ShapeDtypeStruct((B, T, d), out_dtype),
        compiler_params=_cparams(2),
        name="norm",
    )(*args)


def _ffn_up_prompt_body(h_ref, wg_ref, wu_ref, cw_ref, cb_ref, act_ref, st_ref, wg_bf, wu_bf, buf):
    t = pl.program_id(2)
    tm = h_ref.shape[1]

    @pl.when((pl.program_id(1) == 0) & (t == 0))
    def _():
        wg_bf[...] = wg_ref[0].astype(BF16)
        wu_bf[...] = wu_ref[0].astype(BF16)

    @pl.when(t == 0)
    def _():
        buf[0:HIST, :] = jnp.zeros((HIST, buf.shape[1]), F32)

    h = h_ref[0]
    buf[HIST:, :] = jnp.dot(h, wg_bf[...], preferred_element_type=F32)
    cw = cw_ref[0]
    u = (buf[HIST - 2:HIST - 2 + tm, :] * cw[0:1] + buf[HIST - 1:HIST - 1 + tm, :] * cw[1:2]
         + buf[HIST:, :] * cw[2:3] + cb_ref[0])
    up = jnp.dot(h, wu_bf[...], preferred_element_type=F32)
    act_ref[0] = (jax.nn.gelu(u) * up).astype(act_ref.dtype)
    st_ref[0] = buf[HIST + tm - 2:HIST + tm, :]
    buf[0:HIST, :] = buf[tm:tm + HIST, :]


def _ffn_up_prompt(h, w_gate, w_up, conv_w, conv_b, layer, *, tm, tn):
    B, T, d = h.shape
    n = w_gate.shape[-1]
    return pl.pallas_call(
        _ffn_up_prompt_body,
        grid=(n // tn, B, T // tm),
        in_specs=[pl.BlockSpec((1, tm, d), lambda j, b, t: (b, t, 0)),
                  pl.BlockSpec((1, d, tn), lambda j, b, t: (layer, 0, j)),
                  pl.BlockSpec((1, d, tn), lambda j, b, t: (layer, 0, j)),
                  pl.BlockSpec((1, 3, tn), lambda j, b, t: (layer, 0, j)),
                  pl.BlockSpec((1, 1, tn), lambda j, b, t: (layer, 0, j))],
        out_specs=[pl.BlockSpec((1, tm, tn), lambda j, b, t: (b, t, j)),
                   pl.BlockSpec((1, 2, tn), lambda j, b, t: (b, 0, j))],
        out_shape=[jax.ShapeDtypeStruct((B, T, n), BF16),
                   jax.ShapeDtypeStruct((B, 2, n), F32)],
        scratch_shapes=[pltpu.VMEM((d, tn), BF16), pltpu.VMEM((d, tn), BF16),
                        pltpu.VMEM((HIST + tm, tn), F32)],
        compiler_params=_cparams(3),
        name="ffn_up_prompt",
    )(h, w_gate, w_up, conv_w, conv_b.reshape(conv_b.shape[0], 1, n))


def _ffn_up_sample_body(*refs, n_state):
    st_ref = refs[0]
    h_ref, wg_ref, wu_ref, cw_ref, cb_ref, act_ref, nst_ref = refs[n_state:]
    h = h_ref[...]
    pre = _bdot(h, wg_ref[0])
    cw = cw_ref[0]
    u = st_ref[:, 0, :] * cw[0:1] + st_ref[:, 1, :] * cw[1:2] + pre * cw[2:3] + cb_ref[0]
    up = _bdot(h, wu_ref[0])
    act_ref[...] = (jax.nn.gelu(u) * up).astype(act_ref.dtype)
    nst_ref[:, 0, :] = st_ref[:, 1, :]
    nst_ref[:, 1, :] = pre


def _ffn_up_sample(h, w_gate, w_up, conv_w, conv_b, state_all, new_all, layer, *, tn):
    bs, d = h.shape
    n = w_gate.shape[-1]
    s_spec = pl.BlockSpec((None, bs, 2, tn), lambda j: (layer, 0, 0, j))
    st_specs, st_args, aliases = _layer_state_io(state_all, new_all, s_spec, out_index=1)
    return pl.pallas_call(
        functools.partial(_ffn_up_sample_body, n_state=len(st_args)),
        grid=(n // tn,),
        in_specs=st_specs + [pl.BlockSpec((bs, d), lambda j: (0, 0)),
                             pl.BlockSpec((1, d, tn), lambda j: (layer, 0, j)),
                             pl.BlockSpec((1, d, tn), lambda j: (layer, 0, j)),
                             pl.BlockSpec((1, 3, tn), lambda j: (layer, 0, j)),
                             pl.BlockSpec((1, 1, tn), lambda j: (layer, 0, j))],
        out_specs=[pl.BlockSpec((bs, tn), lambda j: (0, j)), s_spec],
        out_shape=[jax.ShapeDtypeStruct((bs, n), BF16),
                   jax.ShapeDtypeStruct(state_all.shape, F32)],
        input_output_aliases=aliases,
        compiler_params=_cparams(1),
        name="ffn_up_sample",
    )(*st_args, h, w_gate, w_up, conv_w, conv_b.reshape(conv_b.shape[0], 1, n))


def _rwkv_prep_prompt_body(pm_ref, pl_ref, mum_ref, mul_ref, w0_ref, w2_ref, a0_ref, a2_ref, g2_ref,
                           kk_ref, ka_ref, r_o, kf_o, v_o, kkn_o, a_o, lw_o, g_o, bufm, bufl):
    t = pl.program_id(1)
    tt = pm_ref.shape[1]

    @pl.when(t == 0)
    def _():
        bufm[0:HIST, :] = jnp.zeros((HIST, bufm.shape[1]), F32)
        bufl[0:HIST, :] = jnp.zeros((HIST, bufl.shape[1]), F32)

    p_main = pm_ref[0]
    p_lora = pl_ref[0][:, :LORA_ALL]
    bufm[HIST:, :] = p_main
    bufl[HIST:, :] = p_lora
    q_main = bufm[HIST - 1:HIST - 1 + tt, :]
    q_lora = bufl[HIST - 1:HIST - 1 + tt, :]
    outs = _rwkv_rows(p_main, q_main, p_lora, q_lora, mum_ref[...], mul_ref[...], w0_ref[...], w2_ref[0],
                      a0_ref[...], a2_ref[0], g2_ref[0], kk_ref[...], ka_ref[...])
    for o_ref, val in zip((r_o, kf_o, v_o, kkn_o, a_o, lw_o, g_o), outs):
        o_ref[0] = val
    bufm[0:HIST, :] = bufm[tt:tt + HIST, :]
    bufl[0:HIST, :] = bufl[tt:tt + HIST, :]


def _row(v):
    return v.reshape(1, -1)


def _rwkv_prep_prompt(p_main, p_lora, prm, i, *, tt):
    B, T, _ = p_main.shape
    lw_pad = p_lora.shape[-1]
    mu = prm['rwkv_mu'][i]
    full = lambda shape: pl.BlockSpec(shape, lambda b, t: (0,) * len(shape))
    lay3 = lambda shape: pl.BlockSpec((1,) + shape, lambda b, t: (i, 0, 0))
    out_spec = pl.BlockSpec((1, tt, A_WIDTH), lambda b, t: (b, t, 0))
    return pl.pallas_call(
        _rwkv_prep_prompt_body,
        grid=(B, T // tt),
        in_specs=[pl.BlockSpec((1, tt, P_A_MAIN), lambda b, t: (b, t, 0)),
                  pl.BlockSpec((1, tt, lw_pad), lambda b, t: (b, t, 0)),
                  full((1, P_A_MAIN)), full((1, LORA_ALL)), full((1, A_WIDTH)),
                  lay3((LORA_W, A_WIDTH)), full((1, A_WIDTH)), lay3((LORA_A, A_WIDTH)),
                  lay3((LORA_G, A_WIDTH)), full((1, A_WIDTH)), full((1, A_WIDTH))],
        out_specs=[out_spec] * 7,
        out_shape=[jax.ShapeDtypeStruct((B, T, A_WIDTH), F32)] * 7,
        scratch_shapes=[pltpu.VMEM((HIST + tt, P_A_MAIN), F32), pltpu.VMEM((HIST + tt, LORA_ALL), F32)],
        compiler_params=_cparams(2),
        name="rwkv_prep_prompt",
    )(p_main, p_lora, _row(mu[:P_A_MAIN]), _row(mu[P_A_MAIN:]), _row(prm['rwkv_w0'][i]), prm['rwkv_w2'],
      _row(prm['rwkv_a0'][i]), prm['rwkv_a2'], prm['rwkv_g2'], _row(prm['rwkv_k_k'][i]), _row(prm['rwkv_k_a'][i]))


def _wkv_prompt_body(r_ref, kf_ref, v_ref, kkn_ref, a_ref, lw_ref, g_ref, rk_ref, gw_ref, gb_ref,
                     ya_ref, st_ref, s_scr):
    t = pl.program_id(2)
    tt = r_ref.shape[1]
    n = A_HEAD_DIM
    C = CHUNK
    nc = tt // C
    nh = r_ref.shape[2] // n

    @pl.when(t == 0)
    def _():
        s_scr[...] = jnp.zeros(s_scr.shape, F32)

    def units(x):
        parts = [x[:, h * n:(h + 1) * n].reshape(nc, C, n) for h in range(nh)]
        return jnp.stack(parts, axis=1).reshape(nc * nh, C, n)

    lw_all = lw_ref[0]
    lc = units(_sel_l(_block_tri(tt, C), lw_all))
    r, kf, v, kkn, a, lw = map(units, (r_ref[0], kf_ref[0], v_ref[0], kkn_ref[0], a_ref[0], lw_all))
    incl = _tri(C, False)
    strict = _tri(C, True)

    kk = kkn * lax.rsqrt(jnp.sum(kkn * kkn, axis=-1, keepdims=True) + 1e-12)
    bb = kk * a
    p = jnp.exp(lc)
    pinv = jnp.exp(-lc)
    kt = kf * pinv
    bt = bb * pinv
    at = kk * jnp.exp(lc - lw)
    rt = r * p
    gram = _bdot(jnp.concatenate([at, rt], axis=1), jnp.concatenate([bt, kt], axis=1), _BNT)
    a_ab = jnp.where(strict, gram[:, :C, :C], 0.0)
    a_ak = jnp.where(strict, gram[:, :C, C:], 0.0)
    a_rb = jnp.where(incl, gram[:, C:, :C], 0.0)
    a_rk = jnp.where(incl, gram[:, C:, C:], 0.0)
    tinv = _unit_lower_inverse(a_ab, _bdot)
    av = _bdot(jnp.concatenate([a_ak, a_rk], axis=1), v, _BNN)
    tz = _bdot(tinv, jnp.concatenate([at, av[:, :C]], axis=2), _BNN)
    rz = _bdot(a_rb, tz, _BNN)
    lhs_s = jnp.concatenate([tz[:, :, :n], rt - rz[:, :, :n]], axis=1)
    z0 = tz[:, :, n:]
    y0 = av[:, C:] - rz[:, :, n:]
    plast = p[:, C - 1:C, :]
    upd = jnp.concatenate([kt * plast, -(bt * plast)], axis=1)

    S = s_scr[...]
    ys = []
    for c in range(nc):
        sl = slice(c * nh, (c + 1) * nh)
        xs = _bdot(lhs_s[sl], S, _BNT)
        z = xs[:, :C] + z0[sl]
        ys.append(xs[:, C:] + y0[sl])
        S = S * plast[sl] + _bdot(jnp.concatenate([v[sl], z], axis=1), upd[sl], _BTN)
    s_scr[...] = S

    outs = []
    for h in range(nh):
        cols = slice(h * n, (h + 1) * n)
        y = jnp.concatenate([ys[c][h] for c in range(nc)], axis=0)
        outs.append(_rwkv_head_out(y, r_ref[0, :, cols], kf_ref[0, :, cols], v_ref[0, :, cols],
                                   g_ref[0, :, cols], rk_ref[:, cols], gw_ref[:, cols], gb_ref[:, cols]))
    ya_ref[0] = jnp.concatenate(outs, axis=1).astype(ya_ref.dtype)

    @pl.when(t == pl.num_programs(2) - 1)
    def _():
        st_ref[0] = S


def _wkv_prompt(vals, prm, i, *, tt, heads_per_step):
    B, T, _ = vals[0].shape
    wblk = heads_per_step * A_HEAD_DIM
    seq_spec = pl.BlockSpec((1, tt, wblk), lambda b, h, t: (b, t, h))
    par_spec = pl.BlockSpec((1, wblk), lambda b, h, t: (0, h))
    return pl.pallas_call(
        _wkv_prompt_body,
        grid=(B, A_HEADS // heads_per_step, T // tt),
        in_specs=[seq_spec] * 7 + [par_spec] * 3,
        out_specs=[seq_spec,
                   pl.BlockSpec((1, heads_per_step, A_HEAD_DIM, A_HEAD_DIM), lambda b, h, t: (b, h, 0, 0))],
        out_shape=[jax.ShapeDtypeStruct((B, T, A_WIDTH), BF16),
                   jax.ShapeDtypeStruct((B, A_HEADS, A_HEAD_DIM, A_HEAD_DIM), F32)],
        scratch_shapes=[pltpu.VMEM((heads_per_step, A_HEAD_DIM, A_HEAD_DIM), F32)],
        compiler_params=_cparams(3),
        name="wkv_prompt",
    )(*vals, _row(prm['rwkv_r_k'][i]), _row(prm['rwkv_gn_w'][i]), _row(prm['rwkv_gn_b'][i]))


def _lru_prompt_body(gate_ref, xb_ref, cw_ref, cb_ref, wa_ref, wx_ref, ba_ref, bx_ref, lam_ref,
                     y_ref, h_ref, cst_ref, buf, h_scr):
    t = pl.program_id(1)
    tt = xb_ref.shape[1]
    w = xb_ref.shape[2]

    @pl.when(t == 0)
    def _():
        buf[0:HIST, :] = jnp.zeros((HIST, w), F32)
        h_scr[...] = jnp.zeros(h_scr.shape, F32)

    buf[HIST:, :] = xb_ref[0]
    cw = cw_ref[0]
    xc = cb_ref[...] + buf[HIST:, :] * cw[3:4]
    for j in range(3):
        xc = xc + buf[HIST - 3 + j:HIST - 3 + j + tt, :] * cw[j:j + 1]
    a, b, gact = _lru_rows(gate_ref[0], xc, wa_ref, wx_ref, ba_ref[...], bx_ref[...], lam_ref[...])
    row = _iota2((tt, w), 0)
    d = 1
    while d < tt:
        a_sh = jnp.where(row >= d, pltpu.roll(a, d, axis=0), 1.0)
        b_sh = jnp.where(row >= d, pltpu.roll(b, d, axis=0), 0.0)
        b = a * b_sh + b
        a = a * a_sh
        d *= 2
    h = a * h_scr[0:1, :] + b
    y_ref[0] = (h * gact).astype(y_ref.dtype)
    h_last = h[tt - 1:tt, :]
    h_scr[0:1, :] = h_last
    h_ref[0] = h_last
    cst_ref[0] = buf[HIST + tt - 3:HIST + tt, :]
    buf[0:HIST, :] = buf[tt:tt + HIST, :]


def _block_diag_pairs(w):
    nb, n, _ = w.shape
    w = w.reshape(nb // 2, 2, n, n)
    z = jnp.zeros((nb // 2, n, n), w.dtype)
    top = jnp.concatenate([w[:, 0], z], axis=2)
    bot = jnp.concatenate([z, w[:, 1]], axis=2)
    return jnp.concatenate([top, bot], axis=1)


def _lru_prompt(pb, prm, i, *, tt):
    B, T, _ = pb.shape
    w = B_WIDTH
    ng = w // LANES
    full = lambda shape: pl.BlockSpec(shape, lambda b, t: (0,) * len(shape))
    return pl.pallas_call(
        _lru_prompt_body,
        grid=(B, T // tt),
        in_specs=[pl.BlockSpec((1, tt, w), lambda b, t: (b, t, 0)),
                  pl.BlockSpec((1, tt, w), lambda b, t: (b, t, 1)),
                  pl.BlockSpec((1, 4, w), lambda b, t: (i, 0, 0)),
                  full((1, w)), full((ng, LANES, LANES)), full((ng, LANES, LANES)),
                  full((1, w)), full((1, w)), full((1, w))],
        out_specs=[pl.BlockSpec((1, tt, w), lambda b, t: (b, t, 0)),
                   pl.BlockSpec((1, 1, w), lambda b, t: (b, 0, 0)),
                   pl.BlockSpec((1, 3, w), lambda b, t: (b, 0, 0))],
        out_shape=[jax.ShapeDtypeStruct((B, T, w), BF16),
                   jax.ShapeDtypeStruct((B, 1, w), F32),
                   jax.ShapeDtypeStruct((B, 3, w), F32)],
        scratch_shapes=[pltpu.VMEM((HIST + tt, w), F32), pltpu.VMEM((SUBLANES, w), F32)],
        compiler_params=_cparams(2),
        name="lru_prompt",
    )(pb, pb, prm['lru_conv_w'], _row(prm['lru_conv_b'][i]), _block_diag_pairs(prm['lru_wa'][i]),
      _block_diag_pairs(prm['lru_wx'][i]), _row(prm['lru_ba'][i]), _row(prm['lru_bx'][i]),
      _row(prm['lru_lambda'][i]))


def _in_c_conv_prompt_body(h_ref, w_ref, cw_ref, o_ref, st_ref, w_bf, buf):
    j = pl.program_id(0)
    t = pl.program_id(2)
    tm = h_ref.shape[1]
    tn = w_ref.shape[1]

    @pl.when((pl.program_id(1) == 0) & (t == 0))
    def _():
        w_bf[...] = w_ref[0].astype(BF16)

    @pl.when(t == 0)
    def _():
        buf[0:HIST, :] = jnp.zeros((HIST, tn), F32)

    buf[HIST:, :] = lax.dot_general(h_ref[0], w_bf[...], _NT, preferred_element_type=F32)
    cw = cw_ref[0]
    y = buf[HIST:, :] * cw[3:4]
    for jj in range(3):
        y = y + buf[HIST - 3 + jj:HIST - 3 + jj + tm, :] * cw[jj:jj + 1]
    n_qk_blocks = 2 * C_KEY_DIM // tn
    q_scale = jnp.where(j < n_qk_blocks // 2, C_HEAD ** -0.5, 1.0).astype(F32)
    o_ref[0] = _gdn_act(y, j < n_qk_blocks, q_scale)
    st_ref[0] = buf[HIST + tm - 3:HIST + tm, :]
    buf[0:HIST, :] = buf[tm:tm + HIST, :]


def _in_c_conv_prompt(h, w_t, conv_w, i, *, tm, tn):
    B, T, d = h.shape
    return pl.pallas_call(
        _in_c_conv_prompt_body,
        grid=(C_QKV_DIM // tn, B, T // tm),
        in_specs=[pl.BlockSpec((1, tm, d), lambda j, b, t: (b, t, 0)),
                  pl.BlockSpec((pl.Element(1), pl.Element(tn), pl.Element(d)),
                               lambda j, b, t: (i, pl.multiple_of(j * tn, SUBLANES), 0)),
                  pl.BlockSpec((1, 4, tn), lambda j, b, t: (i, 0, j))],
        out_specs=[pl.BlockSpec((1, tm, tn), lambda j, b, t: (b, t, j)),
                   pl.BlockSpec((1, 3, tn), lambda j, b, t: (b, 0, j))],
        out_shape=[jax.ShapeDtypeStruct((B, T, C_QKV_DIM), F32),
                   jax.ShapeDtypeStruct((B, 3, C_QKV_DIM), F32)],
        scratch_shapes=[pltpu.VMEM((tn, d), BF16), pltpu.VMEM((HIST + tm, tn), F32)],
        compiler_params=_cparams(3),
        name="in_c_conv_prompt",
    )(h, w_t, conv_w)


def _gdn_gates(ba, a_log, dt_bias):
    hv = C_V_HEADS
    beta = jax.nn.sigmoid(ba[..., :hv])
    g = -jnp.exp(a_log) * jax.nn.softplus(ba[..., hv:2 * hv] + dt_bias)
    return beta, g


def _lane_bcast_col(x, col):
    n = x.shape[1]
    onehot = _iota2((n, LANES), 0) == col
    return _sel_r(x, onehot)


def _gdn_prompt_body(q_ref, k_ref, v_ref, z_ref, ba_ref, alog_ref, dtb_ref, nw_ref, o_ref, st_ref, s_scr):
    hblk = pl.program_id(1)
    t = pl.program_id(2)
    tt = q_ref.shape[1]
    C = CHUNK
    nc = tt // C
    rep = C_V_HEADS // C_K_HEADS
    n_kh = q_ref.shape[2] // C_HEAD
    nh = n_kh * rep

    @pl.when(t == 0)
    def _():
        s_scr[...] = jnp.zeros(s_scr.shape, F32)

    beta_all, g_all = _gdn_gates(ba_ref[0], alog_ref[...], dtb_ref[...])
    gc_all = _sel_l(_block_tri(tt, C), g_all)
    blocks = lambda x: x.reshape(nc, C, x.shape[-1])
    qs, ks, vs, betas, gcs = [], [], [], [], []
    for kh in range(n_kh):
        q_h = blocks(q_ref[0, :, kh * C_HEAD:(kh + 1) * C_HEAD])
        k_h = blocks(k_ref[0, :, kh * C_HEAD:(kh + 1) * C_HEAD])
        for jv in range(rep):
            hu = kh * rep + jv
            hv = hblk * nh + hu
            qs.append(q_h)
            ks.append(k_h)
            vs.append(blocks(v_ref[0, :, hu * C_HEAD:(hu + 1) * C_HEAD]))
            betas.append(blocks(_lane_bcast_col(beta_all, hv)))
            gcs.append(blocks(_lane_bcast_col(gc_all, hv)))
    stack = lambda xs: jnp.stack(xs, axis=1).reshape(nc * nh, C, xs[0].shape[-1])
    q, k, v, beta, gc = map(stack, (qs, ks, vs, betas, gcs))
    incl = _tri(C, False)
    strict = _tri(C, True)

    gcc = gc[:, :, :C]
    diff = gcc - jnp.swapaxes(gcc, 1, 2)
    dec_incl = jnp.exp(jnp.where(incl, diff, -jnp.inf))
    dec_strict = jnp.where(strict, dec_incl, 0.0)
    kb = k * beta
    gram = _bdot(jnp.concatenate([kb, q], axis=1), k, _BNT)
    low = gram[:, :C] * dec_strict
    qk = gram[:, C:] * dec_incl
    tinv = _unit_lower_inverse_refined(low)
    eg = jnp.exp(gc)
    sol = _bdot(tinv, jnp.concatenate([v * beta, kb * eg], axis=2), _BNN)
    u = sol[:, :, :C_HEAD]
    lhs_s = jnp.concatenate([sol[:, :, C_HEAD:], q * eg], axis=1)
    g_last = gc[:, C - 1:C, :]
    k_tail = k * jnp.exp(g_last - gc)
    e_last = jnp.exp(g_last)

    S = s_scr[...]
    os_ = []
    for c in range(nc):
        sl = slice(c * nh, (c + 1) * nh)
        ws = _bdot(lhs_s[sl], S, _BNN)
        v_new = u[sl] - ws[:, :C]
        os_.append(ws[:, C:] + _bdot(qk[sl], v_new, _BNN))
        S = S * e_last[sl] + _bdot(k_tail[sl], v_new, _BTN)
    s_scr[...] = S

    outs = []
    for hu in range(nh):
        cols = slice(hu * C_HEAD, (hu + 1) * C_HEAD)
        o = jnp.concatenate([os_[c][hu] for c in range(nc)], axis=0)
        outs.append(_gdn_head_out(o, z_ref[0, :, cols], nw_ref[...]))
    o_ref[0] = jnp.concatenate(outs, axis=1).astype(o_ref.dtype)

    @pl.when(t == pl.num_programs(2) - 1)
    def _():
        st_ref[0] = S


def _gdn_prompt(qkv_act, z, ba, prm, i, *, tt, kheads_per_step):
    B, T, _ = qkv_act.shape
    rep = C_V_HEADS // C_K_HEADS
    kw = kheads_per_step * C_HEAD
    vw = rep * kw
    k_blk0 = C_KEY_DIM // kw
    v_blk0 = 2 * C_KEY_DIM // vw
    z_blk0 = 0
    nh = kheads_per_step * rep
    full = lambda shape: pl.BlockSpec(shape, lambda b, h, t: (0,) * len(shape))
    return pl.pallas_call(
        _gdn_prompt_body,
        grid=(B, C_K_HEADS // kheads_per_step, T // tt),
        in_specs=[pl.BlockSpec((1, tt, kw), lambda b, h, t: (b, t, h)),
                  pl.BlockSpec((1, tt, kw), lambda b, h, t: (b, t, k_blk0 + h)),
                  pl.BlockSpec((1, tt, vw), lambda b, h, t: (b, t, v_blk0 + h)),
                  pl.BlockSpec((1, tt, vw), lambda b, h, t: (b, t, z_blk0 + h)),
                  pl.BlockSpec((1, tt, 2 * C_V_HEADS), lambda b, h, t: (b, t, 0)),
                  full((1, C_V_HEADS)), full((1, C_V_HEADS)), full((1, C_HEAD))],
        out_specs=[pl.BlockSpec((1, tt, vw), lambda b, h, t: (b, t, h)),
                   pl.BlockSpec((1, nh, C_HEAD, C_HEAD), lambda b, h, t: (b, h, 0, 0))],
        out_shape=[jax.ShapeDtypeStruct((B, T, C_VAL_DIM), BF16),
                   jax.ShapeDtypeStruct((B, C_V_HEADS, C_HEAD, C_HEAD), F32)],
        scratch_shapes=[pltpu.VMEM((nh, C_HEAD, C_HEAD), F32)],
        compiler_params=_cparams(3),
        name="gdn_prompt",
    )(qkv_act, qkv_act, qkv_act, z, ba, _row(prm['gdn_a_log'][i]), _row(prm['gdn_dt_bias'][i]),
      _row(prm['gdn_norm_w'][i]))


def _rwkv_prep_sample_body(pm_ref, pl_ref, qm_ref, ql_ref, mum_ref, mul_ref, w0_ref, w2_ref, a0_ref, a2_ref,
                           g2_ref, kk_ref, ka_ref, r_o, kf_o, v_o, kkn_o, a_o, lw_o, g_o):
    outs = _rwkv_rows(pm_ref[...], qm_ref[...], pl_ref[...][:, :LORA_ALL], ql_ref[...], mum_ref[...],
                      mul_ref[...], w0_ref[...], w2_ref[0], a0_ref[...], a2_ref[0], g2_ref[0],
                      kk_ref[...], ka_ref[...])
    for o_ref, val in zip((r_o, kf_o, v_o, kkn_o, a_o, lw_o, g_o), outs):
        o_ref[...] = val.T


def _rwkv_prep_sample(p_main, p_lora, shift, prm, i):
    bs = p_main.shape[0]
    mu = prm['rwkv_mu'][i]
    full = lambda shape: pl.BlockSpec(shape, lambda s: (0,) * len(shape))
    lay3 = lambda shape: pl.BlockSpec((1,) + shape, lambda s: (i, 0, 0))
    return pl.pallas_call(
        _rwkv_prep_sample_body,
        grid=(1,),
        in_specs=[full(p_main.shape), full(p_lora.shape), full((bs, P_A_MAIN)), full((bs, LORA_ALL)),
                  full((1, P_A_MAIN)), full((1, LORA_ALL)), full((1, A_WIDTH)),
                  lay3((LORA_W, A_WIDTH)), full((1, A_WIDTH)), lay3((LORA_A, A_WIDTH)),
                  lay3((LORA_G, A_WIDTH)), full((1, A_WIDTH)), full((1, A_WIDTH))],
        out_specs=[full((A_WIDTH, bs))] * 7,
        out_shape=[jax.ShapeDtypeStruct((A_WIDTH, bs), F32)] * 7,
        compiler_params=_cparams(1),
        name="rwkv_prep_sample",
    )(p_main, p_lora, shift[:, :P_A_MAIN], shift[:, P_A_MAIN:], _row(mu[:P_A_MAIN]), _row(mu[P_A_MAIN:]),
      _row(prm['rwkv_w0'][i]), prm['rwkv_w2'], _row(prm['rwkv_a0'][i]), prm['rwkv_a2'], prm['rwkv_g2'],
      _row(prm['rwkv_k_k'][i]), _row(prm['rwkv_k_a'][i]))


STEP_GROUP = 16


def _wkv_step_body(*refs, n_state):
    s_ref = refs[0]
    r_ref, kf_ref, v_ref, kkn_ref, a_ref, lw_ref, g_ref, rk_ref, gw_ref, gb_ref, ns_ref, y_ref = refs[n_state:]
    n = A_HEAD_DIM
    outs = []
    for h in range(s_ref.shape[0]):
        rows = slice(h * n, (h + 1) * n)
        ld = lambda ref: ref[rows, :]
        r, kf, v, kkn, a = ld(r_ref), ld(kf_ref), ld(v_ref), ld(kkn_ref), ld(a_ref)
        kk = kkn * lax.rsqrt(jnp.sum(kkn * kkn, axis=0, keepdims=True) + 1e-12)
        S = s_ref[h]
        u = jnp.sum(S * kk[None], axis=1)
        s_new = S * jnp.exp(ld(lw_ref))[None] - u[:, None, :] * (kk * a)[None] + v[:, None, :] * kf[None]
        ns_ref[h] = s_new
        y = jnp.sum(s_new * r[None], axis=1)
        mean = jnp.mean(y, axis=0, keepdims=True)
        var = jnp.mean(jnp.square(y - mean), axis=0, keepdims=True)
        yn = (y - mean) * lax.rsqrt(var + A_GN_EPS) * ld(gw_ref) + ld(gb_ref)
        bonus = jnp.sum(r * kf * ld(rk_ref), axis=0, keepdims=True) * v
        outs.append(((yn + bonus) * ld(g_ref)).T)
    y_ref[...] = jnp.concatenate(outs, axis=1).astype(y_ref.dtype)


def _layer_state_io(state_all, new_all, spec, out_index=0):
    if new_all is None:
        return [spec], [state_all], {}
    return [spec, pl.BlockSpec(memory_space=pl.ANY)], [state_all, new_all], {1: out_index}


def _wkv_step(S_all, new_all, vals, prm, i, *, heads_per_step):
    _, n_heads, n, _, bs = S_all.shape
    wblk = heads_per_step * n
    lanes = lambda p: jnp.broadcast_to(p.reshape(A_WIDTH, 1), (A_WIDTH, bs))
    row_spec = pl.BlockSpec((wblk, bs), lambda h: (h, 0))
    s_spec = pl.BlockSpec((None, heads_per_step, n, n, bs), lambda h: (i, h, 0, 0, 0))
    st_specs, st_args, aliases = _layer_state_io(S_all, new_all, s_spec)
    return pl.pallas_call(
        functools.partial(_wkv_step_body, n_state=len(st_args)),
        grid=(n_heads // heads_per_step,),
        in_specs=st_specs + [row_spec] * 10,
        out_specs=[s_spec, pl.BlockSpec((bs, wblk), lambda h: (0, h))],
        out_shape=[jax.ShapeDtypeStruct(S_all.shape, F32), jax.ShapeDtypeStruct((bs, A_WIDTH), BF16)],
        input_output_aliases=aliases,
        compiler_params=_cparams(1),
        name="wkv_step",
    )(*st_args, *vals, lanes(prm['rwkv_r_k'][i]), lanes(prm['rwkv_gn_w'][i]), lanes(prm['rwkv_gn_b'][i]))


def _lru_sample_body(gate_ref, xb_ref, cst_ref, h0_ref, cw_ref, cb_ref, wa_ref, wx_ref, ba_ref, bx_ref, lam_ref,
                     y_ref, h_ref, ncst_ref):
    cw = cw_ref[0]
    xb = xb_ref[...]
    xc = cb_ref[...] + xb * cw[3:4]
    for j in range(3):
        xc = xc + cst_ref[:, j, :] * cw[j:j + 1]
    a, b, gact = _lru_rows(gate_ref[...], xc, wa_ref, wx_ref, ba_ref[...], bx_ref[...], lam_ref[...])
    h = a * h0_ref[...] + b
    y_ref[...] = (h * gact).astype(y_ref.dtype)
    h_ref[...] = h
    ncst_ref[:, 0, :] = cst_ref[:, 1, :]
    ncst_ref[:, 1, :] = cst_ref[:, 2, :]
    ncst_ref[:, 2, :] = xb


def _lru_sample(pb, conv_state, h0, prm, i):
    bs = pb.shape[0]
    w = B_WIDTH
    ng = w // LANES
    full = lambda shape: pl.BlockSpec(shape, lambda s: (0,) * len(shape))
    return pl.pallas_call(
        _lru_sample_body,
        grid=(1,),
        in_specs=[pl.BlockSpec((bs, w), lambda s: (0, 0)), pl.BlockSpec((bs, w), lambda s: (0, 1)),
                  full((bs, 3, w)), full((bs, w)),
                  pl.BlockSpec((1, 4, w), lambda s: (i, 0, 0)),
                  full((1, w)), full((ng, LANES, LANES)), full((ng, LANES, LANES)),
                  full((1, w)), full((1, w)), full((1, w))],
        out_specs=[full((bs, w)), full((bs, w)), full((bs, 3, w))],
        out_shape=[jax.ShapeDtypeStruct((bs, w), BF16), jax.ShapeDtypeStruct((bs, w), F32),
                   jax.ShapeDtypeStruct((bs, 3, w), F32)],
        compiler_params=_cparams(1),
        name="lru_sample",
    )(pb, pb, conv_state, h0, prm['lru_conv_w'], _row(prm['lru_conv_b'][i]),
      _block_diag_pairs(prm['lru_wa'][i]), _block_diag_pairs(prm['lru_wx'][i]), _row(prm['lru_ba'][i]),
      _row(prm['lru_bx'][i]), _row(prm['lru_lambda'][i]))


def _gdn_prep_sample_body(*refs, n_state):
    cst_ref = refs[0]
    x_ref, cw_ref, o_ref, ncst_ref = refs[n_state:]
    j = pl.program_id(0)
    cw = cw_ref[0]
    x = x_ref[...]
    y = x * cw[3:4]
    for jj in range(3):
        y = y + cst_ref[jj] * cw[jj:jj + 1]
    n_qk_blocks = 2 * C_KEY_DIM // x.shape[1]
    q_scale = jnp.where(j < n_qk_blocks // 2, C_HEAD ** -0.5, 1.0).astype(F32)
    o_ref[...] = _gdn_act(y, j < n_qk_blocks, q_scale)
    ncst_ref[0] = cst_ref[1]
    ncst_ref[1] = cst_ref[2]
    ncst_ref[2] = x


def _gdn_prep_sample(qkvz, conv_all, new_all, conv_w, i, *, tc):
    bs = qkvz.shape[0]
    s_spec = pl.BlockSpec((None, 3, bs, tc), lambda j: (i, 0, 0, j))
    st_specs, st_args, aliases = _layer_state_io(conv_all, new_all, s_spec, out_index=1)
    return pl.pallas_call(
        functools.partial(_gdn_prep_sample_body, n_state=len(st_args)),
        grid=(C_QKV_DIM // tc,),
        in_specs=st_specs + [pl.BlockSpec((bs, tc), lambda j: (0, j)),
                             pl.BlockSpec((1, 4, tc), lambda j: (i, 0, j))],
        out_specs=[pl.BlockSpec((bs, tc), lambda j: (0, j)), s_spec],
        out_shape=[jax.ShapeDtypeStruct((bs, C_QKV_DIM), F32),
                   jax.ShapeDtypeStruct(conv_all.shape, F32)],
        input_output_aliases=aliases,
        compiler_params=_cparams(1),
        name="gdn_prep_sample",
    )(*st_args, qkvz, conv_w)


def _gdn_step_body(*refs, n_state):
    s_ref = refs[0]
    q_ref, k_ref, v_ref, z_ref, ba_ref, alog_ref, dtb_ref, nw_ref, ns_ref, o_ref = refs[n_state:]
    bb = s_ref.shape[0]
    rep = C_V_HEADS // C_K_HEADS
    hk = pl.program_id(1)
    beta_all, g_all = _gdn_gates(ba_ref[...], alog_ref[...], dtb_ref[...])
    lane = lax.broadcasted_iota(jnp.int32, beta_all.shape, 2)
    pick = lambda x, col: jnp.sum(jnp.where(lane == col, x, 0.0), axis=-1, keepdims=True)
    rid = lax.broadcasted_iota(jnp.int32, (bb, SUBLANES, C_HEAD), 1)
    up = lambda x: jnp.broadcast_to(x, (bb, SUBLANES, C_HEAD))
    q = q_ref[...]
    k = k_ref[...]
    qk = jnp.sum(q * k, axis=-1, keepdims=True)
    for jv in range(rep):
        cols = slice(jv * C_HEAD, (jv + 1) * C_HEAD)
        hv = hk * rep + jv
        beta = pick(beta_all, hv)
        eg = jnp.exp(pick(g_all, hv))
        v = v_ref[:, :, cols]
        S = s_ref[:, jv]
        kb = k * beta
        lhs = jnp.where(rid == 0, up(kb * eg), jnp.where(rid == 1, up(q * eg), 0.0))
        ws = _bdot(lhs, S, _BNN)
        v_new = v * beta - ws[:, 0:1]
        o = ws[:, 1:2] + qk * v_new
        left = jnp.where(rid == 0, up(k), 0.0)
        right = jnp.where(rid == 0, up(v_new), 0.0)
        ns_ref[:, jv] = S * eg + _bdot(left, right, _BTN)
        o_ref[:, :, cols] = _gdn_head_out(o, z_ref[:, :, cols], nw_ref[...]).astype(o_ref.dtype)


def _gdn_step(S_all, new_all, qkv_act, qkvz, ba, prm, i, *, bb):
    bs = S_all.shape[1]
    rep = C_V_HEADS // C_K_HEADS
    vw = rep * C_HEAD
    v_blk0 = 2 * C_KEY_DIM // vw
    z_blk0 = C_QKV_DIM // vw
    full = lambda shape: pl.BlockSpec(shape, lambda b, h: (0,) * len(shape))
    s_spec = pl.BlockSpec((None, bb, rep, C_HEAD, C_HEAD), lambda b, h: (i, b, h, 0, 0))
    st_specs, st_args, aliases = _layer_state_io(S_all, new_all, s_spec)
    return pl.pallas_call(
        functools.partial(_gdn_step_body, n_state=len(st_args)),
        grid=(bs // bb, C_K_HEADS),
        in_specs=st_specs + [
                  pl.BlockSpec((bb, 1, C_HEAD), lambda b, h: (b, 0, h)),
                  pl.BlockSpec((bb, 1, C_HEAD), lambda b, h: (b, 0, C_K_HEADS + h)),
                  pl.BlockSpec((bb, 1, vw), lambda b, h: (b, 0, v_blk0 + h)),
                  pl.BlockSpec((bb, 1, vw), lambda b, h: (b, 0, z_blk0 + h)),
                  pl.BlockSpec((bb, 1, 2 * C_V_HEADS), lambda b, h: (b, 0, 0)),
                  full((1, C_V_HEADS)), full((1, C_V_HEADS)), full((1, C_HEAD))],
        out_specs=[s_spec, pl.BlockSpec((bb, 1, vw), lambda b, h: (b, 0, h))],
        out_shape=[jax.ShapeDtypeStruct(S_all.shape, F32), jax.ShapeDtypeStruct((bs, 1, C_VAL_DIM), BF16)],
        input_output_aliases=aliases,
        compiler_params=_cparams(2),
        name="gdn_step",
    )(*st_args, qkv_act, qkv_act, qkv_act, qkvz, ba, _row(prm['gdn_a_log'][i]), _row(prm['gdn_dt_bias'][i]),
      _row(prm['gdn_norm_w'][i]))


MOD_SH1, MOD_SC1, MOD_GT1, MOD_SH2, MOD_SC2, MOD_GT2 = range(6)


def _in_proj_ab(h, prm, i):
    w = [(prm['w_in_ab_t'], i, 0, True)]
    p_main = _mm([h], w, n_out=P_A_MAIN, name="in_ab_main")
    p_lora = _mm([h], w, n_out=LORA_ALL, col0=P_A_MAIN, tn=LORA_ALL, name="in_ab_lora")
    pb = _mm([h], w, n_out=2 * B_WIDTH, col0=P_A, name="in_ab_lru")
    return p_main, p_lora, pb


def _in_proj_c_gates(h, prm, i):
    return _mm([h], [(prm['w_in_c_t'], i, 0, True)], n_out=2 * C_V_HEADS, col0=C_QKV_DIM + C_VAL_DIM,
               tn=2 * C_V_HEADS, name="in_c_gates")


def _ffn_down(act, prm, layer, x, mod):
    return _mm([act], [(prm['ffn_w_down'], layer, 0, False)], n_out=D_MODEL, res=x, gate=mod, gate_blk=MOD_GT2,
               name="ffn_down")


def _out_proj_ab(ya, yb, prm, i, x, mod):
    w = prm['w_out_ab']
    return _mm([ya, yb], [(w, i, 0, False), (w, i, 1, False)], n_out=D_MODEL, res=x, gate=mod, gate_blk=MOD_GT1,
               name="out_ab")


def _out_proj_c(yc, prm, i, x, mod):
    return _mm([yc], [(prm['w_out_c'], i, 0, False)], n_out=D_MODEL, res=x, gate=mod, gate_blk=MOD_GT1,
               name="out_c")


def _trunk_prompt(x, mods, prm):
    B, T, _ = x.shape
    tm, tt, t_norm = min(1024, T), min(256, T), min(512, T)
    wkv, shift, lru_h, lru_conv, gdn, gdn_conv, ffn_conv = [], [], [], [], [], [], []
    for layer in range(DEPTH):
        mod = mods[layer].reshape(B, 1, 6 * D_MODEL)
        i = layer // 2
        h = _norm(x, prm['norm_mix'][layer], mod, MOD_SC1, MOD_SH1, tt=t_norm)
        if layer % 2 == 0:
            p_main, p_lora, pb = _in_proj_ab(h, prm, i)
            vals = _rwkv_prep_prompt(p_main, p_lora, prm, i, tt=tt)
            ya, s_new = _wkv_prompt(vals, prm, i, tt=tt, heads_per_step=A_HEADS)
            yb, h_last, cst = _lru_prompt(pb, prm, i, tt=tt)
            x = _out_proj_ab(ya, yb, prm, i, x, mod)
            wkv.append(s_new)
            shift.append(jnp.concatenate([p_main[:, -1], p_lora[:, -1, :LORA_ALL]], axis=-1))
            lru_h.append(h_last[:, 0])
            lru_conv.append(cst)
        else:
            qkv_act, cst = _in_c_conv_prompt(h, prm['w_in_c_t'], prm['gdn_conv_w'], i, tm=tm, tn=512)
            z = _mm([h], [(prm['w_in_c_t'], i, 0, True)], n_out=C_VAL_DIM, col0=C_QKV_DIM, name="in_c_z")
            ba = _in_proj_c_gates(h, prm, i)
            yc, s_new = _gdn_prompt(qkv_act, z, ba, prm, i, tt=min(2 * CHUNK, T), kheads_per_step=C_K_HEADS // 2)
            x = _out_proj_c(yc, prm, i, x, mod)
            gdn.append(s_new)
            gdn_conv.append(cst)
        h = _norm(x, prm['norm_ffn'][layer], mod, MOD_SC2, MOD_SH2, tt=t_norm)
        act, fst = _ffn_up_prompt(h, prm['ffn_w_gate'], prm['ffn_w_up'], prm['ffn_conv_w'], prm['ffn_conv_b'],
                                  layer, tm=tm, tn=512)
        x = _ffn_down(act, prm, layer, x, mod)
        ffn_conv.append(fst)
    y = _norm(x, prm['norm_out'], tt=t_norm, out_dtype=F32)
    stk = jnp.stack
    return y, stk(wkv), stk(shift), stk(lru_h), stk(lru_conv), stk(gdn), stk(gdn_conv), stk(ffn_conv)


def _trunk_sample(x, mods, st, prm):
    bs = x.shape[0]
    x = x.reshape(1, bs, D_MODEL)
    wkv0, shift0, lru_h0, lru_conv0, gdn0, gdn_conv0, ffn_conv0 = st
    wkv_new = gdn_new = gconv_new = fconv_new = None
    gconv0_t = jnp.transpose(gdn_conv0, (0, 2, 1, 3))
    wkv0_t = jnp.transpose(wkv0, (0, 2, 3, 4, 1))
    shift, lru_h, lru_conv = [], [], []
    for layer in range(DEPTH):
        mod = mods[layer].reshape(1, bs, 6 * D_MODEL)
        i = layer // 2
        h = _norm(x, prm['norm_mix'][layer], mod, MOD_SC1, MOD_SH1, tt=bs)
        if layer % 2 == 0:
            p_main, p_lora, pb = _in_proj_ab(h, prm, i)
            p_main, p_lora, pb = p_main[0], p_lora[0], pb[0]
            vals = _rwkv_prep_sample(p_main, p_lora, shift0[i], prm, i)
            wkv_new, ya = _wkv_step(wkv0_t, wkv_new, vals, prm, i, heads_per_step=2)
            yb, h_last, cst = _lru_sample(pb, lru_conv0[i], lru_h0[i], prm, i)
            x = _out_proj_ab(ya[None], yb[None], prm, i, x, mod)
            shift.append(jnp.concatenate([p_main, p_lora[:, :LORA_ALL]], axis=-1))
            lru_h.append(h_last)
            lru_conv.append(cst)
        else:
            qkvz = _mm([h], [(prm['w_in_c_t'], i, 0, True)], n_out=C_QKV_DIM + C_VAL_DIM, name="in_c_main")
            ba = _in_proj_c_gates(h, prm, i)
            qkv_act, gconv_new = _gdn_prep_sample(qkvz[0], gconv0_t, gconv_new, prm['gdn_conv_w'], i, tc=2048)
            gdn_new, yc = _gdn_step(gdn0, gdn_new, qkv_act[:, None], qkvz[0][:, None], ba[0][:, None], prm, i,
                                    bb=min(STEP_GROUP, bs))
            x = _out_proj_c(yc.reshape(1, bs, C_VAL_DIM), prm, i, x, mod)
        h = _norm(x, prm['norm_ffn'][layer], mod, MOD_SC2, MOD_SH2, tt=bs)
        act, fconv_new = _ffn_up_sample(h[0], prm['ffn_w_gate'], prm['ffn_w_up'], prm['ffn_conv_w'],
                                        prm['ffn_conv_b'], ffn_conv0, fconv_new, layer, tn=512)
        x = _ffn_down(act[None], prm, layer, x, mod)
    y = _norm(x, prm['norm_out'], tt=bs, out_dtype=F32).reshape(bs, 1, D_MODEL)
    stk = jnp.stack
    return (y, jnp.transpose(wkv_new, (0, 4, 1, 2, 3)), stk(shift), stk(lru_h), stk(lru_conv), gdn_new,
            jnp.transpose(gconv_new, (0, 2, 1, 3)), fconv_new)


def kernel(x_prompt, x_sample, c_prompt, c_sample, state_rwkv_wkv, state_rwkv_shift, state_lru_h, state_lru_conv, state_gdn, state_gdn_conv, state_ffn_conv, w_mod, b_mod, norm_mix, norm_ffn, norm_out, w_in_ab, w_out_ab, rwkv_mu, rwkv_w0, rwkv_w2, rwkv_a0, rwkv_a2, rwkv_g2, rwkv_k_k, rwkv_k_a, rwkv_r_k, rwkv_gn_w, rwkv_gn_b, lru_conv_w, lru_conv_b, lru_wa, lru_ba, lru_wx, lru_bx, lru_lambda, w_in_c, w_out_c, gdn_conv_w, gdn_a_log, gdn_dt_bias, gdn_norm_w, ffn_w_gate, ffn_w_up, ffn_conv_w, ffn_conv_b, ffn_w_down):
    prm = dict(norm_mix=norm_mix, norm_ffn=norm_ffn, norm_out=norm_out, w_out_ab=w_out_ab,
               w_in_ab_t=jnp.swapaxes(w_in_ab, 1, 2), w_in_c_t=jnp.swapaxes(w_in_c, 1, 2),
               rwkv_mu=rwkv_mu, rwkv_w0=rwkv_w0, rwkv_w2=rwkv_w2, rwkv_a0=rwkv_a0, rwkv_a2=rwkv_a2,
               rwkv_g2=rwkv_g2, rwkv_k_k=rwkv_k_k, rwkv_k_a=rwkv_k_a,
               rwkv_r_k=rwkv_r_k.reshape(rwkv_r_k.shape[0], A_WIDTH), rwkv_gn_w=rwkv_gn_w,
               rwkv_gn_b=rwkv_gn_b, lru_conv_w=lru_conv_w, lru_conv_b=lru_conv_b, lru_wa=lru_wa, lru_ba=lru_ba,
               lru_wx=lru_wx, lru_bx=lru_bx, lru_lambda=lru_lambda, w_in_c=w_in_c, w_out_c=w_out_c,
               gdn_conv_w=gdn_conv_w, gdn_a_log=gdn_a_log, gdn_dt_bias=gdn_dt_bias, gdn_norm_w=gdn_norm_w,
               ffn_w_gate=ffn_w_gate, ffn_w_up=ffn_w_up, ffn_conv_w=ffn_conv_w, ffn_conv_b=ffn_conv_b,
               ffn_w_down=ffn_w_down)
    mods_p, mods_s = _modulation(c_prompt, c_sample, w_mod, b_mod)
    outs_p = _trunk_prompt(x_prompt, mods_p, prm)
    outs_s = _trunk_sample(x_sample, mods_s,
                           (state_rwkv_wkv, state_rwkv_shift, state_lru_h, state_lru_conv, state_gdn,
                            state_gdn_conv, state_ffn_conv), prm)
    return (outs_p[0], outs_s[0]) + tuple(outs_p[1:]) + tuple(outs_s[1:])
```

```python
import functools

import jax
import jax.numpy as jnp
from jax import lax
from jax.experimental import pallas as pl
from jax.experimental.pallas import tpu as pltpu

F32 = jnp.float32
BF16 = jnp.bfloat16

D_MODEL = 2048
DEPTH = 4
A_HEAD_DIM = 64
A_WIDTH = D_MODEL // 2
A_HEADS = A_WIDTH // A_HEAD_DIM
LORA_W = 64
LORA_A = 64
LORA_G = 160
LORA_ALL = LORA_W + LORA_A + LORA_G
P_A_MAIN = 3 * A_WIDTH
P_A = P_A_MAIN + LORA_ALL
A_GN_EPS = 64e-5
B_WIDTH = D_MODEL - A_WIDTH
B_BLOCK_DIM = 64
LRU_C = 8.0
C_HEAD = 128
C_K_HEADS = D_MODEL // C_HEAD
C_V_HEADS = 2 * C_K_HEADS
C_KEY_DIM = C_K_HEADS * C_HEAD
C_VAL_DIM = C_V_HEADS * C_HEAD
C_QKV_DIM = 2 * C_KEY_DIM + C_VAL_DIM
D_FF = 5632
NORM_EPS = 1e-6

LANES = 128
SUBLANES = 8
HIST = SUBLANES
CHUNK = 64
VMEM_LIMIT = 56 * 1024 * 1024
MM_VMEM_BUDGET = 46 * 1024 * 1024

_NN = (((1,), (0,)), ((), ()))
_NT = (((1,), (1,)), ((), ()))
_TN = (((0,), (0,)), ((), ()))
_BNN = (((2,), (1,)), ((0,), (0,)))
_BNT = (((2,), (2,)), ((0,), (0,)))
_BTN = (((1,), (1,)), ((0,), (0,)))


def _cparams(n_axes):
    return pltpu.CompilerParams(dimension_semantics=("arbitrary",) * n_axes,
                                vmem_limit_bytes=VMEM_LIMIT)


def _bdot(a, b, dims=_NN):
    return lax.dot_general(a.astype(BF16), b.astype(BF16), dims, preferred_element_type=F32)


def _split2(a):
    hi = a.astype(BF16)
    lo = (a - hi.astype(F32)).astype(BF16)
    return hi, lo


def _dot3(a, b, dims=_NN):
    ah, al = _split2(a)
    bh, bl = _split2(b)
    d = lambda x, y: lax.dot_general(x, y, dims, preferred_element_type=F32)
    return d(ah, bh) + (d(ah, bl) + d(al, bh))


def _split3(a):
    h0 = a.astype(BF16)
    r1 = a - h0.astype(F32)
    h1 = r1.astype(BF16)
    h2 = (r1 - h1.astype(F32)).astype(BF16)
    return h0, h1, h2


def _sel_l(mask01, x, dims=_NN):
    m = mask01.astype(BF16)
    d = lambda y: lax.dot_general(m, y, dims, preferred_element_type=F32)
    h0, h1, h2 = _split3(x)
    return d(h0) + (d(h1) + d(h2))


def _sel_r(x, mask01):
    m = mask01.astype(BF16)
    d = lambda y: lax.dot_general(y, m, _NN, preferred_element_type=F32)
    h0, h1, h2 = _split3(x)
    return d(h0) + (d(h1) + d(h2))


def _iota2(shape, axis):
    return lax.broadcasted_iota(jnp.int32, shape, axis)


def _tri(n, strict):
    i = _iota2((n, n), 0)
    j = _iota2((n, n), 1)
    return (i > j) if strict else (i >= j)


def _unit_lower_inverse(low, mm):
    n = low.shape[-1]
    dims = _BNN if low.ndim == 3 else _NN
    eye = (_iota2((n, n), 0) == _iota2((n, n), 1)).astype(F32)
    m = -low
    inv = eye + m
    p = m
    span = 2
    while span < n:
        p = mm(p, p, dims)
        inv = inv + mm(inv, p, dims)
        span *= 2
    return inv


def _unit_lower_inverse_refined(low):
    n = low.shape[-1]
    dims = _BNN if low.ndim == 3 else _NN
    eye = (_iota2((n, n), 0) == _iota2((n, n), 1)).astype(F32)
    x0 = _unit_lower_inverse(low, _bdot)
    resid = eye - _dot3(eye + low, x0, dims)
    return x0 + _bdot(x0, resid, dims)


def _block_tri(tt, block):
    i = _iota2((tt, tt), 0)
    j = _iota2((tt, tt), 1)
    return (i // block == j // block) & (i >= j)


def _neg_expm1(x):
    return -jnp.tanh(0.5 * x) * (jnp.exp(x) + 1.0)


def _segment_sum(x, seg):
    i = _iota2((LANES, LANES), 0) // seg
    j = _iota2((LANES, LANES), 1) // seg
    same = (i == j)
    parts = [_sel_r(x[:, g * LANES:(g + 1) * LANES], same) for g in range(x.shape[1] // LANES)]
    return parts[0] if len(parts) == 1 else jnp.concatenate(parts, axis=1)


def _rwkv_rows(p_main, q_main, p_lora, q_lora, mu_main, mu_lora, w0, w2, a0, a2, g2, k_k, k_a):
    pm = p_main + (q_main - p_main) * mu_main
    pl_ = p_lora + (q_lora - p_lora) * mu_lora
    r = pm[:, :A_WIDTH]
    k = pm[:, A_WIDTH:2 * A_WIDTH]
    v = pm[:, 2 * A_WIDTH:]
    xw = pl_[:, :LORA_W]
    xa = pl_[:, LORA_W:LORA_W + LORA_A]
    xg = pl_[:, LORA_W + LORA_A:LORA_ALL]
    w_ll = -jax.nn.softplus(-(w0 + _bdot(jnp.tanh(xw), w2))) - 0.5
    lw = -jnp.exp(w_ll)
    a = jax.nn.sigmoid(a0 + _bdot(xa, a2))
    g = _bdot(jax.nn.sigmoid(xg), g2)
    kkn = k * k_k
    kf = k * (1.0 + (a - 1.0) * k_a)
    return r, kf, v, kkn, a, lw, g


def _rwkv_head_out(y, r, kf, v, g, r_k, gn_w, gn_b):
    mean = jnp.mean(y, axis=-1, keepdims=True)
    var = jnp.mean(jnp.square(y - mean), axis=-1, keepdims=True)
    yn = (y - mean) * lax.rsqrt(var + A_GN_EPS) * gn_w + gn_b
    bonus = jnp.sum(r * kf * r_k, axis=-1, keepdims=True) * v
    return (yn + bonus) * g


def _lru_rows(gate, xc, wa_bd, wx_bd, ba, bx, lam):
    ng = xc.shape[1] // LANES
    ra = jnp.concatenate([_bdot(xc[:, g * LANES:(g + 1) * LANES], wa_bd[g]) for g in range(ng)], axis=1)
    rx = jnp.concatenate([_bdot(xc[:, g * LANES:(g + 1) * LANES], wx_bd[g]) for g in range(ng)], axis=1)
    r_gate = jax.nn.sigmoid(ra + ba)
    i_gate = jax.nn.sigmoid(rx + bx)
    log_a = -LRU_C * r_gate * jax.nn.softplus(-lam)
    a = jnp.exp(log_a)
    b = jnp.sqrt(_neg_expm1(2.0 * log_a)) * (i_gate * xc)
    return a, b, jax.nn.gelu(gate)


def _silu(x):
    return 0.5 * x * (1.0 + jnp.tanh(0.5 * x))


def _gdn_act(y, is_qk, q_scale):
    y = _silu(y)
    if is_qk is False:
        return y
    outs = []
    for h in range(y.shape[1] // C_HEAD):
        yh = y[:, h * C_HEAD:(h + 1) * C_HEAD]
        nrm = yh * (lax.rsqrt(jnp.sum(yh * yh, axis=-1, keepdims=True) + 1e-6) * q_scale)
        outs.append(nrm if is_qk is True else jnp.where(is_qk, nrm, yh))
    return jnp.concatenate(outs, axis=1)


def _gdn_head_out(o, z, norm_w):
    o = o * lax.rsqrt(jnp.mean(o * o, axis=-1, keepdims=True) + NORM_EPS) * norm_w
    return o * _silu(z)


def _w_block(w_ref):
    return w_ref[0] if len(w_ref.shape) == 3 else w_ref[...]


def _mm_body(*refs, n_x, has_res, transposed):
    x_refs = refs[:n_x]
    w_refs = refs[n_x:2 * n_x]
    k = 2 * n_x
    if has_res:
        res_ref, gate_ref = refs[k], refs[k + 1]
        k += 2
    o_ref = refs[k]
    wbf_refs = refs[k + 1:k + 1 + n_x]

    @pl.when((pl.program_id(1) == 0) & (pl.program_id(2) == 0))
    def _():
        for w_ref, wbf_ref in zip(w_refs, wbf_refs):
            wbf_ref[...] = _w_block(w_ref).astype(BF16)

    acc = None
    for x_ref, wbf_ref, tr in zip(x_refs, wbf_refs, transposed):
        d = lax.dot_general(x_ref[0], wbf_ref[...], _NT if tr else _NN, preferred_element_type=F32)
        acc = d if acc is None else acc + d
    if has_res:
        acc = res_ref[0] + gate_ref[0] * acc
    o_ref[0] = acc.astype(o_ref.dtype)


def _pick_tiles(n_out, k_sum, rows, seq, n_out_bufs):
    best = None
    for tm in [t for t in (1024, 512, 256, 128) if seq % t == 0] or [seq]:
        for tn in (1024, 512, 256, 128):
            if n_out % tn:
                continue
            need = (2 * tm * k_sum * 2 + 2 * k_sum * tn * 4 + k_sum * tn * 2
                    + (2 * n_out_bufs + 1) * tm * tn * 4)
            if need > MM_VMEM_BUDGET:
                continue
            traffic = (n_out // tn) * rows * k_sum * 2
            key = (tm < min(512, seq), traffic, -tn, -tm)
            if best is None or key < best[0]:
                best = (key, tm, tn)
    if best is None:
        raise ValueError("no projection tile fits VMEM")
    return best[1], best[2]


def _mm(xs, ws, *, n_out, col0=0, tm=None, tn=None, out_dtype=F32, res=None, gate=None, gate_blk=0, name="proj"):
    B, T, _ = xs[0].shape
    ks = [x.shape[-1] for x in xs]
    has_res = res is not None
    if tn is None:
        tm, tn = _pick_tiles(n_out, sum(ks), B * T, T, 2 if has_res else 1)
    elif tm is None:
        tm = min(1024, T)
    assert n_out % tn == 0 and T % tm == 0
    in_specs = [pl.BlockSpec((1, tm, k), lambda j, b, t: (b, t, 0)) for k in ks]
    scratch = []
    for k, (w, layer, rb, tr) in zip(ks, ws):
        if tr:
            assert col0 % SUBLANES == 0 and tn % SUBLANES == 0 and w.shape[2] == k
            in_specs.append(pl.BlockSpec(
                (pl.Element(1), pl.Element(tn), pl.Element(k)),
                functools.partial(lambda j, b, t, layer: (layer, pl.multiple_of(col0 + j * tn, SUBLANES), 0),
                                  layer=layer)))
            scratch.append(pltpu.VMEM((tn, k), BF16))
        else:
            assert col0 % tn == 0
            in_specs.append(pl.BlockSpec(
                (None, k, tn),
                functools.partial(lambda j, b, t, layer, rb: (layer, rb, col0 // tn + j), layer=layer, rb=rb)))
            scratch.append(pltpu.VMEM((k, tn), BF16))
    args = list(xs) + [w for w, _, _, _ in ws]
    if has_res:
        in_specs.append(pl.BlockSpec((1, tm, tn), lambda j, b, t: (b, t, j)))
        gb0 = gate_blk * (n_out // tn)
        if gate.shape[1] == 1:
            in_specs.append(pl.BlockSpec((1, 1, tn), lambda j, b, t: (b, 0, gb0 + j)))
        else:
            in_specs.append(pl.BlockSpec((1, tm, tn), lambda j, b, t: (b, t, gb0 + j)))
        args += [res, gate]
    return pl.pallas_call(
        functools.partial(_mm_body, n_x=len(xs), has_res=has_res, transposed=tuple(w[3] for w in ws)),
        grid=(n_out // tn, B, T // tm),
        in_specs=in_specs,
        out_specs=pl.BlockSpec((1, tm, tn), lambda j, b, t: (b, t, j)),
        out_shape=jax.ShapeDtypeStruct((B, T, n_out), out_dtype),
        scratch_shapes=scratch,
        compiler_params=_cparams(3),
        name=name,
    )(*args)


def _mod_body(c_ref, w_ref, b_ref, op_ref, os_ref):
    bs = os_ref.shape[1]
    y = _bdot(jax.nn.silu(c_ref[...]), w_ref[0]) + b_ref[0]
    os_ref[0] = y[:bs]
    op_ref[0] = y[bs:]


def _modulation(c_p, c_s, w_mod, b_mod):
    depth, d, n = w_mod.shape
    tn = 1024
    bp, bs = c_p.shape[0], c_s.shape[0]
    return pl.pallas_call(
        _mod_body,
        grid=(depth, n // tn),
        in_specs=[pl.BlockSpec((bs + bp, d), lambda l, j: (0, 0)),
                  pl.BlockSpec((1, d, tn), lambda l, j: (l, 0, j)),
                  pl.BlockSpec((1, 1, tn), lambda l, j: (l, 0, j))],
        out_specs=[pl.BlockSpec((1, bp, tn), lambda l, j: (l, 0, j)),
                   pl.BlockSpec((1, bs, tn), lambda l, j: (l, 0, j))],
        out_shape=[jax.ShapeDtypeStruct((depth, bp, n), F32),
                   jax.ShapeDtypeStruct((depth, bs, n), F32)],
        compiler_params=_cparams(2),
        name="modulation",
    )(jnp.concatenate([c_s, c_p], axis=0), w_mod, b_mod.reshape(depth, 1, n))


def _norm_body(*refs, has_mod):
    if has_mod:
        x_ref, g_ref, sc_ref, sh_ref, o_ref = refs
    else:
        x_ref, g_ref, o_ref = refs
    x = x_ref[0]
    y = x * lax.rsqrt(jnp.mean(x * x, axis=-1, keepdims=True) + NORM_EPS) * g_ref[...]
    if has_mod:
        y = y * (1.0 + sc_ref[0]) + sh_ref[0]
    o_ref[0] = y.astype(o_ref.dtype)


def _norm(x, gain, mod=None, sc_blk=0, sh_blk=0, *, tt, out_dtype=BF16):
    B, T, d = x.shape
    in_specs = [pl.BlockSpec((1, tt, d), lambda b, t: (b, t, 0)),
                pl.BlockSpec((1, d), lambda b, t: (0, 0))]
    args = [x, gain.reshape(1, d)]
    if mod is not None:
        for blk in (sc_blk, sh_blk):
            if mod.shape[1] == 1:
                in_specs.append(pl.BlockSpec((1, 1, d), functools.partial(lambda b, t, blk: (b, 0, blk), blk=blk)))
            else:
                in_specs.append(pl.BlockSpec((1, tt, d), functools.partial(lambda b, t, blk: (b, t, blk), blk=blk)))
        args += [mod, mod]
    return pl.pallas_call(
        functools.partial(_norm_body, has_mod=mod is not None),
        grid=(B, T // tt),
        in_specs=in_specs,
        out_specs=pl.BlockSpec((1, tt, d), lambda b, t: (b, t, 0)),
        out_shape=jax.ShapeDtypeStruct((B, T, d), out_dtype),
        compiler_params=_cparams(2),
        name="norm",
    )(*args)


def _ffn_up_prompt_body(h_ref, wg_ref, wu_ref, cw_ref, cb_ref, act_ref, st_ref, wg_bf, wu_bf, buf):
    t = pl.program_id(2)
    tm = h_ref.shape[1]

    @pl.when((pl.program_id(1) == 0) & (t == 0))
    def _():
        wg_bf[...] = wg_ref[0].astype(BF16)
        wu_bf[...] = wu_ref[0].astype(BF16)

    @pl.when(t == 0)
    def _():
        buf[0:HIST, :] = jnp.zeros((HIST, buf.shape[1]), F32)

    h = h_ref[0]
    buf[HIST:, :] = jnp.dot(h, wg_bf[...], preferred_element_type=F32)
    cw = cw_ref[0]
    u = (buf[HIST - 2:HIST - 2 + tm, :] * cw[0:1] + buf[HIST - 1:HIST - 1 + tm, :] * cw[1:2]
         + buf[HIST:, :] * cw[2:3] + cb_ref[0])
    up = jnp.dot(h, wu_bf[...], preferred_element_type=F32)
    act_ref[0] = (jax.nn.gelu(u) * up).astype(act_ref.dtype)
    st_ref[0] = buf[HIST + tm - 2:HIST + tm, :]
    buf[0:HIST, :] = buf[tm:tm + HIST, :]


def _ffn_up_prompt(h, w_gate, w_up, conv_w, conv_b, layer, *, tm, tn):
    B, T, d = h.shape
    n = w_gate.shape[-1]
    return pl.pallas_call(
        _ffn_up_prompt_body,
        grid=(n // tn, B, T // tm),
        in_specs=[pl.BlockSpec((1, tm, d), lambda j, b, t: (b, t, 0)),
                  pl.BlockSpec((1, d, tn), lambda j, b, t: (layer, 0, j)),
                  pl.BlockSpec((1, d, tn), lambda j, b, t: (layer, 0, j)),
                  pl.BlockSpec((1, 3, tn), lambda j, b, t: (layer, 0, j)),
                  pl.BlockSpec((1, 1, tn), lambda j, b, t: (layer, 0, j))],
        out_specs=[pl.BlockSpec((1, tm, tn), lambda j, b, t: (b, t, j)),
                   pl.BlockSpec((1, 2, tn), lambda j, b, t: (b, 0, j))],
        out_shape=[jax.ShapeDtypeStruct((B, T, n), BF16),
                   jax.ShapeDtypeStruct((B, 2, n), F32)],
        scratch_shapes=[pltpu.VMEM((d, tn), BF16), pltpu.VMEM((d, tn), BF16),
                        pltpu.VMEM((HIST + tm, tn), F32)],
        compiler_params=_cparams(3),
        name="ffn_up_prompt",
    )(h, w_gate, w_up, conv_w, conv_b.reshape(conv_b.shape[0], 1, n))


def _ffn_up_sample_body(*refs, n_state):
    st_ref = refs[0]
    h_ref, wg_ref, wu_ref, cw_ref, cb_ref, act_ref, nst_ref = refs[n_state:]
    h = h_ref[...]
    pre = _bdot(h, wg_ref[0])
    cw = cw_ref[0]
    u = st_ref[:, 0, :] * cw[0:1] + st_ref[:, 1, :] * cw[1:2] + pre * cw[2:3] + cb_ref[0]
    up = _bdot(h, wu_ref[0])
    act_ref[...] = (jax.nn.gelu(u) * up).astype(act_ref.dtype)
    nst_ref[:, 0, :] = st_ref[:, 1, :]
    nst_ref[:, 1, :] = pre


def _ffn_up_sample(h, w_gate, w_up, conv_w, conv_b, state_all, new_all, layer, *, tn):
    bs, d = h.shape
    n = w_gate.shape[-1]
    s_spec = pl.BlockSpec((None, bs, 2, tn), lambda j: (layer, 0, 0, j))
    st_specs, st_args, aliases = _layer_state_io(state_all, new_all, s_spec, out_index=1)
    return pl.pallas_call(
        functools.partial(_ffn_up_sample_body, n_state=len(st_args)),
        grid=(n // tn,),
        in_specs=st_specs + [pl.BlockSpec((bs, d), lambda j: (0, 0)),
                             pl.BlockSpec((1, d, tn), lambda j: (layer, 0, j)),
                             pl.BlockSpec((1, d, tn), lambda j: (layer, 0, j)),
                             pl.BlockSpec((1, 3, tn), lambda j: (layer, 0, j)),
                             pl.BlockSpec((1, 1, tn), lambda j: (layer, 0, j))],
        out_specs=[pl.BlockSpec((bs, tn), lambda j: (0, j)), s_spec],
        out_shape=[jax.ShapeDtypeStruct((bs, n), BF16),
                   jax.ShapeDtypeStruct(state_all.shape, F32)],
        input_output_aliases=aliases,
        compiler_params=_cparams(1),
        name="ffn_up_sample",
    )(*st_args, h, w_gate, w_up, conv_w, conv_b.reshape(conv_b.shape[0], 1, n))


def _rwkv_prep_prompt_body(pm_ref, pl_ref, mum_ref, mul_ref, w0_ref, w2_ref, a0_ref, a2_ref, g2_ref,
                           kk_ref, ka_ref, r_o, kf_o, v_o, kkn_o, a_o, lw_o, g_o, bufm, bufl):
    t = pl.program_id(1)
    tt = pm_ref.shape[1]

    @pl.when(t == 0)
    def _():
        bufm[0:HIST, :] = jnp.zeros((HIST, bufm.shape[1]), F32)
        bufl[0:HIST, :] = jnp.zeros((HIST, bufl.shape[1]), F32)

    p_main = pm_ref[0]
    p_lora = pl_ref[0][:, :LORA_ALL]
    bufm[HIST:, :] = p_main
    bufl[HIST:, :] = p_lora
    q_main = bufm[HIST - 1:HIST - 1 + tt, :]
    q_lora = bufl[HIST - 1:HIST - 1 + tt, :]
    outs = _rwkv_rows(p_main, q_main, p_lora, q_lora, mum_ref[...], mul_ref[...], w0_ref[...], w2_ref[0],
                      a0_ref[...], a2_ref[0], g2_ref[0], kk_ref[...], ka_ref[...])
    for o_ref, val in zip((r_o, kf_o, v_o, kkn_o, a_o, lw_o, g_o), outs):
        o_ref[0] = val
    bufm[0:HIST, :] = bufm[tt:tt + HIST, :]
    bufl[0:HIST, :] = bufl[tt:tt + HIST, :]


def _row(v):
    return v.reshape(1, -1)


def _rwkv_prep_prompt(p_main, p_lora, prm, i, *, tt):
    B, T, _ = p_main.shape
    lw_pad = p_lora.shape[-1]
    mu = prm['rwkv_mu'][i]
    full = lambda shape: pl.BlockSpec(shape, lambda b, t: (0,) * len(shape))
    lay3 = lambda shape: pl.BlockSpec((1,) + shape, lambda b, t: (i, 0, 0))
    out_spec = pl.BlockSpec((1, tt, A_WIDTH), lambda b, t: (b, t, 0))
    return pl.pallas_call(
        _rwkv_prep_prompt_body,
        grid=(B, T // tt),
        in_specs=[pl.BlockSpec((1, tt, P_A_MAIN), lambda b, t: (b, t, 0)),
                  pl.BlockSpec((1, tt, lw_pad), lambda b, t: (b, t, 0)),
                  full((1, P_A_MAIN)), full((1, LORA_ALL)), full((1, A_WIDTH)),
                  lay3((LORA_W, A_WIDTH)), full((1, A_WIDTH)), lay3((LORA_A, A_WIDTH)),
                  lay3((LORA_G, A_WIDTH)), full((1, A_WIDTH)), full((1, A_WIDTH))],
        out_specs=[out_spec] * 7,
        out_shape=[jax.ShapeDtypeStruct((B, T, A_WIDTH), F32)] * 7,
        scratch_shapes=[pltpu.VMEM((HIST + tt, P_A_MAIN), F32), pltpu.VMEM((HIST + tt, LORA_ALL), F32)],
        compiler_params=_cparams(2),
        name="rwkv_prep_prompt",
    )(p_main, p_lora, _row(mu[:P_A_MAIN]), _row(mu[P_A_MAIN:]), _row(prm['rwkv_w0'][i]), prm['rwkv_w2'],
      _row(prm['rwkv_a0'][i]), prm['rwkv_a2'], prm['rwkv_g2'], _row(prm['rwkv_k_k'][i]), _row(prm['rwkv_k_a'][i]))


def _wkv_prompt_body(r_ref, kf_ref, v_ref, kkn_ref, a_ref, lw_ref, g_ref, rk_ref, gw_ref, gb_ref,
                     ya_ref, st_ref, s_scr):
    t = pl.program_id(2)
    tt = r_ref.shape[1]
    n = A_HEAD_DIM
    C = CHUNK
    nc = tt // C
    nh = r_ref.shape[2] // n

    @pl.when(t == 0)
    def _():
        s_scr[...] = jnp.zeros(s_scr.shape, F32)

    def units(x):
        parts = [x[:, h * n:(h + 1) * n].reshape(nc, C, n) for h in range(nh)]
        return jnp.stack(parts, axis=1).reshape(nc * nh, C, n)

    lw_all = lw_ref[0]
    lc = units(_sel_l(_block_tri(tt, C), lw_all))
    r, kf, v, kkn, a, lw = map(units, (r_ref[0], kf_ref[0], v_ref[0], kkn_ref[0], a_ref[0], lw_all))
    incl = _tri(C, False)
    strict = _tri(C, True)

    kk = kkn * lax.rsqrt(jnp.sum(kkn * kkn, axis=-1, keepdims=True) + 1e-12)
    bb = kk * a
    p = jnp.exp(lc)
    pinv = jnp.exp(-lc)
    kt = kf * pinv
    bt = bb * pinv
    at = kk * jnp.exp(lc - lw)
    rt = r * p
    gram = _bdot(jnp.concatenate([at, rt], axis=1), jnp.concatenate([bt, kt], axis=1), _BNT)
    a_ab = jnp.where(strict, gram[:, :C, :C], 0.0)
    a_ak = jnp.where(strict, gram[:, :C, C:], 0.0)
    a_rb = jnp.where(incl, gram[:, C:, :C], 0.0)
    a_rk = jnp.where(incl, gram[:, C:, C:], 0.0)
    tinv = _unit_lower_inverse(a_ab, _bdot)
    av = _bdot(jnp.concatenate([a_ak, a_rk], axis=1), v, _BNN)
    tz = _bdot(tinv, jnp.concatenate([at, av[:, :C]], axis=2), _BNN)
    rz = _bdot(a_rb, tz, _BNN)
    lhs_s = jnp.concatenate([tz[:, :, :n], rt - rz[:, :, :n]], axis=1)
    z0 = tz[:, :, n:]
    y0 = av[:, C:] - rz[:, :, n:]
    plast = p[:, C - 1:C, :]
    upd = jnp.concatenate([kt * plast, -(bt * plast)], axis=1)

    S = s_scr[...]
    ys = []
    for c in range(nc):
        sl = slice(c * nh, (c + 1) * nh)
        xs = _bdot(lhs_s[sl], S, _BNT)
        z = xs[:, :C] + z0[sl]
        ys.append(xs[:, C:] + y0[sl])
        S = S * plast[sl] + _bdot(jnp.concatenate([v[sl], z], axis=1), upd[sl], _BTN)
    s_scr[...] = S

    outs = []
    for h in range(nh):
        cols = slice(h * n, (h + 1) * n)
        y = jnp.concatenate([ys[c][h] for c in range(nc)], axis=0)
        outs.append(_rwkv_head_out(y, r_ref[0, :, cols], kf_ref[0, :, cols], v_ref[0, :, cols],
                                   g_ref[0, :, cols], rk_ref[:, cols], gw_ref[:, cols], gb_ref[:, cols]))
    ya_ref[0] = jnp.concatenate(outs, axis=1).astype(ya_ref.dtype)

    @pl.when(t == pl.num_programs(2) - 1)
    def _():
        st_ref[0] = S


def _wkv_prompt(vals, prm, i, *, tt, heads_per_step):
    B, T, _ = vals[0].shape
    wblk = heads_per_step * A_HEAD_DIM
    seq_spec = pl.BlockSpec((1, tt, wblk), lambda b, h, t: (b, t, h))
    par_spec = pl.BlockSpec((1, wblk), lambda b, h, t: (0, h))
    return pl.pallas_call(
        _wkv_prompt_body,
        grid=(B, A_HEADS // heads_per_step, T // tt),
        in_specs=[seq_spec] * 7 + [par_spec] * 3,
        out_specs=[seq_spec,
                   pl.BlockSpec((1, heads_per_step, A_HEAD_DIM, A_HEAD_DIM), lambda b, h, t: (b, h, 0, 0))],
        out_shape=[jax.ShapeDtypeStruct((B, T, A_WIDTH), BF16),
                   jax.ShapeDtypeStruct((B, A_HEADS, A_HEAD_DIM, A_HEAD_DIM), F32)],
        scratch_shapes=[pltpu.VMEM((heads_per_step, A_HEAD_DIM, A_HEAD_DIM), F32)],
        compiler_params=_cparams(3),
        name="wkv_prompt",
    )(*vals, _row(prm['rwkv_r_k'][i]), _row(prm['rwkv_gn_w'][i]), _row(prm['rwkv_gn_b'][i]))


def _lru_prompt_body(gate_ref, xb_ref, cw_ref, cb_ref, wa_ref, wx_ref, ba_ref, bx_ref, lam_ref,
                     y_ref, h_ref, cst_ref, buf, h_scr):
    t = pl.program_id(1)
    tt = xb_ref.shape[1]
    w = xb_ref.shape[2]

    @pl.when(t == 0)
    def _():
        buf[0:HIST, :] = jnp.zeros((HIST, w), F32)
        h_scr[...] = jnp.zeros(h_scr.shape, F32)

    buf[HIST:, :] = xb_ref[0]
    cw = cw_ref[0]
    xc = cb_ref[...] + buf[HIST:, :] * cw[3:4]
    for j in range(3):
        xc = xc + buf[HIST - 3 + j:HIST - 3 + j + tt, :] * cw[j:j + 1]
    a, b, gact = _lru_rows(gate_ref[0], xc, wa_ref, wx_ref, ba_ref[...], bx_ref[...], lam_ref[...])
    row = _iota2((tt, w), 0)
    d = 1
    while d < tt:
        a_sh = jnp.where(row >= d, pltpu.roll(a, d, axis=0), 1.0)
        b_sh = jnp.where(row >= d, pltpu.roll(b, d, axis=0), 0.0)
        b = a * b_sh + b
        a = a * a_sh
        d *= 2
    h = a * h_scr[0:1, :] + b
    y_ref[0] = (h * gact).astype(y_ref.dtype)
    h_last = h[tt - 1:tt, :]
    h_scr[0:1, :] = h_last
    h_ref[0] = h_last
    cst_ref[0] = buf[HIST + tt - 3:HIST + tt, :]
    buf[0:HIST, :] = buf[tt:tt + HIST, :]


def _block_diag_pairs(w):
    nb, n, _ = w.shape
    w = w.reshape(nb // 2, 2, n, n)
    z = jnp.zeros((nb // 2, n, n), w.dtype)
    top = jnp.concatenate([w[:, 0], z], axis=2)
    bot = jnp.concatenate([z, w[:, 1]], axis=2)
    return jnp.concatenate([top, bot], axis=1)


def _lru_prompt(pb, prm, i, *, tt):
    B, T, _ = pb.shape
    w = B_WIDTH
    ng = w // LANES
    full = lambda shape: pl.BlockSpec(shape, lambda b, t: (0,) * len(shape))
    return pl.pallas_call(
        _lru_prompt_body,
        grid=(B, T // tt),
        in_specs=[pl.BlockSpec((1, tt, w), lambda b, t: (b, t, 0)),
                  pl.BlockSpec((1, tt, w), lambda b, t: (b, t, 1)),
                  pl.BlockSpec((1, 4, w), lambda b, t: (i, 0, 0)),
                  full((1, w)), full((ng, LANES, LANES)), full((ng, LANES, LANES)),
                  full((1, w)), full((1, w)), full((1, w))],
        out_specs=[pl.BlockSpec((1, tt, w), lambda b, t: (b, t, 0)),
                   pl.BlockSpec((1, 1, w), lambda b, t: (b, 0, 0)),
                   pl.BlockSpec((1, 3, w), lambda b, t: (b, 0, 0))],
        out_shape=[jax.ShapeDtypeStruct((B, T, w), BF16),
                   jax.ShapeDtypeStruct((B, 1, w), F32),
                   jax.ShapeDtypeStruct((B, 3, w), F32)],
        scratch_shapes=[pltpu.VMEM((HIST + tt, w), F32), pltpu.VMEM((SUBLANES, w), F32)],
        compiler_params=_cparams(2),
        name="lru_prompt",
    )(pb, pb, prm['lru_conv_w'], _row(prm['lru_conv_b'][i]), _block_diag_pairs(prm['lru_wa'][i]),
      _block_diag_pairs(prm['lru_wx'][i]), _row(prm['lru_ba'][i]), _row(prm['lru_bx'][i]),
      _row(prm['lru_lambda'][i]))


EPI_ROWS = 256


def _in_c_conv_prompt_body(h_ref, w_ref, cw_ref, o_ref, st_ref, w_bf, buf, *, is_qk):
    j = pl.program_id(0)
    t = pl.program_id(2)
    tm = h_ref.shape[1]
    tn = w_ref.shape[1]

    @pl.when((pl.program_id(1) == 0) & (t == 0))
    def _():
        w_bf[...] = w_ref[0].astype(BF16)

    @pl.when(t == 0)
    def _():
        buf[0:HIST, :] = jnp.zeros((HIST, tn), F32)

    cw = cw_ref[0]
    q_scale = jnp.where(j < C_KEY_DIM // tn, C_HEAD ** -0.5, 1.0).astype(F32)
    rc = min(EPI_ROWS, tm)
    for r0 in range(0, tm, rc):
        buf[HIST + r0:HIST + r0 + rc, :] = lax.dot_general(h_ref[0, r0:r0 + rc, :], w_bf[...], _NT,
                                                           preferred_element_type=F32)
        y = buf[HIST + r0:HIST + r0 + rc, :] * cw[3:4]
        for jj in range(3):
            y = y + buf[HIST - 3 + jj + r0:HIST - 3 + jj + r0 + rc, :] * cw[jj:jj + 1]
        o_ref[0, r0:r0 + rc, :] = _gdn_act(y, is_qk, q_scale)
    st_ref[0] = buf[HIST + tm - 3:HIST + tm, :]
    buf[0:HIST, :] = buf[tm:tm + HIST, :]


def _in_c_conv_prompt(h, w_t, conv_w, i, *, tm, tn, col0, n_cols, is_qk):
    B, T, d = h.shape
    assert col0 % tn == 0 and n_cols % tn == 0
    return pl.pallas_call(
        functools.partial(_in_c_conv_prompt_body, is_qk=is_qk),
        grid=(n_cols // tn, B, T // tm),
        in_specs=[pl.BlockSpec((1, tm, d), lambda j, b, t: (b, t, 0)),
                  pl.BlockSpec((pl.Element(1), pl.Element(tn), pl.Element(d)),
                               lambda j, b, t: (i, pl.multiple_of(col0 + j * tn, SUBLANES), 0)),
                  pl.BlockSpec((1, 4, tn), lambda j, b, t: (i, 0, col0 // tn + j))],
        out_specs=[pl.BlockSpec((1, tm, tn), lambda j, b, t: (b, t, j)),
                   pl.BlockSpec((1, 3, tn), lambda j, b, t: (b, 0, j))],
        out_shape=[jax.ShapeDtypeStruct((B, T, n_cols), F32),
                   jax.ShapeDtypeStruct((B, 3, n_cols), F32)],
        scratch_shapes=[pltpu.VMEM((tn, d), BF16), pltpu.VMEM((HIST + tm, tn), F32)],
        compiler_params=_cparams(3),
        name="in_c_conv_prompt",
    )(h, w_t, conv_w)


def _gdn_gates(ba, a_log, dt_bias):
    hv = C_V_HEADS
    beta = jax.nn.sigmoid(ba[..., :hv])
    g = -jnp.exp(a_log) * jax.nn.softplus(ba[..., hv:2 * hv] + dt_bias)
    return beta, g


def _lane_bcast_col(x, col):
    n = x.shape[1]
    onehot = _iota2((n, LANES), 0) == col
    return _sel_r(x, onehot)


def _gdn_prompt_body(q_ref, k_ref, v_ref, z_ref, ba_ref, alog_ref, dtb_ref, nw_ref, o_ref, st_ref, s_scr):
    hblk = pl.program_id(1)
    t = pl.program_id(2)
    tt = q_ref.shape[1]
    C = CHUNK
    nc = tt // C
    rep = C_V_HEADS // C_K_HEADS
    n_kh = q_ref.shape[2] // C_HEAD
    nh = n_kh * rep

    @pl.when(t == 0)
    def _():
        s_scr[...] = jnp.zeros(s_scr.shape, F32)

    beta_all, g_all = _gdn_gates(ba_ref[0], alog_ref[...], dtb_ref[...])
    gc_all = _sel_l(_block_tri(tt, C), g_all)
    blocks = lambda x: x.reshape(nc, C, x.shape[-1])
    qs, ks, vs, betas, gcs = [], [], [], [], []
    for kh in range(n_kh):
        q_h = blocks(q_ref[0, :, kh * C_HEAD:(kh + 1) * C_HEAD])
        k_h = blocks(k_ref[0, :, kh * C_HEAD:(kh + 1) * C_HEAD])
        for jv in range(rep):
            hu = kh * rep + jv
            hv = hblk * nh + hu
            qs.append(q_h)
            ks.append(k_h)
            vs.append(blocks(v_ref[0, :, hu * C_HEAD:(hu + 1) * C_HEAD]))
            betas.append(blocks(_lane_bcast_col(beta_all, hv)))
            gcs.append(blocks(_lane_bcast_col(gc_all, hv)))
    stack = lambda xs: jnp.stack(xs, axis=1).reshape(nc * nh, C, xs[0].shape[-1])
    q, k, v, beta, gc = map(stack, (qs, ks, vs, betas, gcs))
    incl = _tri(C, False)
    strict = _tri(C, True)

    gcc = gc[:, :, :C]
    diff = gcc - jnp.swapaxes(gcc, 1, 2)
    dec_incl = jnp.exp(jnp.where(incl, diff, -jnp.inf))
    dec_strict = jnp.where(strict, dec_incl, 0.0)
    kb = k * beta
    gram = _bdot(jnp.concatenate([kb, q], axis=1), k, _BNT)
    low = gram[:, :C] * dec_strict
    qk = gram[:, C:] * dec_incl
    tinv = _unit_lower_inverse_refined(low)
    eg = jnp.exp(gc)
    sol = _bdot(tinv, jnp.concatenate([v * beta, kb * eg], axis=2), _BNN)
    u = sol[:, :, :C_HEAD]
    lhs_s = jnp.concatenate([sol[:, :, C_HEAD:], q * eg], axis=1)
    g_last = gc[:, C - 1:C, :]
    k_tail = k * jnp.exp(g_last - gc)
    e_last = jnp.exp(g_last)

    S = s_scr[...]
    os_ = []
    for c in range(nc):
        sl = slice(c * nh, (c + 1) * nh)
        ws = _bdot(lhs_s[sl], S, _BNN)
        v_new = u[sl] - ws[:, :C]
        os_.append(ws[:, C:] + _bdot(qk[sl], v_new, _BNN))
        S = S * e_last[sl] + _bdot(k_tail[sl], v_new, _BTN)
    s_scr[...] = S

    outs = []
    for hu in range(nh):
        cols = slice(hu * C_HEAD, (hu + 1) * C_HEAD)
        o = jnp.concatenate([os_[c][hu] for c in range(nc)], axis=0)
        outs.append(_gdn_head_out(o, z_ref[0, :, cols], nw_ref[...]))
    o_ref[0] = jnp.concatenate(outs, axis=1).astype(o_ref.dtype)

    @pl.when(t == pl.num_programs(2) - 1)
    def _():
        st_ref[0] = S


def _gdn_prompt(qk_act, v_act, z, ba, prm, i, *, tt, kheads_per_step):
    B, T, _ = qk_act.shape
    rep = C_V_HEADS // C_K_HEADS
    kw = kheads_per_step * C_HEAD
    vw = rep * kw
    k_blk0 = C_KEY_DIM // kw
    v_blk0 = 0
    z_blk0 = 0
    nh = kheads_per_step * rep
    full = lambda shape: pl.BlockSpec(shape, lambda b, h, t: (0,) * len(shape))
    return pl.pallas_call(
        _gdn_prompt_body,
        grid=(B, C_K_HEADS // kheads_per_step, T // tt),
        in_specs=[pl.BlockSpec((1, tt, kw), lambda b, h, t: (b, t, h)),
                  pl.BlockSpec((1, tt, kw), lambda b, h, t: (b, t, k_blk0 + h)),
                  pl.BlockSpec((1, tt, vw), lambda b, h, t: (b, t, v_blk0 + h)),
                  pl.BlockSpec((1, tt, vw), lambda b, h, t: (b, t, z_blk0 + h)),
                  pl.BlockSpec((1, tt, 2 * C_V_HEADS), lambda b, h, t: (b, t, 0)),
                  full((1, C_V_HEADS)), full((1, C_V_HEADS)), full((1, C_HEAD))],
        out_specs=[pl.BlockSpec((1, tt, vw), lambda b, h, t: (b, t, h)),
                   pl.BlockSpec((1, nh, C_HEAD, C_HEAD), lambda b, h, t: (b, h, 0, 0))],
        out_shape=[jax.ShapeDtypeStruct((B, T, C_VAL_DIM), BF16),
                   jax.ShapeDtypeStruct((B, C_V_HEADS, C_HEAD, C_HEAD), F32)],
        scratch_shapes=[pltpu.VMEM((nh, C_HEAD, C_HEAD), F32)],
        compiler_params=_cparams(3),
        name="gdn_prompt",
    )(qk_act, qk_act, v_act, z, ba, _row(prm['gdn_a_log'][i]), _row(prm['gdn_dt_bias'][i]),
      _row(prm['gdn_norm_w'][i]))


def _rwkv_prep_sample_body(pm_ref, pl_ref, qm_ref, ql_ref, mum_ref, mul_ref, w0_ref, w2_ref, a0_ref, a2_ref,
                           g2_ref, kk_ref, ka_ref, r_o, kf_o, v_o, kkn_o, a_o, lw_o, g_o):
    outs = _rwkv_rows(pm_ref[...], qm_ref[...], pl_ref[...][:, :LORA_ALL], ql_ref[...], mum_ref[...],
                      mul_ref[...], w0_ref[...], w2_ref[0], a0_ref[...], a2_ref[0], g2_ref[0],
                      kk_ref[...], ka_ref[...])
    for o_ref, val in zip((r_o, kf_o, v_o, kkn_o, a_o, lw_o, g_o), outs):
        o_ref[...] = val.T


def _rwkv_prep_sample(p_main, p_lora, shift, prm, i):
    bs = p_main.shape[0]
    mu = prm['rwkv_mu'][i]
    full = lambda shape: pl.BlockSpec(shape, lambda s: (0,) * len(shape))
    lay3 = lambda shape: pl.BlockSpec((1,) + shape, lambda s: (i, 0, 0))
    return pl.pallas_call(
        _rwkv_prep_sample_body,
        grid=(1,),
        in_specs=[full(p_main.shape), full(p_lora.shape), full((bs, P_A_MAIN)), full((bs, LORA_ALL)),
                  full((1, P_A_MAIN)), full((1, LORA_ALL)), full((1, A_WIDTH)),
                  lay3((LORA_W, A_WIDTH)), full((1, A_WIDTH)), lay3((LORA_A, A_WIDTH)),
                  lay3((LORA_G, A_WIDTH)), full((1, A_WIDTH)), full((1, A_WIDTH))],
        out_specs=[full((A_WIDTH, bs))] * 7,
        out_shape=[jax.ShapeDtypeStruct((A_WIDTH, bs), F32)] * 7,
        compiler_params=_cparams(1),
        name="rwkv_prep_sample",
    )(p_main, p_lora, shift[:, :P_A_MAIN], shift[:, P_A_MAIN:], _row(mu[:P_A_MAIN]), _row(mu[P_A_MAIN:]),
      _row(prm['rwkv_w0'][i]), prm['rwkv_w2'], _row(prm['rwkv_a0'][i]), prm['rwkv_a2'], prm['rwkv_g2'],
      _row(prm['rwkv_k_k'][i]), _row(prm['rwkv_k_a'][i]))


STEP_GROUP = 16


def _wkv_step_body(*refs, n_state):
    s_ref = refs[0]
    r_ref, kf_ref, v_ref, kkn_ref, a_ref, lw_ref, g_ref, rk_ref, gw_ref, gb_ref, ns_ref, y_ref = refs[n_state:]
    n = A_HEAD_DIM
    outs = []
    for h in range(s_ref.shape[0]):
        rows = slice(h * n, (h + 1) * n)
        ld = lambda ref: ref[rows, :]
        r, kf, v, kkn, a = ld(r_ref), ld(kf_ref), ld(v_ref), ld(kkn_ref), ld(a_ref)
        kk = kkn * lax.rsqrt(jnp.sum(kkn * kkn, axis=0, keepdims=True) + 1e-12)
        S = s_ref[h]
        u = jnp.sum(S * kk[None], axis=1)
        s_new = S * jnp.exp(ld(lw_ref))[None] - u[:, None, :] * (kk * a)[None] + v[:, None, :] * kf[None]
        ns_ref[h] = s_new
        y = jnp.sum(s_new * r[None], axis=1)
        mean = jnp.mean(y, axis=0, keepdims=True)
        var = jnp.mean(jnp.square(y - mean), axis=0, keepdims=True)
        yn = (y - mean) * lax.rsqrt(var + A_GN_EPS) * ld(gw_ref) + ld(gb_ref)
        bonus = jnp.sum(r * kf * ld(rk_ref), axis=0, keepdims=True) * v
        outs.append(((yn + bonus) * ld(g_ref)).T)
    y_ref[...] = jnp.concatenate(outs, axis=1).astype(y_ref.dtype)


def _layer_state_io(state_all, new_all, spec, out_index=0):
    if new_all is None:
        return [spec], [state_all], {}
    return [spec, pl.BlockSpec(memory_space=pl.ANY)], [state_all, new_all], {1: out_index}


def _wkv_step(S_all, new_all, vals, prm, i, *, heads_per_step):
    _, n_heads, n, _, bs = S_all.shape
    wblk = heads_per_step * n
    lanes = lambda p: jnp.broadcast_to(p.reshape(A_WIDTH, 1), (A_WIDTH, bs))
    row_spec = pl.BlockSpec((wblk, bs), lambda h: (h, 0))
    s_spec = pl.BlockSpec((None, heads_per_step, n, n, bs), lambda h: (i, h, 0, 0, 0))
    st_specs, st_args, aliases = _layer_state_io(S_all, new_all, s_spec)
    return pl.pallas_call(
        functools.partial(_wkv_step_body, n_state=len(st_args)),
        grid=(n_heads // heads_per_step,),
        in_specs=st_specs + [row_spec] * 10,
        out_specs=[s_spec, pl.BlockSpec((bs, wblk), lambda h: (0, h))],
        out_shape=[jax.ShapeDtypeStruct(S_all.shape, F32), jax.ShapeDtypeStruct((bs, A_WIDTH), BF16)],
        input_output_aliases=aliases,
        compiler_params=_cparams(1),
        name="wkv_step",
    )(*st_args, *vals, lanes(prm['rwkv_r_k'][i]), lanes(prm['rwkv_gn_w'][i]), lanes(prm['rwkv_gn_b'][i]))


def _lru_sample_body(gate_ref, xb_ref, cst_ref, h0_ref, cw_ref, cb_ref, wa_ref, wx_ref, ba_ref, bx_ref, lam_ref,
                     y_ref, h_ref, ncst_ref):
    cw = cw_ref[0]
    xb = xb_ref[...]
    xc = cb_ref[...] + xb * cw[3:4]
    for j in range(3):
        xc = xc + cst_ref[:, j, :] * cw[j:j + 1]
    a, b, gact = _lru_rows(gate_ref[...], xc, wa_ref, wx_ref, ba_ref[...], bx_ref[...], lam_ref[...])
    h = a * h0_ref[...] + b
    y_ref[...] = (h * gact).astype(y_ref.dtype)
    h_ref[...] = h
    ncst_ref[:, 0, :] = cst_ref[:, 1, :]
    ncst_ref[:, 1, :] = cst_ref[:, 2, :]
    ncst_ref[:, 2, :] = xb


def _lru_sample(pb, conv_state, h0, prm, i):
    bs = pb.shape[0]
    w = B_WIDTH
    ng = w // LANES
    full = lambda shape: pl.BlockSpec(shape, lambda s: (0,) * len(shape))
    return pl.pallas_call(
        _lru_sample_body,
        grid=(1,),
        in_specs=[pl.BlockSpec((bs, w), lambda s: (0, 0)), pl.BlockSpec((bs, w), lambda s: (0, 1)),
                  full((bs, 3, w)), full((bs, w)),
                  pl.BlockSpec((1, 4, w), lambda s: (i, 0, 0)),
                  full((1, w)), full((ng, LANES, LANES)), full((ng, LANES, LANES)),
                  full((1, w)), full((1, w)), full((1, w))],
        out_specs=[full((bs, w)), full((bs, w)), full((bs, 3, w))],
        out_shape=[jax.ShapeDtypeStruct((bs, w), BF16), jax.ShapeDtypeStruct((bs, w), F32),
                   jax.ShapeDtypeStruct((bs, 3, w), F32)],
        compiler_params=_cparams(1),
        name="lru_sample",
    )(pb, pb, conv_state, h0, prm['lru_conv_w'], _row(prm['lru_conv_b'][i]),
      _block_diag_pairs(prm['lru_wa'][i]), _block_diag_pairs(prm['lru_wx'][i]), _row(prm['lru_ba'][i]),
      _row(prm['lru_bx'][i]), _row(prm['lru_lambda'][i]))


def _gdn_prep_sample_body(*refs, n_state):
    cst_ref = refs[0]
    x_ref, cw_ref, o_ref, ncst_ref = refs[n_state:]
    j = pl.program_id(0)
    cw = cw_ref[0]
    x = x_ref[...]
    y = x * cw[3:4]
    for jj in range(3):
        y = y + cst_ref[jj] * cw[jj:jj + 1]
    n_qk_blocks = 2 * C_KEY_DIM // x.shape[1]
    q_scale = jnp.where(j < n_qk_blocks // 2, C_HEAD ** -0.5, 1.0).astype(F32)
    o_ref[...] = _gdn_act(y, j < n_qk_blocks, q_scale)
    ncst_ref[0] = cst_ref[1]
    ncst_ref[1] = cst_ref[2]
    ncst_ref[2] = x


def _gdn_prep_sample(qkvz, conv_all, new_all, conv_w, i, *, tc):
    bs = qkvz.shape[0]
    s_spec = pl.BlockSpec((None, 3, bs, tc), lambda j: (i, 0, 0, j))
    st_specs, st_args, aliases = _layer_state_io(conv_all, new_all, s_spec, out_index=1)
    return pl.pallas_call(
        functools.partial(_gdn_prep_sample_body, n_state=len(st_args)),
        grid=(C_QKV_DIM // tc,),
        in_specs=st_specs + [pl.BlockSpec((bs, tc), lambda j: (0, j)),
                             pl.BlockSpec((1, 4, tc), lambda j: (i, 0, j))],
        out_specs=[pl.BlockSpec((bs, tc), lambda j: (0, j)), s_spec],
        out_shape=[jax.ShapeDtypeStruct((bs, C_QKV_DIM), F32),
                   jax.ShapeDtypeStruct(conv_all.shape, F32)],
        input_output_aliases=aliases,
        compiler_params=_cparams(1),
        name="gdn_prep_sample",
    )(*st_args, qkvz, conv_w)


def _gdn_step_body(*refs, n_state):
    s_ref = refs[0]
    q_ref, k_ref, v_ref, z_ref, ba_ref, alog_ref, dtb_ref, nw_ref, ns_ref, o_ref = refs[n_state:]
    bb = s_ref.shape[0]
    rep = C_V_HEADS // C_K_HEADS
    hk = pl.program_id(1)
    beta_all, g_all = _gdn_gates(ba_ref[...], alog_ref[...], dtb_ref[...])
    lane = lax.broadcasted_iota(jnp.int32, beta_all.shape, 2)
    pick = lambda x, col: jnp.sum(jnp.where(lane == col, x, 0.0), axis=-1, keepdims=True)
    rid = lax.broadcasted_iota(jnp.int32, (bb, SUBLANES, C_HEAD), 1)
    up = lambda x: jnp.broadcast_to(x, (bb, SUBLANES, C_HEAD))
    q = q_ref[...]
    k = k_ref[...]
    qk = jnp.sum(q * k, axis=-1, keepdims=True)
    for jv in range(rep):
        cols = slice(jv * C_HEAD, (jv + 1) * C_HEAD)
        hv = hk * rep + jv
        beta = pick(beta_all, hv)
        eg = jnp.exp(pick(g_all, hv))
        v = v_ref[:, :, cols]
        S = s_ref[:, jv]
        kb = k * beta
        lhs = jnp.where(rid == 0, up(kb * eg), jnp.where(rid == 1, up(q * eg), 0.0))
        ws = _bdot(lhs, S, _BNN)
        v_new = v * beta - ws[:, 0:1]
        o = ws[:, 1:2] + qk * v_new
        left = jnp.where(rid == 0, up(k), 0.0)
        right = jnp.where(rid == 0, up(v_new), 0.0)
        ns_ref[:, jv] = S * eg + _bdot(left, right, _BTN)
        o_ref[:, :, cols] = _gdn_head_out(o, z_ref[:, :, cols], nw_ref[...]).astype(o_ref.dtype)


def _gdn_step(S_all, new_all, qkv_act, qkvz, ba, prm, i, *, bb):
    bs = S_all.shape[1]
    rep = C_V_HEADS // C_K_HEADS
    vw = rep * C_HEAD
    v_blk0 = 2 * C_KEY_DIM // vw
    z_blk0 = C_QKV_DIM // vw
    full = lambda shape: pl.BlockSpec(shape, lambda b, h: (0,) * len(shape))
    s_spec = pl.BlockSpec((None, bb, rep, C_HEAD, C_HEAD), lambda b, h: (i, b, h, 0, 0))
    st_specs, st_args, aliases = _layer_state_io(S_all, new_all, s_spec)
    return pl.pallas_call(
        functools.partial(_gdn_step_body, n_state=len(st_args)),
        grid=(bs // bb, C_K_HEADS),
        in_specs=st_specs + [
                  pl.BlockSpec((bb, 1, C_HEAD), lambda b, h: (b, 0, h)),
                  pl.BlockSpec((bb, 1, C_HEAD), lambda b, h: (b, 0, C_K_HEADS + h)),
                  pl.BlockSpec((bb, 1, vw), lambda b, h: (b, 0, v_blk0 + h)),
                  pl.BlockSpec((bb, 1, vw), lambda b, h: (b, 0, z_blk0 + h)),
                  pl.BlockSpec((bb, 1, 2 * C_V_HEADS), lambda b, h: (b, 0, 0)),
                  full((1, C_V_HEADS)), full((1, C_V_HEADS)), full((1, C_HEAD))],
        out_specs=[s_spec, pl.BlockSpec((bb, 1, vw), lambda b, h: (b, 0, h))],
        out_shape=[jax.ShapeDtypeStruct(S_all.shape, F32), jax.ShapeDtypeStruct((bs, 1, C_VAL_DIM), BF16)],
        input_output_aliases=aliases,
        compiler_params=_cparams(2),
        name="gdn_step",
    )(*st_args, qkv_act, qkv_act, qkv_act, qkvz, ba, _row(prm['gdn_a_log'][i]), _row(prm['gdn_dt_bias'][i]),
      _row(prm['gdn_norm_w'][i]))


MOD_SH1, MOD_SC1, MOD_GT1, MOD_SH2, MOD_SC2, MOD_GT2 = range(6)


def _in_proj_ab(h, prm, i):
    w = [(prm['w_in_ab_t'], i, 0, True)]
    p_main = _mm([h], w, n_out=P_A_MAIN, name="in_ab_main")
    p_lora = _mm([h], w, n_out=LORA_ALL, col0=P_A_MAIN, tn=LORA_ALL, name="in_ab_lora")
    pb = _mm([h], w, n_out=2 * B_WIDTH, col0=P_A, name="in_ab_lru")
    return p_main, p_lora, pb


def _in_proj_c_gates(h, prm, i):
    return _mm([h], [(prm['w_in_c_t'], i, 0, True)], n_out=2 * C_V_HEADS, col0=C_QKV_DIM + C_VAL_DIM,
               tn=2 * C_V_HEADS, name="in_c_gates")


def _ffn_down(act, prm, layer, x, mod):
    return _mm([act], [(prm['ffn_w_down'], layer, 0, False)], n_out=D_MODEL, res=x, gate=mod, gate_blk=MOD_GT2,
               name="ffn_down")


def _out_proj_ab(ya, yb, prm, i, x, mod):
    w = prm['w_out_ab']
    return _mm([ya, yb], [(w, i, 0, False), (w, i, 1, False)], n_out=D_MODEL, res=x, gate=mod, gate_blk=MOD_GT1,
               name="out_ab")


def _out_proj_c(yc, prm, i, x, mod):
    return _mm([yc], [(prm['w_out_c'], i, 0, False)], n_out=D_MODEL, res=x, gate=mod, gate_blk=MOD_GT1,
               name="out_c")


def _trunk_prompt(x, mods, prm):
    B, T, _ = x.shape
    tm, tt, t_norm = min(1024, T), min(256, T), min(512, T)
    wkv, shift, lru_h, lru_conv, gdn, gdn_conv, ffn_conv = [], [], [], [], [], [], []
    for layer in range(DEPTH):
        mod = mods[layer].reshape(B, 1, 6 * D_MODEL)
        i = layer // 2
        h = _norm(x, prm['norm_mix'][layer], mod, MOD_SC1, MOD_SH1, tt=t_norm)
        if layer % 2 == 0:
            p_main, p_lora, pb = _in_proj_ab(h, prm, i)
            vals = _rwkv_prep_prompt(p_main, p_lora, prm, i, tt=tt)
            ya, s_new = _wkv_prompt(vals, prm, i, tt=tt, heads_per_step=A_HEADS)
            yb, h_last, cst = _lru_prompt(pb, prm, i, tt=tt)
            x = _out_proj_ab(ya, yb, prm, i, x, mod)
            wkv.append(s_new)
            shift.append(jnp.concatenate([p_main[:, -1], p_lora[:, -1, :LORA_ALL]], axis=-1))
            lru_h.append(h_last[:, 0])
            lru_conv.append(cst)
        else:
            qk_act, cst_qk = _in_c_conv_prompt(h, prm['w_in_c_t'], prm['gdn_conv_w'], i, tm=tm, tn=512, col0=0,
                                               n_cols=2 * C_KEY_DIM, is_qk=True)
            v_act, cst_v = _in_c_conv_prompt(h, prm['w_in_c_t'], prm['gdn_conv_w'], i, tm=tm, tn=512,
                                             col0=2 * C_KEY_DIM, n_cols=C_VAL_DIM, is_qk=False)
            cst = jnp.concatenate([cst_qk, cst_v], axis=-1)
            z = _mm([h], [(prm['w_in_c_t'], i, 0, True)], n_out=C_VAL_DIM, col0=C_QKV_DIM, name="in_c_z")
            ba = _in_proj_c_gates(h, prm, i)
            yc, s_new = _gdn_prompt(qk_act, v_act, z, ba, prm, i, tt=min(2 * CHUNK, T), kheads_per_step=C_K_HEADS // 2)
            x = _out_proj_c(yc, prm, i, x, mod)
            gdn.append(s_new)
            gdn_conv.append(cst)
        h = _norm(x, prm['norm_ffn'][layer], mod, MOD_SC2, MOD_SH2, tt=t_norm)
        act, fst = _ffn_up_prompt(h, prm['ffn_w_gate'], prm['ffn_w_up'], prm['ffn_conv_w'], prm['ffn_conv_b'],
                                  layer, tm=tm, tn=512)
        x = _ffn_down(act, prm, layer, x, mod)
        ffn_conv.append(fst)
    y = _norm(x, prm['norm_out'], tt=t_norm, out_dtype=F32)
    stk = jnp.stack
    return y, stk(wkv), stk(shift), stk(lru_h), stk(lru_conv), stk(gdn), stk(gdn_conv), stk(ffn_conv)


def _trunk_sample(x, mods, st, prm):
    bs = x.shape[0]
    x = x.reshape(1, bs, D_MODEL)
    wkv0, shift0, lru_h0, lru_conv0, gdn0, gdn_conv0, ffn_conv0 = st
    wkv_new = gdn_new = gconv_new = fconv_new = None
    gconv0_t = jnp.transpose(gdn_conv0, (0, 2, 1, 3))
    wkv0_t = jnp.transpose(wkv0, (0, 2, 3, 4, 1))
    shift, lru_h, lru_conv = [], [], []
    for layer in range(DEPTH):
        mod = mods[layer].reshape(1, bs, 6 * D_MODEL)
        i = layer // 2
        h = _norm(x, prm['norm_mix'][layer], mod, MOD_SC1, MOD_SH1, tt=bs)
        if layer % 2 == 0:
            p_main, p_lora, pb = _in_proj_ab(h, prm, i)
            p_main, p_lora, pb = p_main[0], p_lora[0], pb[0]
            vals = _rwkv_prep_sample(p_main, p_lora, shift0[i], prm, i)
            wkv_new, ya = _wkv_step(wkv0_t, wkv_new, vals, prm, i, heads_per_step=2)
            yb, h_last, cst = _lru_sample(pb, lru_conv0[i], lru_h0[i], prm, i)
            x = _out_proj_ab(ya[None], yb[None], prm, i, x, mod)
            shift.append(jnp.concatenate([p_main, p_lora[:, :LORA_ALL]], axis=-1))
            lru_h.append(h_last)
            lru_conv.append(cst)
        else:
            qkvz = _mm([h], [(prm['w_in_c_t'], i, 0, True)], n_out=C_QKV_DIM + C_VAL_DIM, name="in_c_main")
            ba = _in_proj_c_gates(h, prm, i)
            qkv_act, gconv_new = _gdn_prep_sample(qkvz[0], gconv0_t, gconv_new, prm['gdn_conv_w'], i, tc=2048)
            gdn_new, yc = _gdn_step(gdn0, gdn_new, qkv_act[:, None], qkvz[0][:, None], ba[0][:, None], prm, i,
                                    bb=min(STEP_GROUP, bs))
            x = _out_proj_c(yc.reshape(1, bs, C_VAL_DIM), prm, i, x, mod)
        h = _norm(x, prm['norm_ffn'][layer], mod, MOD_SC2, MOD_SH2, tt=bs)
        act, fconv_new = _ffn_up_sample(h[0], prm['ffn_w_gate'], prm['ffn_w_up'], prm['ffn_conv_w'],
                                        prm['ffn_conv_b'], ffn_conv0, fconv_new, layer, tn=512)
        x = _ffn_down(act[None], prm, layer, x, mod)
    y = _norm(x, prm['norm_out'], tt=bs, out_dtype=F32).reshape(bs, 1, D_MODEL)
    stk = jnp.stack
    return (y, jnp.transpose(wkv_new, (0, 4, 1, 2, 3)), stk(shift), stk(lru_h), stk(lru_conv), gdn_new,
            jnp.transpose(gconv_new, (0, 2, 1, 3)), fconv_new)


def kernel(x_prompt, x_sample, c_prompt, c_sample, state_rwkv_wkv, state_rwkv_shift, state_lru_h, state_lru_conv, state_gdn, state_gdn_conv, state_ffn_conv, w_mod, b_mod, norm_mix, norm_ffn, norm_out, w_in_ab, w_out_ab, rwkv_mu, rwkv_w0, rwkv_w2, rwkv_a0, rwkv_a2, rwkv_g2, rwkv_k_k, rwkv_k_a, rwkv_r_k, rwkv_gn_w, rwkv_gn_b, lru_conv_w, lru_conv_b, lru_wa, lru_ba, lru_wx, lru_bx, lru_lambda, w_in_c, w_out_c, gdn_conv_w, gdn_a_log, gdn_dt_bias, gdn_norm_w, ffn_w_gate, ffn_w_up, ffn_conv_w, ffn_conv_b, ffn_w_down):
    prm = dict(norm_mix=norm_mix, norm_ffn=norm_ffn, norm_out=norm_out, w_out_ab=w_out_ab,
               w_in_ab_t=jnp.swapaxes(w_in_ab, 1, 2), w_in_c_t=jnp.swapaxes(w_in_c, 1, 2),
               rwkv_mu=rwkv_mu, rwkv_w0=rwkv_w0, rwkv_w2=rwkv_w2, rwkv_a0=rwkv_a0, rwkv_a2=rwkv_a2,
               rwkv_g2=rwkv_g2, rwkv_k_k=rwkv_k_k, rwkv_k_a=rwkv_k_a,
               rwkv_r_k=rwkv_r_k.reshape(rwkv_r_k.shape[0], A_WIDTH), rwkv_gn_w=rwkv_gn_w,
               rwkv_gn_b=rwkv_gn_b, lru_conv_w=lru_conv_w, lru_conv_b=lru_conv_b, lru_wa=lru_wa, lru_ba=lru_ba,
               lru_wx=lru_wx, lru_bx=lru_bx, lru_lambda=lru_lambda, w_in_c=w_in_c, w_out_c=w_out_c,
               gdn_conv_w=gdn_conv_w, gdn_a_log=gdn_a_log, gdn_dt_bias=gdn_dt_bias, gdn_norm_w=gdn_norm_w,
               ffn_w_gate=ffn_w_gate, ffn_w_up=ffn_w_up, ffn_conv_w=ffn_conv_w, ffn_conv_b=ffn_conv_b,
               ffn_w_down=ffn_w_down)
    mods_p, mods_s = _modulation(c_prompt, c_sample, w_mod, b_mod)
    outs_p = _trunk_prompt(x_prompt, mods_p, prm)
    outs_s = _trunk_sample(x_sample, mods_s,
                           (state_rwkv_wkv, state_rwkv_shift, state_lru_h, state_lru_conv, state_gdn,
                            state_gdn_conv, state_ffn_conv), prm)
    return (outs_p[0], outs_s[0]) + tuple(outs_p[1:]) + tuple(outs_s[1:])
```

```python
import functools

import jax
import jax.numpy as jnp
from jax import lax
from jax.experimental import pallas as pl
from jax.experimental.pallas import tpu as pltpu

F32 = jnp.float32
BF16 = jnp.bfloat16

D_MODEL = 2048
DEPTH = 4
A_HEAD_DIM = 64
A_WIDTH = D_MODEL // 2
A_HEADS = A_WIDTH // A_HEAD_DIM
LORA_W = 64
LORA_A = 64
LORA_G = 160
LORA_ALL = LORA_W + LORA_A + LORA_G
P_A_MAIN = 3 * A_WIDTH
P_A = P_A_MAIN + LORA_ALL
A_GN_EPS = 64e-5
B_WIDTH = D_MODEL - A_WIDTH
B_BLOCK_DIM = 64
LRU_C = 8.0
C_HEAD = 128
C_K_HEADS = D_MODEL // C_HEAD
C_V_HEADS = 2 * C_K_HEADS
C_KEY_DIM = C_K_HEADS * C_HEAD
C_VAL_DIM = C_V_HEADS * C_HEAD
C_QKV_DIM = 2 * C_KEY_DIM + C_VAL_DIM
D_FF = 5632
NORM_EPS = 1e-6

LANES = 128
SUBLANES = 8
MXU_DEPTH = 256
HIST = SUBLANES
CHUNK = 64
VMEM_LIMIT = 56 * 1024 * 1024
MM_VMEM_BUDGET = 46 * 1024 * 1024

_NN = (((1,), (0,)), ((), ()))
_NT = (((1,), (1,)), ((), ()))
_TN = (((0,), (0,)), ((), ()))
_BNN = (((2,), (1,)), ((0,), (0,)))
_BNT = (((2,), (2,)), ((0,), (0,)))
_BTN = (((1,), (1,)), ((0,), (0,)))


def _cparams(n_axes):
    return pltpu.CompilerParams(dimension_semantics=("arbitrary",) * n_axes,
                                vmem_limit_bytes=VMEM_LIMIT)


def _bdot(a, b, dims=_NN):
    return lax.dot_general(a.astype(BF16), b.astype(BF16), dims, preferred_element_type=F32)


def _split2(a):
    hi = a.astype(BF16)
    lo = (a - hi.astype(F32)).astype(BF16)
    return hi, lo


def _dot3(a, b, dims=_NN):
    ah, al = _split2(a)
    bh, bl = _split2(b)
    d = lambda x, y: lax.dot_general(x, y, dims, preferred_element_type=F32)
    (lc,), (rc,) = dims[0]
    if 3 * a.shape[lc] <= MXU_DEPTH:
        return d(jnp.concatenate([ah, ah, al], axis=lc), jnp.concatenate([bh, bl, bh], axis=rc))
    return d(ah, bh) + (d(ah, bl) + d(al, bh))


def _split3(a):
    h0 = a.astype(BF16)
    r1 = a - h0.astype(F32)
    h1 = r1.astype(BF16)
    h2 = (r1 - h1.astype(F32)).astype(BF16)
    return h0, h1, h2


def _sel_l(mask01, x, dims=_NN):
    m = mask01.astype(BF16)
    d = lambda y: lax.dot_general(m, y, dims, preferred_element_type=F32)
    h0, h1, h2 = _split3(x)
    return d(h0) + (d(h1) + d(h2))


def _sel_r(x, mask01):
    m = mask01.astype(BF16)
    d = lambda y, w: lax.dot_general(y, w, _NN, preferred_element_type=F32)
    h0, h1, h2 = _split3(x)
    if 3 * x.shape[1] <= MXU_DEPTH:
        return d(jnp.concatenate([h0, h1, h2], axis=1), jnp.concatenate([m, m, m], axis=0))
    return d(h0, m) + (d(h1, m) + d(h2, m))


def _iota2(shape, axis):
    return lax.broadcasted_iota(jnp.int32, shape, axis)


def _tri(n, strict):
    i = _iota2((n, n), 0)
    j = _iota2((n, n), 1)
    return (i > j) if strict else (i >= j)


def _unit_lower_inverse(low, mm):
    n = low.shape[-1]
    dims = _BNN if low.ndim == 3 else _NN
    eye = (_iota2((n, n), 0) == _iota2((n, n), 1)).astype(F32)
    m = -low
    inv = eye + m
    p = m
    span = 2
    while span < n:
        p = mm(p, p, dims)
        inv = inv + mm(inv, p, dims)
        span *= 2
    return inv


def _unit_lower_inverse_refined(low):
    n = low.shape[-1]
    dims = _BNN if low.ndim == 3 else _NN
    eye = (_iota2((n, n), 0) == _iota2((n, n), 1)).astype(F32)
    x0 = _unit_lower_inverse(low, _bdot)
    resid = eye - _dot3(eye + low, x0, dims)
    return x0 + _bdot(x0, resid, dims)


def _block_tri(tt, block):
    i = _iota2((tt, tt), 0)
    j = _iota2((tt, tt), 1)
    return (i // block == j // block) & (i >= j)


def _neg_expm1(x):
    return -jnp.tanh(0.5 * x) * (jnp.exp(x) + 1.0)


def _rwkv_rows(p_main, q_main, p_lora, q_lora, mu_main, mu_lora, w0, w2, a0, a2, g2, k_k, k_a):
    pm = p_main + (q_main - p_main) * mu_main
    pl_ = p_lora + (q_lora - p_lora) * mu_lora
    r = pm[:, :A_WIDTH]
    k = pm[:, A_WIDTH:2 * A_WIDTH]
    v = pm[:, 2 * A_WIDTH:]
    xw = pl_[:, :LORA_W]
    xa = pl_[:, LORA_W:LORA_W + LORA_A]
    xg = pl_[:, LORA_W + LORA_A:LORA_ALL]
    w_ll = -jax.nn.softplus(-(w0 + _bdot(jnp.tanh(xw), w2))) - 0.5
    lw = -jnp.exp(w_ll)
    a = jax.nn.sigmoid(a0 + _bdot(xa, a2))
    g = _bdot(jax.nn.sigmoid(xg), g2)
    kkn = k * k_k
    kf = k * (1.0 + (a - 1.0) * k_a)
    return r, kf, v, kkn, a, lw, g


def _rwkv_head_out(y, r, kf, v, g, r_k, gn_w, gn_b):
    mean = jnp.mean(y, axis=-1, keepdims=True)
    var = jnp.mean(jnp.square(y - mean), axis=-1, keepdims=True)
    yn = (y - mean) * lax.rsqrt(var + A_GN_EPS) * gn_w + gn_b
    bonus = jnp.sum(r * kf * r_k, axis=-1, keepdims=True) * v
    return (yn + bonus) * g


def _lru_rows(gate, xc, wa_bd, wx_bd, ba, bx, lam):
    ng = xc.shape[1] // LANES
    ra = jnp.concatenate([_bdot(xc[:, g * LANES:(g + 1) * LANES], wa_bd[g]) for g in range(ng)], axis=1)
    rx = jnp.concatenate([_bdot(xc[:, g * LANES:(g + 1) * LANES], wx_bd[g]) for g in range(ng)], axis=1)
    r_gate = jax.nn.sigmoid(ra + ba)
    i_gate = jax.nn.sigmoid(rx + bx)
    log_a = -LRU_C * r_gate * jax.nn.softplus(-lam)
    a = jnp.exp(log_a)
    b = jnp.sqrt(_neg_expm1(2.0 * log_a)) * (i_gate * xc)
    return a, b, jax.nn.gelu(gate)


def _silu(x):
    return 0.5 * x * (1.0 + jnp.tanh(0.5 * x))


def _gdn_act(y, is_qk, q_scale):
    y = _silu(y)
    if is_qk is False:
        return y
    outs = []
    for h in range(y.shape[1] // C_HEAD):
        yh = y[:, h * C_HEAD:(h + 1) * C_HEAD]
        nrm = yh * (lax.rsqrt(jnp.sum(yh * yh, axis=-1, keepdims=True) + 1e-6) * q_scale)
        outs.append(nrm if is_qk is True else jnp.where(is_qk, nrm, yh))
    return jnp.concatenate(outs, axis=1)


def _gdn_head_out(o, z, norm_w):
    o = o * lax.rsqrt(jnp.mean(o * o, axis=-1, keepdims=True) + NORM_EPS) * norm_w
    return o * _silu(z)


def _w_block(w_ref):
    return w_ref[0] if len(w_ref.shape) == 3 else w_ref[...]


def _mm_body(*refs, n_x, has_res, transposed):
    x_refs = refs[:n_x]
    w_refs = refs[n_x:2 * n_x]
    k = 2 * n_x
    if has_res:
        res_ref, gate_ref = refs[k], refs[k + 1]
        k += 2
    o_ref = refs[k]
    wbf_refs = refs[k + 1:k + 1 + n_x]

    @pl.when((pl.program_id(1) == 0) & (pl.program_id(2) == 0))
    def _():
        for w_ref, wbf_ref in zip(w_refs, wbf_refs):
            wbf_ref[...] = _w_block(w_ref).astype(BF16)

    acc = None
    for x_ref, wbf_ref, tr in zip(x_refs, wbf_refs, transposed):
        d = lax.dot_general(x_ref[0], wbf_ref[...], _NT if tr else _NN, preferred_element_type=F32)
        acc = d if acc is None else acc + d
    if has_res:
        acc = res_ref[0] + gate_ref[0] * acc
    o_ref[0] = acc.astype(o_ref.dtype)


def _pick_tiles(n_out, k_sum, rows, seq, n_out_bufs):
    best = None
    for tm in [t for t in (1024, 512, 256, 128) if seq % t == 0] or [seq]:
        for tn in (1024, 512, 256, 128):
            if n_out % tn:
                continue
            need = (2 * tm * k_sum * 2 + 2 * k_sum * tn * 4 + k_sum * tn * 2
                    + (2 * n_out_bufs + 1) * tm * tn * 4)
            if need > MM_VMEM_BUDGET:
                continue
            traffic = (n_out // tn) * rows * k_sum * 2
            key = (tm < min(512, seq), traffic, -tn, -tm)
            if best is None or key < best[0]:
                best = (key, tm, tn)
    if best is None:
        raise ValueError("no projection tile fits VMEM")
    return best[1], best[2]


def _mm(xs, ws, *, n_out, col0=0, tm=None, tn=None, out_dtype=F32, res=None, gate=None, gate_blk=0, name="proj"):
    B, T, _ = xs[0].shape
    ks = [x.shape[-1] for x in xs]
    has_res = res is not None
    if tn is None:
        tm, tn = _pick_tiles(n_out, sum(ks), B * T, T, 2 if has_res else 1)
    elif tm is None:
        tm = min(1024, T)
    assert n_out % tn == 0 and T % tm == 0
    in_specs = [pl.BlockSpec((1, tm, k), lambda j, b, t: (b, t, 0)) for k in ks]
    scratch = []
    for k, (w, layer, rb, tr) in zip(ks, ws):
        if tr:
            assert col0 % SUBLANES == 0 and tn % SUBLANES == 0 and w.shape[2] == k
            in_specs.append(pl.BlockSpec(
                (pl.Element(1), pl.Element(tn), pl.Element(k)),
                functools.partial(lambda j, b, t, layer: (layer, pl.multiple_of(col0 + j * tn, SUBLANES), 0),
                                  layer=layer)))
            scratch.append(pltpu.VMEM((tn, k), BF16))
        else:
            assert col0 % tn == 0
            in_specs.append(pl.BlockSpec(
                (None, k, tn),
                functools.partial(lambda j, b, t, layer, rb: (layer, rb, col0 // tn + j), layer=layer, rb=rb)))
            scratch.append(pltpu.VMEM((k, tn), BF16))
    args = list(xs) + [w for w, _, _, _ in ws]
    if has_res:
        in_specs.append(pl.BlockSpec((1, tm, tn), lambda j, b, t: (b, t, j)))
        gb0 = gate_blk * (n_out // tn)
        if gate.shape[1] == 1:
            in_specs.append(pl.BlockSpec((1, 1, tn), lambda j, b, t: (b, 0, gb0 + j)))
        else:
            in_specs.append(pl.BlockSpec((1, tm, tn), lambda j, b, t: (b, t, gb0 + j)))
        args += [res, gate]
    return pl.pallas_call(
        functools.partial(_mm_body, n_x=len(xs), has_res=has_res, transposed=tuple(w[3] for w in ws)),
        grid=(n_out // tn, B, T // tm),
        in_specs=in_specs,
        out_specs=pl.BlockSpec((1, tm, tn), lambda j, b, t: (b, t, j)),
        out_shape=jax.ShapeDtypeStruct((B, T, n_out), out_dtype),
        scratch_shapes=scratch,
        compiler_params=_cparams(3),
        name=name,
    )(*args)


def _mod_body(c_ref, w_ref, b_ref, op_ref, os_ref):
    bs = os_ref.shape[1]
    y = _bdot(jax.nn.silu(c_ref[...]), w_ref[0]) + b_ref[0]
    os_ref[0] = y[:bs]
    op_ref[0] = y[bs:]


def _modulation(c_p, c_s, w_mod, b_mod):
    depth, d, n = w_mod.shape
    tn = 1024
    bp, bs = c_p.shape[0], c_s.shape[0]
    return pl.pallas_call(
        _mod_body,
        grid=(depth, n // tn),
        in_specs=[pl.BlockSpec((bs + bp, d), lambda l, j: (0, 0)),
                  pl.BlockSpec((1, d, tn), lambda l, j: (l, 0, j)),
                  pl.BlockSpec((1, 1, tn), lambda l, j: (l, 0, j))],
        out_specs=[pl.BlockSpec((1, bp, tn), lambda l, j: (l, 0, j)),
                   pl.BlockSpec((1, bs, tn), lambda l, j: (l, 0, j))],
        out_shape=[jax.ShapeDtypeStruct((depth, bp, n), F32),
                   jax.ShapeDtypeStruct((depth, bs, n), F32)],
        compiler_params=_cparams(2),
        name="modulation",
    )(jnp.concatenate([c_s, c_p], axis=0), w_mod, b_mod.reshape(depth, 1, n))


def _norm_body(*refs, has_mod):
    if has_mod:
        x_ref, g_ref, sc_ref, sh_ref, o_ref = refs
    else:
        x_ref, g_ref, o_ref = refs
    x = x_ref[0]
    y = x * lax.rsqrt(jnp.mean(x * x, axis=-1, keepdims=True) + NORM_EPS) * g_ref[...]
    if has_mod:
        y = y * (1.0 + sc_ref[0]) + sh_ref[0]
    o_ref[0] = y.astype(o_ref.dtype)


def _norm(x, gain, mod=None, sc_blk=0, sh_blk=0, *, tt, out_dtype=BF16):
    B, T, d = x.shape
    in_specs = [pl.BlockSpec((1, tt, d), lambda b, t: (b, t, 0)),
                pl.BlockSpec((1, d), lambda b, t: (0, 0))]
    args = [x, gain.reshape(1, d)]
    if mod is not None:
        for blk in (sc_blk, sh_blk):
            if mod.shape[1] == 1:
                in_specs.append(pl.BlockSpec((1, 1, d), functools.partial(lambda b, t, blk: (b, 0, blk), blk=blk)))
            else:
                in_specs.append(pl.BlockSpec((1, tt, d), functools.partial(lambda b, t, blk: (b, t, blk), blk=blk)))
        args += [mod, mod]
    return pl.pallas_call(
        functools.partial(_norm_body, has_mod=mod is not None),
        grid=(B, T // tt),
        in_specs=in_specs,
        out_specs=pl.BlockSpec((1, tt, d), lambda b, t: (b, t, 0)),
        out_shape=jax.ShapeDtypeStruct((B, T, d), out_dtype),
        compiler_params=_cparams(2),
        name="norm",
    )(*args)


def _ffn_up_prompt_body(h_ref, wg_ref, wu_ref, cw_ref, cb_ref, act_ref, st_ref, wg_bf, wu_bf, buf):
    t = pl.program_id(2)
    tm = h_ref.shape[1]

    @pl.when((pl.program_id(1) == 0) & (t == 0))
    def _():
        wg_bf[...] = wg_ref[0].astype(BF16)
        wu_bf[...] = wu_ref[0].astype(BF16)

    @pl.when(t == 0)
    def _():
        buf[0:HIST, :] = jnp.zeros((HIST, buf.shape[1]), F32)

    h = h_ref[0]
    buf[HIST:, :] = jnp.dot(h, wg_bf[...], preferred_element_type=F32)
    cw = cw_ref[0]
    u = (buf[HIST - 2:HIST - 2 + tm, :] * cw[0:1] + buf[HIST - 1:HIST - 1 + tm, :] * cw[1:2]
         + buf[HIST:, :] * cw[2:3] + cb_ref[0])
    up = jnp.dot(h, wu_bf[...], preferred_element_type=F32)
    act_ref[0] = (jax.nn.gelu(u) * up).astype(act_ref.dtype)
    st_ref[0] = buf[HIST + tm - 2:HIST + tm, :]
    buf[0:HIST, :] = buf[tm:tm + HIST, :]


def _ffn_up_prompt(h, w_gate, w_up, conv_w, conv_b, layer, *, tm, tn):
    B, T, d = h.shape
    n = w_gate.shape[-1]
    return pl.pallas_call(
        _ffn_up_prompt_body,
        grid=(n // tn, B, T // tm),
        in_specs=[pl.BlockSpec((1, tm, d), lambda j, b, t: (b, t, 0)),
                  pl.BlockSpec((1, d, tn), lambda j, b, t: (layer, 0, j)),
                  pl.BlockSpec((1, d, tn), lambda j, b, t: (layer, 0, j)),
                  pl.BlockSpec((1, 3, tn), lambda j, b, t: (layer, 0, j)),
                  pl.BlockSpec((1, 1, tn), lambda j, b, t: (layer, 0, j))],
        out_specs=[pl.BlockSpec((1, tm, tn), lambda j, b, t: (b, t, j)),
                   pl.BlockSpec((1, 2, tn), lambda j, b, t: (b, 0, j))],
        out_shape=[jax.ShapeDtypeStruct((B, T, n), BF16),
                   jax.ShapeDtypeStruct((B, 2, n), F32)],
        scratch_shapes=[pltpu.VMEM((d, tn), BF16), pltpu.VMEM((d, tn), BF16),
                        pltpu.VMEM((HIST + tm, tn), F32)],
        compiler_params=_cparams(3),
        name="ffn_up_prompt",
    )(h, w_gate, w_up, conv_w, conv_b.reshape(conv_b.shape[0], 1, n))


def _ffn_up_sample_body(*refs, n_state):
    st_ref = refs[0]
    h_ref, wg_ref, wu_ref, cw_ref, cb_ref, act_ref, nst_ref = refs[n_state:]
    h = h_ref[...]
    pre = _bdot(h, wg_ref[0])
    cw = cw_ref[0]
    u = st_ref[:, 0, :] * cw[0:1] + st_ref[:, 1, :] * cw[1:2] + pre * cw[2:3] + cb_ref[0]
    up = _bdot(h, wu_ref[0])
    act_ref[...] = (jax.nn.gelu(u) * up).astype(act_ref.dtype)
    nst_ref[:, 0, :] = st_ref[:, 1, :]
    nst_ref[:, 1, :] = pre


def _ffn_up_sample(h, w_gate, w_up, conv_w, conv_b, state_all, new_all, layer, *, tn):
    bs, d = h.shape
    n = w_gate.shape[-1]
    s_spec = pl.BlockSpec((None, bs, 2, tn), lambda j: (layer, 0, 0, j))
    st_specs, st_args, aliases = _layer_state_io(state_all, new_all, s_spec, out_index=1)
    return pl.pallas_call(
        functools.partial(_ffn_up_sample_body, n_state=len(st_args)),
        grid=(n // tn,),
        in_specs=st_specs + [pl.BlockSpec((bs, d), lambda j: (0, 0)),
                             pl.BlockSpec((1, d, tn), lambda j: (layer, 0, j)),
                             pl.BlockSpec((1, d, tn), lambda j: (layer, 0, j)),
                             pl.BlockSpec((1, 3, tn), lambda j: (layer, 0, j)),
                             pl.BlockSpec((1, 1, tn), lambda j: (layer, 0, j))],
        out_specs=[pl.BlockSpec((bs, tn), lambda j: (0, j)), s_spec],
        out_shape=[jax.ShapeDtypeStruct((bs, n), BF16),
                   jax.ShapeDtypeStruct(state_all.shape, F32)],
        input_output_aliases=aliases,
        compiler_params=_cparams(1),
        name="ffn_up_sample",
    )(*st_args, h, w_gate, w_up, conv_w, conv_b.reshape(conv_b.shape[0], 1, n))


def _rwkv_prep_prompt_body(pm_ref, pl_ref, mum_ref, mul_ref, w0_ref, w2_ref, a0_ref, a2_ref, g2_ref,
                           kk_ref, ka_ref, r_o, kf_o, v_o, kkn_o, a_o, lw_o, g_o, bufm, bufl):
    t = pl.program_id(1)
    tt = pm_ref.shape[1]

    @pl.when(t == 0)
    def _():
        bufm[0:HIST, :] = jnp.zeros((HIST, bufm.shape[1]), F32)
        bufl[0:HIST, :] = jnp.zeros((HIST, bufl.shape[1]), F32)

    p_main = pm_ref[0]
    p_lora = pl_ref[0][:, :LORA_ALL]
    bufm[HIST:, :] = p_main
    bufl[HIST:, :] = p_lora
    q_main = bufm[HIST - 1:HIST - 1 + tt, :]
    q_lora = bufl[HIST - 1:HIST - 1 + tt, :]
    outs = _rwkv_rows(p_main, q_main, p_lora, q_lora, mum_ref[...], mul_ref[...], w0_ref[...], w2_ref[0],
                      a0_ref[...], a2_ref[0], g2_ref[0], kk_ref[...], ka_ref[...])
    for o_ref, val in zip((r_o, kf_o, v_o, kkn_o, a_o, lw_o, g_o), outs):
        o_ref[0] = val
    bufm[0:HIST, :] = bufm[tt:tt + HIST, :]
    bufl[0:HIST, :] = bufl[tt:tt + HIST, :]


def _row(v):
    return v.reshape(1, -1)


def _rwkv_prep_prompt(p_main, p_lora, prm, i, *, tt):
    B, T, _ = p_main.shape
    lw_pad = p_lora.shape[-1]
    mu = prm['rwkv_mu'][i]
    full = lambda shape: pl.BlockSpec(shape, lambda b, t: (0,) * len(shape))
    lay3 = lambda shape: pl.BlockSpec((1,) + shape, lambda b, t: (i, 0, 0))
    out_spec = pl.BlockSpec((1, tt, A_WIDTH), lambda b, t: (b, t, 0))
    return pl.pallas_call(
        _rwkv_prep_prompt_body,
        grid=(B, T // tt),
        in_specs=[pl.BlockSpec((1, tt, P_A_MAIN), lambda b, t: (b, t, 0)),
                  pl.BlockSpec((1, tt, lw_pad), lambda b, t: (b, t, 0)),
                  full((1, P_A_MAIN)), full((1, LORA_ALL)), full((1, A_WIDTH)),
                  lay3((LORA_W, A_WIDTH)), full((1, A_WIDTH)), lay3((LORA_A, A_WIDTH)),
                  lay3((LORA_G, A_WIDTH)), full((1, A_WIDTH)), full((1, A_WIDTH))],
        out_specs=[out_spec] * 7,
        out_shape=[jax.ShapeDtypeStruct((B, T, A_WIDTH), F32)] * 7,
        scratch_shapes=[pltpu.VMEM((HIST + tt, P_A_MAIN), F32), pltpu.VMEM((HIST + tt, LORA_ALL), F32)],
        compiler_params=_cparams(2),
        name="rwkv_prep_prompt",
    )(p_main, p_lora, _row(mu[:P_A_MAIN]), _row(mu[P_A_MAIN:]), _row(prm['rwkv_w0'][i]), prm['rwkv_w2'],
      _row(prm['rwkv_a0'][i]), prm['rwkv_a2'], prm['rwkv_g2'], _row(prm['rwkv_k_k'][i]), _row(prm['rwkv_k_a'][i]))


def _wkv_prompt_body(r_ref, kf_ref, v_ref, kkn_ref, a_ref, lw_ref, g_ref, rk_ref, gw_ref, gb_ref,
                     ya_ref, st_ref, s_scr):
    t = pl.program_id(2)
    tt = r_ref.shape[1]
    n = A_HEAD_DIM
    C = CHUNK
    nc = tt // C
    nh = r_ref.shape[2] // n

    @pl.when(t == 0)
    def _():
        s_scr[...] = jnp.zeros(s_scr.shape, F32)

    def units(x):
        parts = [x[:, h * n:(h + 1) * n].reshape(nc, C, n) for h in range(nh)]
        return jnp.stack(parts, axis=1).reshape(nc * nh, C, n)

    lw_all = lw_ref[0]
    lc = units(_sel_l(_block_tri(tt, C), lw_all))
    r, kf, v, kkn, a, lw = map(units, (r_ref[0], kf_ref[0], v_ref[0], kkn_ref[0], a_ref[0], lw_all))
    incl = _tri(C, False)
    strict = _tri(C, True)

    kk = kkn * lax.rsqrt(jnp.sum(kkn * kkn, axis=-1, keepdims=True) + 1e-12)
    bb = kk * a
    p = jnp.exp(lc)
    pinv = jnp.exp(-lc)
    kt = kf * pinv
    bt = bb * pinv
    at = kk * jnp.exp(lc - lw)
    rt = r * p
    gram = _bdot(jnp.concatenate([at, rt], axis=1), jnp.concatenate([bt, kt], axis=1), _BNT)
    a_ab = jnp.where(strict, gram[:, :C, :C], 0.0)
    a_ak = jnp.where(strict, gram[:, :C, C:], 0.0)
    a_rb = jnp.where(incl, gram[:, C:, :C], 0.0)
    a_rk = jnp.where(incl, gram[:, C:, C:], 0.0)
    tinv = _unit_lower_inverse(a_ab, _bdot)
    av = _bdot(jnp.concatenate([a_ak, a_rk], axis=1), v, _BNN)
    tz = _bdot(tinv, jnp.concatenate([at, av[:, :C]], axis=2), _BNN)
    rz = _bdot(a_rb, tz, _BNN)
    lhs_s = jnp.concatenate([tz[:, :, :n], rt - rz[:, :, :n]], axis=1)
    z0 = tz[:, :, n:]
    y0 = av[:, C:] - rz[:, :, n:]
    plast = p[:, C - 1:C, :]
    upd = jnp.concatenate([kt * plast, -(bt * plast)], axis=1)

    S = s_scr[...]
    ys = []
    for c in range(nc):
        sl = slice(c * nh, (c + 1) * nh)
        xs = _bdot(lhs_s[sl], S, _BNT)
        z = xs[:, :C] + z0[sl]
        ys.append(xs[:, C:] + y0[sl])
        S = S * plast[sl] + _bdot(jnp.concatenate([v[sl], z], axis=1), upd[sl], _BTN)
    s_scr[...] = S

    outs = []
    for h in range(nh):
        cols = slice(h * n, (h + 1) * n)
        y = jnp.concatenate([ys[c][h] for c in range(nc)], axis=0)
        outs.append(_rwkv_head_out(y, r_ref[0, :, cols], kf_ref[0, :, cols], v_ref[0, :, cols],
                                   g_ref[0, :, cols], rk_ref[:, cols], gw_ref[:, cols], gb_ref[:, cols]))
    ya_ref[0] = jnp.concatenate(outs, axis=1).astype(ya_ref.dtype)

    @pl.when(t == pl.num_programs(2) - 1)
    def _():
        st_ref[0] = S


def _wkv_prompt(vals, prm, i, *, tt, heads_per_step):
    B, T, _ = vals[0].shape
    wblk = heads_per_step * A_HEAD_DIM
    seq_spec = pl.BlockSpec((1, tt, wblk), lambda b, h, t: (b, t, h))
    par_spec = pl.BlockSpec((1, wblk), lambda b, h, t: (0, h))
    return pl.pallas_call(
        _wkv_prompt_body,
        grid=(B, A_HEADS // heads_per_step, T // tt),
        in_specs=[seq_spec] * 7 + [par_spec] * 3,
        out_specs=[seq_spec,
                   pl.BlockSpec((1, heads_per_step, A_HEAD_DIM, A_HEAD_DIM), lambda b, h, t: (b, h, 0, 0))],
        out_shape=[jax.ShapeDtypeStruct((B, T, A_WIDTH), BF16),
                   jax.ShapeDtypeStruct((B, A_HEADS, A_HEAD_DIM, A_HEAD_DIM), F32)],
        scratch_shapes=[pltpu.VMEM((heads_per_step, A_HEAD_DIM, A_HEAD_DIM), F32)],
        compiler_params=_cparams(3),
        name="wkv_prompt",
    )(*vals, _row(prm['rwkv_r_k'][i]), _row(prm['rwkv_gn_w'][i]), _row(prm['rwkv_gn_b'][i]))


def _lru_prompt_body(gate_ref, xb_ref, cw_ref, cb_ref, wa_ref, wx_ref, ba_ref, bx_ref, lam_ref,
                     y_ref, h_ref, cst_ref, buf, h_scr):
    t = pl.program_id(1)
    tt = xb_ref.shape[1]
    w = xb_ref.shape[2]

    @pl.when(t == 0)
    def _():
        buf[0:HIST, :] = jnp.zeros((HIST, w), F32)
        h_scr[...] = jnp.zeros(h_scr.shape, F32)

    buf[HIST:, :] = xb_ref[0]
    cw = cw_ref[0]
    xc = cb_ref[...] + buf[HIST:, :] * cw[3:4]
    for j in range(3):
        xc = xc + buf[HIST - 3 + j:HIST - 3 + j + tt, :] * cw[j:j + 1]
    a, b, gact = _lru_rows(gate_ref[0], xc, wa_ref, wx_ref, ba_ref[...], bx_ref[...], lam_ref[...])
    row = _iota2((tt, w), 0)
    d = 1
    while d < tt:
        a_sh = jnp.where(row >= d, pltpu.roll(a, d, axis=0), 1.0)
        b_sh = jnp.where(row >= d, pltpu.roll(b, d, axis=0), 0.0)
        b = a * b_sh + b
        a = a * a_sh
        d *= 2
    h = a * h_scr[0:1, :] + b
    y_ref[0] = (h * gact).astype(y_ref.dtype)
    h_last = h[tt - 1:tt, :]
    h_scr[0:1, :] = h_last
    h_ref[0] = h_last
    cst_ref[0] = buf[HIST + tt - 3:HIST + tt, :]
    buf[0:HIST, :] = buf[tt:tt + HIST, :]


def _block_diag_pairs(w):
    nb, n, _ = w.shape
    w = w.reshape(nb // 2, 2, n, n)
    z = jnp.zeros((nb // 2, n, n), w.dtype)
    top = jnp.concatenate([w[:, 0], z], axis=2)
    bot = jnp.concatenate([z, w[:, 1]], axis=2)
    return jnp.concatenate([top, bot], axis=1)


def _lru_prompt(pb, prm, i, *, tt):
    B, T, _ = pb.shape
    w = B_WIDTH
    ng = w // LANES
    full = lambda shape: pl.BlockSpec(shape, lambda b, t: (0,) * len(shape))
    return pl.pallas_call(
        _lru_prompt_body,
        grid=(B, T // tt),
        in_specs=[pl.BlockSpec((1, tt, w), lambda b, t: (b, t, 0)),
                  pl.BlockSpec((1, tt, w), lambda b, t: (b, t, 1)),
                  pl.BlockSpec((1, 4, w), lambda b, t: (i, 0, 0)),
                  full((1, w)), full((ng, LANES, LANES)), full((ng, LANES, LANES)),
                  full((1, w)), full((1, w)), full((1, w))],
        out_specs=[pl.BlockSpec((1, tt, w), lambda b, t: (b, t, 0)),
                   pl.BlockSpec((1, 1, w), lambda b, t: (b, 0, 0)),
                   pl.BlockSpec((1, 3, w), lambda b, t: (b, 0, 0))],
        out_shape=[jax.ShapeDtypeStruct((B, T, w), BF16),
                   jax.ShapeDtypeStruct((B, 1, w), F32),
                   jax.ShapeDtypeStruct((B, 3, w), F32)],
        scratch_shapes=[pltpu.VMEM((HIST + tt, w), F32), pltpu.VMEM((SUBLANES, w), F32)],
        compiler_params=_cparams(2),
        name="lru_prompt",
    )(pb, pb, prm['lru_conv_w'], _row(prm['lru_conv_b'][i]), _block_diag_pairs(prm['lru_wa'][i]),
      _block_diag_pairs(prm['lru_wx'][i]), _row(prm['lru_ba'][i]), _row(prm['lru_bx'][i]),
      _row(prm['lru_lambda'][i]))


EPI_ROWS = 256


def _in_c_conv_prompt_body(h_ref, w_ref, cw_ref, o_ref, st_ref, w_bf, buf, *, is_qk):
    j = pl.program_id(0)
    t = pl.program_id(2)
    tm = h_ref.shape[1]
    tn = w_ref.shape[1]

    @pl.when((pl.program_id(1) == 0) & (t == 0))
    def _():
        w_bf[...] = w_ref[0].astype(BF16)

    @pl.when(t == 0)
    def _():
        buf[0:HIST, :] = jnp.zeros((HIST, tn), F32)

    cw = cw_ref[0]
    q_scale = jnp.where(j < C_KEY_DIM // tn, C_HEAD ** -0.5, 1.0).astype(F32)
    rc = min(EPI_ROWS, tm)
    for r0 in range(0, tm, rc):
        buf[HIST + r0:HIST + r0 + rc, :] = lax.dot_general(h_ref[0, r0:r0 + rc, :], w_bf[...], _NT,
                                                           preferred_element_type=F32)
        y = buf[HIST + r0:HIST + r0 + rc, :] * cw[3:4]
        for jj in range(3):
            y = y + buf[HIST - 3 + jj + r0:HIST - 3 + jj + r0 + rc, :] * cw[jj:jj + 1]
        o_ref[0, r0:r0 + rc, :] = _gdn_act(y, is_qk, q_scale)
    st_ref[0] = buf[HIST + tm - 3:HIST + tm, :]
    buf[0:HIST, :] = buf[tm:tm + HIST, :]


def _in_c_conv_prompt(h, w_t, conv_w, i, *, tm, tn, col0, n_cols, is_qk):
    B, T, d = h.shape
    assert col0 % tn == 0 and n_cols % tn == 0
    return pl.pallas_call(
        functools.partial(_in_c_conv_prompt_body, is_qk=is_qk),
        grid=(n_cols // tn, B, T // tm),
        in_specs=[pl.BlockSpec((1, tm, d), lambda j, b, t: (b, t, 0)),
                  pl.BlockSpec((pl.Element(1), pl.Element(tn), pl.Element(d)),
                               lambda j, b, t: (i, pl.multiple_of(col0 + j * tn, SUBLANES), 0)),
                  pl.BlockSpec((1, 4, tn), lambda j, b, t: (i, 0, col0 // tn + j))],
        out_specs=[pl.BlockSpec((1, tm, tn), lambda j, b, t: (b, t, j)),
                   pl.BlockSpec((1, 3, tn), lambda j, b, t: (b, 0, j))],
        out_shape=[jax.ShapeDtypeStruct((B, T, n_cols), F32),
                   jax.ShapeDtypeStruct((B, 3, n_cols), F32)],
        scratch_shapes=[pltpu.VMEM((tn, d), BF16), pltpu.VMEM((HIST + tm, tn), F32)],
        compiler_params=_cparams(3),
        name="in_c_conv_prompt",
    )(h, w_t, conv_w)


def _gdn_gates(ba, a_log, dt_bias):
    hv = C_V_HEADS
    beta = jax.nn.sigmoid(ba[..., :hv])
    g = -jnp.exp(a_log) * jax.nn.softplus(ba[..., hv:2 * hv] + dt_bias)
    return beta, g


def _lane_bcast_col(x, col):
    n = x.shape[1]
    onehot = _iota2((n, LANES), 0) == col
    return _sel_r(x, onehot)


def _gdn_prompt_body(q_ref, k_ref, v_ref, z_ref, ba_ref, alog_ref, dtb_ref, nw_ref, o_ref, st_ref, s_scr):
    hblk = pl.program_id(1)
    t = pl.program_id(2)
    tt = q_ref.shape[1]
    C = CHUNK
    nc = tt // C
    rep = C_V_HEADS // C_K_HEADS
    n_kh = q_ref.shape[2] // C_HEAD
    nh = n_kh * rep

    @pl.when(t == 0)
    def _():
        s_scr[...] = jnp.zeros(s_scr.shape, F32)

    beta_all, g_all = _gdn_gates(ba_ref[0], alog_ref[...], dtb_ref[...])
    gc_all = _sel_l(_block_tri(tt, C), g_all)
    blocks = lambda x: x.reshape(nc, C, x.shape[-1])
    qs, ks, vs, betas, gcs = [], [], [], [], []
    for kh in range(n_kh):
        q_h = blocks(q_ref[0, :, kh * C_HEAD:(kh + 1) * C_HEAD])
        k_h = blocks(k_ref[0, :, kh * C_HEAD:(kh + 1) * C_HEAD])
        for jv in range(rep):
            hu = kh * rep + jv
            hv = hblk * nh + hu
            qs.append(q_h)
            ks.append(k_h)
            vs.append(blocks(v_ref[0, :, hu * C_HEAD:(hu + 1) * C_HEAD]))
            betas.append(blocks(_lane_bcast_col(beta_all, hv)))
            gcs.append(blocks(_lane_bcast_col(gc_all, hv)))
    stack = lambda xs: jnp.stack(xs, axis=1).reshape(nc * nh, C, xs[0].shape[-1])
    q, k, v, beta, gc = map(stack, (qs, ks, vs, betas, gcs))
    incl = _tri(C, False)
    strict = _tri(C, True)

    gcc = gc[:, :, :C]
    diff = gcc - jnp.swapaxes(gcc, 1, 2)
    dec_incl = jnp.exp(jnp.where(incl, diff, -jnp.inf))
    dec_strict = jnp.where(strict, dec_incl, 0.0)
    kb = k * beta
    gram = _bdot(jnp.concatenate([kb, q], axis=1), k, _BNT)
    low = gram[:, :C] * dec_strict
    qk = gram[:, C:] * dec_incl
    tinv = _unit_lower_inverse_refined(low)
    eg = jnp.exp(gc)
    sol = _bdot(tinv, jnp.concatenate([v * beta, kb * eg], axis=2), _BNN)
    u = sol[:, :, :C_HEAD]
    lhs_s = jnp.concatenate([sol[:, :, C_HEAD:], q * eg], axis=1)
    g_last = gc[:, C - 1:C, :]
    k_tail = k * jnp.exp(g_last - gc)
    e_last = jnp.exp(g_last)

    S = s_scr[...]
    os_ = []
    for c in range(nc):
        sl = slice(c * nh, (c + 1) * nh)
        ws = _bdot(lhs_s[sl], S, _BNN)
        v_new = u[sl] - ws[:, :C]
        os_.append(ws[:, C:] + _bdot(qk[sl], v_new, _BNN))
        S = S * e_last[sl] + _bdot(k_tail[sl], v_new, _BTN)
    s_scr[...] = S

    outs = []
    for hu in range(nh):
        cols = slice(hu * C_HEAD, (hu + 1) * C_HEAD)
        o = jnp.concatenate([os_[c][hu] for c in range(nc)], axis=0)
        outs.append(_gdn_head_out(o, z_ref[0, :, cols], nw_ref[...]))
    o_ref[0] = jnp.concatenate(outs, axis=1).astype(o_ref.dtype)

    @pl.when(t == pl.num_programs(2) - 1)
    def _():
        st_ref[0] = S


def _gdn_prompt(qk_act, v_act, z, ba, prm, i, *, tt, kheads_per_step):
    B, T, _ = qk_act.shape
    rep = C_V_HEADS // C_K_HEADS
    kw = kheads_per_step * C_HEAD
    vw = rep * kw
    k_blk0 = C_KEY_DIM // kw
    v_blk0 = 0
    z_blk0 = 0
    nh = kheads_per_step * rep
    full = lambda shape: pl.BlockSpec(shape, lambda b, h, t: (0,) * len(shape))
    return pl.pallas_call(
        _gdn_prompt_body,
        grid=(B, C_K_HEADS // kheads_per_step, T // tt),
        in_specs=[pl.BlockSpec((1, tt, kw), lambda b, h, t: (b, t, h)),
                  pl.BlockSpec((1, tt, kw), lambda b, h, t: (b, t, k_blk0 + h)),
                  pl.BlockSpec((1, tt, vw), lambda b, h, t: (b, t, v_blk0 + h)),
                  pl.BlockSpec((1, tt, vw), lambda b, h, t: (b, t, z_blk0 + h)),
                  pl.BlockSpec((1, tt, 2 * C_V_HEADS), lambda b, h, t: (b, t, 0)),
                  full((1, C_V_HEADS)), full((1, C_V_HEADS)), full((1, C_HEAD))],
        out_specs=[pl.BlockSpec((1, tt, vw), lambda b, h, t: (b, t, h)),
                   pl.BlockSpec((1, nh, C_HEAD, C_HEAD), lambda b, h, t: (b, h, 0, 0))],
        out_shape=[jax.ShapeDtypeStruct((B, T, C_VAL_DIM), BF16),
                   jax.ShapeDtypeStruct((B, C_V_HEADS, C_HEAD, C_HEAD), F32)],
        scratch_shapes=[pltpu.VMEM((nh, C_HEAD, C_HEAD), F32)],
        compiler_params=_cparams(3),
        name="gdn_prompt",
    )(qk_act, qk_act, v_act, z, ba, _row(prm['gdn_a_log'][i]), _row(prm['gdn_dt_bias'][i]),
      _row(prm['gdn_norm_w'][i]))


def _rwkv_prep_sample_body(pm_ref, pl_ref, qm_ref, ql_ref, mum_ref, mul_ref, w0_ref, w2_ref, a0_ref, a2_ref,
                           g2_ref, kk_ref, ka_ref, r_o, kf_o, v_o, kkn_o, a_o, lw_o, g_o):
    outs = _rwkv_rows(pm_ref[...], qm_ref[...], pl_ref[...][:, :LORA_ALL], ql_ref[...], mum_ref[...],
                      mul_ref[...], w0_ref[...], w2_ref[0], a0_ref[...], a2_ref[0], g2_ref[0],
                      kk_ref[...], ka_ref[...])
    for o_ref, val in zip((r_o, kf_o, v_o, kkn_o, a_o, lw_o, g_o), outs):
        o_ref[...] = val.T


def _rwkv_prep_sample(p_main, p_lora, shift, prm, i):
    bs = p_main.shape[0]
    mu = prm['rwkv_mu'][i]
    full = lambda shape: pl.BlockSpec(shape, lambda s: (0,) * len(shape))
    lay3 = lambda shape: pl.BlockSpec((1,) + shape, lambda s: (i, 0, 0))
    return pl.pallas_call(
        _rwkv_prep_sample_body,
        grid=(1,),
        in_specs=[full(p_main.shape), full(p_lora.shape), full((bs, P_A_MAIN)), full((bs, LORA_ALL)),
                  full((1, P_A_MAIN)), full((1, LORA_ALL)), full((1, A_WIDTH)),
                  lay3((LORA_W, A_WIDTH)), full((1, A_WIDTH)), lay3((LORA_A, A_WIDTH)),
                  lay3((LORA_G, A_WIDTH)), full((1, A_WIDTH)), full((1, A_WIDTH))],
        out_specs=[full((A_WIDTH, bs))] * 7,
        out_shape=[jax.ShapeDtypeStruct((A_WIDTH, bs), F32)] * 7,
        compiler_params=_cparams(1),
        name="rwkv_prep_sample",
    )(p_main, p_lora, shift[:, :P_A_MAIN], shift[:, P_A_MAIN:], _row(mu[:P_A_MAIN]), _row(mu[P_A_MAIN:]),
      _row(prm['rwkv_w0'][i]), prm['rwkv_w2'], _row(prm['rwkv_a0'][i]), prm['rwkv_a2'], prm['rwkv_g2'],
      _row(prm['rwkv_k_k'][i]), _row(prm['rwkv_k_a'][i]))


STEP_GROUP = 16


def _wkv_step_body(*refs, n_state):
    s_ref = refs[0]
    r_ref, kf_ref, v_ref, kkn_ref, a_ref, lw_ref, g_ref, rk_ref, gw_ref, gb_ref, ns_ref, y_ref = refs[n_state:]
    n = A_HEAD_DIM
    outs = []
    for h in range(s_ref.shape[0]):
        rows = slice(h * n, (h + 1) * n)
        ld = lambda ref: ref[rows, :]
        r, kf, v, kkn, a = ld(r_ref), ld(kf_ref), ld(v_ref), ld(kkn_ref), ld(a_ref)
        kk = kkn * lax.rsqrt(jnp.sum(kkn * kkn, axis=0, keepdims=True) + 1e-12)
        S = s_ref[h]
        u = jnp.sum(S * kk[None], axis=1)
        s_new = S * jnp.exp(ld(lw_ref))[None] - u[:, None, :] * (kk * a)[None] + v[:, None, :] * kf[None]
        ns_ref[h] = s_new
        y = jnp.sum(s_new * r[None], axis=1)
        mean = jnp.mean(y, axis=0, keepdims=True)
        var = jnp.mean(jnp.square(y - mean), axis=0, keepdims=True)
        yn = (y - mean) * lax.rsqrt(var + A_GN_EPS) * ld(gw_ref) + ld(gb_ref)
        bonus = jnp.sum(r * kf * ld(rk_ref), axis=0, keepdims=True) * v
        outs.append(((yn + bonus) * ld(g_ref)).T)
    y_ref[...] = jnp.concatenate(outs, axis=1).astype(y_ref.dtype)


def _layer_state_io(state_all, new_all, spec, out_index=0):
    if new_all is None:
        return [spec], [state_all], {}
    return [spec, pl.BlockSpec(memory_space=pl.ANY)], [state_all, new_all], {1: out_index}


def _wkv_step(S_all, new_all, vals, prm, i, *, heads_per_step):
    _, n_heads, n, _, bs = S_all.shape
    wblk = heads_per_step * n
    lanes = lambda p: jnp.broadcast_to(p.reshape(A_WIDTH, 1), (A_WIDTH, bs))
    row_spec = pl.BlockSpec((wblk, bs), lambda h: (h, 0))
    s_spec = pl.BlockSpec((None, heads_per_step, n, n, bs), lambda h: (i, h, 0, 0, 0))
    st_specs, st_args, aliases = _layer_state_io(S_all, new_all, s_spec)
    return pl.pallas_call(
        functools.partial(_wkv_step_body, n_state=len(st_args)),
        grid=(n_heads // heads_per_step,),
        in_specs=st_specs + [row_spec] * 10,
        out_specs=[s_spec, pl.BlockSpec((bs, wblk), lambda h: (0, h))],
        out_shape=[jax.ShapeDtypeStruct(S_all.shape, F32), jax.ShapeDtypeStruct((bs, A_WIDTH), BF16)],
        input_output_aliases=aliases,
        compiler_params=_cparams(1),
        name="wkv_step",
    )(*st_args, *vals, lanes(prm['rwkv_r_k'][i]), lanes(prm['rwkv_gn_w'][i]), lanes(prm['rwkv_gn_b'][i]))


def _lru_sample_body(gate_ref, xb_ref, cst_ref, h0_ref, cw_ref, cb_ref, wa_ref, wx_ref, ba_ref, bx_ref, lam_ref,
                     y_ref, h_ref, ncst_ref):
    cw = cw_ref[0]
    xb = xb_ref[...]
    xc = cb_ref[...] + xb * cw[3:4]
    for j in range(3):
        xc = xc + cst_ref[:, j, :] * cw[j:j + 1]
    a, b, gact = _lru_rows(gate_ref[...], xc, wa_ref, wx_ref, ba_ref[...], bx_ref[...], lam_ref[...])
    h = a * h0_ref[...] + b
    y_ref[...] = (h * gact).astype(y_ref.dtype)
    h_ref[...] = h
    ncst_ref[:, 0, :] = cst_ref[:, 1, :]
    ncst_ref[:, 1, :] = cst_ref[:, 2, :]
    ncst_ref[:, 2, :] = xb


def _lru_sample(pb, conv_state, h0, prm, i):
    bs = pb.shape[0]
    w = B_WIDTH
    ng = w // LANES
    full = lambda shape: pl.BlockSpec(shape, lambda s: (0,) * len(shape))
    return pl.pallas_call(
        _lru_sample_body,
        grid=(1,),
        in_specs=[pl.BlockSpec((bs, w), lambda s: (0, 0)), pl.BlockSpec((bs, w), lambda s: (0, 1)),
                  full((bs, 3, w)), full((bs, w)),
                  pl.BlockSpec((1, 4, w), lambda s: (i, 0, 0)),
                  full((1, w)), full((ng, LANES, LANES)), full((ng, LANES, LANES)),
                  full((1, w)), full((1, w)), full((1, w))],
        out_specs=[full((bs, w)), full((bs, w)), full((bs, 3, w))],
        out_shape=[jax.ShapeDtypeStruct((bs, w), BF16), jax.ShapeDtypeStruct((bs, w), F32),
                   jax.ShapeDtypeStruct((bs, 3, w), F32)],
        compiler_params=_cparams(1),
        name="lru_sample",
    )(pb, pb, conv_state, h0, prm['lru_conv_w'], _row(prm['lru_conv_b'][i]),
      _block_diag_pairs(prm['lru_wa'][i]), _block_diag_pairs(prm['lru_wx'][i]), _row(prm['lru_ba'][i]),
      _row(prm['lru_bx'][i]), _row(prm['lru_lambda'][i]))


def _gdn_prep_sample_body(*refs, n_state):
    cst_ref = refs[0]
    x_ref, cw_ref, o_ref, ncst_ref = refs[n_state:]
    j = pl.program_id(0)
    cw = cw_ref[0]
    x = x_ref[...]
    y = x * cw[3:4]
    for jj in range(3):
        y = y + cst_ref[jj] * cw[jj:jj + 1]
    n_qk_blocks = 2 * C_KEY_DIM // x.shape[1]
    q_scale = jnp.where(j < n_qk_blocks // 2, C_HEAD ** -0.5, 1.0).astype(F32)
    o_ref[...] = _gdn_act(y, j < n_qk_blocks, q_scale)
    ncst_ref[0] = cst_ref[1]
    ncst_ref[1] = cst_ref[2]
    ncst_ref[2] = x


def _gdn_prep_sample(qkvz, conv_all, new_all, conv_w, i, *, tc):
    bs = qkvz.shape[0]
    s_spec = pl.BlockSpec((None, 3, bs, tc), lambda j: (i, 0, 0, j))
    st_specs, st_args, aliases = _layer_state_io(conv_all, new_all, s_spec, out_index=1)
    return pl.pallas_call(
        functools.partial(_gdn_prep_sample_body, n_state=len(st_args)),
        grid=(C_QKV_DIM // tc,),
        in_specs=st_specs + [pl.BlockSpec((bs, tc), lambda j: (0, j)),
                             pl.BlockSpec((1, 4, tc), lambda j: (i, 0, j))],
        out_specs=[pl.BlockSpec((bs, tc), lambda j: (0, j)), s_spec],
        out_shape=[jax.ShapeDtypeStruct((bs, C_QKV_DIM), F32),
                   jax.ShapeDtypeStruct(conv_all.shape, F32)],
        input_output_aliases=aliases,
        compiler_params=_cparams(1),
        name="gdn_prep_sample",
    )(*st_args, qkvz, conv_w)


def _gdn_step_body(*refs, n_state):
    s_ref = refs[0]
    q_ref, k_ref, v_ref, z_ref, ba_ref, alog_ref, dtb_ref, nw_ref, ns_ref, o_ref = refs[n_state:]
    bb = s_ref.shape[0]
    rep = C_V_HEADS // C_K_HEADS
    hk = pl.program_id(1)
    beta_all, g_all = _gdn_gates(ba_ref[...], alog_ref[...], dtb_ref[...])
    lane = lax.broadcasted_iota(jnp.int32, beta_all.shape, 2)
    pick = lambda x, col: jnp.sum(jnp.where(lane == col, x, 0.0), axis=-1, keepdims=True)
    rid = lax.broadcasted_iota(jnp.int32, (bb, SUBLANES, C_HEAD), 1)
    up = lambda x: jnp.broadcast_to(x, (bb, SUBLANES, C_HEAD))
    q = q_ref[...]
    k = k_ref[...]
    qk = jnp.sum(q * k, axis=-1, keepdims=True)
    for jv in range(rep):
        cols = slice(jv * C_HEAD, (jv + 1) * C_HEAD)
        hv = hk * rep + jv
        beta = pick(beta_all, hv)
        eg = jnp.exp(pick(g_all, hv))
        v = v_ref[:, :, cols]
        S = s_ref[:, jv]
        kb = k * beta
        lhs = jnp.where(rid == 0, up(kb * eg), jnp.where(rid == 1, up(q * eg), 0.0))
        ws = _bdot(lhs, S, _BNN)
        v_new = v * beta - ws[:, 0:1]
        o = ws[:, 1:2] + qk * v_new
        left = jnp.where(rid == 0, up(k), 0.0)
        right = jnp.where(rid == 0, up(v_new), 0.0)
        ns_ref[:, jv] = S * eg + _bdot(left, right, _BTN)
        o_ref[:, :, cols] = _gdn_head_out(o, z_ref[:, :, cols], nw_ref[...]).astype(o_ref.dtype)


def _gdn_step(S_all, new_all, qkv_act, qkvz, ba, prm, i, *, bb):
    bs = S_all.shape[1]
    rep = C_V_HEADS // C_K_HEADS
    vw = rep * C_HEAD
    v_blk0 = 2 * C_KEY_DIM // vw
    z_blk0 = C_QKV_DIM // vw
    full = lambda shape: pl.BlockSpec(shape, lambda b, h: (0,) * len(shape))
    s_spec = pl.BlockSpec((None, bb, rep, C_HEAD, C_HEAD), lambda b, h: (i, b, h, 0, 0))
    st_specs, st_args, aliases = _layer_state_io(S_all, new_all, s_spec)
    return pl.pallas_call(
        functools.partial(_gdn_step_body, n_state=len(st_args)),
        grid=(bs // bb, C_K_HEADS),
        in_specs=st_specs + [
                  pl.BlockSpec((bb, 1, C_HEAD), lambda b, h: (b, 0, h)),
                  pl.BlockSpec((bb, 1, C_HEAD), lambda b, h: (b, 0, C_K_HEADS + h)),
                  pl.BlockSpec((bb, 1, vw), lambda b, h: (b, 0, v_blk0 + h)),
                  pl.BlockSpec((bb, 1, vw), lambda b, h: (b, 0, z_blk0 + h)),
                  pl.BlockSpec((bb, 1, 2 * C_V_HEADS), lambda b, h: (b, 0, 0)),
                  full((1, C_V_HEADS)), full((1, C_V_HEADS)), full((1, C_HEAD))],
        out_specs=[s_spec, pl.BlockSpec((bb, 1, vw), lambda b, h: (b, 0, h))],
        out_shape=[jax.ShapeDtypeStruct(S_all.shape, F32), jax.ShapeDtypeStruct((bs, 1, C_VAL_DIM), BF16)],
        input_output_aliases=aliases,
        compiler_params=_cparams(2),
        name="gdn_step",
    )(*st_args, qkv_act, qkv_act, qkv_act, qkvz, ba, _row(prm['gdn_a_log'][i]), _row(prm['gdn_dt_bias'][i]),
      _row(prm['gdn_norm_w'][i]))


MOD_SH1, MOD_SC1, MOD_GT1, MOD_SH2, MOD_SC2, MOD_GT2 = range(6)


def _in_proj_ab(h, prm, i):
    w = [(prm['w_in_ab_t'], i, 0, True)]
    p_main = _mm([h], w, n_out=P_A_MAIN, name="in_ab_main")
    p_lora = _mm([h], w, n_out=LORA_ALL, col0=P_A_MAIN, tn=LORA_ALL, name="in_ab_lora")
    pb = _mm([h], w, n_out=2 * B_WIDTH, col0=P_A, name="in_ab_lru")
    return p_main, p_lora, pb


def _in_proj_c_gates(h, prm, i):
    return _mm([h], [(prm['w_in_c_t'], i, 0, True)], n_out=2 * C_V_HEADS, col0=C_QKV_DIM + C_VAL_DIM,
               tn=2 * C_V_HEADS, name="in_c_gates")


def _ffn_down(act, prm, layer, x, mod):
    return _mm([act], [(prm['ffn_w_down'], layer, 0, False)], n_out=D_MODEL, res=x, gate=mod, gate_blk=MOD_GT2,
               name="ffn_down")


def _out_proj_ab(ya, yb, prm, i, x, mod):
    w = prm['w_out_ab']
    return _mm([ya, yb], [(w, i, 0, False), (w, i, 1, False)], n_out=D_MODEL, res=x, gate=mod, gate_blk=MOD_GT1,
               name="out_ab")


def _out_proj_c(yc, prm, i, x, mod):
    return _mm([yc], [(prm['w_out_c'], i, 0, False)], n_out=D_MODEL, res=x, gate=mod, gate_blk=MOD_GT1,
               name="out_c")


def _trunk_prompt(x, mods, prm):
    B, T, _ = x.shape
    tm, tt, t_norm = min(1024, T), min(256, T), min(512, T)
    wkv, shift, lru_h, lru_conv, gdn, gdn_conv, ffn_conv = [], [], [], [], [], [], []
    for layer in range(DEPTH):
        mod = mods[layer].reshape(B, 1, 6 * D_MODEL)
        i = layer // 2
        h = _norm(x, prm['norm_mix'][layer], mod, MOD_SC1, MOD_SH1, tt=t_norm)
        if layer % 2 == 0:
            p_main, p_lora, pb = _in_proj_ab(h, prm, i)
            vals = _rwkv_prep_prompt(p_main, p_lora, prm, i, tt=tt)
            ya, s_new = _wkv_prompt(vals, prm, i, tt=tt, heads_per_step=A_HEADS)
            yb, h_last, cst = _lru_prompt(pb, prm, i, tt=tt)
            x = _out_proj_ab(ya, yb, prm, i, x, mod)
            wkv.append(s_new)
            shift.append(jnp.concatenate([p_main[:, -1], p_lora[:, -1, :LORA_ALL]], axis=-1))
            lru_h.append(h_last[:, 0])
            lru_conv.append(cst)
        else:
            qk_act, cst_qk = _in_c_conv_prompt(h, prm['w_in_c_t'], prm['gdn_conv_w'], i, tm=tm, tn=512, col0=0,
                                               n_cols=2 * C_KEY_DIM, is_qk=True)
            v_act, cst_v = _in_c_conv_prompt(h, prm['w_in_c_t'], prm['gdn_conv_w'], i, tm=tm, tn=512,
                                             col0=2 * C_KEY_DIM, n_cols=C_VAL_DIM, is_qk=False)
            cst = jnp.concatenate([cst_qk, cst_v], axis=-1)
            z = _mm([h], [(prm['w_in_c_t'], i, 0, True)], n_out=C_VAL_DIM, col0=C_QKV_DIM, name="in_c_z")
            ba = _in_proj_c_gates(h, prm, i)
            yc, s_new = _gdn_prompt(qk_act, v_act, z, ba, prm, i, tt=min(2 * CHUNK, T), kheads_per_step=C_K_HEADS // 2)
            x = _out_proj_c(yc, prm, i, x, mod)
            gdn.append(s_new)
            gdn_conv.append(cst)
        h = _norm(x, prm['norm_ffn'][layer], mod, MOD_SC2, MOD_SH2, tt=t_norm)
        act, fst = _ffn_up_prompt(h, prm['ffn_w_gate'], prm['ffn_w_up'], prm['ffn_conv_w'], prm['ffn_conv_b'],
                                  layer, tm=tm, tn=512)
        x = _ffn_down(act, prm, layer, x, mod)
        ffn_conv.append(fst)
    y = _norm(x, prm['norm_out'], tt=t_norm, out_dtype=F32)
    stk = jnp.stack
    return y, stk(wkv), stk(shift), stk(lru_h), stk(lru_conv), stk(gdn), stk(gdn_conv), stk(ffn_conv)


def _trunk_sample(x, mods, st, prm):
    bs = x.shape[0]
    x = x.reshape(1, bs, D_MODEL)
    wkv0, shift0, lru_h0, lru_conv0, gdn0, gdn_conv0, ffn_conv0 = st
    wkv_new = gdn_new = gconv_new = fconv_new = None
    gconv0_t = jnp.transpose(gdn_conv0, (0, 2, 1, 3))
    wkv0_t = jnp.transpose(wkv0, (0, 2, 3, 4, 1))
    shift, lru_h, lru_conv = [], [], []
    for layer in range(DEPTH):
        mod = mods[layer].reshape(1, bs, 6 * D_MODEL)
        i = layer // 2
        h = _norm(x, prm['norm_mix'][layer], mod, MOD_SC1, MOD_SH1, tt=bs)
        if layer % 2 == 0:
            p_main, p_lora, pb = _in_proj_ab(h, prm, i)
            p_main, p_lora, pb = p_main[0], p_lora[0], pb[0]
            vals = _rwkv_prep_sample(p_main, p_lora, shift0[i], prm, i)
            wkv_new, ya = _wkv_step(wkv0_t, wkv_new, vals, prm, i, heads_per_step=2)
            yb, h_last, cst = _lru_sample(pb, lru_conv0[i], lru_h0[i], prm, i)
            x = _out_proj_ab(ya[None], yb[None], prm, i, x, mod)
            shift.append(jnp.concatenate([p_main, p_lora[:, :LORA_ALL]], axis=-1))
            lru_h.append(h_last)
            lru_conv.append(cst)
        else:
            qkvz = _mm([h], [(prm['w_in_c_t'], i, 0, True)], n_out=C_QKV_DIM + C_VAL_DIM, name="in_c_main")
            ba = _in_proj_c_gates(h, prm, i)
            qkv_act, gconv_new = _gdn_prep_sample(qkvz[0], gconv0_t, gconv_new, prm['gdn_conv_w'], i, tc=2048)
            gdn_new, yc = _gdn_step(gdn0, gdn_new, qkv_act[:, None], qkvz[0][:, None], ba[0][:, None], prm, i,
                                    bb=min(STEP_GROUP, bs))
            x = _out_proj_c(yc.reshape(1, bs, C_VAL_DIM), prm, i, x, mod)
        h = _norm(x, prm['norm_ffn'][layer], mod, MOD_SC2, MOD_SH2, tt=bs)
        act, fconv_new = _ffn_up_sample(h[0], prm['ffn_w_gate'], prm['ffn_w_up'], prm['ffn_conv_w'],
                                        prm['ffn_conv_b'], ffn_conv0, fconv_new, layer, tn=512)
        x = _ffn_down(act[None], prm, layer, x, mod)
    y = _norm(x, prm['norm_out'], tt=bs, out_dtype=F32).reshape(bs, 1, D_MODEL)
    stk = jnp.stack
    return (y, jnp.transpose(wkv_new, (0, 4, 1, 2, 3)), stk(shift), stk(lru_h), stk(lru_conv), gdn_new,
            jnp.transpose(gconv_new, (0, 2, 1, 3)), fconv_new)


def kernel(x_prompt, x_sample, c_prompt, c_sample, state_rwkv_wkv, state_rwkv_shift, state_lru_h, state_lru_conv, state_gdn, state_gdn_conv, state_ffn_conv, w_mod, b_mod, norm_mix, norm_ffn, norm_out, w_in_ab, w_out_ab, rwkv_mu, rwkv_w0, rwkv_w2, rwkv_a0, rwkv_a2, rwkv_g2, rwkv_k_k, rwkv_k_a, rwkv_r_k, rwkv_gn_w, rwkv_gn_b, lru_conv_w, lru_conv_b, lru_wa, lru_ba, lru_wx, lru_bx, lru_lambda, w_in_c, w_out_c, gdn_conv_w, gdn_a_log, gdn_dt_bias, gdn_norm_w, ffn_w_gate, ffn_w_up, ffn_conv_w, ffn_conv_b, ffn_w_down):
    prm = dict(norm_mix=norm_mix, norm_ffn=norm_ffn, norm_out=norm_out, w_out_ab=w_out_ab,
               w_in_ab_t=jnp.swapaxes(w_in_ab, 1, 2), w_in_c_t=jnp.swapaxes(w_in_c, 1, 2),
               rwkv_mu=rwkv_mu, rwkv_w0=rwkv_w0, rwkv_w2=rwkv_w2, rwkv_a0=rwkv_a0, rwkv_a2=rwkv_a2,
               rwkv_g2=rwkv_g2, rwkv_k_k=rwkv_k_k, rwkv_k_a=rwkv_k_a,
               rwkv_r_k=rwkv_r_k.reshape(rwkv_r_k.shape[0], A_WIDTH), rwkv_gn_w=rwkv_gn_w,
               rwkv_gn_b=rwkv_gn_b, lru_conv_w=lru_conv_w, lru_conv_b=lru_conv_b, lru_wa=lru_wa, lru_ba=lru_ba,
               lru_wx=lru_wx, lru_bx=lru_bx, lru_lambda=lru_lambda, w_in_c=w_in_c, w_out_c=w_out_c,
               gdn_conv_w=gdn_conv_w, gdn_a_log=gdn_a_log, gdn_dt_bias=gdn_dt_bias, gdn_norm_w=gdn_norm_w,
               ffn_w_gate=ffn_w_gate, ffn_w_up=ffn_w_up, ffn_conv_w=ffn_conv_w, ffn_conv_b=ffn_conv_b,
               ffn_w_down=ffn_w_down)
    mods_p, mods_s = _modulation(c_prompt, c_sample, w_mod, b_mod)
    outs_p = _trunk_prompt(x_prompt, mods_p, prm)
    outs_s = _trunk_sample(x_sample, mods_s,
                           (state_rwkv_wkv, state_rwkv_shift, state_lru_h, state_lru_conv, state_gdn,
                            state_gdn_conv, state_ffn_conv), prm)
    return (outs_p[0], outs_s[0]) + tuple(outs_p[1:]) + tuple(outs_s[1:])
```

```python
import functools

import jax
import jax.numpy as jnp
from jax import lax
from jax.experimental import pallas as pl
from jax.experimental.pallas import tpu as pltpu

F32 = jnp.float32
BF16 = jnp.bfloat16

D_MODEL = 2048
DEPTH = 4
A_HEAD_DIM = 64
A_WIDTH = D_MODEL // 2
A_HEADS = A_WIDTH // A_HEAD_DIM
LORA_W = 64
LORA_A = 64
LORA_G = 160
LORA_ALL = LORA_W + LORA_A + LORA_G
P_A_MAIN = 3 * A_WIDTH
P_A = P_A_MAIN + LORA_ALL
A_GN_EPS = 64e-5
B_WIDTH = D_MODEL - A_WIDTH
LRU_C = 8.0
C_HEAD = 128
C_K_HEADS = D_MODEL // C_HEAD
C_V_HEADS = 2 * C_K_HEADS
C_KEY_DIM = C_K_HEADS * C_HEAD
C_VAL_DIM = C_V_HEADS * C_HEAD
C_QKV_DIM = 2 * C_KEY_DIM + C_VAL_DIM
NORM_EPS = 1e-6

LANES = 128
SUBLANES = 8
MXU_DEPTH = 256
HIST = SUBLANES
CHUNK = 64
VMEM_LIMIT = 56 * 1024 * 1024
MM_VMEM_BUDGET = 46 * 1024 * 1024

_NN = (((1,), (0,)), ((), ()))
_NT = (((1,), (1,)), ((), ()))
_BNN = (((2,), (1,)), ((0,), (0,)))
_BNT = (((2,), (2,)), ((0,), (0,)))
_BTN = (((1,), (1,)), ((0,), (0,)))


def _cparams(n_axes):
    return pltpu.CompilerParams(dimension_semantics=("arbitrary",) * n_axes,
                                vmem_limit_bytes=VMEM_LIMIT)


def _bdot(a, b, dims=_NN):
    return lax.dot_general(a.astype(BF16), b.astype(BF16), dims, preferred_element_type=F32)


def _split2(a):
    hi = a.astype(BF16)
    lo = (a - hi.astype(F32)).astype(BF16)
    return hi, lo


def _dot3(a, b, dims=_NN):
    ah, al = _split2(a)
    bh, bl = _split2(b)
    d = lambda x, y: lax.dot_general(x, y, dims, preferred_element_type=F32)
    (lc,), (rc,) = dims[0]
    if 3 * a.shape[lc] <= MXU_DEPTH:
        return d(jnp.concatenate([ah, ah, al], axis=lc), jnp.concatenate([bh, bl, bh], axis=rc))
    return d(ah, bh) + (d(ah, bl) + d(al, bh))


def _split3(a):
    h0 = a.astype(BF16)
    r1 = a - h0.astype(F32)
    h1 = r1.astype(BF16)
    h2 = (r1 - h1.astype(F32)).astype(BF16)
    return h0, h1, h2


def _sel_l(mask01, x, dims=_NN):
    m = mask01.astype(BF16)
    d = lambda y: lax.dot_general(m, y, dims, preferred_element_type=F32)
    h0, h1, h2 = _split3(x)
    return d(h0) + (d(h1) + d(h2))


def _sel_r(x, mask01):
    m = mask01.astype(BF16)
    d = lambda y, w: lax.dot_general(y, w, _NN, preferred_element_type=F32)
    h0, h1, h2 = _split3(x)
    if 3 * x.shape[1] <= MXU_DEPTH:
        return d(jnp.concatenate([h0, h1, h2], axis=1), jnp.concatenate([m, m, m], axis=0))
    return d(h0, m) + (d(h1, m) + d(h2, m))


def _iota2(shape, axis):
    return lax.broadcasted_iota(jnp.int32, shape, axis)


def _tri(n, strict):
    i = _iota2((n, n), 0)
    j = _iota2((n, n), 1)
    return (i > j) if strict else (i >= j)


def _unit_lower_inverse(low, mm):
    n = low.shape[-1]
    dims = _BNN if low.ndim == 3 else _NN
    eye = (_iota2((n, n), 0) == _iota2((n, n), 1)).astype(F32)
    m = -low
    inv = eye + m
    p = m
    span = 2
    while span < n:
        p = mm(p, p, dims)
        inv = inv + mm(inv, p, dims)
        span *= 2
    return inv


def _unit_lower_inverse_refined(low):
    n = low.shape[-1]
    dims = _BNN if low.ndim == 3 else _NN
    eye = (_iota2((n, n), 0) == _iota2((n, n), 1)).astype(F32)
    x0 = _unit_lower_inverse(low, _bdot)
    resid = eye - _dot3(eye + low, x0, dims)
    return x0 + _bdot(x0, resid, dims)


def _block_tri(tt, block):
    i = _iota2((tt, tt), 0)
    j = _iota2((tt, tt), 1)
    return (i // block == j // block) & (i >= j)


def _neg_expm1(x):
    return -jnp.tanh(0.5 * x) * (jnp.exp(x) + 1.0)


def _rwkv_rows(p_main, q_main, p_lora, q_lora, mu_main, mu_lora, w0, w2, a0, a2, g2, k_k, k_a):
    pm = p_main + (q_main - p_main) * mu_main
    pl_ = p_lora + (q_lora - p_lora) * mu_lora
    r = pm[:, :A_WIDTH]
    k = pm[:, A_WIDTH:2 * A_WIDTH]
    v = pm[:, 2 * A_WIDTH:]
    xw = pl_[:, :LORA_W]
    xa = pl_[:, LORA_W:LORA_W + LORA_A]
    xg = pl_[:, LORA_W + LORA_A:LORA_ALL]
    w_ll = -jax.nn.softplus(-(w0 + _bdot(jnp.tanh(xw), w2))) - 0.5
    lw = -jnp.exp(w_ll)
    a = jax.nn.sigmoid(a0 + _bdot(xa, a2))
    g = _bdot(jax.nn.sigmoid(xg), g2)
    kkn = k * k_k
    kf = k * (1.0 + (a - 1.0) * k_a)
    return r, kf, v, kkn, a, lw, g


def _rwkv_head_out(y, r, kf, v, g, r_k, gn_w, gn_b):
    mean = jnp.mean(y, axis=-1, keepdims=True)
    var = jnp.mean(jnp.square(y - mean), axis=-1, keepdims=True)
    yn = (y - mean) * lax.rsqrt(var + A_GN_EPS) * gn_w + gn_b
    bonus = jnp.sum(r * kf * r_k, axis=-1, keepdims=True) * v
    return (yn + bonus) * g


def _lru_rows(gate, xc, wa_bd, wx_bd, ba, bx, lam):
    ng = xc.shape[1] // LANES
    ra = jnp.concatenate([_bdot(xc[:, g * LANES:(g + 1) * LANES], wa_bd[g]) for g in range(ng)], axis=1)
    rx = jnp.concatenate([_bdot(xc[:, g * LANES:(g + 1) * LANES], wx_bd[g]) for g in range(ng)], axis=1)
    r_gate = jax.nn.sigmoid(ra + ba)
    i_gate = jax.nn.sigmoid(rx + bx)
    log_a = -LRU_C * r_gate * jax.nn.softplus(-lam)
    a = jnp.exp(log_a)
    b = jnp.sqrt(_neg_expm1(2.0 * log_a)) * (i_gate * xc)
    return a, b, jax.nn.gelu(gate)


def _silu(x):
    return 0.5 * x * (1.0 + jnp.tanh(0.5 * x))


def _gdn_act(y, is_qk, q_scale):
    y = _silu(y)
    if is_qk is False:
        return y
    outs = []
    for h in range(y.shape[1] // C_HEAD):
        yh = y[:, h * C_HEAD:(h + 1) * C_HEAD]
        nrm = yh * (lax.rsqrt(jnp.sum(yh * yh, axis=-1, keepdims=True) + 1e-6) * q_scale)
        outs.append(nrm if is_qk is True else jnp.where(is_qk, nrm, yh))
    return jnp.concatenate(outs, axis=1)


def _gdn_head_out(o, z, norm_w):
    o = o * lax.rsqrt(jnp.mean(o * o, axis=-1, keepdims=True) + NORM_EPS) * norm_w
    return o * _silu(z)


def _w_block(w_ref):
    return w_ref[0] if len(w_ref.shape) == 3 else w_ref[...]


def _mm_body(*refs, n_x, has_res, transposed):
    x_refs = refs[:n_x]
    w_refs = refs[n_x:2 * n_x]
    k = 2 * n_x
    if has_res:
        res_ref, gate_ref = refs[k], refs[k + 1]
        k += 2
    o_ref = refs[k]
    wbf_refs = refs[k + 1:k + 1 + n_x]

    @pl.when((pl.program_id(1) == 0) & (pl.program_id(2) == 0))
    def _():
        for w_ref, wbf_ref in zip(w_refs, wbf_refs):
            wbf_ref[...] = _w_block(w_ref).astype(BF16)

    acc = None
    for x_ref, wbf_ref, tr in zip(x_refs, wbf_refs, transposed):
        d = lax.dot_general(x_ref[0], wbf_ref[...], _NT if tr else _NN, preferred_element_type=F32)
        acc = d if acc is None else acc + d
    if has_res:
        acc = res_ref[0] + gate_ref[0] * acc
    o_ref[0] = acc.astype(o_ref.dtype)


def _pick_tiles(n_out, k_sum, rows, seq, n_out_bufs):
    best = None
    for tm in [t for t in (1024, 512, 256, 128) if seq % t == 0] or [seq]:
        for tn in (1024, 512, 256, 128):
            if n_out % tn:
                continue
            need = (2 * tm * k_sum * 2 + 2 * k_sum * tn * 4 + k_sum * tn * 2
                    + (2 * n_out_bufs + 1) * tm * tn * 4)
            if need > MM_VMEM_BUDGET:
                continue
            traffic = (n_out // tn) * rows * k_sum * 2
            key = (tm < min(512, seq), traffic, -tn, -tm)
            if best is None or key < best[0]:
                best = (key, tm, tn)
    if best is None:
        raise ValueError("no projection tile fits VMEM")
    return best[1], best[2]


def _mm(xs, ws, *, n_out, col0=0, tm=None, tn=None, out_dtype=F32, res=None, gate=None, gate_blk=0, name="proj"):
    B, T, _ = xs[0].shape
    ks = [x.shape[-1] for x in xs]
    has_res = res is not None
    if tn is None:
        tm, tn = _pick_tiles(n_out, sum(ks), B * T, T, 2 if has_res else 1)
    elif tm is None:
        tm = min(1024, T)
    assert n_out % tn == 0 and T % tm == 0
    in_specs = [pl.BlockSpec((1, tm, k), lambda j, b, t: (b, t, 0)) for k in ks]
    scratch = []
    for k, (w, layer, rb, tr) in zip(ks, ws):
        if tr:
            assert col0 % SUBLANES == 0 and tn % SUBLANES == 0 and w.shape[2] == k
            in_specs.append(pl.BlockSpec(
                (pl.Element(1), pl.Element(tn), pl.Element(k)),
                functools.partial(lambda j, b, t, layer: (layer, pl.multiple_of(col0 + j * tn, SUBLANES), 0),
                                  layer=layer)))
            scratch.append(pltpu.VMEM((tn, k), BF16))
        else:
            assert col0 % tn == 0
            in_specs.append(pl.BlockSpec(
                (None, k, tn),
                functools.partial(lambda j, b, t, layer, rb: (layer, rb, col0 // tn + j), layer=layer, rb=rb)))
            scratch.append(pltpu.VMEM((k, tn), BF16))
    args = list(xs) + [w for w, _, _, _ in ws]
    if has_res:
        in_specs.append(pl.BlockSpec((1, tm, tn), lambda j, b, t: (b, t, j)))
        gb0 = gate_blk * (n_out // tn)
        if gate.shape[1] == 1:
            in_specs.append(pl.BlockSpec((1, 1, tn), lambda j, b, t: (b, 0, gb0 + j)))
        else:
            in_specs.append(pl.BlockSpec((1, tm, tn), lambda j, b, t: (b, t, gb0 + j)))
        args += [res, gate]
    return pl.pallas_call(
        functools.partial(_mm_body, n_x=len(xs), has_res=has_res, transposed=tuple(w[3] for w in ws)),
        grid=(n_out // tn, B, T // tm),
        in_specs=in_specs,
        out_specs=pl.BlockSpec((1, tm, tn), lambda j, b, t: (b, t, j)),
        out_shape=jax.ShapeDtypeStruct((B, T, n_out), out_dtype),
        scratch_shapes=scratch,
        compiler_params=_cparams(3),
        name=name,
    )(*args)


def _mod_body(c_ref, w_ref, b_ref, op_ref, os_ref):
    bs = os_ref.shape[1]
    y = _bdot(jax.nn.silu(c_ref[...]), w_ref[0]) + b_ref[0]
    os_ref[0] = y[:bs]
    op_ref[0] = y[bs:]


def _modulation(c_p, c_s, w_mod, b_mod):
    depth, d, n = w_mod.shape
    tn = 1024
    bp, bs = c_p.shape[0], c_s.shape[0]
    return pl.pallas_call(
        _mod_body,
        grid=(depth, n // tn),
        in_specs=[pl.BlockSpec((bs + bp, d), lambda l, j: (0, 0)),
                  pl.BlockSpec((1, d, tn), lambda l, j: (l, 0, j)),
                  pl.BlockSpec((1, 1, tn), lambda l, j: (l, 0, j))],
        out_specs=[pl.BlockSpec((1, bp, tn), lambda l, j: (l, 0, j)),
                   pl.BlockSpec((1, bs, tn), lambda l, j: (l, 0, j))],
        out_shape=[jax.ShapeDtypeStruct((depth, bp, n), F32),
                   jax.ShapeDtypeStruct((depth, bs, n), F32)],
        compiler_params=_cparams(2),
        name="modulation",
    )(jnp.concatenate([c_s, c_p], axis=0), w_mod, b_mod.reshape(depth, 1, n))


def _norm_body(*refs, has_mod):
    if has_mod:
        x_ref, g_ref, sc_ref, sh_ref, o_ref = refs
    else:
        x_ref, g_ref, o_ref = refs
    x = x_ref[0]
    y = x * lax.rsqrt(jnp.mean(x * x, axis=-1, keepdims=True) + NORM_EPS) * g_ref[...]
    if has_mod:
        y = y * (1.0 + sc_ref[0]) + sh_ref[0]
    o_ref[0] = y.astype(o_ref.dtype)


def _norm(x, gain, mod=None, sc_blk=0, sh_blk=0, *, tt, out_dtype=BF16):
    B, T, d = x.shape
    in_specs = [pl.BlockSpec((1, tt, d), lambda b, t: (b, t, 0)),
                pl.BlockSpec((1, d), lambda b, t: (0, 0))]
    args = [x, gain.reshape(1, d)]
    if mod is not None:
        for blk in (sc_blk, sh_blk):
            if mod.shape[1] == 1:
                in_specs.append(pl.BlockSpec((1, 1, d), functools.partial(lambda b, t, blk: (b, 0, blk), blk=blk)))
            else:
                in_specs.append(pl.BlockSpec((1, tt, d), functools.partial(lambda b, t, blk: (b, t, blk), blk=blk)))
        args += [mod, mod]
    return pl.pallas_call(
        functools.partial(_norm_body, has_mod=mod is not None),
        grid=(B, T // tt),
        in_specs=in_specs,
        out_specs=pl.BlockSpec((1, tt, d), lambda b, t: (b, t, 0)),
        out_shape=jax.ShapeDtypeStruct((B, T, d), out_dtype),
        compiler_params=_cparams(2),
        name="norm",
    )(*args)


def _ffn_up_prompt_body(h_ref, wg_ref, wu_ref, cw_ref, cb_ref, act_ref, st_ref, wg_bf, wu_bf, buf):
    t = pl.program_id(2)
    tm = h_ref.shape[1]

    @pl.when((pl.program_id(1) == 0) & (t == 0))
    def _():
        wg_bf[...] = wg_ref[0].astype(BF16)
        wu_bf[...] = wu_ref[0].astype(BF16)

    @pl.when(t == 0)
    def _():
        buf[0:HIST, :] = jnp.zeros((HIST, buf.shape[1]), F32)

    h = h_ref[0]
    buf[HIST:, :] = jnp.dot(h, wg_bf[...], preferred_element_type=F32)
    cw = cw_ref[0]
    u = (buf[HIST - 2:HIST - 2 + tm, :] * cw[0:1] + buf[HIST - 1:HIST - 1 + tm, :] * cw[1:2]
         + buf[HIST:, :] * cw[2:3] + cb_ref[0])
    up = jnp.dot(h, wu_bf[...], preferred_element_type=F32)
    act_ref[0] = (jax.nn.gelu(u) * up).astype(act_ref.dtype)
    st_ref[0] = buf[HIST + tm - 2:HIST + tm, :]
    buf[0:HIST, :] = buf[tm:tm + HIST, :]


def _ffn_up_prompt(h, w_gate, w_up, conv_w, conv_b, layer, *, tm, tn):
    B, T, d = h.shape
    n = w_gate.shape[-1]
    return pl.pallas_call(
        _ffn_up_prompt_body,
        grid=(n // tn, B, T // tm),
        in_specs=[pl.BlockSpec((1, tm, d), lambda j, b, t: (b, t, 0)),
                  pl.BlockSpec((1, d, tn), lambda j, b, t: (layer, 0, j)),
                  pl.BlockSpec((1, d, tn), lambda j, b, t: (layer, 0, j)),
                  pl.BlockSpec((1, 3, tn), lambda j, b, t: (layer, 0, j)),
                  pl.BlockSpec((1, 1, tn), lambda j, b, t: (layer, 0, j))],
        out_specs=[pl.BlockSpec((1, tm, tn), lambda j, b, t: (b, t, j)),
                   pl.BlockSpec((1, 2, tn), lambda j, b, t: (b, 0, j))],
        out_shape=[jax.ShapeDtypeStruct((B, T, n), BF16),
                   jax.ShapeDtypeStruct((B, 2, n), F32)],
        scratch_shapes=[pltpu.VMEM((d, tn), BF16), pltpu.VMEM((d, tn), BF16),
                        pltpu.VMEM((HIST + tm, tn), F32)],
        compiler_params=_cparams(3),
        name="ffn_up_prompt",
    )(h, w_gate, w_up, conv_w, conv_b.reshape(conv_b.shape[0], 1, n))


def _ffn_up_sample_body(*refs, n_state):
    st_ref = refs[0]
    h_ref, wg_ref, wu_ref, cw_ref, cb_ref, act_ref, nst_ref = refs[n_state:]
    h = h_ref[...]
    pre = _bdot(h, wg_ref[0])
    cw = cw_ref[0]
    u = st_ref[:, 0, :] * cw[0:1] + st_ref[:, 1, :] * cw[1:2] + pre * cw[2:3] + cb_ref[0]
    up = _bdot(h, wu_ref[0])
    act_ref[...] = (jax.nn.gelu(u) * up).astype(act_ref.dtype)
    nst_ref[:, 0, :] = st_ref[:, 1, :]
    nst_ref[:, 1, :] = pre


def _ffn_up_sample(h, w_gate, w_up, conv_w, conv_b, state_all, new_all, layer, *, tn):
    bs, d = h.shape
    n = w_gate.shape[-1]
    s_spec = pl.BlockSpec((None, bs, 2, tn), lambda j: (layer, 0, 0, j))
    st_specs, st_args, aliases = _layer_state_io(state_all, new_all, s_spec, out_index=1)
    return pl.pallas_call(
        functools.partial(_ffn_up_sample_body, n_state=len(st_args)),
        grid=(n // tn,),
        in_specs=st_specs + [pl.BlockSpec((bs, d), lambda j: (0, 0)),
                             pl.BlockSpec((1, d, tn), lambda j: (layer, 0, j)),
                             pl.BlockSpec((1, d, tn), lambda j: (layer, 0, j)),
                             pl.BlockSpec((1, 3, tn), lambda j: (layer, 0, j)),
                             pl.BlockSpec((1, 1, tn), lambda j: (layer, 0, j))],
        out_specs=[pl.BlockSpec((bs, tn), lambda j: (0, j)), s_spec],
        out_shape=[jax.ShapeDtypeStruct((bs, n), BF16),
                   jax.ShapeDtypeStruct(state_all.shape, F32)],
        input_output_aliases=aliases,
        compiler_params=_cparams(1),
        name="ffn_up_sample",
    )(*st_args, h, w_gate, w_up, conv_w, conv_b.reshape(conv_b.shape[0], 1, n))


def _rwkv_prep_prompt_body(pm_ref, pl_ref, mum_ref, mul_ref, w0_ref, w2_ref, a0_ref, a2_ref, g2_ref,
                           kk_ref, ka_ref, r_o, kf_o, v_o, kkn_o, a_o, lw_o, g_o, bufm, bufl):
    t = pl.program_id(1)
    tt = pm_ref.shape[1]

    @pl.when(t == 0)
    def _():
        bufm[0:HIST, :] = jnp.zeros((HIST, bufm.shape[1]), F32)
        bufl[0:HIST, :] = jnp.zeros((HIST, bufl.shape[1]), F32)

    p_main = pm_ref[0]
    p_lora = pl_ref[0][:, :LORA_ALL]
    bufm[HIST:, :] = p_main
    bufl[HIST:, :] = p_lora
    q_main = bufm[HIST - 1:HIST - 1 + tt, :]
    q_lora = bufl[HIST - 1:HIST - 1 + tt, :]
    outs = _rwkv_rows(p_main, q_main, p_lora, q_lora, mum_ref[...], mul_ref[...], w0_ref[...], w2_ref[0],
                      a0_ref[...], a2_ref[0], g2_ref[0], kk_ref[...], ka_ref[...])
    for o_ref, val in zip((r_o, kf_o, v_o, kkn_o, a_o, lw_o, g_o), outs):
        o_ref[0] = val
    bufm[0:HIST, :] = bufm[tt:tt + HIST, :]
    bufl[0:HIST, :] = bufl[tt:tt + HIST, :]


def _row(v):
    return v.reshape(1, -1)


def _rwkv_prep_prompt(p_main, p_lora, prm, i, *, tt):
    B, T, _ = p_main.shape
    lw_pad = p_lora.shape[-1]
    mu = prm['rwkv_mu'][i]
    full = lambda shape: pl.BlockSpec(shape, lambda b, t: (0,) * len(shape))
    lay3 = lambda shape: pl.BlockSpec((1,) + shape, lambda b, t: (i, 0, 0))
    out_spec = pl.BlockSpec((1, tt, A_WIDTH), lambda b, t: (b, t, 0))
    return pl.pallas_call(
        _rwkv_prep_prompt_body,
        grid=(B, T // tt),
        in_specs=[pl.BlockSpec((1, tt, P_A_MAIN), lambda b, t: (b, t, 0)),
                  pl.BlockSpec((1, tt, lw_pad), lambda b, t: (b, t, 0)),
                  full((1, P_A_MAIN)), full((1, LORA_ALL)), full((1, A_WIDTH)),
                  lay3((LORA_W, A_WIDTH)), full((1, A_WIDTH)), lay3((LORA_A, A_WIDTH)),
                  lay3((LORA_G, A_WIDTH)), full((1, A_WIDTH)), full((1, A_WIDTH))],
        out_specs=[out_spec] * 7,
        out_shape=[jax.ShapeDtypeStruct((B, T, A_WIDTH), F32)] * 7,
        scratch_shapes=[pltpu.VMEM((HIST + tt, P_A_MAIN), F32), pltpu.VMEM((HIST + tt, LORA_ALL), F32)],
        compiler_params=_cparams(2),
        name="rwkv_prep_prompt",
    )(p_main, p_lora, _row(mu[:P_A_MAIN]), _row(mu[P_A_MAIN:]), _row(prm['rwkv_w0'][i]), prm['rwkv_w2'],
      _row(prm['rwkv_a0'][i]), prm['rwkv_a2'], prm['rwkv_g2'], _row(prm['rwkv_k_k'][i]), _row(prm['rwkv_k_a'][i]))


def _wkv_prompt_body(r_ref, kf_ref, v_ref, kkn_ref, a_ref, lw_ref, g_ref, rk_ref, gw_ref, gb_ref,
                     ya_ref, st_ref, s_scr):
    t = pl.program_id(2)
    tt = r_ref.shape[1]
    n = A_HEAD_DIM
    C = CHUNK
    nc = tt // C
    nh = r_ref.shape[2] // n

    @pl.when(t == 0)
    def _():
        s_scr[...] = jnp.zeros(s_scr.shape, F32)

    def units(x):
        parts = [x[:, h * n:(h + 1) * n].reshape(nc, C, n) for h in range(nh)]
        return jnp.stack(parts, axis=1).reshape(nc * nh, C, n)

    lw_all = lw_ref[0]
    lc = units(_sel_l(_block_tri(tt, C), lw_all))
    r, kf, v, kkn, a, lw = map(units, (r_ref[0], kf_ref[0], v_ref[0], kkn_ref[0], a_ref[0], lw_all))
    incl = _tri(C, False)
    strict = _tri(C, True)

    kk = kkn * lax.rsqrt(jnp.sum(kkn * kkn, axis=-1, keepdims=True) + 1e-12)
    bb = kk * a
    p = jnp.exp(lc)
    pinv = jnp.exp(-lc)
    kt = kf * pinv
    bt = bb * pinv
    at = kk * jnp.exp(lc - lw)
    rt = r * p
    gram = _bdot(jnp.concatenate([at, rt], axis=1), jnp.concatenate([bt, kt], axis=1), _BNT)
    a_ab = jnp.where(strict, gram[:, :C, :C], 0.0)
    a_ak = jnp.where(strict, gram[:, :C, C:], 0.0)
    a_rb = jnp.where(incl, gram[:, C:, :C], 0.0)
    a_rk = jnp.where(incl, gram[:, C:, C:], 0.0)
    tinv = _unit_lower_inverse(a_ab, _bdot)
    av = _bdot(jnp.concatenate([a_ak, a_rk], axis=1), v, _BNN)
    tz = _bdot(tinv, jnp.concatenate([at, av[:, :C]], axis=2), _BNN)
    rz = _bdot(a_rb, tz, _BNN)
    lhs_s = jnp.concatenate([tz[:, :, :n], rt - rz[:, :, :n]], axis=1)
    z0 = tz[:, :, n:]
    y0 = av[:, C:] - rz[:, :, n:]
    plast = p[:, C - 1:C, :]
    upd = jnp.concatenate([kt * plast, -(bt * plast)], axis=1)

    S = s_scr[...]
    ys = []
    for c in range(nc):
        sl = slice(c * nh, (c + 1) * nh)
        xs = _bdot(lhs_s[sl], S, _BNT)
        z = xs[:, :C] + z0[sl]
        ys.append(xs[:, C:] + y0[sl])
        S = S * plast[sl] + _bdot(jnp.concatenate([v[sl], z], axis=1), upd[sl], _BTN)
    s_scr[...] = S

    outs = []
    for h in range(nh):
        cols = slice(h * n, (h + 1) * n)
        y = jnp.concatenate([ys[c][h] for c in range(nc)], axis=0)
        outs.append(_rwkv_head_out(y, r_ref[0, :, cols], kf_ref[0, :, cols], v_ref[0, :, cols],
                                   g_ref[0, :, cols], rk_ref[:, cols], gw_ref[:, cols], gb_ref[:, cols]))
    ya_ref[0] = jnp.concatenate(outs, axis=1).astype(ya_ref.dtype)

    @pl.when(t == pl.num_programs(2) - 1)
    def _():
        st_ref[0] = S


def _wkv_prompt(vals, prm, i, *, tt, heads_per_step):
    B, T, _ = vals[0].shape
    wblk = heads_per_step * A_HEAD_DIM
    seq_spec = pl.BlockSpec((1, tt, wblk), lambda b, h, t: (b, t, h))
    par_spec = pl.BlockSpec((1, wblk), lambda b, h, t: (0, h))
    return pl.pallas_call(
        _wkv_prompt_body,
        grid=(B, A_HEADS // heads_per_step, T // tt),
        in_specs=[seq_spec] * 7 + [par_spec] * 3,
        out_specs=[seq_spec,
                   pl.BlockSpec((1, heads_per_step, A_HEAD_DIM, A_HEAD_DIM), lambda b, h, t: (b, h, 0, 0))],
        out_shape=[jax.ShapeDtypeStruct((B, T, A_WIDTH), BF16),
                   jax.ShapeDtypeStruct((B, A_HEADS, A_HEAD_DIM, A_HEAD_DIM), F32)],
        scratch_shapes=[pltpu.VMEM((heads_per_step, A_HEAD_DIM, A_HEAD_DIM), F32)],
        compiler_params=_cparams(3),
        name="wkv_prompt",
    )(*vals, _row(prm['rwkv_r_k'][i]), _row(prm['rwkv_gn_w'][i]), _row(prm['rwkv_gn_b'][i]))


def _lru_prompt_body(gate_ref, xb_ref, cw_ref, cb_ref, wa_ref, wx_ref, ba_ref, bx_ref, lam_ref,
                     y_ref, h_ref, cst_ref, buf, h_scr):
    t = pl.program_id(1)
    tt = xb_ref.shape[1]
    w = xb_ref.shape[2]

    @pl.when(t == 0)
    def _():
        buf[0:HIST, :] = jnp.zeros((HIST, w), F32)
        h_scr[...] = jnp.zeros(h_scr.shape, F32)

    buf[HIST:, :] = xb_ref[0]
    cw = cw_ref[0]
    xc = cb_ref[...] + buf[HIST:, :] * cw[3:4]
    for j in range(3):
        xc = xc + buf[HIST - 3 + j:HIST - 3 + j + tt, :] * cw[j:j + 1]
    a, b, gact = _lru_rows(gate_ref[0], xc, wa_ref, wx_ref, ba_ref[...], bx_ref[...], lam_ref[...])
    row = _iota2((tt, w), 0)
    d = 1
    while d < tt:
        a_sh = jnp.where(row >= d, pltpu.roll(a, d, axis=0), 1.0)
        b_sh = jnp.where(row >= d, pltpu.roll(b, d, axis=0), 0.0)
        b = a * b_sh + b
        a = a * a_sh
        d *= 2
    h = a * h_scr[0:1, :] + b
    y_ref[0] = (h * gact).astype(y_ref.dtype)
    h_last = h[tt - 1:tt, :]
    h_scr[0:1, :] = h_last
    h_ref[0] = h_last
    cst_ref[0] = buf[HIST + tt - 3:HIST + tt, :]
    buf[0:HIST, :] = buf[tt:tt + HIST, :]


def _block_diag_pairs(w):
    nb, n, _ = w.shape
    w = w.reshape(nb // 2, 2, n, n)
    z = jnp.zeros((nb // 2, n, n), w.dtype)
    top = jnp.concatenate([w[:, 0], z], axis=2)
    bot = jnp.concatenate([z, w[:, 1]], axis=2)
    return jnp.concatenate([top, bot], axis=1)


def _lru_prompt(pb, prm, i, *, tt):
    B, T, _ = pb.shape
    w = B_WIDTH
    ng = w // LANES
    full = lambda shape: pl.BlockSpec(shape, lambda b, t: (0,) * len(shape))
    return pl.pallas_call(
        _lru_prompt_body,
        grid=(B, T // tt),
        in_specs=[pl.BlockSpec((1, tt, w), lambda b, t: (b, t, 0)),
                  pl.BlockSpec((1, tt, w), lambda b, t: (b, t, 1)),
                  pl.BlockSpec((1, 4, w), lambda b, t: (i, 0, 0)),
                  full((1, w)), full((ng, LANES, LANES)), full((ng, LANES, LANES)),
                  full((1, w)), full((1, w)), full((1, w))],
        out_specs=[pl.BlockSpec((1, tt, w), lambda b, t: (b, t, 0)),
                   pl.BlockSpec((1, 1, w), lambda b, t: (b, 0, 0)),
                   pl.BlockSpec((1, 3, w), lambda b, t: (b, 0, 0))],
        out_shape=[jax.ShapeDtypeStruct((B, T, w), BF16),
                   jax.ShapeDtypeStruct((B, 1, w), F32),
                   jax.ShapeDtypeStruct((B, 3, w), F32)],
        scratch_shapes=[pltpu.VMEM((HIST + tt, w), F32), pltpu.VMEM((SUBLANES, w), F32)],
        compiler_params=_cparams(2),
        name="lru_prompt",
    )(pb, pb, prm['lru_conv_w'], _row(prm['lru_conv_b'][i]), _block_diag_pairs(prm['lru_wa'][i]),
      _block_diag_pairs(prm['lru_wx'][i]), _row(prm['lru_ba'][i]), _row(prm['lru_bx'][i]),
      _row(prm['lru_lambda'][i]))


EPI_ROWS = 256


def _in_c_conv_prompt_body(h_ref, w_ref, cw_ref, o_ref, st_ref, w_bf, buf, *, is_qk):
    j = pl.program_id(0)
    t = pl.program_id(2)
    tm = h_ref.shape[1]
    tn = w_ref.shape[1]

    @pl.when((pl.program_id(1) == 0) & (t == 0))
    def _():
        w_bf[...] = w_ref[0].astype(BF16)

    @pl.when(t == 0)
    def _():
        buf[0:HIST, :] = jnp.zeros((HIST, tn), F32)

    cw = cw_ref[0]
    q_scale = jnp.where(j < C_KEY_DIM // tn, C_HEAD ** -0.5, 1.0).astype(F32)
    rc = min(EPI_ROWS, tm)
    for r0 in range(0, tm, rc):
        buf[HIST + r0:HIST + r0 + rc, :] = lax.dot_general(h_ref[0, r0:r0 + rc, :], w_bf[...], _NT,
                                                           preferred_element_type=F32)
        y = buf[HIST + r0:HIST + r0 + rc, :] * cw[3:4]
        for jj in range(3):
            y = y + buf[HIST - 3 + jj + r0:HIST - 3 + jj + r0 + rc, :] * cw[jj:jj + 1]
        o_ref[0, r0:r0 + rc, :] = _gdn_act(y, is_qk, q_scale)
    st_ref[0] = buf[HIST + tm - 3:HIST + tm, :]
    buf[0:HIST, :] = buf[tm:tm + HIST, :]


def _in_c_conv_prompt(h, w_t, conv_w, i, *, tm, tn, col0, n_cols, is_qk):
    B, T, d = h.shape
    assert col0 % tn == 0 and n_cols % tn == 0
    return pl.pallas_call(
        functools.partial(_in_c_conv_prompt_body, is_qk=is_qk),
        grid=(n_cols // tn, B, T // tm),
        in_specs=[pl.BlockSpec((1, tm, d), lambda j, b, t: (b, t, 0)),
                  pl.BlockSpec((pl.Element(1), pl.Element(tn), pl.Element(d)),
                               lambda j, b, t: (i, pl.multiple_of(col0 + j * tn, SUBLANES), 0)),
                  pl.BlockSpec((1, 4, tn), lambda j, b, t: (i, 0, col0 // tn + j))],
        out_specs=[pl.BlockSpec((1, tm, tn), lambda j, b, t: (b, t, j)),
                   pl.BlockSpec((1, 3, tn), lambda j, b, t: (b, 0, j))],
        out_shape=[jax.ShapeDtypeStruct((B, T, n_cols), F32),
                   jax.ShapeDtypeStruct((B, 3, n_cols), F32)],
        scratch_shapes=[pltpu.VMEM((tn, d), BF16), pltpu.VMEM((HIST + tm, tn), F32)],
        compiler_params=_cparams(3),
        name="in_c_conv_prompt",
    )(h, w_t, conv_w)


def _gdn_gates(ba, a_log, dt_bias):
    hv = C_V_HEADS
    beta = jax.nn.sigmoid(ba[..., :hv])
    g = -jnp.exp(a_log) * jax.nn.softplus(ba[..., hv:2 * hv] + dt_bias)
    return beta, g


def _lane_bcast_col(x, col):
    n = x.shape[1]
    onehot = _iota2((n, LANES), 0) == col
    return _sel_r(x, onehot)


def _gdn_prompt_body(q_ref, k_ref, v_ref, z_ref, ba_ref, alog_ref, dtb_ref, nw_ref, o_ref, st_ref, s_scr):
    hblk = pl.program_id(1)
    t = pl.program_id(2)
    tt = q_ref.shape[1]
    C = CHUNK
    nc = tt // C
    rep = C_V_HEADS // C_K_HEADS
    n_kh = q_ref.shape[2] // C_HEAD
    nh = n_kh * rep

    @pl.when(t == 0)
    def _():
        s_scr[...] = jnp.zeros(s_scr.shape, F32)

    beta_all, g_all = _gdn_gates(ba_ref[0], alog_ref[...], dtb_ref[...])
    gc_all = _sel_l(_block_tri(tt, C), g_all)
    blocks = lambda x: x.reshape(nc, C, x.shape[-1])
    qs, ks, vs, betas, gcs = [], [], [], [], []
    for kh in range(n_kh):
        q_h = blocks(q_ref[0, :, kh * C_HEAD:(kh + 1) * C_HEAD])
        k_h = blocks(k_ref[0, :, kh * C_HEAD:(kh + 1) * C_HEAD])
        for jv in range(rep):
            hu = kh * rep + jv
            hv = hblk * nh + hu
            qs.append(q_h)
            ks.append(k_h)
            vs.append(blocks(v_ref[0, :, hu * C_HEAD:(hu + 1) * C_HEAD]))
            betas.append(blocks(_lane_bcast_col(beta_all, hv)))
            gcs.append(blocks(_lane_bcast_col(gc_all, hv)))
    stack = lambda xs: jnp.stack(xs, axis=1).reshape(nc * nh, C, xs[0].shape[-1])
    q, k, v, beta, gc = map(stack, (qs, ks, vs, betas, gcs))
    incl = _tri(C, False)
    strict = _tri(C, True)

    gcc = gc[:, :, :C]
    diff = gcc - jnp.swapaxes(gcc, 1, 2)
    dec_incl = jnp.exp(jnp.where(incl, diff, -jnp.inf))
    dec_strict = jnp.where(strict, dec_incl, 0.0)
    kb = k * beta
    gram = _bdot(jnp.concatenate([kb, q], axis=1), k, _BNT)
    low = gram[:, :C] * dec_strict
    qk = gram[:, C:] * dec_incl
    tinv = _unit_lower_inverse_refined(low)
    eg = jnp.exp(gc)
    sol = _bdot(tinv, jnp.concatenate([v * beta, kb * eg], axis=2), _BNN)
    u = sol[:, :, :C_HEAD]
    lhs_s = jnp.concatenate([sol[:, :, C_HEAD:], q * eg], axis=1)
    g_last = gc[:, C - 1:C, :]
    k_tail = k * jnp.exp(g_last - gc)
    e_last = jnp.exp(g_last)

    S = s_scr[...]
    os_ = []
    for c in range(nc):
        sl = slice(c * nh, (c + 1) * nh)
        ws = _bdot(lhs_s[sl], S, _BNN)
        v_new = u[sl] - ws[:, :C]
        os_.append(ws[:, C:] + _bdot(qk[sl], v_new, _BNN))
        S = S * e_last[sl] + _bdot(k_tail[sl], v_new, _BTN)
    s_scr[...] = S

    outs = []
    for hu in range(nh):
        cols = slice(hu * C_HEAD, (hu + 1) * C_HEAD)
        o = jnp.concatenate([os_[c][hu] for c in range(nc)], axis=0)
        outs.append(_gdn_head_out(o, z_ref[0, :, cols], nw_ref[...]))
    o_ref[0] = jnp.concatenate(outs, axis=1).astype(o_ref.dtype)

    @pl.when(t == pl.num_programs(2) - 1)
    def _():
        st_ref[0] = S


def _gdn_prompt(qk_act, v_act, z, ba, prm, i, *, tt, kheads_per_step):
    B, T, _ = qk_act.shape
    rep = C_V_HEADS // C_K_HEADS
    kw = kheads_per_step * C_HEAD
    vw = rep * kw
    k_blk0 = C_KEY_DIM // kw
    v_blk0 = 0
    z_blk0 = 0
    nh = kheads_per_step * rep
    full = lambda shape: pl.BlockSpec(shape, lambda b, h, t: (0,) * len(shape))
    return pl.pallas_call(
        _gdn_prompt_body,
        grid=(B, C_K_HEADS // kheads_per_step, T // tt),
        in_specs=[pl.BlockSpec((1, tt, kw), lambda b, h, t: (b, t, h)),
                  pl.BlockSpec((1, tt, kw), lambda b, h, t: (b, t, k_blk0 + h)),
                  pl.BlockSpec((1, tt, vw), lambda b, h, t: (b, t, v_blk0 + h)),
                  pl.BlockSpec((1, tt, vw), lambda b, h, t: (b, t, z_blk0 + h)),
                  pl.BlockSpec((1, tt, 2 * C_V_HEADS), lambda b, h, t: (b, t, 0)),
                  full((1, C_V_HEADS)), full((1, C_V_HEADS)), full((1, C_HEAD))],
        out_specs=[pl.BlockSpec((1, tt, vw), lambda b, h, t: (b, t, h)),
                   pl.BlockSpec((1, nh, C_HEAD, C_HEAD), lambda b, h, t: (b, h, 0, 0))],
        out_shape=[jax.ShapeDtypeStruct((B, T, C_VAL_DIM), BF16),
                   jax.ShapeDtypeStruct((B, C_V_HEADS, C_HEAD, C_HEAD), F32)],
        scratch_shapes=[pltpu.VMEM((nh, C_HEAD, C_HEAD), F32)],
        compiler_params=_cparams(3),
        name="gdn_prompt",
    )(qk_act, qk_act, v_act, z, ba, _row(prm['gdn_a_log'][i]), _row(prm['gdn_dt_bias'][i]),
      _row(prm['gdn_norm_w'][i]))


def _rwkv_prep_sample_body(pm_ref, pl_ref, qm_ref, ql_ref, mum_ref, mul_ref, w0_ref, w2_ref, a0_ref, a2_ref,
                           g2_ref, kk_ref, ka_ref, r_o, kf_o, v_o, kkn_o, a_o, lw_o, g_o):
    outs = _rwkv_rows(pm_ref[...], qm_ref[...], pl_ref[...][:, :LORA_ALL], ql_ref[...], mum_ref[...],
                      mul_ref[...], w0_ref[...], w2_ref[0], a0_ref[...], a2_ref[0], g2_ref[0],
                      kk_ref[...], ka_ref[...])
    for o_ref, val in zip((r_o, kf_o, v_o, kkn_o, a_o, lw_o, g_o), outs):
        o_ref[...] = val.T


def _rwkv_prep_sample(p_main, p_lora, shift, prm, i):
    bs = p_main.shape[0]
    mu = prm['rwkv_mu'][i]
    full = lambda shape: pl.BlockSpec(shape, lambda s: (0,) * len(shape))
    lay3 = lambda shape: pl.BlockSpec((1,) + shape, lambda s: (i, 0, 0))
    return pl.pallas_call(
        _rwkv_prep_sample_body,
        grid=(1,),
        in_specs=[full(p_main.shape), full(p_lora.shape), full((bs, P_A_MAIN)), full((bs, LORA_ALL)),
                  full((1, P_A_MAIN)), full((1, LORA_ALL)), full((1, A_WIDTH)),
                  lay3((LORA_W, A_WIDTH)), full((1, A_WIDTH)), lay3((LORA_A, A_WIDTH)),
                  lay3((LORA_G, A_WIDTH)), full((1, A_WIDTH)), full((1, A_WIDTH))],
        out_specs=[full((A_WIDTH, bs))] * 7,
        out_shape=[jax.ShapeDtypeStruct((A_WIDTH, bs), F32)] * 7,
        compiler_params=_cparams(1),
        name="rwkv_prep_sample",
    )(p_main, p_lora, shift[:, :P_A_MAIN], shift[:, P_A_MAIN:], _row(mu[:P_A_MAIN]), _row(mu[P_A_MAIN:]),
      _row(prm['rwkv_w0'][i]), prm['rwkv_w2'], _row(prm['rwkv_a0'][i]), prm['rwkv_a2'], prm['rwkv_g2'],
      _row(prm['rwkv_k_k'][i]), _row(prm['rwkv_k_a'][i]))


STEP_GROUP = 16


def _wkv_step_body(*refs, n_state):
    s_ref = refs[0]
    r_ref, kf_ref, v_ref, kkn_ref, a_ref, lw_ref, g_ref, rk_ref, gw_ref, gb_ref, ns_ref, y_ref = refs[n_state:]
    n = A_HEAD_DIM
    outs = []
    for h in range(s_ref.shape[0]):
        rows = slice(h * n, (h + 1) * n)
        ld = lambda ref: ref[rows, :]
        r, kf, v, kkn, a = ld(r_ref), ld(kf_ref), ld(v_ref), ld(kkn_ref), ld(a_ref)
        kk = kkn * lax.rsqrt(jnp.sum(kkn * kkn, axis=0, keepdims=True) + 1e-12)
        S = s_ref[h]
        u = jnp.sum(S * kk[None], axis=1)
        s_new = S * jnp.exp(ld(lw_ref))[None] - u[:, None, :] * (kk * a)[None] + v[:, None, :] * kf[None]
        ns_ref[h] = s_new
        y = jnp.sum(s_new * r[None], axis=1)
        mean = jnp.mean(y, axis=0, keepdims=True)
        var = jnp.mean(jnp.square(y - mean), axis=0, keepdims=True)
        yn = (y - mean) * lax.rsqrt(var + A_GN_EPS) * ld(gw_ref) + ld(gb_ref)
        bonus = jnp.sum(r * kf * ld(rk_ref), axis=0, keepdims=True) * v
        outs.append(((yn + bonus) * ld(g_ref)).T)
    y_ref[...] = jnp.concatenate(outs, axis=1).astype(y_ref.dtype)


def _layer_state_io(state_all, new_all, spec, out_index=0):
    if new_all is None:
        return [spec], [state_all], {}
    return [spec, pl.BlockSpec(memory_space=pl.ANY)], [state_all, new_all], {1: out_index}


def _wkv_step(S_all, new_all, vals, prm, i, *, heads_per_step):
    _, n_heads, n, _, bs = S_all.shape
    wblk = heads_per_step * n
    lanes = lambda p: jnp.broadcast_to(p.reshape(A_WIDTH, 1), (A_WIDTH, bs))
    row_spec = pl.BlockSpec((wblk, bs), lambda h: (h, 0))
    s_spec = pl.BlockSpec((None, heads_per_step, n, n, bs), lambda h: (i, h, 0, 0, 0))
    st_specs, st_args, aliases = _layer_state_io(S_all, new_all, s_spec)
    return pl.pallas_call(
        functools.partial(_wkv_step_body, n_state=len(st_args)),
        grid=(n_heads // heads_per_step,),
        in_specs=st_specs + [row_spec] * 10,
        out_specs=[s_spec, pl.BlockSpec((bs, wblk), lambda h: (0, h))],
        out_shape=[jax.ShapeDtypeStruct(S_all.shape, F32), jax.ShapeDtypeStruct((bs, A_WIDTH), BF16)],
        input_output_aliases=aliases,
        compiler_params=_cparams(1),
        name="wkv_step",
    )(*st_args, *vals, lanes(prm['rwkv_r_k'][i]), lanes(prm['rwkv_gn_w'][i]), lanes(prm['rwkv_gn_b'][i]))


def _lru_sample_body(gate_ref, xb_ref, cst_ref, h0_ref, cw_ref, cb_ref, wa_ref, wx_ref, ba_ref, bx_ref, lam_ref,
                     y_ref, h_ref, ncst_ref):
    cw = cw_ref[0]
    xb = xb_ref[...]
    xc = cb_ref[...] + xb * cw[3:4]
    for j in range(3):
        xc = xc + cst_ref[:, j, :] * cw[j:j + 1]
    a, b, gact = _lru_rows(gate_ref[...], xc, wa_ref, wx_ref, ba_ref[...], bx_ref[...], lam_ref[...])
    h = a * h0_ref[...] + b
    y_ref[...] = (h * gact).astype(y_ref.dtype)
    h_ref[...] = h
    ncst_ref[:, 0, :] = cst_ref[:, 1, :]
    ncst_ref[:, 1, :] = cst_ref[:, 2, :]
    ncst_ref[:, 2, :] = xb


def _lru_sample(pb, conv_state, h0, prm, i):
    bs = pb.shape[0]
    w = B_WIDTH
    ng = w // LANES
    full = lambda shape: pl.BlockSpec(shape, lambda s: (0,) * len(shape))
    return pl.pallas_call(
        _lru_sample_body,
        grid=(1,),
        in_specs=[pl.BlockSpec((bs, w), lambda s: (0, 0)), pl.BlockSpec((bs, w), lambda s: (0, 1)),
                  full((bs, 3, w)), full((bs, w)),
                  pl.BlockSpec((1, 4, w), lambda s: (i, 0, 0)),
                  full((1, w)), full((ng, LANES, LANES)), full((ng, LANES, LANES)),
                  full((1, w)), full((1, w)), full((1, w))],
        out_specs=[full((bs, w)), full((bs, w)), full((bs, 3, w))],
        out_shape=[jax.ShapeDtypeStruct((bs, w), BF16), jax.ShapeDtypeStruct((bs, w), F32),
                   jax.ShapeDtypeStruct((bs, 3, w), F32)],
        compiler_params=_cparams(1),
        name="lru_sample",
    )(pb, pb, conv_state, h0, prm['lru_conv_w'], _row(prm['lru_conv_b'][i]),
      _block_diag_pairs(prm['lru_wa'][i]), _block_diag_pairs(prm['lru_wx'][i]), _row(prm['lru_ba'][i]),
      _row(prm['lru_bx'][i]), _row(prm['lru_lambda'][i]))


def _gdn_prep_sample_body(*refs, n_state):
    cst_ref = refs[0]
    x_ref, cw_ref, o_ref, ncst_ref = refs[n_state:]
    j = pl.program_id(0)
    cw = cw_ref[0]
    x = x_ref[...]
    y = x * cw[3:4]
    for jj in range(3):
        y = y + cst_ref[jj] * cw[jj:jj + 1]
    n_qk_blocks = 2 * C_KEY_DIM // x.shape[1]
    q_scale = jnp.where(j < n_qk_blocks // 2, C_HEAD ** -0.5, 1.0).astype(F32)
    o_ref[...] = _gdn_act(y, j < n_qk_blocks, q_scale)
    ncst_ref[0] = cst_ref[1]
    ncst_ref[1] = cst_ref[2]
    ncst_ref[2] = x


def _gdn_prep_sample(qkvz, conv_all, new_all, conv_w, i, *, tc):
    bs = qkvz.shape[0]
    s_spec = pl.BlockSpec((None, 3, bs, tc), lambda j: (i, 0, 0, j))
    st_specs, st_args, aliases = _layer_state_io(conv_all, new_all, s_spec, out_index=1)
    return pl.pallas_call(
        functools.partial(_gdn_prep_sample_body, n_state=len(st_args)),
        grid=(C_QKV_DIM // tc,),
        in_specs=st_specs + [pl.BlockSpec((bs, tc), lambda j: (0, j)),
                             pl.BlockSpec((1, 4, tc), lambda j: (i, 0, j))],
        out_specs=[pl.BlockSpec((bs, tc), lambda j: (0, j)), s_spec],
        out_shape=[jax.ShapeDtypeStruct((bs, C_QKV_DIM), F32),
                   jax.ShapeDtypeStruct(conv_all.shape, F32)],
        input_output_aliases=aliases,
        compiler_params=_cparams(1),
        name="gdn_prep_sample",
    )(*st_args, qkvz, conv_w)


def _gdn_step_body(*refs, n_state):
    s_ref = refs[0]
    q_ref, k_ref, v_ref, z_ref, ba_ref, alog_ref, dtb_ref, nw_ref, ns_ref, o_ref = refs[n_state:]
    bb = s_ref.shape[0]
    rep = C_V_HEADS // C_K_HEADS
    hk = pl.program_id(1)
    beta_all, g_all = _gdn_gates(ba_ref[...], alog_ref[...], dtb_ref[...])
    lane = lax.broadcasted_iota(jnp.int32, beta_all.shape, 2)
    pick = lambda x, col: jnp.sum(jnp.where(lane == col, x, 0.0), axis=-1, keepdims=True)
    rid = lax.broadcasted_iota(jnp.int32, (bb, SUBLANES, C_HEAD), 1)
    up = lambda x: jnp.broadcast_to(x, (bb, SUBLANES, C_HEAD))
    q = q_ref[...]
    k = k_ref[...]
    qk = jnp.sum(q * k, axis=-1, keepdims=True)
    for jv in range(rep):
        cols = slice(jv * C_HEAD, (jv + 1) * C_HEAD)
        hv = hk * rep + jv
        beta = pick(beta_all, hv)
        eg = jnp.exp(pick(g_all, hv))
        v = v_ref[:, :, cols]
        S = s_ref[:, jv]
        kb = k * beta
        lhs = jnp.where(rid == 0, up(kb * eg), jnp.where(rid == 1, up(q * eg), 0.0))
        ws = _bdot(lhs, S, _BNN)
        v_new = v * beta - ws[:, 0:1]
        o = ws[:, 1:2] + qk * v_new
        left = jnp.where(rid == 0, up(k), 0.0)
        right = jnp.where(rid == 0, up(v_new), 0.0)
        ns_ref[:, jv] = S * eg + _bdot(left, right, _BTN)
        o_ref[:, :, cols] = _gdn_head_out(o, z_ref[:, :, cols], nw_ref[...]).astype(o_ref.dtype)


def _gdn_step(S_all, new_all, qkv_act, qkvz, ba, prm, i, *, bb):
    bs = S_all.shape[1]
    rep = C_V_HEADS // C_K_HEADS
    vw = rep * C_HEAD
    v_blk0 = 2 * C_KEY_DIM // vw
    z_blk0 = C_QKV_DIM // vw
    full = lambda shape: pl.BlockSpec(shape, lambda b, h: (0,) * len(shape))
    s_spec = pl.BlockSpec((None, bb, rep, C_HEAD, C_HEAD), lambda b, h: (i, b, h, 0, 0))
    st_specs, st_args, aliases = _layer_state_io(S_all, new_all, s_spec)
    return pl.pallas_call(
        functools.partial(_gdn_step_body, n_state=len(st_args)),
        grid=(bs // bb, C_K_HEADS),
        in_specs=st_specs + [
                  pl.BlockSpec((bb, 1, C_HEAD), lambda b, h: (b, 0, h)),
                  pl.BlockSpec((bb, 1, C_HEAD), lambda b, h: (b, 0, C_K_HEADS + h)),
                  pl.BlockSpec((bb, 1, vw), lambda b, h: (b, 0, v_blk0 + h)),
                  pl.BlockSpec((bb, 1, vw), lambda b, h: (b, 0, z_blk0 + h)),
                  pl.BlockSpec((bb, 1, 2 * C_V_HEADS), lambda b, h: (b, 0, 0)),
                  full((1, C_V_HEADS)), full((1, C_V_HEADS)), full((1, C_HEAD))],
        out_specs=[s_spec, pl.BlockSpec((bb, 1, vw), lambda b, h: (b, 0, h))],
        out_shape=[jax.ShapeDtypeStruct(S_all.shape, F32), jax.ShapeDtypeStruct((bs, 1, C_VAL_DIM), BF16)],
        input_output_aliases=aliases,
        compiler_params=_cparams(2),
        name="gdn_step",
    )(*st_args, qkv_act, qkv_act, qkv_act, qkvz, ba, _row(prm['gdn_a_log'][i]), _row(prm['gdn_dt_bias'][i]),
      _row(prm['gdn_norm_w'][i]))


MOD_SH1, MOD_SC1, MOD_GT1, MOD_SH2, MOD_SC2, MOD_GT2 = range(6)


def _in_proj_ab(h, prm, i):
    w = [(prm['w_in_ab_t'], i, 0, True)]
    p_main = _mm([h], w, n_out=P_A_MAIN, name="in_ab_main")
    p_lora = _mm([h], w, n_out=LORA_ALL, col0=P_A_MAIN, tn=LORA_ALL, name="in_ab_lora")
    pb = _mm([h], w, n_out=2 * B_WIDTH, col0=P_A, name="in_ab_lru")
    return p_main, p_lora, pb


def _in_proj_c_gates(h, prm, i):
    return _mm([h], [(prm['w_in_c_t'], i, 0, True)], n_out=2 * C_V_HEADS, col0=C_QKV_DIM + C_VAL_DIM,
               tn=2 * C_V_HEADS, name="in_c_gates")


def _ffn_down(act, prm, layer, x, mod):
    return _mm([act], [(prm['ffn_w_down'], layer, 0, False)], n_out=D_MODEL, res=x, gate=mod, gate_blk=MOD_GT2,
               name="ffn_down")


def _out_proj_ab(ya, yb, prm, i, x, mod):
    w = prm['w_out_ab']
    return _mm([ya, yb], [(w, i, 0, False), (w, i, 1, False)], n_out=D_MODEL, res=x, gate=mod, gate_blk=MOD_GT1,
               name="out_ab")


def _out_proj_c(yc, prm, i, x, mod):
    return _mm([yc], [(prm['w_out_c'], i, 0, False)], n_out=D_MODEL, res=x, gate=mod, gate_blk=MOD_GT1,
               name="out_c")


def _trunk_prompt(x, mods, prm):
    B, T, _ = x.shape
    tm, tt, t_norm = min(1024, T), min(256, T), min(512, T)
    wkv, shift, lru_h, lru_conv, gdn, gdn_conv, ffn_conv = [], [], [], [], [], [], []
    for layer in range(DEPTH):
        mod = mods[layer].reshape(B, 1, 6 * D_MODEL)
        i = layer // 2
        h = _norm(x, prm['norm_mix'][layer], mod, MOD_SC1, MOD_SH1, tt=t_norm)
        if layer % 2 == 0:
            p_main, p_lora, pb = _in_proj_ab(h, prm, i)
            vals = _rwkv_prep_prompt(p_main, p_lora, prm, i, tt=tt)
            ya, s_new = _wkv_prompt(vals, prm, i, tt=tt, heads_per_step=A_HEADS)
            yb, h_last, cst = _lru_prompt(pb, prm, i, tt=tt)
            x = _out_proj_ab(ya, yb, prm, i, x, mod)
            wkv.append(s_new)
            shift.append(jnp.concatenate([p_main[:, -1], p_lora[:, -1, :LORA_ALL]], axis=-1))
            lru_h.append(h_last[:, 0])
            lru_conv.append(cst)
        else:
            qk_act, cst_qk = _in_c_conv_prompt(h, prm['w_in_c_t'], prm['gdn_conv_w'], i, tm=tm, tn=512, col0=0,
                                               n_cols=2 * C_KEY_DIM, is_qk=True)
            v_act, cst_v = _in_c_conv_prompt(h, prm['w_in_c_t'], prm['gdn_conv_w'], i, tm=tm, tn=512,
                                             col0=2 * C_KEY_DIM, n_cols=C_VAL_DIM, is_qk=False)
            cst = jnp.concatenate([cst_qk, cst_v], axis=-1)
            z = _mm([h], [(prm['w_in_c_t'], i, 0, True)], n_out=C_VAL_DIM, col0=C_QKV_DIM, name="in_c_z")
            ba = _in_proj_c_gates(h, prm, i)
            yc, s_new = _gdn_prompt(qk_act, v_act, z, ba, prm, i, tt=min(2 * CHUNK, T), kheads_per_step=C_K_HEADS // 2)
            x = _out_proj_c(yc, prm, i, x, mod)
            gdn.append(s_new)
            gdn_conv.append(cst)
        h = _norm(x, prm['norm_ffn'][layer], mod, MOD_SC2, MOD_SH2, tt=t_norm)
        act, fst = _ffn_up_prompt(h, prm['ffn_w_gate'], prm['ffn_w_up'], prm['ffn_conv_w'], prm['ffn_conv_b'],
                                  layer, tm=tm, tn=512)
        x = _ffn_down(act, prm, layer, x, mod)
        ffn_conv.append(fst)
    y = _norm(x, prm['norm_out'], tt=t_norm, out_dtype=F32)
    stk = jnp.stack
    return y, stk(wkv), stk(shift), stk(lru_h), stk(lru_conv), stk(gdn), stk(gdn_conv), stk(ffn_conv)


def _trunk_sample(x, mods, st, prm):
    bs = x.shape[0]
    x = x.reshape(1, bs, D_MODEL)
    wkv0, shift0, lru_h0, lru_conv0, gdn0, gdn_conv0, ffn_conv0 = st
    wkv_new = gdn_new = gconv_new = fconv_new = None
    gconv0_t = jnp.transpose(gdn_conv0, (0, 2, 1, 3))
    wkv0_t = jnp.transpose(wkv0, (0, 2, 3, 4, 1))
    shift, lru_h, lru_conv = [], [], []
    for layer in range(DEPTH):
        mod = mods[layer].reshape(1, bs, 6 * D_MODEL)
        i = layer // 2
        h = _norm(x, prm['norm_mix'][layer], mod, MOD_SC1, MOD_SH1, tt=bs)
        if layer % 2 == 0:
            p_main, p_lora, pb = _in_proj_ab(h, prm, i)
            p_main, p_lora, pb = p_main[0], p_lora[0], pb[0]
            vals = _rwkv_prep_sample(p_main, p_lora, shift0[i], prm, i)
            wkv_new, ya = _wkv_step(wkv0_t, wkv_new, vals, prm, i, heads_per_step=2)
            yb, h_last, cst = _lru_sample(pb, lru_conv0[i], lru_h0[i], prm, i)
            x = _out_proj_ab(ya[None], yb[None], prm, i, x, mod)
            shift.append(jnp.concatenate([p_main, p_lora[:, :LORA_ALL]], axis=-1))
            lru_h.append(h_last)
            lru_conv.append(cst)
        else:
            qkvz = _mm([h], [(prm['w_in_c_t'], i, 0, True)], n_out=C_QKV_DIM + C_VAL_DIM, name="in_c_main")
            ba = _in_proj_c_gates(h, prm, i)
            qkv_act, gconv_new = _gdn_prep_sample(qkvz[0], gconv0_t, gconv_new, prm['gdn_conv_w'], i, tc=2048)
            gdn_new, yc = _gdn_step(gdn0, gdn_new, qkv_act[:, None], qkvz[0][:, None], ba[0][:, None], prm, i,
                                    bb=min(STEP_GROUP, bs))
            x = _out_proj_c(yc.reshape(1, bs, C_VAL_DIM), prm, i, x, mod)
        h = _norm(x, prm['norm_ffn'][layer], mod, MOD_SC2, MOD_SH2, tt=bs)
        act, fconv_new = _ffn_up_sample(h[0], prm['ffn_w_gate'], prm['ffn_w_up'], prm['ffn_conv_w'],
                                        prm['ffn_conv_b'], ffn_conv0, fconv_new, layer, tn=512)
        x = _ffn_down(act[None], prm, layer, x, mod)
    y = _norm(x, prm['norm_out'], tt=bs, out_dtype=F32).reshape(bs, 1, D_MODEL)
    stk = jnp.stack
    return (y, jnp.transpose(wkv_new, (0, 4, 1, 2, 3)), stk(shift), stk(lru_h), stk(lru_conv), gdn_new,
            jnp.transpose(gconv_new, (0, 2, 1, 3)), fconv_new)


def kernel(x_prompt, x_sample, c_prompt, c_sample, state_rwkv_wkv, state_rwkv_shift, state_lru_h, state_lru_conv, state_gdn, state_gdn_conv, state_ffn_conv, w_mod, b_mod, norm_mix, norm_ffn, norm_out, w_in_ab, w_out_ab, rwkv_mu, rwkv_w0, rwkv_w2, rwkv_a0, rwkv_a2, rwkv_g2, rwkv_k_k, rwkv_k_a, rwkv_r_k, rwkv_gn_w, rwkv_gn_b, lru_conv_w, lru_conv_b, lru_wa, lru_ba, lru_wx, lru_bx, lru_lambda, w_in_c, w_out_c, gdn_conv_w, gdn_a_log, gdn_dt_bias, gdn_norm_w, ffn_w_gate, ffn_w_up, ffn_conv_w, ffn_conv_b, ffn_w_down):
    prm = dict(norm_mix=norm_mix, norm_ffn=norm_ffn, norm_out=norm_out, w_out_ab=w_out_ab,
               w_in_ab_t=jnp.swapaxes(w_in_ab, 1, 2), w_in_c_t=jnp.swapaxes(w_in_c, 1, 2),
               rwkv_mu=rwkv_mu, rwkv_w0=rwkv_w0, rwkv_w2=rwkv_w2, rwkv_a0=rwkv_a0, rwkv_a2=rwkv_a2,
               rwkv_g2=rwkv_g2, rwkv_k_k=rwkv_k_k, rwkv_k_a=rwkv_k_a,
               rwkv_r_k=rwkv_r_k.reshape(rwkv_r_k.shape[0], A_WIDTH), rwkv_gn_w=rwkv_gn_w,
               rwkv_gn_b=rwkv_gn_b, lru_conv_w=lru_conv_w, lru_conv_b=lru_conv_b, lru_wa=lru_wa, lru_ba=lru_ba,
               lru_wx=lru_wx, lru_bx=lru_bx, lru_lambda=lru_lambda, w_out_c=w_out_c,
               gdn_conv_w=gdn_conv_w, gdn_a_log=gdn_a_log, gdn_dt_bias=gdn_dt_bias, gdn_norm_w=gdn_norm_w,
               ffn_w_gate=ffn_w_gate, ffn_w_up=ffn_w_up, ffn_conv_w=ffn_conv_w, ffn_conv_b=ffn_conv_b,
               ffn_w_down=ffn_w_down)
    mods_p, mods_s = _modulation(c_prompt, c_sample, w_mod, b_mod)
    outs_p = _trunk_prompt(x_prompt, mods_p, prm)
    outs_s = _trunk_sample(x_sample, mods_s,
                           (state_rwkv_wkv, state_rwkv_shift, state_lru_h, state_lru_conv, state_gdn,
                            state_gdn_conv, state_ffn_conv), prm)
    return (outs_p[0], outs_s[0]) + tuple(outs_p[1:]) + tuple(outs_s[1:])
```

```python
import functools

import jax
import jax.numpy as jnp
from jax import lax
from jax.experimental import pallas as pl
from jax.experimental.pallas import tpu as pltpu

F32 = jnp.float32
BF16 = jnp.bfloat16

D_MODEL = 2048
DEPTH = 4
A_HEAD_DIM = 64
A_WIDTH = D_MODEL // 2
A_HEADS = A_WIDTH // A_HEAD_DIM
LORA_W = 64
LORA_A = 64
LORA_G = 160
LORA_ALL = LORA_W + LORA_A + LORA_G
P_A_MAIN = 3 * A_WIDTH
P_A = P_A_MAIN + LORA_ALL
A_GN_EPS = 64e-5
B_WIDTH = D_MODEL - A_WIDTH
B_BLOCK_DIM = 64
LRU_C = 8.0
C_HEAD = 128
C_K_HEADS = D_MODEL // C_HEAD
C_V_HEADS = 2 * C_K_HEADS
C_KEY_DIM = C_K_HEADS * C_HEAD
C_VAL_DIM = C_V_HEADS * C_HEAD
C_QKV_DIM = 2 * C_KEY_DIM + C_VAL_DIM
D_FF = 5632
NORM_EPS = 1e-6

LANES = 128
SUBLANES = 8
MXU_DEPTH = 256
HIST = SUBLANES
CHUNK = 64
VMEM_LIMIT = 56 * 1024 * 1024
MM_VMEM_BUDGET = 46 * 1024 * 1024

_NN = (((1,), (0,)), ((), ()))
_NT = (((1,), (1,)), ((), ()))
_TN = (((0,), (0,)), ((), ()))
_BNN = (((2,), (1,)), ((0,), (0,)))
_BNT = (((2,), (2,)), ((0,), (0,)))
_BTN = (((1,), (1,)), ((0,), (0,)))


def _cparams(n_axes):
    return pltpu.CompilerParams(dimension_semantics=("arbitrary",) * n_axes,
                                vmem_limit_bytes=VMEM_LIMIT)


def _bdot(a, b, dims=_NN):
    return lax.dot_general(a.astype(BF16), b.astype(BF16), dims, preferred_element_type=F32)


def _split2(a):
    hi = a.astype(BF16)
    lo = (a - hi.astype(F32)).astype(BF16)
    return hi, lo


def _dot3(a, b, dims=_NN):
    ah, al = _split2(a)
    bh, bl = _split2(b)
    d = lambda x, y: lax.dot_general(x, y, dims, preferred_element_type=F32)
    (lc,), (rc,) = dims[0]
    if 3 * a.shape[lc] <= MXU_DEPTH:
        return d(jnp.concatenate([ah, ah, al], axis=lc), jnp.concatenate([bh, bl, bh], axis=rc))
    return d(ah, bh) + (d(ah, bl) + d(al, bh))


def _split3(a):
    h0 = a.astype(BF16)
    r1 = a - h0.astype(F32)
    h1 = r1.astype(BF16)
    h2 = (r1 - h1.astype(F32)).astype(BF16)
    return h0, h1, h2


def _sel_l(mask01, x, dims=_NN):
    m = mask01.astype(BF16)
    d = lambda y: lax.dot_general(m, y, dims, preferred_element_type=F32)
    h0, h1, h2 = _split3(x)
    return d(h0) + (d(h1) + d(h2))


def _sel_r(x, mask01):
    m = mask01.astype(BF16)
    d = lambda y, w: lax.dot_general(y, w, _NN, preferred_element_type=F32)
    h0, h1, h2 = _split3(x)
    if 3 * x.shape[1] <= MXU_DEPTH:
        return d(jnp.concatenate([h0, h1, h2], axis=1), jnp.concatenate([m, m, m], axis=0))
    return d(h0, m) + (d(h1, m) + d(h2, m))


def _iota2(shape, axis):
    return lax.broadcasted_iota(jnp.int32, shape, axis)


def _tri(n, strict):
    i = _iota2((n, n), 0)
    j = _iota2((n, n), 1)
    return (i > j) if strict else (i >= j)


def _unit_lower_inverse(low, mm):
    n = low.shape[-1]
    dims = _BNN if low.ndim == 3 else _NN
    eye = (_iota2((n, n), 0) == _iota2((n, n), 1)).astype(F32)
    m = -low
    inv = eye + m
    p = m
    span = 2
    while span < n:
        p = mm(p, p, dims)
        inv = inv + mm(inv, p, dims)
        span *= 2
    return inv


def _unit_lower_inverse_refined(low):
    n = low.shape[-1]
    dims = _BNN if low.ndim == 3 else _NN
    eye = (_iota2((n, n), 0) == _iota2((n, n), 1)).astype(F32)
    x0 = _unit_lower_inverse(low, _bdot)
    resid = eye - _dot3(eye + low, x0, dims)
    return x0 + _bdot(x0, resid, dims)


def _block_tri(tt, block):
    i = _iota2((tt, tt), 0)
    j = _iota2((tt, tt), 1)
    return (i // block == j // block) & (i >= j)


def _neg_expm1(x):
    return -jnp.tanh(0.5 * x) * (jnp.exp(x) + 1.0)


def _rwkv_rows(p_main, q_main, p_lora, q_lora, mu_main, mu_lora, w0, w2, a0, a2, g2, k_k, k_a):
    pm = p_main + (q_main - p_main) * mu_main
    pl_ = p_lora + (q_lora - p_lora) * mu_lora
    r = pm[:, :A_WIDTH]
    k = pm[:, A_WIDTH:2 * A_WIDTH]
    v = pm[:, 2 * A_WIDTH:]
    xw = pl_[:, :LORA_W]
    xa = pl_[:, LORA_W:LORA_W + LORA_A]
    xg = pl_[:, LORA_W + LORA_A:LORA_ALL]
    w_ll = -jax.nn.softplus(-(w0 + _bdot(jnp.tanh(xw), w2))) - 0.5
    lw = -jnp.exp(w_ll)
    a = jax.nn.sigmoid(a0 + _bdot(xa, a2))
    g = _bdot(jax.nn.sigmoid(xg), g2)
    kkn = k * k_k
    kf = k * (1.0 + (a - 1.0) * k_a)
    return r, kf, v, kkn, a, lw, g


def _rwkv_head_out(y, r, kf, v, g, r_k, gn_w, gn_b):
    mean = jnp.mean(y, axis=-1, keepdims=True)
    var = jnp.mean(jnp.square(y - mean), axis=-1, keepdims=True)
    yn = (y - mean) * lax.rsqrt(var + A_GN_EPS) * gn_w + gn_b
    bonus = jnp.sum(r * kf * r_k, axis=-1, keepdims=True) * v
    return (yn + bonus) * g


def _lru_rows(gate, xc, wa_bd, wx_bd, ba, bx, lam):
    ng = xc.shape[1] // LANES
    ra = jnp.concatenate([_bdot(xc[:, g * LANES:(g + 1) * LANES], wa_bd[g]) for g in range(ng)], axis=1)
    rx = jnp.concatenate([_bdot(xc[:, g * LANES:(g + 1) * LANES], wx_bd[g]) for g in range(ng)], axis=1)
    r_gate = jax.nn.sigmoid(ra + ba)
    i_gate = jax.nn.sigmoid(rx + bx)
    log_a = -LRU_C * r_gate * jax.nn.softplus(-lam)
    a = jnp.exp(log_a)
    b = jnp.sqrt(_neg_expm1(2.0 * log_a)) * (i_gate * xc)
    return a, b, jax.nn.gelu(gate)


def _silu(x):
    return 0.5 * x * (1.0 + jnp.tanh(0.5 * x))


def _gdn_act(y, is_qk, q_scale):
    y = _silu(y)
    if is_qk is False:
        return y
    outs = []
    for h in range(y.shape[1] // C_HEAD):
        yh = y[:, h * C_HEAD:(h + 1) * C_HEAD]
        nrm = yh * (lax.rsqrt(jnp.sum(yh * yh, axis=-1, keepdims=True) + 1e-6) * q_scale)
        outs.append(nrm if is_qk is True else jnp.where(is_qk, nrm, yh))
    return jnp.concatenate(outs, axis=1)


def _gdn_head_out(o, z, norm_w):
    o = o * lax.rsqrt(jnp.mean(o * o, axis=-1, keepdims=True) + NORM_EPS) * norm_w
    return o * _silu(z)


def _w_block(w_ref):
    return w_ref[0] if len(w_ref.shape) == 3 else w_ref[...]


def _mm_body(*refs, n_x, has_res, transposed):
    x_refs = refs[:n_x]
    w_refs = refs[n_x:2 * n_x]
    k = 2 * n_x
    if has_res:
        res_ref, gate_ref = refs[k], refs[k + 1]
        k += 2
    o_ref = refs[k]
    wbf_refs = refs[k + 1:k + 1 + n_x]

    @pl.when((pl.program_id(1) == 0) & (pl.program_id(2) == 0))
    def _():
        for w_ref, wbf_ref in zip(w_refs, wbf_refs):
            wbf_ref[...] = _w_block(w_ref).astype(BF16)

    acc = None
    for x_ref, wbf_ref, tr in zip(x_refs, wbf_refs, transposed):
        d = lax.dot_general(x_ref[0], wbf_ref[...], _NT if tr else _NN, preferred_element_type=F32)
        acc = d if acc is None else acc + d
    if has_res:
        acc = res_ref[0] + gate_ref[0] * acc
    o_ref[0] = acc.astype(o_ref.dtype)


def _pick_tiles(n_out, k_sum, rows, seq, n_out_bufs):
    best = None
    for tm in [t for t in (1024, 512, 256, 128) if seq % t == 0] or [seq]:
        for tn in (1024, 512, 256, 128):
            if n_out % tn:
                continue
            need = (2 * tm * k_sum * 2 + 2 * k_sum * tn * 4 + k_sum * tn * 2
                    + (2 * n_out_bufs + 1) * tm * tn * 4)
            if need > MM_VMEM_BUDGET:
                continue
            traffic = (n_out // tn) * rows * k_sum * 2
            key = (tm < min(512, seq), traffic, -tn, -tm)
            if best is None or key < best[0]:
                best = (key, tm, tn)
    if best is None:
        raise ValueError("no projection tile fits VMEM")
    return best[1], best[2]


def _mm(xs, ws, *, n_out, col0=0, tm=None, tn=None, out_dtype=F32, res=None, gate=None, gate_blk=0, name="proj"):
    B, T, _ = xs[0].shape
    ks = [x.shape[-1] for x in xs]
    has_res = res is not None
    if tn is None:
        tm, tn = _pick_tiles(n_out, sum(ks), B * T, T, 2 if has_res else 1)
    elif tm is None:
        tm = min(1024, T)
    assert n_out % tn == 0 and T % tm == 0
    in_specs = [pl.BlockSpec((1, tm, k), lambda j, b, t: (b, t, 0)) for k in ks]
    scratch = []
    for k, (w, layer, rb, tr) in zip(ks, ws):
        if tr:
            assert col0 % SUBLANES == 0 and tn % SUBLANES == 0 and w.shape[2] == k
            in_specs.append(pl.BlockSpec(
                (pl.Element(1), pl.Element(tn), pl.Element(k)),
                functools.partial(lambda j, b, t, layer: (layer, pl.multiple_of(col0 + j * tn, SUBLANES), 0),
                                  layer=layer)))
            scratch.append(pltpu.VMEM((tn, k), BF16))
        else:
            assert col0 % tn == 0
            in_specs.append(pl.BlockSpec(
                (None, k, tn),
                functools.partial(lambda j, b, t, layer, rb: (layer, rb, col0 // tn + j), layer=layer, rb=rb)))
            scratch.append(pltpu.VMEM((k, tn), BF16))
    args = list(xs) + [w for w, _, _, _ in ws]
    if has_res:
        in_specs.append(pl.BlockSpec((1, tm, tn), lambda j, b, t: (b, t, j)))
        gb0 = gate_blk * (n_out // tn)
        if gate.shape[1] == 1:
            in_specs.append(pl.BlockSpec((1, 1, tn), lambda j, b, t: (b, 0, gb0 + j)))
        else:
            in_specs.append(pl.BlockSpec((1, tm, tn), lambda j, b, t: (b, t, gb0 + j)))
        args += [res, gate]
    return pl.pallas_call(
        functools.partial(_mm_body, n_x=len(xs), has_res=has_res, transposed=tuple(w[3] for w in ws)),
        grid=(n_out // tn, B, T // tm),
        in_specs=in_specs,
        out_specs=pl.BlockSpec((1, tm, tn), lambda j, b, t: (b, t, j)),
        out_shape=jax.ShapeDtypeStruct((B, T, n_out), out_dtype),
        scratch_shapes=scratch,
        compiler_params=_cparams(3),
        name=name,
    )(*args)


def _mod_body(c_ref, w_ref, b_ref, op_ref, os_ref):
    bs = os_ref.shape[1]
    y = _bdot(jax.nn.silu(c_ref[...]), w_ref[0]) + b_ref[0]
    os_ref[0] = y[:bs]
    op_ref[0] = y[bs:]


def _modulation(c_p, c_s, w_mod, b_mod):
    depth, d, n = w_mod.shape
    tn = 1024
    bp, bs = c_p.shape[0], c_s.shape[0]
    return pl.pallas_call(
        _mod_body,
        grid=(depth, n // tn),
        in_specs=[pl.BlockSpec((bs + bp, d), lambda l, j: (0, 0)),
                  pl.BlockSpec((1, d, tn), lambda l, j: (l, 0, j)),
                  pl.BlockSpec((1, 1, tn), lambda l, j: (l, 0, j))],
        out_specs=[pl.BlockSpec((1, bp, tn), lambda l, j: (l, 0, j)),
                   pl.BlockSpec((1, bs, tn), lambda l, j: (l, 0, j))],
        out_shape=[jax.ShapeDtypeStruct((depth, bp, n), F32),
                   jax.ShapeDtypeStruct((depth, bs, n), F32)],
        compiler_params=_cparams(2),
        name="modulation",
    )(jnp.concatenate([c_s, c_p], axis=0), w_mod, b_mod.reshape(depth, 1, n))


def _norm_body(*refs, has_mod):
    if has_mod:
        x_ref, g_ref, sc_ref, sh_ref, o_ref = refs
    else:
        x_ref, g_ref, o_ref = refs
    x = x_ref[0]
    y = x * lax.rsqrt(jnp.mean(x * x, axis=-1, keepdims=True) + NORM_EPS) * g_ref[...]
    if has_mod:
        y = y * (1.0 + sc_ref[0]) + sh_ref[0]
    o_ref[0] = y.astype(o_ref.dtype)


def _norm(x, gain, mod=None, sc_blk=0, sh_blk=0, *, tt, out_dtype=BF16):
    B, T, d = x.shape
    in_specs = [pl.BlockSpec((1, tt, d), lambda b, t: (b, t, 0)),
                pl.BlockSpec((1, d), lambda b, t: (0, 0))]
    args = [x, gain.reshape(1, d)]
    if mod is not None:
        for blk in (sc_blk, sh_blk):
            if mod.shape[1] == 1:
                in_specs.append(pl.BlockSpec((1, 1, d), functools.partial(lambda b, t, blk: (b, 0, blk), blk=blk)))
            else:
                in_specs.append(pl.BlockSpec((1, tt, d), functools.partial(lambda b, t, blk: (b, t, blk), blk=blk)))
        args += [mod, mod]
    return pl.pallas_call(
        functools.partial(_norm_body, has_mod=mod is not None),
        grid=(B, T // tt),
        in_specs=in_specs,
        out_specs=pl.BlockSpec((1, tt, d), lambda b, t: (b, t, 0)),
        out_shape=jax.ShapeDtypeStruct((B, T, d), out_dtype),
        compiler_params=_cparams(2),
        name="norm",
    )(*args)


def _ffn_up_prompt_body(h_ref, wg_ref, wu_ref, cw_ref, cb_ref, act_ref, st_ref, wg_bf, wu_bf, buf):
    t = pl.program_id(2)
    tm = h_ref.shape[1]

    @pl.when((pl.program_id(1) == 0) & (t == 0))
    def _():
        wg_bf[...] = wg_ref[0].astype(BF16)
        wu_bf[...] = wu_ref[0].astype(BF16)

    @pl.when(t == 0)
    def _():
        buf[0:HIST, :] = jnp.zeros((HIST, buf.shape[1]), F32)

    h = h_ref[0]
    buf[HIST:, :] = jnp.dot(h, wg_bf[...], preferred_element_type=F32)
    cw = cw_ref[0]
    u = (buf[HIST - 2:HIST - 2 + tm, :] * cw[0:1] + buf[HIST - 1:HIST - 1 + tm, :] * cw[1:2]
         + buf[HIST:, :] * cw[2:3] + cb_ref[0])
    up = jnp.dot(h, wu_bf[...], preferred_element_type=F32)
    act_ref[0] = (jax.nn.gelu(u) * up).astype(act_ref.dtype)
    st_ref[0] = buf[HIST + tm - 2:HIST + tm, :]
    buf[0:HIST, :] = buf[tm:tm + HIST, :]


def _ffn_up_prompt(h, w_gate, w_up, conv_w, conv_b, layer, *, tm, tn):
    B, T, d = h.shape
    n = w_gate.shape[-1]
    return pl.pallas_call(
        _ffn_up_prompt_body,
        grid=(n // tn, B, T // tm),
        in_specs=[pl.BlockSpec((1, tm, d), lambda j, b, t: (b, t, 0)),
                  pl.BlockSpec((1, d, tn), lambda j, b, t: (layer, 0, j)),
                  pl.BlockSpec((1, d, tn), lambda j, b, t: (layer, 0, j)),
                  pl.BlockSpec((1, 3, tn), lambda j, b, t: (layer, 0, j)),
                  pl.BlockSpec((1, 1, tn), lambda j, b, t: (layer, 0, j))],
        out_specs=[pl.BlockSpec((1, tm, tn), lambda j, b, t: (b, t, j)),
                   pl.BlockSpec((1, 2, tn), lambda j, b, t: (b, 0, j))],
        out_shape=[jax.ShapeDtypeStruct((B, T, n), BF16),
                   jax.ShapeDtypeStruct((B, 2, n), F32)],
        scratch_shapes=[pltpu.VMEM((d, tn), BF16), pltpu.VMEM((d, tn), BF16),
                        pltpu.VMEM((HIST + tm, tn), F32)],
        compiler_params=_cparams(3),
        name="ffn_up_prompt",
    )(h, w_gate, w_up, conv_w, conv_b.reshape(conv_b.shape[0], 1, n))


def _ffn_up_sample_body(*refs, n_state):
    st_ref = refs[0]
    h_ref, wg_ref, wu_ref, cw_ref, cb_ref, act_ref, nst_ref = refs[n_state:]
    h = h_ref[...]
    pre = _bdot(h, wg_ref[0])
    cw = cw_ref[0]
    u = st_ref[:, 0, :] * cw[0:1] + st_ref[:, 1, :] * cw[1:2] + pre * cw[2:3] + cb_ref[0]
    up = _bdot(h, wu_ref[0])
    act_ref[...] = (jax.nn.gelu(u) * up).astype(act_ref.dtype)
    nst_ref[:, 0, :] = st_ref[:, 1, :]
    nst_ref[:, 1, :] = pre


def _ffn_up_sample(h, w_gate, w_up, conv_w, conv_b, state_all, new_all, layer, *, tn):
    bs, d = h.shape
    n = w_gate.shape[-1]
    s_spec = pl.BlockSpec((None, bs, 2, tn), lambda j: (layer, 0, 0, j))
    st_specs, st_args, aliases = _layer_state_io(state_all, new_all, s_spec, out_index=1)
    return pl.pallas_call(
        functools.partial(_ffn_up_sample_body, n_state=len(st_args)),
        grid=(n // tn,),
        in_specs=st_specs + [pl.BlockSpec((bs, d), lambda j: (0, 0)),
                             pl.BlockSpec((1, d, tn), lambda j: (layer, 0, j)),
                             pl.BlockSpec((1, d, tn), lambda j: (layer, 0, j)),
                             pl.BlockSpec((1, 3, tn), lambda j: (layer, 0, j)),
                             pl.BlockSpec((1, 1, tn), lambda j: (layer, 0, j))],
        out_specs=[pl.BlockSpec((bs, tn), lambda j: (0, j)), s_spec],
        out_shape=[jax.ShapeDtypeStruct((bs, n), BF16),
                   jax.ShapeDtypeStruct(state_all.shape, F32)],
        input_output_aliases=aliases,
        compiler_params=_cparams(1),
        name="ffn_up_sample",
    )(*st_args, h, w_gate, w_up, conv_w, conv_b.reshape(conv_b.shape[0], 1, n))


def _rwkv_prep_prompt_body(pm_ref, pl_ref, mum_ref, mul_ref, w0_ref, w2_ref, a0_ref, a2_ref, g2_ref,
                           kk_ref, ka_ref, r_o, kf_o, v_o, kkn_o, a_o, lw_o, g_o, bufm, bufl):
    t = pl.program_id(1)
    tt = pm_ref.shape[1]

    @pl.when(t == 0)
    def _():
        bufm[0:HIST, :] = jnp.zeros((HIST, bufm.shape[1]), F32)
        bufl[0:HIST, :] = jnp.zeros((HIST, bufl.shape[1]), F32)

    p_main = pm_ref[0]
    p_lora = pl_ref[0][:, :LORA_ALL]
    bufm[HIST:, :] = p_main
    bufl[HIST:, :] = p_lora
    q_main = bufm[HIST - 1:HIST - 1 + tt, :]
    q_lora = bufl[HIST - 1:HIST - 1 + tt, :]
    outs = _rwkv_rows(p_main, q_main, p_lora, q_lora, mum_ref[...], mul_ref[...], w0_ref[...], w2_ref[0],
                      a0_ref[...], a2_ref[0], g2_ref[0], kk_ref[...], ka_ref[...])
    for o_ref, val in zip((r_o, kf_o, v_o, kkn_o, a_o, lw_o, g_o), outs):
        o_ref[0] = val
    bufm[0:HIST, :] = bufm[tt:tt + HIST, :]
    bufl[0:HIST, :] = bufl[tt:tt + HIST, :]


def _row(v):
    return v.reshape(1, -1)


def _rwkv_prep_prompt(p_main, p_lora, prm, i, *, tt):
    B, T, _ = p_main.shape
    lw_pad = p_lora.shape[-1]
    mu = prm['rwkv_mu'][i]
    full = lambda shape: pl.BlockSpec(shape, lambda b, t: (0,) * len(shape))
    lay3 = lambda shape: pl.BlockSpec((1,) + shape, lambda b, t: (i, 0, 0))
    out_spec = pl.BlockSpec((1, tt, A_WIDTH), lambda b, t: (b, t, 0))
    return pl.pallas_call(
        _rwkv_prep_prompt_body,
        grid=(B, T // tt),
        in_specs=[pl.BlockSpec((1, tt, P_A_MAIN), lambda b, t: (b, t, 0)),
                  pl.BlockSpec((1, tt, lw_pad), lambda b, t: (b, t, 0)),
                  full((1, P_A_MAIN)), full((1, LORA_ALL)), full((1, A_WIDTH)),
                  lay3((LORA_W, A_WIDTH)), full((1, A_WIDTH)), lay3((LORA_A, A_WIDTH)),
                  lay3((LORA_G, A_WIDTH)), full((1, A_WIDTH)), full((1, A_WIDTH))],
        out_specs=[out_spec] * 7,
        out_shape=[jax.ShapeDtypeStruct((B, T, A_WIDTH), F32)] * 7,
        scratch_shapes=[pltpu.VMEM((HIST + tt, P_A_MAIN), F32), pltpu.VMEM((HIST + tt, LORA_ALL), F32)],
        compiler_params=_cparams(2),
        name="rwkv_prep_prompt",
    )(p_main, p_lora, _row(mu[:P_A_MAIN]), _row(mu[P_A_MAIN:]), _row(prm['rwkv_w0'][i]), prm['rwkv_w2'],
      _row(prm['rwkv_a0'][i]), prm['rwkv_a2'], prm['rwkv_g2'], _row(prm['rwkv_k_k'][i]), _row(prm['rwkv_k_a'][i]))


def _wkv_prompt_body(r_ref, kf_ref, v_ref, kkn_ref, a_ref, lw_ref, g_ref, rk_ref, gw_ref, gb_ref,
                     ya_ref, st_ref, s_scr):
    t = pl.program_id(2)
    tt = r_ref.shape[1]
    n = A_HEAD_DIM
    C = CHUNK
    nc = tt // C
    nh = r_ref.shape[2] // n

    @pl.when(t == 0)
    def _():
        s_scr[...] = jnp.zeros(s_scr.shape, F32)

    def units(x):
        parts = [x[:, h * n:(h + 1) * n].reshape(nc, C, n) for h in range(nh)]
        return jnp.stack(parts, axis=1).reshape(nc * nh, C, n)

    lw_all = lw_ref[0]
    lc = units(_sel_l(_block_tri(tt, C), lw_all))
    r, kf, v, kkn, a, lw = map(units, (r_ref[0], kf_ref[0], v_ref[0], kkn_ref[0], a_ref[0], lw_all))
    incl = _tri(C, False)
    strict = _tri(C, True)

    kk = kkn * lax.rsqrt(jnp.sum(kkn * kkn, axis=-1, keepdims=True) + 1e-12)
    bb = kk * a
    p = jnp.exp(lc)
    pinv = jnp.exp(-lc)
    kt = kf * pinv
    bt = bb * pinv
    at = kk * jnp.exp(lc - lw)
    rt = r * p
    gram = _bdot(jnp.concatenate([at, rt], axis=1), jnp.concatenate([bt, kt], axis=1), _BNT)
    a_ab = jnp.where(strict, gram[:, :C, :C], 0.0)
    a_ak = jnp.where(strict, gram[:, :C, C:], 0.0)
    a_rb = jnp.where(incl, gram[:, C:, :C], 0.0)
    a_rk = jnp.where(incl, gram[:, C:, C:], 0.0)
    tinv = _unit_lower_inverse(a_ab, _bdot)
    av = _bdot(jnp.concatenate([a_ak, a_rk], axis=1), v, _BNN)
    tz = _bdot(tinv, jnp.concatenate([at, av[:, :C]], axis=2), _BNN)
    rz = _bdot(a_rb, tz, _BNN)
    lhs_s = jnp.concatenate([tz[:, :, :n], rt - rz[:, :, :n]], axis=1)
    z0 = tz[:, :, n:]
    y0 = av[:, C:] - rz[:, :, n:]
    plast = p[:, C - 1:C, :]
    upd = jnp.concatenate([kt * plast, -(bt * plast)], axis=1)

    S = s_scr[...]
    ys = []
    for c in range(nc):
        sl = slice(c * nh, (c + 1) * nh)
        xs = _bdot(lhs_s[sl], S, _BNT)
        z = xs[:, :C] + z0[sl]
        ys.append(xs[:, C:] + y0[sl])
        S = S * plast[sl] + _bdot(jnp.concatenate([v[sl], z], axis=1), upd[sl], _BTN)
    s_scr[...] = S

    outs = []
    for h in range(nh):
        cols = slice(h * n, (h + 1) * n)
        y = jnp.concatenate([ys[c][h] for c in range(nc)], axis=0)
        outs.append(_rwkv_head_out(y, r_ref[0, :, cols], kf_ref[0, :, cols], v_ref[0, :, cols],
                                   g_ref[0, :, cols], rk_ref[:, cols], gw_ref[:, cols], gb_ref[:, cols]))
    ya_ref[0] = jnp.concatenate(outs, axis=1).astype(ya_ref.dtype)

    @pl.when(t == pl.num_programs(2) - 1)
    def _():
        st_ref[0] = S


def _wkv_prompt(vals, prm, i, *, tt, heads_per_step):
    B, T, _ = vals[0].shape
    wblk = heads_per_step * A_HEAD_DIM
    seq_spec = pl.BlockSpec((1, tt, wblk), lambda b, h, t: (b, t, h))
    par_spec = pl.BlockSpec((1, wblk), lambda b, h, t: (0, h))
    return pl.pallas_call(
        _wkv_prompt_body,
        grid=(B, A_HEADS // heads_per_step, T // tt),
        in_specs=[seq_spec] * 7 + [par_spec] * 3,
        out_specs=[seq_spec,
                   pl.BlockSpec((1, heads_per_step, A_HEAD_DIM, A_HEAD_DIM), lambda b, h, t: (b, h, 0, 0))],
        out_shape=[jax.ShapeDtypeStruct((B, T, A_WIDTH), BF16),
                   jax.ShapeDtypeStruct((B, A_HEADS, A_HEAD_DIM, A_HEAD_DIM), F32)],
        scratch_shapes=[pltpu.VMEM((heads_per_step, A_HEAD_DIM, A_HEAD_DIM), F32)],
        compiler_params=_cparams(3),
        name="wkv_prompt",
    )(*vals, _row(prm['rwkv_r_k'][i]), _row(prm['rwkv_gn_w'][i]), _row(prm['rwkv_gn_b'][i]))


def _lru_prompt_body(gate_ref, xb_ref, cw_ref, cb_ref, wa_ref, wx_ref, ba_ref, bx_ref, lam_ref,
                     y_ref, h_ref, cst_ref, buf, h_scr):
    t = pl.program_id(1)
    tt = xb_ref.shape[1]
    w = xb_ref.shape[2]

    @pl.when(t == 0)
    def _():
        buf[0:HIST, :] = jnp.zeros((HIST, w), F32)
        h_scr[...] = jnp.zeros(h_scr.shape, F32)

    buf[HIST:, :] = xb_ref[0]
    cw = cw_ref[0]
    xc = cb_ref[...] + buf[HIST:, :] * cw[3:4]
    for j in range(3):
        xc = xc + buf[HIST - 3 + j:HIST - 3 + j + tt, :] * cw[j:j + 1]
    a, b, gact = _lru_rows(gate_ref[0], xc, wa_ref, wx_ref, ba_ref[...], bx_ref[...], lam_ref[...])
    row = _iota2((tt, w), 0)
    d = 1
    while d < tt:
        a_sh = jnp.where(row >= d, pltpu.roll(a, d, axis=0), 1.0)
        b_sh = jnp.where(row >= d, pltpu.roll(b, d, axis=0), 0.0)
        b = a * b_sh + b
        a = a * a_sh
        d *= 2
    h = a * h_scr[0:1, :] + b
    y_ref[0] = (h * gact).astype(y_ref.dtype)
    h_last = h[tt - 1:tt, :]
    h_scr[0:1, :] = h_last
    h_ref[0] = h_last
    cst_ref[0] = buf[HIST + tt - 3:HIST + tt, :]
    buf[0:HIST, :] = buf[tt:tt + HIST, :]


def _block_diag_pairs(w):
    nb, n, _ = w.shape
    w = w.reshape(nb // 2, 2, n, n)
    z = jnp.zeros((nb // 2, n, n), w.dtype)
    top = jnp.concatenate([w[:, 0], z], axis=2)
    bot = jnp.concatenate([z, w[:, 1]], axis=2)
    return jnp.concatenate([top, bot], axis=1)


def _lru_prompt(pb, prm, i, *, tt):
    B, T, _ = pb.shape
    w = B_WIDTH
    ng = w // LANES
    full = lambda shape: pl.BlockSpec(shape, lambda b, t: (0,) * len(shape))
    return pl.pallas_call(
        _lru_prompt_body,
        grid=(B, T // tt),
        in_specs=[pl.BlockSpec((1, tt, w), lambda b, t: (b, t, 0)),
                  pl.BlockSpec((1, tt, w), lambda b, t: (b, t, 1)),
                  pl.BlockSpec((1, 4, w), lambda b, t: (i, 0, 0)),
                  full((1, w)), full((ng, LANES, LANES)), full((ng, LANES, LANES)),
                  full((1, w)), full((1, w)), full((1, w))],
        out_specs=[pl.BlockSpec((1, tt, w), lambda b, t: (b, t, 0)),
                   pl.BlockSpec((1, 1, w), lambda b, t: (b, 0, 0)),
                   pl.BlockSpec((1, 3, w), lambda b, t: (b, 0, 0))],
        out_shape=[jax.ShapeDtypeStruct((B, T, w), BF16),
                   jax.ShapeDtypeStruct((B, 1, w), F32),
                   jax.ShapeDtypeStruct((B, 3, w), F32)],
        scratch_shapes=[pltpu.VMEM((HIST + tt, w), F32), pltpu.VMEM((SUBLANES, w), F32)],
        compiler_params=_cparams(2),
        name="lru_prompt",
    )(pb, pb, prm['lru_conv_w'], _row(prm['lru_conv_b'][i]), _block_diag_pairs(prm['lru_wa'][i]),
      _block_diag_pairs(prm['lru_wx'][i]), _row(prm['lru_ba'][i]), _row(prm['lru_bx'][i]),
      _row(prm['lru_lambda'][i]))


EPI_ROWS = 256


def _in_c_conv_prompt_body(h_ref, w_ref, cw_ref, o_ref, st_ref, w_bf, buf, *, is_qk):
    j = pl.program_id(0)
    t = pl.program_id(2)
    tm = h_ref.shape[1]
    tn = w_ref.shape[1]

    @pl.when((pl.program_id(1) == 0) & (t == 0))
    def _():
        w_bf[...] = w_ref[0].astype(BF16)

    @pl.when(t == 0)
    def _():
        buf[0:HIST, :] = jnp.zeros((HIST, tn), F32)

    cw = cw_ref[0]
    q_scale = jnp.where(j < C_KEY_DIM // tn, C_HEAD ** -0.5, 1.0).astype(F32)
    rc = min(EPI_ROWS, tm)
    for r0 in range(0, tm, rc):
        buf[HIST + r0:HIST + r0 + rc, :] = lax.dot_general(h_ref[0, r0:r0 + rc, :], w_bf[...], _NT,
                                                           preferred_element_type=F32)
        y = buf[HIST + r0:HIST + r0 + rc, :] * cw[3:4]
        for jj in range(3):
            y = y + buf[HIST - 3 + jj + r0:HIST - 3 + jj + r0 + rc, :] * cw[jj:jj + 1]
        o_ref[0, r0:r0 + rc, :] = _gdn_act(y, is_qk, q_scale)
    st_ref[0] = buf[HIST + tm - 3:HIST + tm, :]
    buf[0:HIST, :] = buf[tm:tm + HIST, :]


def _in_c_conv_prompt(h, w_t, conv_w, i, *, tm, tn, col0, n_cols, is_qk):
    B, T, d = h.shape
    assert col0 % tn == 0 and n_cols % tn == 0
    return pl.pallas_call(
        functools.partial(_in_c_conv_prompt_body, is_qk=is_qk),
        grid=(n_cols // tn, B, T // tm),
        in_specs=[pl.BlockSpec((1, tm, d), lambda j, b, t: (b, t, 0)),
                  pl.BlockSpec((pl.Element(1), pl.Element(tn), pl.Element(d)),
                               lambda j, b, t: (i, pl.multiple_of(col0 + j * tn, SUBLANES), 0)),
                  pl.BlockSpec((1, 4, tn), lambda j, b, t: (i, 0, col0 // tn + j))],
        out_specs=[pl.BlockSpec((1, tm, tn), lambda j, b, t: (b, t, j)),
                   pl.BlockSpec((1, 3, tn), lambda j, b, t: (b, 0, j))],
        out_shape=[jax.ShapeDtypeStruct((B, T, n_cols), F32),
                   jax.ShapeDtypeStruct((B, 3, n_cols), F32)],
        scratch_shapes=[pltpu.VMEM((tn, d), BF16), pltpu.VMEM((HIST + tm, tn), F32)],
        compiler_params=_cparams(3),
        name="in_c_conv_prompt",
    )(h, w_t, conv_w)


def _gdn_gates(ba, a_log, dt_bias):
    hv = C_V_HEADS
    beta = jax.nn.sigmoid(ba[..., :hv])
    g = -jnp.exp(a_log) * jax.nn.softplus(ba[..., hv:2 * hv] + dt_bias)
    return beta, g


def _lane_bcast_col(x, col):
    n = x.shape[1]
    onehot = _iota2((n, LANES), 0) == col
    return _sel_r(x, onehot)


def _gdn_prompt_body(q_ref, k_ref, v_ref, z_ref, ba_ref, alog_ref, dtb_ref, nw_ref, o_ref, st_ref, s_scr):
    hblk = pl.program_id(1)
    t = pl.program_id(2)
    tt = q_ref.shape[1]
    C = CHUNK
    nc = tt // C
    rep = C_V_HEADS // C_K_HEADS
    n_kh = q_ref.shape[2] // C_HEAD
    nh = n_kh * rep

    @pl.when(t == 0)
    def _():
        s_scr[...] = jnp.zeros(s_scr.shape, F32)

    beta_all, g_all = _gdn_gates(ba_ref[0], alog_ref[...], dtb_ref[...])
    gc_all = _sel_l(_block_tri(tt, C), g_all)
    blocks = lambda x: x.reshape(nc, C, x.shape[-1])
    qs, ks, vs, betas, gcs = [], [], [], [], []
    for kh in range(n_kh):
        q_h = blocks(q_ref[0, :, kh * C_HEAD:(kh + 1) * C_HEAD])
        k_h = blocks(k_ref[0, :, kh * C_HEAD:(kh + 1) * C_HEAD])
        for jv in range(rep):
            hu = kh * rep + jv
            hv = hblk * nh + hu
            qs.append(q_h)
            ks.append(k_h)
            vs.append(blocks(v_ref[0, :, hu * C_HEAD:(hu + 1) * C_HEAD]))
            betas.append(blocks(_lane_bcast_col(beta_all, hv)))
            gcs.append(blocks(_lane_bcast_col(gc_all, hv)))
    stack = lambda xs: jnp.stack(xs, axis=1).reshape(nc * nh, C, xs[0].shape[-1])
    q, k, v, beta, gc = map(stack, (qs, ks, vs, betas, gcs))
    incl = _tri(C, False)
    strict = _tri(C, True)

    gcc = gc[:, :, :C]
    diff = gcc - jnp.swapaxes(gcc, 1, 2)
    dec_incl = jnp.exp(jnp.where(incl, diff, -jnp.inf))
    dec_strict = jnp.where(strict, dec_incl, 0.0)
    kb = k * beta
    gram = _bdot(jnp.concatenate([kb, q], axis=1), k, _BNT)
    low = gram[:, :C] * dec_strict
    qk = gram[:, C:] * dec_incl
    tinv = _unit_lower_inverse_refined(low)
    eg = jnp.exp(gc)
    sol = _bdot(tinv, jnp.concatenate([v * beta, kb * eg], axis=2), _BNN)
    u = sol[:, :, :C_HEAD]
    lhs_s = jnp.concatenate([sol[:, :, C_HEAD:], q * eg], axis=1)
    g_last = gc[:, C - 1:C, :]
    k_tail = k * jnp.exp(g_last - gc)
    e_last = jnp.exp(g_last)

    S = s_scr[...]
    os_ = []
    for c in range(nc):
        sl = slice(c * nh, (c + 1) * nh)
        ws = _bdot(lhs_s[sl], S, _BNN)
        v_new = u[sl] - ws[:, :C]
        os_.append(ws[:, C:] + _bdot(qk[sl], v_new, _BNN))
        S = S * e_last[sl] + _bdot(k_tail[sl], v_new, _BTN)
    s_scr[...] = S

    outs = []
    for hu in range(nh):
        cols = slice(hu * C_HEAD, (hu + 1) * C_HEAD)
        o = jnp.concatenate([os_[c][hu] for c in range(nc)], axis=0)
        outs.append(_gdn_head_out(o, z_ref[0, :, cols], nw_ref[...]))
    o_ref[0] = jnp.concatenate(outs, axis=1).astype(o_ref.dtype)

    @pl.when(t == pl.num_programs(2) - 1)
    def _():
        st_ref[0] = S


def _gdn_prompt(qk_act, v_act, z, ba, prm, i, *, tt, kheads_per_step):
    B, T, _ = qk_act.shape
    rep = C_V_HEADS // C_K_HEADS
    kw = kheads_per_step * C_HEAD
    vw = rep * kw
    k_blk0 = C_KEY_DIM // kw
    v_blk0 = 0
    z_blk0 = 0
    nh = kheads_per_step * rep
    full = lambda shape: pl.BlockSpec(shape, lambda b, h, t: (0,) * len(shape))
    return pl.pallas_call(
        _gdn_prompt_body,
        grid=(B, C_K_HEADS // kheads_per_step, T // tt),
        in_specs=[pl.BlockSpec((1, tt, kw), lambda b, h, t: (b, t, h)),
                  pl.BlockSpec((1, tt, kw), lambda b, h, t: (b, t, k_blk0 + h)),
                  pl.BlockSpec((1, tt, vw), lambda b, h, t: (b, t, v_blk0 + h)),
                  pl.BlockSpec((1, tt, vw), lambda b, h, t: (b, t, z_blk0 + h)),
                  pl.BlockSpec((1, tt, 2 * C_V_HEADS), lambda b, h, t: (b, t, 0)),
                  full((1, C_V_HEADS)), full((1, C_V_HEADS)), full((1, C_HEAD))],
        out_specs=[pl.BlockSpec((1, tt, vw), lambda b, h, t: (b, t, h)),
                   pl.BlockSpec((1, nh, C_HEAD, C_HEAD), lambda b, h, t: (b, h, 0, 0))],
        out_shape=[jax.ShapeDtypeStruct((B, T, C_VAL_DIM), BF16),
                   jax.ShapeDtypeStruct((B, C_V_HEADS, C_HEAD, C_HEAD), F32)],
        scratch_shapes=[pltpu.VMEM((nh, C_HEAD, C_HEAD), F32)],
        compiler_params=_cparams(3),
        name="gdn_prompt",
    )(qk_act, qk_act, v_act, z, ba, _row(prm['gdn_a_log'][i]), _row(prm['gdn_dt_bias'][i]),
      _row(prm['gdn_norm_w'][i]))


def _rwkv_prep_sample_body(pm_ref, pl_ref, qm_ref, ql_ref, mum_ref, mul_ref, w0_ref, w2_ref, a0_ref, a2_ref,
                           g2_ref, kk_ref, ka_ref, r_o, kf_o, v_o, kkn_o, a_o, lw_o, g_o):
    outs = _rwkv_rows(pm_ref[...], qm_ref[...], pl_ref[...][:, :LORA_ALL], ql_ref[...], mum_ref[...],
                      mul_ref[...], w0_ref[...], w2_ref[0], a0_ref[...], a2_ref[0], g2_ref[0],
                      kk_ref[...], ka_ref[...])
    for o_ref, val in zip((r_o, kf_o, v_o, kkn_o, a_o, lw_o, g_o), outs):
        o_ref[...] = val.T


def _rwkv_prep_sample(p_main, p_lora, shift, prm, i):
    bs = p_main.shape[0]
    mu = prm['rwkv_mu'][i]
    full = lambda shape: pl.BlockSpec(shape, lambda s: (0,) * len(shape))
    lay3 = lambda shape: pl.BlockSpec((1,) + shape, lambda s: (i, 0, 0))
    return pl.pallas_call(
        _rwkv_prep_sample_body,
        grid=(1,),
        in_specs=[full(p_main.shape), full(p_lora.shape), full((bs, P_A_MAIN)), full((bs, LORA_ALL)),
                  full((1, P_A_MAIN)), full((1, LORA_ALL)), full((1, A_WIDTH)),
                  lay3((LORA_W, A_WIDTH)), full((1, A_WIDTH)), lay3((LORA_A, A_WIDTH)),
                  lay3((LORA_G, A_WIDTH)), full((1, A_WIDTH)), full((1, A_WIDTH))],
        out_specs=[full((A_WIDTH, bs))] * 7,
        out_shape=[jax.ShapeDtypeStruct((A_WIDTH, bs), F32)] * 7,
        compiler_params=_cparams(1),
        name="rwkv_prep_sample",
    )(p_main, p_lora, shift[:, :P_A_MAIN], shift[:, P_A_MAIN:], _row(mu[:P_A_MAIN]), _row(mu[P_A_MAIN:]),
      _row(prm['rwkv_w0'][i]), prm['rwkv_w2'], _row(prm['rwkv_a0'][i]), prm['rwkv_a2'], prm['rwkv_g2'],
      _row(prm['rwkv_k_k'][i]), _row(prm['rwkv_k_a'][i]))


STEP_GROUP = 16


def _wkv_step_body(*refs, n_state):
    s_ref = refs[0]
    r_ref, kf_ref, v_ref, kkn_ref, a_ref, lw_ref, g_ref, rk_ref, gw_ref, gb_ref, ns_ref, y_ref = refs[n_state:]
    n = A_HEAD_DIM
    outs = []
    for h in range(s_ref.shape[0]):
        rows = slice(h * n, (h + 1) * n)
        ld = lambda ref: ref[rows, :]
        r, kf, v, kkn, a = ld(r_ref), ld(kf_ref), ld(v_ref), ld(kkn_ref), ld(a_ref)
        kk = kkn * lax.rsqrt(jnp.sum(kkn * kkn, axis=0, keepdims=True) + 1e-12)
        S = s_ref[h]
        u = jnp.sum(S * kk[None], axis=1)
        s_new = S * jnp.exp(ld(lw_ref))[None] - u[:, None, :] * (kk * a)[None] + v[:, None, :] * kf[None]
        ns_ref[h] = s_new
        y = jnp.sum(s_new * r[None], axis=1)
        mean = jnp.mean(y, axis=0, keepdims=True)
        var = jnp.mean(jnp.square(y - mean), axis=0, keepdims=True)
        yn = (y - mean) * lax.rsqrt(var + A_GN_EPS) * ld(gw_ref) + ld(gb_ref)
        bonus = jnp.sum(r * kf * ld(rk_ref), axis=0, keepdims=True) * v
        outs.append(((yn + bonus) * ld(g_ref)).T)
    y_ref[...] = jnp.concatenate(outs, axis=1).astype(y_ref.dtype)


def _layer_state_io(state_all, new_all, spec, out_index=0):
    if new_all is None:
        return [spec], [state_all], {}
    return [spec, pl.BlockSpec(memory_space=pl.ANY)], [state_all, new_all], {1: out_index}


def _wkv_step(S_all, new_all, vals, prm, i, *, heads_per_step):
    _, n_heads, n, _, bs = S_all.shape
    wblk = heads_per_step * n
    lanes = lambda p: jnp.broadcast_to(p.reshape(A_WIDTH, 1), (A_WIDTH, bs))
    row_spec = pl.BlockSpec((wblk, bs), lambda h: (h, 0))
    s_spec = pl.BlockSpec((None, heads_per_step, n, n, bs), lambda h: (i, h, 0, 0, 0))
    st_specs, st_args, aliases = _layer_state_io(S_all, new_all, s_spec)
    return pl.pallas_call(
        functools.partial(_wkv_step_body, n_state=len(st_args)),
        grid=(n_heads // heads_per_step,),
        in_specs=st_specs + [row_spec] * 10,
        out_specs=[s_spec, pl.BlockSpec((bs, wblk), lambda h: (0, h))],
        out_shape=[jax.ShapeDtypeStruct(S_all.shape, F32), jax.ShapeDtypeStruct((bs, A_WIDTH), BF16)],
        input_output_aliases=aliases,
        compiler_params=_cparams(1),
        name="wkv_step",
    )(*st_args, *vals, lanes(prm['rwkv_r_k'][i]), lanes(prm['rwkv_gn_w'][i]), lanes(prm['rwkv_gn_b'][i]))


def _lru_sample_body(gate_ref, xb_ref, cst_ref, h0_ref, cw_ref, cb_ref, wa_ref, wx_ref, ba_ref, bx_ref, lam_ref,
                     y_ref, h_ref, ncst_ref):
    cw = cw_ref[0]
    xb = xb_ref[...]
    xc = cb_ref[...] + xb * cw[3:4]
    for j in range(3):
        xc = xc + cst_ref[:, j, :] * cw[j:j + 1]
    a, b, gact = _lru_rows(gate_ref[...], xc, wa_ref, wx_ref, ba_ref[...], bx_ref[...], lam_ref[...])
    h = a * h0_ref[...] + b
    y_ref[...] = (h * gact).astype(y_ref.dtype)
    h_ref[...] = h
    ncst_ref[:, 0, :] = cst_ref[:, 1, :]
    ncst_ref[:, 1, :] = cst_ref[:, 2, :]
    ncst_ref[:, 2, :] = xb


def _lru_sample(pb, conv_state, h0, prm, i):
    bs = pb.shape[0]
    w = B_WIDTH
    ng = w // LANES
    full = lambda shape: pl.BlockSpec(shape, lambda s: (0,) * len(shape))
    return pl.pallas_call(
        _lru_sample_body,
        grid=(1,),
        in_specs=[pl.BlockSpec((bs, w), lambda s: (0, 0)), pl.BlockSpec((bs, w), lambda s: (0, 1)),
                  full((bs, 3, w)), full((bs, w)),
                  pl.BlockSpec((1, 4, w), lambda s: (i, 0, 0)),
                  full((1, w)), full((ng, LANES, LANES)), full((ng, LANES, LANES)),
                  full((1, w)), full((1, w)), full((1, w))],
        out_specs=[full((bs, w)), full((bs, w)), full((bs, 3, w))],
        out_shape=[jax.ShapeDtypeStruct((bs, w), BF16), jax.ShapeDtypeStruct((bs, w), F32),
                   jax.ShapeDtypeStruct((bs, 3, w), F32)],
        compiler_params=_cparams(1),
        name="lru_sample",
    )(pb, pb, conv_state, h0, prm['lru_conv_w'], _row(prm['lru_conv_b'][i]),
      _block_diag_pairs(prm['lru_wa'][i]), _block_diag_pairs(prm['lru_wx'][i]), _row(prm['lru_ba'][i]),
      _row(prm['lru_bx'][i]), _row(prm['lru_lambda'][i]))


def _gdn_prep_sample_body(*refs, n_state):
    cst_ref = refs[0]
    x_ref, cw_ref, o_ref, ncst_ref = refs[n_state:]
    j = pl.program_id(0)
    cw = cw_ref[0]
    x = x_ref[...]
    y = x * cw[3:4]
    for jj in range(3):
        y = y + cst_ref[jj] * cw[jj:jj + 1]
    n_qk_blocks = 2 * C_KEY_DIM // x.shape[1]
    q_scale = jnp.where(j < n_qk_blocks // 2, C_HEAD ** -0.5, 1.0).astype(F32)
    o_ref[...] = _gdn_act(y, j < n_qk_blocks, q_scale)
    ncst_ref[0] = cst_ref[1]
    ncst_ref[1] = cst_ref[2]
    ncst_ref[2] = x


def _gdn_prep_sample(qkvz, conv_all, new_all, conv_w, i, *, tc):
    bs = qkvz.shape[0]
    s_spec = pl.BlockSpec((None, 3, bs, tc), lambda j: (i, 0, 0, j))
    st_specs, st_args, aliases = _layer_state_io(conv_all, new_all, s_spec, out_index=1)
    return pl.pallas_call(
        functools.partial(_gdn_prep_sample_body, n_state=len(st_args)),
        grid=(C_QKV_DIM // tc,),
        in_specs=st_specs + [pl.BlockSpec((bs, tc), lambda j: (0, j)),
                             pl.BlockSpec((1, 4, tc), lambda j: (i, 0, j))],
        out_specs=[pl.BlockSpec((bs, tc), lambda j: (0, j)), s_spec],
        out_shape=[jax.ShapeDtypeStruct((bs, C_QKV_DIM), F32),
                   jax.ShapeDtypeStruct(conv_all.shape, F32)],
        input_output_aliases=aliases,
        compiler_params=_cparams(1),
        name="gdn_prep_sample",
    )(*st_args, qkvz, conv_w)


def _gdn_step_body(*refs, n_state):
    s_ref = refs[0]
    q_ref, k_ref, v_ref, z_ref, ba_ref, alog_ref, dtb_ref, nw_ref, ns_ref, o_ref = refs[n_state:]
    bb = s_ref.shape[0]
    rep = C_V_HEADS // C_K_HEADS
    n_kh = q_ref.shape[2] // C_HEAD
    hblk = pl.program_id(1)
    beta_all, g_all = _gdn_gates(ba_ref[...], alog_ref[...], dtb_ref[...])
    lane = lax.broadcasted_iota(jnp.int32, beta_all.shape, 2)
    pick = lambda x, col: jnp.sum(jnp.where(lane == col, x, 0.0), axis=-1, keepdims=True)
    rid = lax.broadcasted_iota(jnp.int32, (bb, SUBLANES, C_HEAD), 1)
    up = lambda x: jnp.broadcast_to(x, (bb, SUBLANES, C_HEAD))
    for kh in range(n_kh):
        q = q_ref[:, :, kh * C_HEAD:(kh + 1) * C_HEAD]
        k = k_ref[:, :, kh * C_HEAD:(kh + 1) * C_HEAD]
        qk = jnp.sum(q * k, axis=-1, keepdims=True)
        for jv in range(rep):
            hu = kh * rep + jv
            cols = slice(hu * C_HEAD, (hu + 1) * C_HEAD)
            hv = (hblk * n_kh + kh) * rep + jv
            beta = pick(beta_all, hv)
            eg = jnp.exp(pick(g_all, hv))
            v = v_ref[:, :, cols]
            S = s_ref[:, hu]
            kb = k * beta
            lhs = jnp.where(rid == 0, up(kb * eg), jnp.where(rid == 1, up(q * eg), 0.0))
            ws = _bdot(lhs, S, _BNN)
            v_new = v * beta - ws[:, 0:1]
            o = ws[:, 1:2] + qk * v_new
            left = jnp.where(rid == 0, up(k), 0.0)
            right = jnp.where(rid == 0, up(v_new), 0.0)
            ns_ref[:, hu] = S * eg + _bdot(left, right, _BTN)
            o_ref[:, :, cols] = _gdn_head_out(o, z_ref[:, :, cols], nw_ref[...]).astype(o_ref.dtype)


def _gdn_step(S_all, new_all, qkv_act, qkvz, ba, prm, i, *, bb, kheads_per_step):
    bs = S_all.shape[1]
    rep = C_V_HEADS // C_K_HEADS
    kw = kheads_per_step * C_HEAD
    vw = rep * kw
    k_blk0 = C_KEY_DIM // kw
    v_blk0 = 2 * C_KEY_DIM // vw
    z_blk0 = C_QKV_DIM // vw
    full = lambda shape: pl.BlockSpec(shape, lambda b, h: (0,) * len(shape))
    s_spec = pl.BlockSpec((None, bb, kheads_per_step * rep, C_HEAD, C_HEAD), lambda b, h: (i, b, h, 0, 0))
    st_specs, st_args, aliases = _layer_state_io(S_all, new_all, s_spec)
    return pl.pallas_call(
        functools.partial(_gdn_step_body, n_state=len(st_args)),
        grid=(bs // bb, C_K_HEADS // kheads_per_step),
        in_specs=st_specs + [
                  pl.BlockSpec((bb, 1, kw), lambda b, h: (b, 0, h)),
                  pl.BlockSpec((bb, 1, kw), lambda b, h: (b, 0, k_blk0 + h)),
                  pl.BlockSpec((bb, 1, vw), lambda b, h: (b, 0, v_blk0 + h)),
                  pl.BlockSpec((bb, 1, vw), lambda b, h: (b, 0, z_blk0 + h)),
                  pl.BlockSpec((bb, 1, 2 * C_V_HEADS), lambda b, h: (b, 0, 0)),
                  full((1, C_V_HEADS)), full((1, C_V_HEADS)), full((1, C_HEAD))],
        out_specs=[s_spec, pl.BlockSpec((bb, 1, vw), lambda b, h: (b, 0, h))],
        out_shape=[jax.ShapeDtypeStruct(S_all.shape, F32), jax.ShapeDtypeStruct((bs, 1, C_VAL_DIM), BF16)],
        input_output_aliases=aliases,
        compiler_params=_cparams(2),
        name="gdn_step",
    )(*st_args, qkv_act, qkv_act, qkv_act, qkvz, ba, _row(prm['gdn_a_log'][i]), _row(prm['gdn_dt_bias'][i]),
      _row(prm['gdn_norm_w'][i]))


MOD_SH1, MOD_SC1, MOD_GT1, MOD_SH2, MOD_SC2, MOD_GT2 = range(6)


def _in_proj_ab(h, prm, i):
    w = [(prm['w_in_ab_t'], i, 0, True)]
    p_main = _mm([h], w, n_out=P_A_MAIN, name="in_ab_main")
    p_lora = _mm([h], w, n_out=LORA_ALL, col0=P_A_MAIN, tn=LORA_ALL, name="in_ab_lora")
    pb = _mm([h], w, n_out=2 * B_WIDTH, col0=P_A, name="in_ab_lru")
    return p_main, p_lora, pb


def _in_proj_c_gates(h, prm, i):
    return _mm([h], [(prm['w_in_c_t'], i, 0, True)], n_out=2 * C_V_HEADS, col0=C_QKV_DIM + C_VAL_DIM,
               tn=2 * C_V_HEADS, name="in_c_gates")


def _ffn_down(act, prm, layer, x, mod):
    return _mm([act], [(prm['ffn_w_down'], layer, 0, False)], n_out=D_MODEL, res=x, gate=mod, gate_blk=MOD_GT2,
               name="ffn_down")


def _out_proj_ab(ya, yb, prm, i, x, mod):
    w = prm['w_out_ab']
    return _mm([ya, yb], [(w, i, 0, False), (w, i, 1, False)], n_out=D_MODEL, res=x, gate=mod, gate_blk=MOD_GT1,
               name="out_ab")


def _out_proj_c(yc, prm, i, x, mod):
    return _mm([yc], [(prm['w_out_c'], i, 0, False)], n_out=D_MODEL, res=x, gate=mod, gate_blk=MOD_GT1,
               name="out_c")


def _trunk_prompt(x, mods, prm):
    B, T, _ = x.shape
    tm, tt, t_norm = min(1024, T), min(256, T), min(512, T)
    wkv, shift, lru_h, lru_conv, gdn, gdn_conv, ffn_conv = [], [], [], [], [], [], []
    for layer in range(DEPTH):
        mod = mods[layer].reshape(B, 1, 6 * D_MODEL)
        i = layer // 2
        h = _norm(x, prm['norm_mix'][layer], mod, MOD_SC1, MOD_SH1, tt=t_norm)
        if layer % 2 == 0:
            p_main, p_lora, pb = _in_proj_ab(h, prm, i)
            vals = _rwkv_prep_prompt(p_main, p_lora, prm, i, tt=tt)
            ya, s_new = _wkv_prompt(vals, prm, i, tt=tt, heads_per_step=A_HEADS)
            yb, h_last, cst = _lru_prompt(pb, prm, i, tt=tt)
            x = _out_proj_ab(ya, yb, prm, i, x, mod)
            wkv.append(s_new)
            shift.append(jnp.concatenate([p_main[:, -1], p_lora[:, -1, :LORA_ALL]], axis=-1))
            lru_h.append(h_last[:, 0])
            lru_conv.append(cst)
        else:
            qk_act, cst_qk = _in_c_conv_prompt(h, prm['w_in_c_t'], prm['gdn_conv_w'], i, tm=tm, tn=512, col0=0,
                                               n_cols=2 * C_KEY_DIM, is_qk=True)
            v_act, cst_v = _in_c_conv_prompt(h, prm['w_in_c_t'], prm['gdn_conv_w'], i, tm=tm, tn=512,
                                             col0=2 * C_KEY_DIM, n_cols=C_VAL_DIM, is_qk=False)
            cst = jnp.concatenate([cst_qk, cst_v], axis=-1)
            z = _mm([h], [(prm['w_in_c_t'], i, 0, True)], n_out=C_VAL_DIM, col0=C_QKV_DIM, name="in_c_z")
            ba = _in_proj_c_gates(h, prm, i)
            yc, s_new = _gdn_prompt(qk_act, v_act, z, ba, prm, i, tt=min(2 * CHUNK, T), kheads_per_step=C_K_HEADS // 2)
            x = _out_proj_c(yc, prm, i, x, mod)
            gdn.append(s_new)
            gdn_conv.append(cst)
        h = _norm(x, prm['norm_ffn'][layer], mod, MOD_SC2, MOD_SH2, tt=t_norm)
        act, fst = _ffn_up_prompt(h, prm['ffn_w_gate'], prm['ffn_w_up'], prm['ffn_conv_w'], prm['ffn_conv_b'],
                                  layer, tm=tm, tn=512)
        x = _ffn_down(act, prm, layer, x, mod)
        ffn_conv.append(fst)
    y = _norm(x, prm['norm_out'], tt=t_norm, out_dtype=F32)
    stk = jnp.stack
    return y, stk(wkv), stk(shift), stk(lru_h), stk(lru_conv), stk(gdn), stk(gdn_conv), stk(ffn_conv)


def _trunk_sample(x, mods, st, prm):
    bs = x.shape[0]
    x = x.reshape(1, bs, D_MODEL)
    wkv0, shift0, lru_h0, lru_conv0, gdn0, gdn_conv0, ffn_conv0 = st
    wkv_new = gdn_new = gconv_new = fconv_new = None
    gconv0_t = jnp.transpose(gdn_conv0, (0, 2, 1, 3))
    wkv0_t = jnp.transpose(wkv0, (0, 2, 3, 4, 1))
    shift, lru_h, lru_conv = [], [], []
    for layer in range(DEPTH):
        mod = mods[layer].reshape(1, bs, 6 * D_MODEL)
        i = layer // 2
        h = _norm(x, prm['norm_mix'][layer], mod, MOD_SC1, MOD_SH1, tt=bs)
        if layer % 2 == 0:
            p_main, p_lora, pb = _in_proj_ab(h, prm, i)
            p_main, p_lora, pb = p_main[0], p_lora[0], pb[0]
            vals = _rwkv_prep_sample(p_main, p_lora, shift0[i], prm, i)
            wkv_new, ya = _wkv_step(wkv0_t, wkv_new, vals, prm, i, heads_per_step=2)
            yb, h_last, cst = _lru_sample(pb, lru_conv0[i], lru_h0[i], prm, i)
            x = _out_proj_ab(ya[None], yb[None], prm, i, x, mod)
            shift.append(jnp.concatenate([p_main, p_lora[:, :LORA_ALL]], axis=-1))
            lru_h.append(h_last)
            lru_conv.append(cst)
        else:
            qkvz = _mm([h], [(prm['w_in_c_t'], i, 0, True)], n_out=C_QKV_DIM + C_VAL_DIM, name="in_c_main")
            ba = _in_proj_c_gates(h, prm, i)
            qkv_act, gconv_new = _gdn_prep_sample(qkvz[0], gconv0_t, gconv_new, prm['gdn_conv_w'], i, tc=2048)
            gdn_new, yc = _gdn_step(gdn0, gdn_new, qkv_act[:, None], qkvz[0][:, None], ba[0][:, None], prm, i,
                                    bb=min(STEP_GROUP, bs), kheads_per_step=2)
            x = _out_proj_c(yc.reshape(1, bs, C_VAL_DIM), prm, i, x, mod)
        h = _norm(x, prm['norm_ffn'][layer], mod, MOD_SC2, MOD_SH2, tt=bs)
        act, fconv_new = _ffn_up_sample(h[0], prm['ffn_w_gate'], prm['ffn_w_up'], prm['ffn_conv_w'],
                                        prm['ffn_conv_b'], ffn_conv0, fconv_new, layer, tn=512)
        x = _ffn_down(act[None], prm, layer, x, mod)
    y = _norm(x, prm['norm_out'], tt=bs, out_dtype=F32).reshape(bs, 1, D_MODEL)
    stk = jnp.stack
    return (y, jnp.transpose(wkv_new, (0, 4, 1, 2, 3)), stk(shift), stk(lru_h), stk(lru_conv), gdn_new,
            jnp.transpose(gconv_new, (0, 2, 1, 3)), fconv_new)


def kernel(x_prompt, x_sample, c_prompt, c_sample, state_rwkv_wkv, state_rwkv_shift, state_lru_h, state_lru_conv, state_gdn, state_gdn_conv, state_ffn_conv, w_mod, b_mod, norm_mix, norm_ffn, norm_out, w_in_ab, w_out_ab, rwkv_mu, rwkv_w0, rwkv_w2, rwkv_a0, rwkv_a2, rwkv_g2, rwkv_k_k, rwkv_k_a, rwkv_r_k, rwkv_gn_w, rwkv_gn_b, lru_conv_w, lru_conv_b, lru_wa, lru_ba, lru_wx, lru_bx, lru_lambda, w_in_c, w_out_c, gdn_conv_w, gdn_a_log, gdn_dt_bias, gdn_norm_w, ffn_w_gate, ffn_w_up, ffn_conv_w, ffn_conv_b, ffn_w_down):
    prm = dict(norm_mix=norm_mix, norm_ffn=norm_ffn, norm_out=norm_out, w_out_ab=w_out_ab,
               w_in_ab_t=jnp.swapaxes(w_in_ab, 1, 2), w_in_c_t=jnp.swapaxes(w_in_c, 1, 2),
               rwkv_mu=rwkv_mu, rwkv_w0=rwkv_w0, rwkv_w2=rwkv_w2, rwkv_a0=rwkv_a0, rwkv_a2=rwkv_a2,
               rwkv_g2=rwkv_g2, rwkv_k_k=rwkv_k_k, rwkv_k_a=rwkv_k_a,
               rwkv_r_k=rwkv_r_k.reshape(rwkv_r_k.shape[0], A_WIDTH), rwkv_gn_w=rwkv_gn_w,
               rwkv_gn_b=rwkv_gn_b, lru_conv_w=lru_conv_w, lru_conv_b=lru_conv_b, lru_wa=lru_wa, lru_ba=lru_ba,
               lru_wx=lru_wx, lru_bx=lru_bx, lru_lambda=lru_lambda, w_in_c=w_in_c, w_out_c=w_out_c,
               gdn_conv_w=gdn_conv_w, gdn_a_log=gdn_a_log, gdn_dt_bias=gdn_dt_bias, gdn_norm_w=gdn_norm_w,
               ffn_w_gate=ffn_w_gate, ffn_w_up=ffn_w_up, ffn_conv_w=ffn_conv_w, ffn_conv_b=ffn_conv_b,
               ffn_w_down=ffn_w_down)
    mods_p, mods_s = _modulation(c_prompt, c_sample, w_mod, b_mod)
    outs_p = _trunk_prompt(x_prompt, mods_p, prm)
    outs_s = _trunk_sample(x_sample, mods_s,
                           (state_rwkv_wkv, state_rwkv_shift, state_lru_h, state_lru_conv, state_gdn,
                            state_gdn_conv, state_ffn_conv), prm)
    return (outs_p[0], outs_s[0]) + tuple(outs_p[1:]) + tuple(outs_s[1:])
```

```python
import functools

import jax
import jax.numpy as jnp
from jax import lax
from jax.experimental import pallas as pl
from jax.experimental.pallas import tpu as pltpu

F32 = jnp.float32
BF16 = jnp.bfloat16

D_MODEL = 2048
DEPTH = 4
A_HEAD_DIM = 64
A_WIDTH = D_MODEL // 2
A_HEADS = A_WIDTH // A_HEAD_DIM
LORA_W = 64
LORA_A = 64
LORA_G = 160
LORA_ALL = LORA_W + LORA_A + LORA_G
P_A_MAIN = 3 * A_WIDTH
P_A = P_A_MAIN + LORA_ALL
A_GN_EPS = 64e-5
B_WIDTH = D_MODEL - A_WIDTH
B_BLOCK_DIM = 64
LRU_C = 8.0
C_HEAD = 128
C_K_HEADS = D_MODEL // C_HEAD
C_V_HEADS = 2 * C_K_HEADS
C_KEY_DIM = C_K_HEADS * C_HEAD
C_VAL_DIM = C_V_HEADS * C_HEAD
C_QKV_DIM = 2 * C_KEY_DIM + C_VAL_DIM
D_FF = 5632
NORM_EPS = 1e-6

LANES = 128
SUBLANES = 8
MXU_DEPTH = 256
HIST = SUBLANES
CHUNK = 64
VMEM_LIMIT = 56 * 1024 * 1024
MM_VMEM_BUDGET = 46 * 1024 * 1024

_NN = (((1,), (0,)), ((), ()))
_NT = (((1,), (1,)), ((), ()))
_TN = (((0,), (0,)), ((), ()))
_BNN = (((2,), (1,)), ((0,), (0,)))
_BNT = (((2,), (2,)), ((0,), (0,)))
_BTN = (((1,), (1,)), ((0,), (0,)))


def _cparams(n_axes):
    return pltpu.CompilerParams(dimension_semantics=("arbitrary",) * n_axes,
                                vmem_limit_bytes=VMEM_LIMIT)


def _bdot(a, b, dims=_NN):
    return lax.dot_general(a.astype(BF16), b.astype(BF16), dims, preferred_element_type=F32)


def _split2(a):
    hi = a.astype(BF16)
    lo = (a - hi.astype(F32)).astype(BF16)
    return hi, lo


def _dot3(a, b, dims=_NN):
    ah, al = _split2(a)
    bh, bl = _split2(b)
    d = lambda x, y: lax.dot_general(x, y, dims, preferred_element_type=F32)
    (lc,), (rc,) = dims[0]
    if 3 * a.shape[lc] <= MXU_DEPTH:
        return d(jnp.concatenate([ah, ah, al], axis=lc), jnp.concatenate([bh, bl, bh], axis=rc))
    return d(ah, bh) + (d(ah, bl) + d(al, bh))


def _split3(a):
    h0 = a.astype(BF16)
    r1 = a - h0.astype(F32)
    h1 = r1.astype(BF16)
    h2 = (r1 - h1.astype(F32)).astype(BF16)
    return h0, h1, h2


def _sel_l(mask01, x, dims=_NN):
    m = mask01.astype(BF16)
    d = lambda y: lax.dot_general(m, y, dims, preferred_element_type=F32)
    h0, h1, h2 = _split3(x)
    return d(h0) + (d(h1) + d(h2))


def _sel_r(x, mask01):
    m = mask01.astype(BF16)
    d = lambda y, w: lax.dot_general(y, w, _NN, preferred_element_type=F32)
    h0, h1, h2 = _split3(x)
    if 3 * x.shape[1] <= MXU_DEPTH:
        return d(jnp.concatenate([h0, h1, h2], axis=1), jnp.concatenate([m, m, m], axis=0))
    return d(h0, m) + (d(h1, m) + d(h2, m))


def _iota2(shape, axis):
    return lax.broadcasted_iota(jnp.int32, shape, axis)


def _tri(n, strict):
    i = _iota2((n, n), 0)
    j = _iota2((n, n), 1)
    return (i > j) if strict else (i >= j)


def _unit_lower_inverse(low, mm):
    n = low.shape[-1]
    dims = _BNN if low.ndim == 3 else _NN
    eye = (_iota2((n, n), 0) == _iota2((n, n), 1)).astype(F32)
    m = -low
    inv = eye + m
    p = m
    span = 2
    while span < n:
        p = mm(p, p, dims)
        inv = inv + mm(inv, p, dims)
        span *= 2
    return inv


def _unit_lower_inverse_refined(low):
    n = low.shape[-1]
    dims = _BNN if low.ndim == 3 else _NN
    eye = (_iota2((n, n), 0) == _iota2((n, n), 1)).astype(F32)
    x0 = _unit_lower_inverse(low, _bdot)
    resid = eye - _dot3(eye + low, x0, dims)
    return x0 + _bdot(x0, resid, dims)


def _block_tri(tt, block):
    i = _iota2((tt, tt), 0)
    j = _iota2((tt, tt), 1)
    return (i // block == j // block) & (i >= j)


def _neg_expm1(x):
    return -jnp.tanh(0.5 * x) * (jnp.exp(x) + 1.0)


def _rwkv_rows(p_main, q_main, p_lora, q_lora, mu_main, mu_lora, w0, w2, a0, a2, g2, k_k, k_a):
    pm = p_main + (q_main - p_main) * mu_main
    pl_ = p_lora + (q_lora - p_lora) * mu_lora
    r = pm[:, :A_WIDTH]
    k = pm[:, A_WIDTH:2 * A_WIDTH]
    v = pm[:, 2 * A_WIDTH:]
    xw = pl_[:, :LORA_W]
    xa = pl_[:, LORA_W:LORA_W + LORA_A]
    xg = pl_[:, LORA_W + LORA_A:LORA_ALL]
    w_ll = -jax.nn.softplus(-(w0 + _bdot(jnp.tanh(xw), w2))) - 0.5
    lw = -jnp.exp(w_ll)
    a = jax.nn.sigmoid(a0 + _bdot(xa, a2))
    g = _bdot(jax.nn.sigmoid(xg), g2)
    kkn = k * k_k
    kf = k * (1.0 + (a - 1.0) * k_a)
    return r, kf, v, kkn, a, lw, g


def _rwkv_head_out(y, r, kf, v, g, r_k, gn_w, gn_b):
    mean = jnp.mean(y, axis=-1, keepdims=True)
    var = jnp.mean(jnp.square(y - mean), axis=-1, keepdims=True)
    yn = (y - mean) * lax.rsqrt(var + A_GN_EPS) * gn_w + gn_b
    bonus = jnp.sum(r * kf * r_k, axis=-1, keepdims=True) * v
    return (yn + bonus) * g


def _lru_rows(gate, xc, wa_bd, wx_bd, ba, bx, lam):
    ng = xc.shape[1] // LANES
    ra = jnp.concatenate([_bdot(xc[:, g * LANES:(g + 1) * LANES], wa_bd[g]) for g in range(ng)], axis=1)
    rx = jnp.concatenate([_bdot(xc[:, g * LANES:(g + 1) * LANES], wx_bd[g]) for g in range(ng)], axis=1)
    r_gate = jax.nn.sigmoid(ra + ba)
    i_gate = jax.nn.sigmoid(rx + bx)
    log_a = -LRU_C * r_gate * jax.nn.softplus(-lam)
    a = jnp.exp(log_a)
    b = jnp.sqrt(_neg_expm1(2.0 * log_a)) * (i_gate * xc)
    return a, b, jax.nn.gelu(gate)


def _silu(x):
    return 0.5 * x * (1.0 + jnp.tanh(0.5 * x))


def _gdn_act(y, is_qk, q_scale):
    y = _silu(y)
    if is_qk is False:
        return y
    outs = []
    for h in range(y.shape[1] // C_HEAD):
        yh = y[:, h * C_HEAD:(h + 1) * C_HEAD]
        nrm = yh * (lax.rsqrt(jnp.sum(yh * yh, axis=-1, keepdims=True) + 1e-6) * q_scale)
        outs.append(nrm if is_qk is True else jnp.where(is_qk, nrm, yh))
    return jnp.concatenate(outs, axis=1)


def _gdn_head_out(o, z, norm_w):
    o = o * lax.rsqrt(jnp.mean(o * o, axis=-1, keepdims=True) + NORM_EPS) * norm_w
    return o * _silu(z)


def _w_block(w_ref):
    return w_ref[0] if len(w_ref.shape) == 3 else w_ref[...]


def _mm_body(*refs, n_x, has_res, transposed):
    x_refs = refs[:n_x]
    w_refs = refs[n_x:2 * n_x]
    k = 2 * n_x
    if has_res:
        res_ref, gate_ref = refs[k], refs[k + 1]
        k += 2
    o_ref = refs[k]
    wbf_refs = refs[k + 1:k + 1 + n_x]

    @pl.when((pl.program_id(1) == 0) & (pl.program_id(2) == 0))
    def _():
        for w_ref, wbf_ref in zip(w_refs, wbf_refs):
            wbf_ref[...] = _w_block(w_ref).astype(BF16)

    acc = None
    for x_ref, wbf_ref, tr in zip(x_refs, wbf_refs, transposed):
        d = lax.dot_general(x_ref[0], wbf_ref[...], _NT if tr else _NN, preferred_element_type=F32)
        acc = d if acc is None else acc + d
    if has_res:
        acc = res_ref[0] + gate_ref[0] * acc
    o_ref[0] = acc.astype(o_ref.dtype)


def _pick_tiles(n_out, k_sum, rows, seq, n_out_bufs):
    best = None
    for tm in [t for t in (1024, 512, 256, 128) if seq % t == 0] or [seq]:
        for tn in (1024, 512, 256, 128):
            if n_out % tn:
                continue
            need = (2 * tm * k_sum * 2 + 2 * k_sum * tn * 4 + k_sum * tn * 2
                    + (2 * n_out_bufs + 1) * tm * tn * 4)
            if need > MM_VMEM_BUDGET:
                continue
            traffic = (n_out // tn) * rows * k_sum * 2
            key = (tm < min(512, seq), traffic, -tn, -tm)
            if best is None or key < best[0]:
                best = (key, tm, tn)
    if best is None:
        raise ValueError("no projection tile fits VMEM")
    return best[1], best[2]


def _mm(xs, ws, *, n_out, col0=0, tm=None, tn=None, out_dtype=F32, res=None, gate=None, gate_blk=0, name="proj"):
    B, T, _ = xs[0].shape
    ks = [x.shape[-1] for x in xs]
    has_res = res is not None
    if tn is None:
        tm, tn = _pick_tiles(n_out, sum(ks), B * T, T, 2 if has_res else 1)
    elif tm is None:
        tm = min(1024, T)
    assert n_out % tn == 0 and T % tm == 0
    in_specs = [pl.BlockSpec((1, tm, k), lambda j, b, t: (b, t, 0)) for k in ks]
    scratch = []
    for k, (w, layer, rb, tr) in zip(ks, ws):
        if tr:
            assert col0 % SUBLANES == 0 and tn % SUBLANES == 0 and w.shape[2] == k
            in_specs.append(pl.BlockSpec(
                (pl.Element(1), pl.Element(tn), pl.Element(k)),
                functools.partial(lambda j, b, t, layer: (layer, pl.multiple_of(col0 + j * tn, SUBLANES), 0),
                                  layer=layer)))
            scratch.append(pltpu.VMEM((tn, k), BF16))
        else:
            assert col0 % tn == 0
            in_specs.append(pl.BlockSpec(
                (None, k, tn),
                functools.partial(lambda j, b, t, layer, rb: (layer, rb, col0 // tn + j), layer=layer, rb=rb)))
            scratch.append(pltpu.VMEM((k, tn), BF16))
    args = list(xs) + [w for w, _, _, _ in ws]
    if has_res:
        in_specs.append(pl.BlockSpec((1, tm, tn), lambda j, b, t: (b, t, j)))
        gb0 = gate_blk * (n_out // tn)
        if gate.shape[1] == 1:
            in_specs.append(pl.BlockSpec((1, 1, tn), lambda j, b, t: (b, 0, gb0 + j)))
        else:
            in_specs.append(pl.BlockSpec((1, tm, tn), lambda j, b, t: (b, t, gb0 + j)))
        args += [res, gate]
    return pl.pallas_call(
        functools.partial(_mm_body, n_x=len(xs), has_res=has_res, transposed=tuple(w[3] for w in ws)),
        grid=(n_out // tn, B, T // tm),
        in_specs=in_specs,
        out_specs=pl.BlockSpec((1, tm, tn), lambda j, b, t: (b, t, j)),
        out_shape=jax.ShapeDtypeStruct((B, T, n_out), out_dtype),
        scratch_shapes=scratch,
        compiler_params=_cparams(3),
        name=name,
    )(*args)


def _mod_body(c_ref, w_ref, b_ref, op_ref, os_ref):
    bs = os_ref.shape[1]
    y = _bdot(jax.nn.silu(c_ref[...]), w_ref[0]) + b_ref[0]
    os_ref[0] = y[:bs]
    op_ref[0] = y[bs:]


def _modulation(c_p, c_s, w_mod, b_mod):
    depth, d, n = w_mod.shape
    tn = 1024
    bp, bs = c_p.shape[0], c_s.shape[0]
    return pl.pallas_call(
        _mod_body,
        grid=(depth, n // tn),
        in_specs=[pl.BlockSpec((bs + bp, d), lambda l, j: (0, 0)),
                  pl.BlockSpec((1, d, tn), lambda l, j: (l, 0, j)),
                  pl.BlockSpec((1, 1, tn), lambda l, j: (l, 0, j))],
        out_specs=[pl.BlockSpec((1, bp, tn), lambda l, j: (l, 0, j)),
                   pl.BlockSpec((1, bs, tn), lambda l, j: (l, 0, j))],
        out_shape=[jax.ShapeDtypeStruct((depth, bp, n), F32),
                   jax.ShapeDtypeStruct((depth, bs, n), F32)],
        compiler_params=_cparams(2),
        name="modulation",
    )(jnp.concatenate([c_s, c_p], axis=0), w_mod, b_mod.reshape(depth, 1, n))


def _norm_body(*refs, has_mod):
    if has_mod:
        x_ref, g_ref, sc_ref, sh_ref, o_ref = refs
    else:
        x_ref, g_ref, o_ref = refs
    x = x_ref[0]
    y = x * lax.rsqrt(jnp.mean(x * x, axis=-1, keepdims=True) + NORM_EPS) * g_ref[...]
    if has_mod:
        y = y * (1.0 + sc_ref[0]) + sh_ref[0]
    o_ref[0] = y.astype(o_ref.dtype)


def _norm(x, gain, mod=None, sc_blk=0, sh_blk=0, *, tt, out_dtype=BF16):
    B, T, d = x.shape
    in_specs = [pl.BlockSpec((1, tt, d), lambda b, t: (b, t, 0)),
                pl.BlockSpec((1, d), lambda b, t: (0, 0))]
    args = [x, gain.reshape(1, d)]
    if mod is not None:
        for blk in (sc_blk, sh_blk):
            if mod.shape[1] == 1:
                in_specs.append(pl.BlockSpec((1, 1, d), functools.partial(lambda b, t, blk: (b, 0, blk), blk=blk)))
            else:
                in_specs.append(pl.BlockSpec((1, tt, d), functools.partial(lambda b, t, blk: (b, t, blk), blk=blk)))
        args += [mod, mod]
    return pl.pallas_call(
        functools.partial(_norm_body, has_mod=mod is not None),
        grid=(B, T // tt),
        in_specs=in_specs,
        out_specs=pl.BlockSpec((1, tt, d), lambda b, t: (b, t, 0)),
        out_shape=jax.ShapeDtypeStruct((B, T, d), out_dtype),
        compiler_params=_cparams(2),
        name="norm",
    )(*args)


def _ffn_up_prompt_body(h_ref, wg_ref, wu_ref, cw_ref, cb_ref, act_ref, st_ref, wg_bf, wu_bf, buf):
    t = pl.program_id(2)
    tm = h_ref.shape[1]

    @pl.when((pl.program_id(1) == 0) & (t == 0))
    def _():
        wg_bf[...] = wg_ref[0].astype(BF16)
        wu_bf[...] = wu_ref[0].astype(BF16)

    @pl.when(t == 0)
    def _():
        buf[0:HIST, :] = jnp.zeros((HIST, buf.shape[1]), F32)

    h = h_ref[0]
    buf[HIST:, :] = jnp.dot(h, wg_bf[...], preferred_element_type=F32)
    cw = cw_ref[0]
    u = (buf[HIST - 2:HIST - 2 + tm, :] * cw[0:1] + buf[HIST - 1:HIST - 1 + tm, :] * cw[1:2]
         + buf[HIST:, :] * cw[2:3] + cb_ref[0])
    up = jnp.dot(h, wu_bf[...], preferred_element_type=F32)
    act_ref[0] = (jax.nn.gelu(u) * up).astype(act_ref.dtype)
    st_ref[0] = buf[HIST + tm - 2:HIST + tm, :]
    buf[0:HIST, :] = buf[tm:tm + HIST, :]


def _ffn_up_prompt(h, w_gate, w_up, conv_w, conv_b, layer, *, tm, tn):
    B, T, d = h.shape
    n = w_gate.shape[-1]
    return pl.pallas_call(
        _ffn_up_prompt_body,
        grid=(n // tn, B, T // tm),
        in_specs=[pl.BlockSpec((1, tm, d), lambda j, b, t: (b, t, 0)),
                  pl.BlockSpec((1, d, tn), lambda j, b, t: (layer, 0, j)),
                  pl.BlockSpec((1, d, tn), lambda j, b, t: (layer, 0, j)),
                  pl.BlockSpec((1, 3, tn), lambda j, b, t: (layer, 0, j)),
                  pl.BlockSpec((1, 1, tn), lambda j, b, t: (layer, 0, j))],
        out_specs=[pl.BlockSpec((1, tm, tn), lambda j, b, t: (b, t, j)),
                   pl.BlockSpec((1, 2, tn), lambda j, b, t: (b, 0, j))],
        out_shape=[jax.ShapeDtypeStruct((B, T, n), BF16),
                   jax.ShapeDtypeStruct((B, 2, n), F32)],
        scratch_shapes=[pltpu.VMEM((d, tn), BF16), pltpu.VMEM((d, tn), BF16),
                        pltpu.VMEM((HIST + tm, tn), F32)],
        compiler_params=_cparams(3),
        name="ffn_up_prompt",
    )(h, w_gate, w_up, conv_w, conv_b.reshape(conv_b.shape[0], 1, n))


def _ffn_up_sample_body(*refs, n_state):
    st_ref = refs[0]
    h_ref, wg_ref, wu_ref, cw_ref, cb_ref, act_ref, nst_ref = refs[n_state:]
    h = h_ref[...]
    pre = _bdot(h, wg_ref[0])
    cw = cw_ref[0]
    u = st_ref[:, 0, :] * cw[0:1] + st_ref[:, 1, :] * cw[1:2] + pre * cw[2:3] + cb_ref[0]
    up = _bdot(h, wu_ref[0])
    act_ref[...] = (jax.nn.gelu(u) * up).astype(act_ref.dtype)
    nst_ref[:, 0, :] = st_ref[:, 1, :]
    nst_ref[:, 1, :] = pre


def _ffn_up_sample(h, w_gate, w_up, conv_w, conv_b, state_all, new_all, layer, *, tn):
    bs, d = h.shape
    n = w_gate.shape[-1]
    s_spec = pl.BlockSpec((None, bs, 2, tn), lambda j: (layer, 0, 0, j))
    st_specs, st_args, aliases = _layer_state_io(state_all, new_all, s_spec, out_index=1)
    return pl.pallas_call(
        functools.partial(_ffn_up_sample_body, n_state=len(st_args)),
        grid=(n // tn,),
        in_specs=st_specs + [pl.BlockSpec((bs, d), lambda j: (0, 0)),
                             pl.BlockSpec((1, d, tn), lambda j: (layer, 0, j)),
                             pl.BlockSpec((1, d, tn), lambda j: (layer, 0, j)),
                             pl.BlockSpec((1, 3, tn), lambda j: (layer, 0, j)),
                             pl.BlockSpec((1, 1, tn), lambda j: (layer, 0, j))],
        out_specs=[pl.BlockSpec((bs, tn), lambda j: (0, j)), s_spec],
        out_shape=[jax.ShapeDtypeStruct((bs, n), BF16),
                   jax.ShapeDtypeStruct(state_all.shape, F32)],
        input_output_aliases=aliases,
        compiler_params=_cparams(1),
        name="ffn_up_sample",
    )(*st_args, h, w_gate, w_up, conv_w, conv_b.reshape(conv_b.shape[0], 1, n))


def _rwkv_prep_prompt_body(pm_ref, pl_ref, mum_ref, mul_ref, w0_ref, w2_ref, a0_ref, a2_ref, g2_ref,
                           kk_ref, ka_ref, r_o, kf_o, v_o, kkn_o, a_o, lw_o, g_o, bufm, bufl):
    t = pl.program_id(1)
    tt = pm_ref.shape[1]

    @pl.when(t == 0)
    def _():
        bufm[0:HIST, :] = jnp.zeros((HIST, bufm.shape[1]), F32)
        bufl[0:HIST, :] = jnp.zeros((HIST, bufl.shape[1]), F32)

    p_main = pm_ref[0]
    p_lora = pl_ref[0][:, :LORA_ALL]
    bufm[HIST:, :] = p_main
    bufl[HIST:, :] = p_lora
    q_main = bufm[HIST - 1:HIST - 1 + tt, :]
    q_lora = bufl[HIST - 1:HIST - 1 + tt, :]
    outs = _rwkv_rows(p_main, q_main, p_lora, q_lora, mum_ref[...], mul_ref[...], w0_ref[...], w2_ref[0],
                      a0_ref[...], a2_ref[0], g2_ref[0], kk_ref[...], ka_ref[...])
    for o_ref, val in zip((r_o, kf_o, v_o, kkn_o, a_o, lw_o, g_o), outs):
        o_ref[0] = val
    bufm[0:HIST, :] = bufm[tt:tt + HIST, :]
    bufl[0:HIST, :] = bufl[tt:tt + HIST, :]


def _row(v):
    return v.reshape(1, -1)


def _rwkv_prep_prompt(p_main, p_lora, prm, i, *, tt):
    B, T, _ = p_main.shape
    lw_pad = p_lora.shape[-1]
    mu = prm['rwkv_mu'][i]
    full = lambda shape: pl.BlockSpec(shape, lambda b, t: (0,) * len(shape))
    lay3 = lambda shape: pl.BlockSpec((1,) + shape, lambda b, t: (i, 0, 0))
    out_spec = pl.BlockSpec((1, tt, A_WIDTH), lambda b, t: (b, t, 0))
    return pl.pallas_call(
        _rwkv_prep_prompt_body,
        grid=(B, T // tt),
        in_specs=[pl.BlockSpec((1, tt, P_A_MAIN), lambda b, t: (b, t, 0)),
                  pl.BlockSpec((1, tt, lw_pad), lambda b, t: (b, t, 0)),
                  full((1, P_A_MAIN)), full((1, LORA_ALL)), full((1, A_WIDTH)),
                  lay3((LORA_W, A_WIDTH)), full((1, A_WIDTH)), lay3((LORA_A, A_WIDTH)),
                  lay3((LORA_G, A_WIDTH)), full((1, A_WIDTH)), full((1, A_WIDTH))],
        out_specs=[out_spec] * 7,
        out_shape=[jax.ShapeDtypeStruct((B, T, A_WIDTH), F32)] * 7,
        scratch_shapes=[pltpu.VMEM((HIST + tt, P_A_MAIN), F32), pltpu.VMEM((HIST + tt, LORA_ALL), F32)],
        compiler_params=_cparams(2),
        name="rwkv_prep_prompt",
    )(p_main, p_lora, _row(mu[:P_A_MAIN]), _row(mu[P_A_MAIN:]), _row(prm['rwkv_w0'][i]), prm['rwkv_w2'],
      _row(prm['rwkv_a0'][i]), prm['rwkv_a2'], prm['rwkv_g2'], _row(prm['rwkv_k_k'][i]), _row(prm['rwkv_k_a'][i]))


def _wkv_prompt_body(r_ref, kf_ref, v_ref, kkn_ref, a_ref, lw_ref, g_ref, rk_ref, gw_ref, gb_ref,
                     ya_ref, st_ref, s_scr):
    t = pl.program_id(2)
    tt = r_ref.shape[1]
    n = A_HEAD_DIM
    C = CHUNK
    nc = tt // C
    nh = r_ref.shape[2] // n

    @pl.when(t == 0)
    def _():
        s_scr[...] = jnp.zeros(s_scr.shape, F32)

    def units(x):
        parts = [x[:, h * n:(h + 1) * n].reshape(nc, C, n) for h in range(nh)]
        return jnp.stack(parts, axis=1).reshape(nc * nh, C, n)

    lw_all = lw_ref[0]
    lc = units(_sel_l(_block_tri(tt, C), lw_all))
    r, kf, v, kkn, a, lw = map(units, (r_ref[0], kf_ref[0], v_ref[0], kkn_ref[0], a_ref[0], lw_all))
    incl = _tri(C, False)
    strict = _tri(C, True)

    kk = kkn * lax.rsqrt(jnp.sum(kkn * kkn, axis=-1, keepdims=True) + 1e-12)
    bb = kk * a
    p = jnp.exp(lc)
    pinv = jnp.exp(-lc)
    kt = kf * pinv
    bt = bb * pinv
    at = kk * jnp.exp(lc - lw)
    rt = r * p
    gram = _bdot(jnp.concatenate([at, rt], axis=1), jnp.concatenate([bt, kt], axis=1), _BNT)
    a_ab = jnp.where(strict, gram[:, :C, :C], 0.0)
    a_ak = jnp.where(strict, gram[:, :C, C:], 0.0)
    a_rb = jnp.where(incl, gram[:, C:, :C], 0.0)
    a_rk = jnp.where(incl, gram[:, C:, C:], 0.0)
    tinv = _unit_lower_inverse(a_ab, _bdot)
    av = _bdot(jnp.concatenate([a_ak, a_rk], axis=1), v, _BNN)
    tz = _bdot(tinv, jnp.concatenate([at, av[:, :C]], axis=2), _BNN)
    rz = _bdot(a_rb, tz, _BNN)
    lhs_s = jnp.concatenate([tz[:, :, :n], rt - rz[:, :, :n]], axis=1)
    z0 = tz[:, :, n:]
    y0 = av[:, C:] - rz[:, :, n:]
    plast = p[:, C - 1:C, :]
    upd = jnp.concatenate([kt * plast, -(bt * plast)], axis=1)

    S = s_scr[...]
    ys = []
    for c in range(nc):
        sl = slice(c * nh, (c + 1) * nh)
        xs = _bdot(lhs_s[sl], S, _BNT)
        z = xs[:, :C] + z0[sl]
        ys.append(xs[:, C:] + y0[sl])
        S = S * plast[sl] + _bdot(jnp.concatenate([v[sl], z], axis=1), upd[sl], _BTN)
    s_scr[...] = S

    outs = []
    for h in range(nh):
        cols = slice(h * n, (h + 1) * n)
        y = jnp.concatenate([ys[c][h] for c in range(nc)], axis=0)
        outs.append(_rwkv_head_out(y, r_ref[0, :, cols], kf_ref[0, :, cols], v_ref[0, :, cols],
                                   g_ref[0, :, cols], rk_ref[:, cols], gw_ref[:, cols], gb_ref[:, cols]))
    ya_ref[0] = jnp.concatenate(outs, axis=1).astype(ya_ref.dtype)

    @pl.when(t == pl.num_programs(2) - 1)
    def _():
        st_ref[0] = S


def _wkv_prompt(vals, prm, i, *, tt, heads_per_step):
    B, T, _ = vals[0].shape
    wblk = heads_per_step * A_HEAD_DIM
    seq_spec = pl.BlockSpec((1, tt, wblk), lambda b, h, t: (b, t, h))
    par_spec = pl.BlockSpec((1, wblk), lambda b, h, t: (0, h))
    return pl.pallas_call(
        _wkv_prompt_body,
        grid=(B, A_HEADS // heads_per_step, T // tt),
        in_specs=[seq_spec] * 7 + [par_spec] * 3,
        out_specs=[seq_spec,
                   pl.BlockSpec((1, heads_per_step, A_HEAD_DIM, A_HEAD_DIM), lambda b, h, t: (b, h, 0, 0))],
        out_shape=[jax.ShapeDtypeStruct((B, T, A_WIDTH), BF16),
                   jax.ShapeDtypeStruct((B, A_HEADS, A_HEAD_DIM, A_HEAD_DIM), F32)],
        scratch_shapes=[pltpu.VMEM((heads_per_step, A_HEAD_DIM, A_HEAD_DIM), F32)],
        compiler_params=_cparams(3),
        name="wkv_prompt",
    )(*vals, _row(prm['rwkv_r_k'][i]), _row(prm['rwkv_gn_w'][i]), _row(prm['rwkv_gn_b'][i]))


def _lru_prompt_body(gate_ref, xb_ref, cw_ref, cb_ref, wa_ref, wx_ref, ba_ref, bx_ref, lam_ref,
                     y_ref, h_ref, cst_ref, buf, h_scr):
    t = pl.program_id(1)
    tt = xb_ref.shape[1]
    w = xb_ref.shape[2]

    @pl.when(t == 0)
    def _():
        buf[0:HIST, :] = jnp.zeros((HIST, w), F32)
        h_scr[...] = jnp.zeros(h_scr.shape, F32)

    buf[HIST:, :] = xb_ref[0]
    cw = cw_ref[0]
    xc = cb_ref[...] + buf[HIST:, :] * cw[3:4]
    for j in range(3):
        xc = xc + buf[HIST - 3 + j:HIST - 3 + j + tt, :] * cw[j:j + 1]
    a, b, gact = _lru_rows(gate_ref[0], xc, wa_ref, wx_ref, ba_ref[...], bx_ref[...], lam_ref[...])
    row = _iota2((tt, w), 0)
    d = 1
    while d < tt:
        a_sh = jnp.where(row >= d, pltpu.roll(a, d, axis=0), 1.0)
        b_sh = jnp.where(row >= d, pltpu.roll(b, d, axis=0), 0.0)
        b = a * b_sh + b
        a = a * a_sh
        d *= 2
    h = a * h_scr[0:1, :] + b
    y_ref[0] = (h * gact).astype(y_ref.dtype)
    h_last = h[tt - 1:tt, :]
    h_scr[0:1, :] = h_last
    h_ref[0] = h_last
    cst_ref[0] = buf[HIST + tt - 3:HIST + tt, :]
    buf[0:HIST, :] = buf[tt:tt + HIST, :]


def _block_diag_pairs(w):
    nb, n, _ = w.shape
    w = w.reshape(nb // 2, 2, n, n)
    z = jnp.zeros((nb // 2, n, n), w.dtype)
    top = jnp.concatenate([w[:, 0], z], axis=2)
    bot = jnp.concatenate([z, w[:, 1]], axis=2)
    return jnp.concatenate([top, bot], axis=1)


def _lru_prompt(pb, prm, i, *, tt):
    B, T, _ = pb.shape
    w = B_WIDTH
    ng = w // LANES
    full = lambda shape: pl.BlockSpec(shape, lambda b, t: (0,) * len(shape))
    return pl.pallas_call(
        _lru_prompt_body,
        grid=(B, T // tt),
        in_specs=[pl.BlockSpec((1, tt, w), lambda b, t: (b, t, 0)),
                  pl.BlockSpec((1, tt, w), lambda b, t: (b, t, 1)),
                  pl.BlockSpec((1, 4, w), lambda b, t: (i, 0, 0)),
                  full((1, w)), full((ng, LANES, LANES)), full((ng, LANES, LANES)),
                  full((1, w)), full((1, w)), full((1, w))],
        out_specs=[pl.BlockSpec((1, tt, w), lambda b, t: (b, t, 0)),
                   pl.BlockSpec((1, 1, w), lambda b, t: (b, 0, 0)),
                   pl.BlockSpec((1, 3, w), lambda b, t: (b, 0, 0))],
        out_shape=[jax.ShapeDtypeStruct((B, T, w), BF16),
                   jax.ShapeDtypeStruct((B, 1, w), F32),
                   jax.ShapeDtypeStruct((B, 3, w), F32)],
        scratch_shapes=[pltpu.VMEM((HIST + tt, w), F32), pltpu.VMEM((SUBLANES, w), F32)],
        compiler_params=_cparams(2),
        name="lru_prompt",
    )(pb, pb, prm['lru_conv_w'], _row(prm['lru_conv_b'][i]), _block_diag_pairs(prm['lru_wa'][i]),
      _block_diag_pairs(prm['lru_wx'][i]), _row(prm['lru_ba'][i]), _row(prm['lru_bx'][i]),
      _row(prm['lru_lambda'][i]))


EPI_ROWS = 256


def _in_c_conv_prompt_body(h_ref, w_ref, cw_ref, o_ref, st_ref, w_bf, buf, *, is_qk):
    j = pl.program_id(0)
    t = pl.program_id(2)
    tm = h_ref.shape[1]
    tn = w_ref.shape[1]

    @pl.when((pl.program_id(1) == 0) & (t == 0))
    def _():
        w_bf[...] = w_ref[0].astype(BF16)

    @pl.when(t == 0)
    def _():
        buf[0:HIST, :] = jnp.zeros((HIST, tn), F32)

    cw = cw_ref[0]
    q_scale = jnp.where(j < C_KEY_DIM // tn, C_HEAD ** -0.5, 1.0).astype(F32)
    rc = min(EPI_ROWS, tm)
    for r0 in range(0, tm, rc):
        buf[HIST + r0:HIST + r0 + rc, :] = lax.dot_general(h_ref[0, r0:r0 + rc, :], w_bf[...], _NT,
                                                           preferred_element_type=F32)
        y = buf[HIST + r0:HIST + r0 + rc, :] * cw[3:4]
        for jj in range(3):
            y = y + buf[HIST - 3 + jj + r0:HIST - 3 + jj + r0 + rc, :] * cw[jj:jj + 1]
        o_ref[0, r0:r0 + rc, :] = _gdn_act(y, is_qk, q_scale)
    st_ref[0] = buf[HIST + tm - 3:HIST + tm, :]
    buf[0:HIST, :] = buf[tm:tm + HIST, :]


def _in_c_conv_prompt(h, w_t, conv_w, i, *, tm, tn, col0, n_cols, is_qk):
    B, T, d = h.shape
    assert col0 % tn == 0 and n_cols % tn == 0
    return pl.pallas_call(
        functools.partial(_in_c_conv_prompt_body, is_qk=is_qk),
        grid=(n_cols // tn, B, T // tm),
        in_specs=[pl.BlockSpec((1, tm, d), lambda j, b, t: (b, t, 0)),
                  pl.BlockSpec((pl.Element(1), pl.Element(tn), pl.Element(d)),
                               lambda j, b, t: (i, pl.multiple_of(col0 + j * tn, SUBLANES), 0)),
                  pl.BlockSpec((1, 4, tn), lambda j, b, t: (i, 0, col0 // tn + j))],
        out_specs=[pl.BlockSpec((1, tm, tn), lambda j, b, t: (b, t, j)),
                   pl.BlockSpec((1, 3, tn), lambda j, b, t: (b, 0, j))],
        out_shape=[jax.ShapeDtypeStruct((B, T, n_cols), F32),
                   jax.ShapeDtypeStruct((B, 3, n_cols), F32)],
        scratch_shapes=[pltpu.VMEM((tn, d), BF16), pltpu.VMEM((HIST + tm, tn), F32)],
        compiler_params=_cparams(3),
        name="in_c_conv_prompt",
    )(h, w_t, conv_w)


def _gdn_gates(ba, a_log, dt_bias):
    hv = C_V_HEADS
    beta = jax.nn.sigmoid(ba[..., :hv])
    g = -jnp.exp(a_log) * jax.nn.softplus(ba[..., hv:2 * hv] + dt_bias)
    return beta, g


def _lane_bcast_col(x, col):
    n = x.shape[1]
    onehot = _iota2((n, LANES), 0) == col
    return _sel_r(x, onehot)


def _gdn_prompt_body(q_ref, k_ref, v_ref, z_ref, ba_ref, alog_ref, dtb_ref, nw_ref, o_ref, st_ref, s_scr):
    hblk = pl.program_id(1)
    t = pl.program_id(2)
    tt = q_ref.shape[1]
    C = CHUNK
    nc = tt // C
    rep = C_V_HEADS // C_K_HEADS
    n_kh = q_ref.shape[2] // C_HEAD
    nh = n_kh * rep

    @pl.when(t == 0)
    def _():
        s_scr[...] = jnp.zeros(s_scr.shape, F32)

    beta_all, g_all = _gdn_gates(ba_ref[0], alog_ref[...], dtb_ref[...])
    gc_all = _sel_l(_block_tri(tt, C), g_all)
    blocks = lambda x: x.reshape(nc, C, x.shape[-1])
    qs, ks, vs, betas, gcs = [], [], [], [], []
    for kh in range(n_kh):
        q_h = blocks(q_ref[0, :, kh * C_HEAD:(kh + 1) * C_HEAD])
        k_h = blocks(k_ref[0, :, kh * C_HEAD:(kh + 1) * C_HEAD])
        for jv in range(rep):
            hu = kh * rep + jv
            hv = hblk * nh + hu
            qs.append(q_h)
            ks.append(k_h)
            vs.append(blocks(v_ref[0, :, hu * C_HEAD:(hu + 1) * C_HEAD]))
            betas.append(blocks(_lane_bcast_col(beta_all, hv)))
            gcs.append(blocks(_lane_bcast_col(gc_all, hv)))
    stack = lambda xs: jnp.stack(xs, axis=1).reshape(nc * nh, C, xs[0].shape[-1])
    q, k, v, beta, gc = map(stack, (qs, ks, vs, betas, gcs))
    incl = _tri(C, False)
    strict = _tri(C, True)

    gcc = gc[:, :, :C]
    diff = gcc - jnp.swapaxes(gcc, 1, 2)
    dec_incl = jnp.exp(jnp.where(incl, diff, -jnp.inf))
    dec_strict = jnp.where(strict, dec_incl, 0.0)
    kb = k * beta
    gram = _bdot(jnp.concatenate([kb, q], axis=1), k, _BNT)
    low = gram[:, :C] * dec_strict
    qk = gram[:, C:] * dec_incl
    tinv = _unit_lower_inverse_refined(low)
    eg = jnp.exp(gc)
    sol = _bdot(tinv, jnp.concatenate([v * beta, kb * eg], axis=2), _BNN)
    u = sol[:, :, :C_HEAD]
    lhs_s = jnp.concatenate([sol[:, :, C_HEAD:], q * eg], axis=1)
    g_last = gc[:, C - 1:C, :]
    k_tail = k * jnp.exp(g_last - gc)
    e_last = jnp.exp(g_last)

    S = s_scr[...]
    os_ = []
    for c in range(nc):
        sl = slice(c * nh, (c + 1) * nh)
        ws = _bdot(lhs_s[sl], S, _BNN)
        v_new = u[sl] - ws[:, :C]
        os_.append(ws[:, C:] + _bdot(qk[sl], v_new, _BNN))
        S = S * e_last[sl] + _bdot(k_tail[sl], v_new, _BTN)
    s_scr[...] = S

    outs = []
    for hu in range(nh):
        cols = slice(hu * C_HEAD, (hu + 1) * C_HEAD)
        o = jnp.concatenate([os_[c][hu] for c in range(nc)], axis=0)
        outs.append(_gdn_head_out(o, z_ref[0, :, cols], nw_ref[...]))
    o_ref[0] = jnp.concatenate(outs, axis=1).astype(o_ref.dtype)

    @pl.when(t == pl.num_programs(2) - 1)
    def _():
        st_ref[0] = S


def _gdn_prompt(qk_act, v_act, z, ba, prm, i, *, tt, kheads_per_step):
    B, T, _ = qk_act.shape
    rep = C_V_HEADS // C_K_HEADS
    kw = kheads_per_step * C_HEAD
    vw = rep * kw
    k_blk0 = C_KEY_DIM // kw
    v_blk0 = 0
    z_blk0 = 0
    nh = kheads_per_step * rep
    full = lambda shape: pl.BlockSpec(shape, lambda b, h, t: (0,) * len(shape))
    return pl.pallas_call(
        _gdn_prompt_body,
        grid=(B, C_K_HEADS // kheads_per_step, T // tt),
        in_specs=[pl.BlockSpec((1, tt, kw), lambda b, h, t: (b, t, h)),
                  pl.BlockSpec((1, tt, kw), lambda b, h, t: (b, t, k_blk0 + h)),
                  pl.BlockSpec((1, tt, vw), lambda b, h, t: (b, t, v_blk0 + h)),
                  pl.BlockSpec((1, tt, vw), lambda b, h, t: (b, t, z_blk0 + h)),
                  pl.BlockSpec((1, tt, 2 * C_V_HEADS), lambda b, h, t: (b, t, 0)),
                  full((1, C_V_HEADS)), full((1, C_V_HEADS)), full((1, C_HEAD))],
        out_specs=[pl.BlockSpec((1, tt, vw), lambda b, h, t: (b, t, h)),
                   pl.BlockSpec((1, nh, C_HEAD, C_HEAD), lambda b, h, t: (b, h, 0, 0))],
        out_shape=[jax.ShapeDtypeStruct((B, T, C_VAL_DIM), BF16),
                   jax.ShapeDtypeStruct((B, C_V_HEADS, C_HEAD, C_HEAD), F32)],
        scratch_shapes=[pltpu.VMEM((nh, C_HEAD, C_HEAD), F32)],
        compiler_params=_cparams(3),
        name="gdn_prompt",
    )(qk_act, qk_act, v_act, z, ba, _row(prm['gdn_a_log'][i]), _row(prm['gdn_dt_bias'][i]),
      _row(prm['gdn_norm_w'][i]))


def _rwkv_prep_sample_body(pm_ref, pl_ref, qm_ref, ql_ref, mum_ref, mul_ref, w0_ref, w2_ref, a0_ref, a2_ref,
                           g2_ref, kk_ref, ka_ref, r_o, kf_o, v_o, kkn_o, a_o, lw_o, g_o):
    outs = _rwkv_rows(pm_ref[...], qm_ref[...], pl_ref[...][:, :LORA_ALL], ql_ref[...], mum_ref[...],
                      mul_ref[...], w0_ref[...], w2_ref[0], a0_ref[...], a2_ref[0], g2_ref[0],
                      kk_ref[...], ka_ref[...])
    for o_ref, val in zip((r_o, kf_o, v_o, kkn_o, a_o, lw_o, g_o), outs):
        o_ref[...] = val.T


def _rwkv_prep_sample(p_main, p_lora, shift, prm, i):
    bs = p_main.shape[0]
    mu = prm['rwkv_mu'][i]
    full = lambda shape: pl.BlockSpec(shape, lambda s: (0,) * len(shape))
    lay3 = lambda shape: pl.BlockSpec((1,) + shape, lambda s: (i, 0, 0))
    return pl.pallas_call(
        _rwkv_prep_sample_body,
        grid=(1,),
        in_specs=[full(p_main.shape), full(p_lora.shape), full((bs, P_A_MAIN)), full((bs, LORA_ALL)),
                  full((1, P_A_MAIN)), full((1, LORA_ALL)), full((1, A_WIDTH)),
                  lay3((LORA_W, A_WIDTH)), full((1, A_WIDTH)), lay3((LORA_A, A_WIDTH)),
                  lay3((LORA_G, A_WIDTH)), full((1, A_WIDTH)), full((1, A_WIDTH))],
        out_specs=[full((A_WIDTH, bs))] * 7,
        out_shape=[jax.ShapeDtypeStruct((A_WIDTH, bs), F32)] * 7,
        compiler_params=_cparams(1),
        name="rwkv_prep_sample",
    )(p_main, p_lora, shift[:, :P_A_MAIN], shift[:, P_A_MAIN:], _row(mu[:P_A_MAIN]), _row(mu[P_A_MAIN:]),
      _row(prm['rwkv_w0'][i]), prm['rwkv_w2'], _row(prm['rwkv_a0'][i]), prm['rwkv_a2'], prm['rwkv_g2'],
      _row(prm['rwkv_k_k'][i]), _row(prm['rwkv_k_a'][i]))


STEP_GROUP = 16


def _wkv_step_body(*refs, n_state):
    s_ref = refs[0]
    r_ref, kf_ref, v_ref, kkn_ref, a_ref, lw_ref, g_ref, rk_ref, gw_ref, gb_ref, ns_ref, y_ref = refs[n_state:]
    n = A_HEAD_DIM
    outs = []
    for h in range(s_ref.shape[0]):
        rows = slice(h * n, (h + 1) * n)
        ld = lambda ref: ref[rows, :]
        r, kf, v, kkn, a = ld(r_ref), ld(kf_ref), ld(v_ref), ld(kkn_ref), ld(a_ref)
        kk = kkn * lax.rsqrt(jnp.sum(kkn * kkn, axis=0, keepdims=True) + 1e-12)
        S = s_ref[h]
        u = jnp.sum(S * kk[None], axis=1)
        s_new = S * jnp.exp(ld(lw_ref))[None] - u[:, None, :] * (kk * a)[None] + v[:, None, :] * kf[None]
        ns_ref[h] = s_new
        y = jnp.sum(s_new * r[None], axis=1)
        mean = jnp.mean(y, axis=0, keepdims=True)
        var = jnp.mean(jnp.square(y - mean), axis=0, keepdims=True)
        yn = (y - mean) * lax.rsqrt(var + A_GN_EPS) * ld(gw_ref) + ld(gb_ref)
        bonus = jnp.sum(r * kf * ld(rk_ref), axis=0, keepdims=True) * v
        outs.append(((yn + bonus) * ld(g_ref)).T)
    y_ref[...] = jnp.concatenate(outs, axis=1).astype(y_ref.dtype)


def _layer_state_io(state_all, new_all, spec, out_index=0):
    if new_all is None:
        return [spec], [state_all], {}
    return [spec, pl.BlockSpec(memory_space=pl.ANY)], [state_all, new_all], {1: out_index}


def _wkv_step(S_all, new_all, vals, prm, i, *, heads_per_step):
    _, n_heads, n, _, bs = S_all.shape
    wblk = heads_per_step * n
    lanes = lambda p: jnp.broadcast_to(p.reshape(A_WIDTH, 1), (A_WIDTH, bs))
    row_spec = pl.BlockSpec((wblk, bs), lambda h: (h, 0))
    s_spec = pl.BlockSpec((None, heads_per_step, n, n, bs), lambda h: (i, h, 0, 0, 0))
    st_specs, st_args, aliases = _layer_state_io(S_all, new_all, s_spec)
    return pl.pallas_call(
        functools.partial(_wkv_step_body, n_state=len(st_args)),
        grid=(n_heads // heads_per_step,),
        in_specs=st_specs + [row_spec] * 10,
        out_specs=[s_spec, pl.BlockSpec((bs, wblk), lambda h: (0, h))],
        out_shape=[jax.ShapeDtypeStruct(S_all.shape, F32), jax.ShapeDtypeStruct((bs, A_WIDTH), BF16)],
        input_output_aliases=aliases,
        compiler_params=_cparams(1),
        name="wkv_step",
    )(*st_args, *vals, lanes(prm['rwkv_r_k'][i]), lanes(prm['rwkv_gn_w'][i]), lanes(prm['rwkv_gn_b'][i]))


def _lru_sample_body(gate_ref, xb_ref, cst_ref, h0_ref, cw_ref, cb_ref, wa_ref, wx_ref, ba_ref, bx_ref, lam_ref,
                     y_ref, h_ref, ncst_ref):
    cw = cw_ref[0]
    xb = xb_ref[...]
    xc = cb_ref[...] + xb * cw[3:4]
    for j in range(3):
        xc = xc + cst_ref[:, j, :] * cw[j:j + 1]
    a, b, gact = _lru_rows(gate_ref[...], xc, wa_ref, wx_ref, ba_ref[...], bx_ref[...], lam_ref[...])
    h = a * h0_ref[...] + b
    y_ref[...] = (h * gact).astype(y_ref.dtype)
    h_ref[...] = h
    ncst_ref[:, 0, :] = cst_ref[:, 1, :]
    ncst_ref[:, 1, :] = cst_ref[:, 2, :]
    ncst_ref[:, 2, :] = xb


def _lru_sample(pb, conv_state, h0, prm, i):
    bs = pb.shape[0]
    w = B_WIDTH
    ng = w // LANES
    full = lambda shape: pl.BlockSpec(shape, lambda s: (0,) * len(shape))
    return pl.pallas_call(
        _lru_sample_body,
        grid=(1,),
        in_specs=[pl.BlockSpec((bs, w), lambda s: (0, 0)), pl.BlockSpec((bs, w), lambda s: (0, 1)),
                  full((bs, 3, w)), full((bs, w)),
                  pl.BlockSpec((1, 4, w), lambda s: (i, 0, 0)),
                  full((1, w)), full((ng, LANES, LANES)), full((ng, LANES, LANES)),
                  full((1, w)), full((1, w)), full((1, w))],
        out_specs=[full((bs, w)), full((bs, w)), full((bs, 3, w))],
        out_shape=[jax.ShapeDtypeStruct((bs, w), BF16), jax.ShapeDtypeStruct((bs, w), F32),
                   jax.ShapeDtypeStruct((bs, 3, w), F32)],
        compiler_params=_cparams(1),
        name="lru_sample",
    )(pb, pb, conv_state, h0, prm['lru_conv_w'], _row(prm['lru_conv_b'][i]),
      _block_diag_pairs(prm['lru_wa'][i]), _block_diag_pairs(prm['lru_wx'][i]), _row(prm['lru_ba'][i]),
      _row(prm['lru_bx'][i]), _row(prm['lru_lambda'][i]))


def _gdn_prep_sample_body(*refs, n_state):
    cst_ref = refs[0]
    x_ref, cw_ref, o_ref, ncst_ref = refs[n_state:]
    j = pl.program_id(0)
    cw = cw_ref[0]
    x = x_ref[...]
    y = x * cw[3:4]
    for jj in range(3):
        y = y + cst_ref[jj] * cw[jj:jj + 1]
    n_qk_blocks = 2 * C_KEY_DIM // x.shape[1]
    q_scale = jnp.where(j < n_qk_blocks // 2, C_HEAD ** -0.5, 1.0).astype(F32)
    o_ref[...] = _gdn_act(y, j < n_qk_blocks, q_scale)
    ncst_ref[0] = cst_ref[1]
    ncst_ref[1] = cst_ref[2]
    ncst_ref[2] = x


def _gdn_prep_sample(qkvz, conv_all, new_all, conv_w, i, *, tc):
    bs = qkvz.shape[0]
    s_spec = pl.BlockSpec((None, 3, bs, tc), lambda j: (i, 0, 0, j))
    st_specs, st_args, aliases = _layer_state_io(conv_all, new_all, s_spec, out_index=1)
    return pl.pallas_call(
        functools.partial(_gdn_prep_sample_body, n_state=len(st_args)),
        grid=(C_QKV_DIM // tc,),
        in_specs=st_specs + [pl.BlockSpec((bs, tc), lambda j: (0, j)),
                             pl.BlockSpec((1, 4, tc), lambda j: (i, 0, j))],
        out_specs=[pl.BlockSpec((bs, tc), lambda j: (0, j)), s_spec],
        out_shape=[jax.ShapeDtypeStruct((bs, C_QKV_DIM), F32),
                   jax.ShapeDtypeStruct(conv_all.shape, F32)],
        input_output_aliases=aliases,
        compiler_params=_cparams(1),
        name="gdn_prep_sample",
    )(*st_args, qkvz, conv_w)


def _gdn_step_body(*refs, n_state):
    s_ref = refs[0]
    q_ref, k_ref, v_ref, z_ref, ba_ref, alog_ref, dtb_ref, nw_ref, ns_ref, o_ref = refs[n_state:]
    bb = s_ref.shape[0]
    rep = C_V_HEADS // C_K_HEADS
    n_kh = q_ref.shape[2] // C_HEAD
    hblk = pl.program_id(1)
    beta_all, g_all = _gdn_gates(ba_ref[...], alog_ref[...], dtb_ref[...])
    lane = lax.broadcasted_iota(jnp.int32, beta_all.shape, 2)
    pick = lambda x, col: jnp.sum(jnp.where(lane == col, x, 0.0), axis=-1, keepdims=True)
    rid = lax.broadcasted_iota(jnp.int32, (bb, SUBLANES, C_HEAD), 1)
    up = lambda x: jnp.broadcast_to(x, (bb, SUBLANES, C_HEAD))
    for kh in range(n_kh):
        q = q_ref[:, :, kh * C_HEAD:(kh + 1) * C_HEAD]
        k = k_ref[:, :, kh * C_HEAD:(kh + 1) * C_HEAD]
        qk = jnp.sum(q * k, axis=-1, keepdims=True)
        for jv in range(rep):
            hu = kh * rep + jv
            cols = slice(hu * C_HEAD, (hu + 1) * C_HEAD)
            hv = (hblk * n_kh + kh) * rep + jv
            beta = pick(beta_all, hv)
            eg = jnp.exp(pick(g_all, hv))
            v = v_ref[:, :, cols]
            S = s_ref[:, hu]
            kb = k * beta
            lhs = jnp.where(rid == 0, up(kb * eg), jnp.where(rid == 1, up(q * eg), 0.0))
            ws = _bdot(lhs, S, _BNN)
            v_new = v * beta - ws[:, 0:1]
            o = ws[:, 1:2] + qk * v_new
            left = jnp.where(rid == 0, up(k), 0.0)
            right = jnp.where(rid == 0, up(v_new), 0.0)
            ns_ref[:, hu] = S * eg + _bdot(left, right, _BTN)
            o_ref[:, :, cols] = _gdn_head_out(o, z_ref[:, :, cols], nw_ref[...]).astype(o_ref.dtype)


def _gdn_step(S_all, new_all, qkv_act, qkvz, ba, prm, i, *, bb, kheads_per_step):
    bs = S_all.shape[1]
    rep = C_V_HEADS // C_K_HEADS
    kw = kheads_per_step * C_HEAD
    vw = rep * kw
    k_blk0 = C_KEY_DIM // kw
    v_blk0 = 2 * C_KEY_DIM // vw
    z_blk0 = C_QKV_DIM // vw
    full = lambda shape: pl.BlockSpec(shape, lambda b, h: (0,) * len(shape))
    s_spec = pl.BlockSpec((None, bb, kheads_per_step * rep, C_HEAD, C_HEAD), lambda b, h: (i, b, h, 0, 0))
    st_specs, st_args, aliases = _layer_state_io(S_all, new_all, s_spec)
    return pl.pallas_call(
        functools.partial(_gdn_step_body, n_state=len(st_args)),
        grid=(bs // bb, C_K_HEADS // kheads_per_step),
        in_specs=st_specs + [
                  pl.BlockSpec((bb, 1, kw), lambda b, h: (b, 0, h)),
                  pl.BlockSpec((bb, 1, kw), lambda b, h: (b, 0, k_blk0 + h)),
                  pl.BlockSpec((bb, 1, vw), lambda b, h: (b, 0, v_blk0 + h)),
                  pl.BlockSpec((bb, 1, vw), lambda b, h: (b, 0, z_blk0 + h)),
                  pl.BlockSpec((bb, 1, 2 * C_V_HEADS), lambda b, h: (b, 0, 0)),
                  full((1, C_V_HEADS)), full((1, C_V_HEADS)), full((1, C_HEAD))],
        out_specs=[s_spec, pl.BlockSpec((bb, 1, vw), lambda b, h: (b, 0, h))],
        out_shape=[jax.ShapeDtypeStruct(S_all.shape, F32), jax.ShapeDtypeStruct((bs, 1, C_VAL_DIM), BF16)],
        input_output_aliases=aliases,
        compiler_params=_cparams(2),
        name="gdn_step",
    )(*st_args, qkv_act, qkv_act, qkv_act, qkvz, ba, _row(prm['gdn_a_log'][i]), _row(prm['gdn_dt_bias'][i]),
      _row(prm['gdn_norm_w'][i]))


MOD_SH1, MOD_SC1, MOD_GT1, MOD_SH2, MOD_SC2, MOD_GT2 = range(6)


def _in_proj_ab(h, prm, i):
    w = [(prm['w_in_ab_t'], i, 0, True)]
    p_main = _mm([h], w, n_out=P_A_MAIN, name="in_ab_main")
    p_lora = _mm([h], w, n_out=LORA_ALL, col0=P_A_MAIN, tn=LORA_ALL, name="in_ab_lora")
    pb = _mm([h], w, n_out=2 * B_WIDTH, col0=P_A, name="in_ab_lru")
    return p_main, p_lora, pb


def _in_proj_c_gates(h, prm, i):
    return _mm([h], [(prm['w_in_c_t'], i, 0, True)], n_out=2 * C_V_HEADS, col0=C_QKV_DIM + C_VAL_DIM,
               tn=2 * C_V_HEADS, name="in_c_gates")


def _ffn_down(act, prm, layer, x, mod):
    return _mm([act], [(prm['ffn_w_down'], layer, 0, False)], n_out=D_MODEL, res=x, gate=mod, gate_blk=MOD_GT2,
               name="ffn_down")


def _out_proj_ab(ya, yb, prm, i, x, mod):
    w = prm['w_out_ab']
    return _mm([ya, yb], [(w, i, 0, False), (w, i, 1, False)], n_out=D_MODEL, res=x, gate=mod, gate_blk=MOD_GT1,
               name="out_ab")


def _out_proj_c(yc, prm, i, x, mod):
    return _mm([yc], [(prm['w_out_c'], i, 0, False)], n_out=D_MODEL, res=x, gate=mod, gate_blk=MOD_GT1,
               name="out_c")


def _trunk_prompt(x, mods, prm):
    B, T, _ = x.shape
    tm, tt, t_norm = min(1024, T), min(256, T), min(512, T)
    wkv, shift, lru_h, lru_conv, gdn, gdn_conv, ffn_conv = [], [], [], [], [], [], []
    for layer in range(DEPTH):
        mod = mods[layer].reshape(B, 1, 6 * D_MODEL)
        i = layer // 2
        h = _norm(x, prm['norm_mix'][layer], mod, MOD_SC1, MOD_SH1, tt=t_norm)
        if layer % 2 == 0:
            p_main, p_lora, pb = _in_proj_ab(h, prm, i)
            vals = _rwkv_prep_prompt(p_main, p_lora, prm, i, tt=tt)
            ya, s_new = _wkv_prompt(vals, prm, i, tt=tt, heads_per_step=A_HEADS)
            yb, h_last, cst = _lru_prompt(pb, prm, i, tt=tt)
            x = _out_proj_ab(ya, yb, prm, i, x, mod)
            wkv.append(s_new)
            shift.append(jnp.concatenate([p_main[:, -1], p_lora[:, -1, :LORA_ALL]], axis=-1))
            lru_h.append(h_last[:, 0])
            lru_conv.append(cst)
        else:
            qk_act, cst_qk = _in_c_conv_prompt(h, prm['w_in_c_t'], prm['gdn_conv_w'], i, tm=tm, tn=512, col0=0,
                                               n_cols=2 * C_KEY_DIM, is_qk=True)
            v_act, cst_v = _in_c_conv_prompt(h, prm['w_in_c_t'], prm['gdn_conv_w'], i, tm=tm, tn=512,
                                             col0=2 * C_KEY_DIM, n_cols=C_VAL_DIM, is_qk=False)
            cst = jnp.concatenate([cst_qk, cst_v], axis=-1)
            z = _mm([h], [(prm['w_in_c_t'], i, 0, True)], n_out=C_VAL_DIM, col0=C_QKV_DIM, name="in_c_z")
            ba = _in_proj_c_gates(h, prm, i)
            yc, s_new = _gdn_prompt(qk_act, v_act, z, ba, prm, i, tt=min(2 * CHUNK, T), kheads_per_step=C_K_HEADS // 2)
            x = _out_proj_c(yc, prm, i, x, mod)
            gdn.append(s_new)
            gdn_conv.append(cst)
        h = _norm(x, prm['norm_ffn'][layer], mod, MOD_SC2, MOD_SH2, tt=t_norm)
        act, fst = _ffn_up_prompt(h, prm['ffn_w_gate'], prm['ffn_w_up'], prm['ffn_conv_w'], prm['ffn_conv_b'],
                                  layer, tm=tm, tn=512)
        x = _ffn_down(act, prm, layer, x, mod)
        ffn_conv.append(fst)
    y = _norm(x, prm['norm_out'], tt=t_norm, out_dtype=F32)
    stk = jnp.stack
    return y, stk(wkv), stk(shift), stk(lru_h), stk(lru_conv), stk(gdn), stk(gdn_conv), stk(ffn_conv)


def _trunk_sample(x, mods, st, prm):
    bs = x.shape[0]
    x = x.reshape(1, bs, D_MODEL)
    wkv0, shift0, lru_h0, lru_conv0, gdn0, gdn_conv0, ffn_conv0 = st
    wkv_new = gdn_new = gconv_new = fconv_new = None
    gconv0_t = jnp.transpose(gdn_conv0, (0, 2, 1, 3))
    wkv0_t = jnp.transpose(wkv0, (0, 2, 3, 4, 1))
    shift, lru_h, lru_conv = [], [], []
    for layer in range(DEPTH):
        mod = mods[layer].reshape(1, bs, 6 * D_MODEL)
        i = layer // 2
        h = _norm(x, prm['norm_mix'][layer], mod, MOD_SC1, MOD_SH1, tt=bs)
        if layer % 2 == 0:
            p_main, p_lora, pb = _in_proj_ab(h, prm, i)
            p_main, p_lora, pb = p_main[0], p_lora[0], pb[0]
            vals = _rwkv_prep_sample(p_main, p_lora, shift0[i], prm, i)
            wkv_new, ya = _wkv_step(wkv0_t, wkv_new, vals, prm, i, heads_per_step=2)
            yb, h_last, cst = _lru_sample(pb, lru_conv0[i], lru_h0[i], prm, i)
            x = _out_proj_ab(ya[None], yb[None], prm, i, x, mod)
            shift.append(jnp.concatenate([p_main, p_lora[:, :LORA_ALL]], axis=-1))
            lru_h.append(h_last)
            lru_conv.append(cst)
        else:
            qkvz = _mm([h], [(prm['w_in_c_t'], i, 0, True)], n_out=C_QKV_DIM + C_VAL_DIM, name="in_c_main")
            ba = _in_proj_c_gates(h, prm, i)
            qkv_act, gconv_new = _gdn_prep_sample(qkvz[0], gconv0_t, gconv_new, prm['gdn_conv_w'], i, tc=2048)
            gdn_new, yc = _gdn_step(gdn0, gdn_new, qkv_act[:, None], qkvz[0][:, None], ba[0][:, None], prm, i,
                                    bb=min(STEP_GROUP, bs), kheads_per_step=4)
            x = _out_proj_c(yc.reshape(1, bs, C_VAL_DIM), prm, i, x, mod)
        h = _norm(x, prm['norm_ffn'][layer], mod, MOD_SC2, MOD_SH2, tt=bs)
        act, fconv_new = _ffn_up_sample(h[0], prm['ffn_w_gate'], prm['ffn_w_up'], prm['ffn_conv_w'],
                                        prm['ffn_conv_b'], ffn_conv0, fconv_new, layer, tn=512)
        x = _ffn_down(act[None], prm, layer, x, mod)
    y = _norm(x, prm['norm_out'], tt=bs, out_dtype=F32).reshape(bs, 1, D_MODEL)
    stk = jnp.stack
    return (y, jnp.transpose(wkv_new, (0, 4, 1, 2, 3)), stk(shift), stk(lru_h), stk(lru_conv), gdn_new,
            jnp.transpose(gconv_new, (0, 2, 1, 3)), fconv_new)


def kernel(x_prompt, x_sample, c_prompt, c_sample, state_rwkv_wkv, state_rwkv_shift, state_lru_h, state_lru_conv, state_gdn, state_gdn_conv, state_ffn_conv, w_mod, b_mod, norm_mix, norm_ffn, norm_out, w_in_ab, w_out_ab, rwkv_mu, rwkv_w0, rwkv_w2, rwkv_a0, rwkv_a2, rwkv_g2, rwkv_k_k, rwkv_k_a, rwkv_r_k, rwkv_gn_w, rwkv_gn_b, lru_conv_w, lru_conv_b, lru_wa, lru_ba, lru_wx, lru_bx, lru_lambda, w_in_c, w_out_c, gdn_conv_w, gdn_a_log, gdn_dt_bias, gdn_norm_w, ffn_w_gate, ffn_w_up, ffn_conv_w, ffn_conv_b, ffn_w_down):
    prm = dict(norm_mix=norm_mix, norm_ffn=norm_ffn, norm_out=norm_out, w_out_ab=w_out_ab,
               w_in_ab_t=jnp.swapaxes(w_in_ab, 1, 2), w_in_c_t=jnp.swapaxes(w_in_c, 1, 2),
               rwkv_mu=rwkv_mu, rwkv_w0=rwkv_w0, rwkv_w2=rwkv_w2, rwkv_a0=rwkv_a0, rwkv_a2=rwkv_a2,
               rwkv_g2=rwkv_g2, rwkv_k_k=rwkv_k_k, rwkv_k_a=rwkv_k_a,
               rwkv_r_k=rwkv_r_k.reshape(rwkv_r_k.shape[0], A_WIDTH), rwkv_gn_w=rwkv_gn_w,
               rwkv_gn_b=rwkv_gn_b, lru_conv_w=lru_conv_w, lru_conv_b=lru_conv_b, lru_wa=lru_wa, lru_ba=lru_ba,
               lru_wx=lru_wx, lru_bx=lru_bx, lru_lambda=lru_lambda, w_in_c=w_in_c, w_out_c=w_out_c,
               gdn_conv_w=gdn_conv_w, gdn_a_log=gdn_a_log, gdn_dt_bias=gdn_dt_bias, gdn_norm_w=gdn_norm_w,
               ffn_w_gate=ffn_w_gate, ffn_w_up=ffn_w_up, ffn_conv_w=ffn_conv_w, ffn_conv_b=ffn_conv_b,
               ffn_w_down=ffn_w_down)
    mods_p, mods_s = _modulation(c_prompt, c_sample, w_mod, b_mod)
    outs_p = _trunk_prompt(x_prompt, mods_p, prm)
    outs_s = _trunk_sample(x_sample, mods_s,
                           (state_rwkv_wkv, state_rwkv_shift, state_lru_h, state_lru_conv, state_gdn,
                            state_gdn_conv, state_ffn_conv), prm)
    return (outs_p[0], outs_s[0]) + tuple(outs_p[1:]) + tuple(outs_s[1:])
```
